```python
import math
import jax
import jax.numpy as jnp
from jax import lax
import numpy as np

D_MODEL = 2048
BATCH = 4
SEQ = 2048
DEPTH = 4
DEC_BATCH = 8
DEC_SEQ = 1
PAST_LEN = 16384
PAGE_SIZE = 128

D_MIX = D_MODEL
D_SSD = D_MIX // 2
SSD_HD = 64
SSD_HEADS = D_SSD // SSD_HD
SSD_GROUPS = 2
SSD_STATE = 128
SSD_CHUNK = 128
D_GDN = D_MIX // 4
GDN_HD = 128
GDN_HEADS = D_GDN // GDN_HD
GDN_CHUNK = 64
D_NSA = D_MIX - D_SSD - D_GDN
NSA_HD = 64
NSA_HEADS = D_NSA // NSA_HD
NSA_KV = 2
NSA_REP = NSA_HEADS // NSA_KV
CMP_STRIDE = 16
CMP_LEN = 2 * CMP_STRIDE
CMP_HID = 128
SLC_BLOCK = 64
SLC_TOPN = 16
SLC_QBLOCK = 64
WINDOW = 512
WIN_BLOCK = 128
N_BUCKETS = 32
MAX_DISTANCE = 128
CONV_W = 4
D_FF = 5632
SSD_CONV_CH = D_SSD + 2 * SSD_GROUPS * SSD_STATE
GDN_CONV_CH = 3 * D_GDN
D_IN = D_SSD + SSD_CONV_CH + SSD_HEADS + GDN_CONV_CH + D_GDN + 2 * GDN_HEADS + D_NSA + 6 * NSA_KV * NSA_HD + 3 * NSA_HEADS
EPS = 1e-6
NEG = -1e30

kernel_name = 'hymba_ssd_gdn_nsa_macaron_step'


def _rms(x, gain=None):
    xf = x.astype(jnp.float32)
    y = xf * lax.rsqrt(jnp.mean(xf * xf, axis=-1, keepdims=True) + EPS)
    if gain is not None:
        y = y * gain.astype(jnp.float32)
    return y.astype(x.dtype)


def _l2norm(x):
    xf = x.astype(jnp.float32)
    return xf * lax.rsqrt(jnp.sum(xf * xf, axis=-1, keepdims=True) + EPS)


def _swiglu(x, w_gate, w_up, w_down):
    return (jax.nn.silu(x @ w_gate) * (x @ w_up)) @ w_down


def _causal_conv(x, w, prev):
    L = x.shape[1]
    xp = jnp.concatenate([prev.astype(x.dtype), x], axis=1)
    y = xp[:, 0:L] * w[0]
    for j in range(1, CONV_W):
        y = y + xp[:, j:j + L] * w[j]
    return y, xp[:, xp.shape[1] - (CONV_W - 1):]


def _to_chunks(t, cl, nc):
    pad = nc * cl - t.shape[1]
    t = jnp.pad(t, [(0, 0), (0, pad)] + [(0, 0)] * (t.ndim - 2))
    return t.reshape((t.shape[0], nc, cl) + t.shape[2:])


def _ssd_chunked(x, dt, a, b, c, h0):
    f32 = jnp.float32
    Bsz, L, H, P = x.shape
    G, N = b.shape[-2:]
    R = H // G
    cl = min(SSD_CHUNK, L)
    nc = -(-L // cl)
    xdt = _to_chunks(x.astype(f32) * dt[..., None], cl, nc).reshape(Bsz, nc, cl, G, R, P)
    da = _to_chunks(dt * a, cl, nc).reshape(Bsz, nc, cl, G, R)
    bc = _to_chunks(b.astype(f32), cl, nc)
    cc = _to_chunks(c.astype(f32), cl, nc)
    cum = jnp.cumsum(da, axis=2)
    causal = jnp.tril(jnp.ones((cl, cl), bool))[:, :, None, None]
    seg = cum[:, :, :, None] - cum[:, :, None, :]
    decay = jnp.where(causal, jnp.exp(jnp.where(causal, seg, 0.0)), 0.0)
    cb = jnp.einsum('bclgn,bcsgn->bclsg', cc, bc)
    y_diag = jnp.einsum('bclsg,bclsgr,bcsgrp->bclgrp', cb, decay, xdt)
    last = cum[:, :, -1]
    states = jnp.einsum('bcsgn,bcsgr,bcsgrp->bcgrpn', bc, jnp.exp(last[:, :, None] - cum), xdt)

    def step(h, inp):
        st, lc = inp
        return jnp.exp(lc)[..., None, None] * h + st, h

    h_last, h_prev = lax.scan(step, h0.astype(f32).reshape(Bsz, G, R, P, N),
                              (jnp.moveaxis(states, 1, 0), jnp.moveaxis(last, 1, 0)))
    h_prev = jnp.moveaxis(h_prev, 0, 1)
    y_off = jnp.einsum('bclgn,bcgrpn,bclgr->bclgrp', cc, h_prev, jnp.exp(cum))
    y = (y_diag + y_off).reshape(Bsz, nc * cl, H, P)[:, :L]
    return y, h_last.reshape(Bsz, H, P, N)


def _gated_delta_chunked(q, k, v, beta, g, s0):
    f32 = jnp.float32
    Bsz, L, H, Dk = q.shape
    Dv = v.shape[-1]
    cl = min(GDN_CHUNK, L)
    nc = -(-L // cl)
    ch = lambda t: jnp.moveaxis(_to_chunks(t.astype(f32), cl, nc), 3, 2)
    q, k, v, beta, g = ch(q), ch(k), ch(v), ch(beta), ch(g)
    cum = jnp.cumsum(g, axis=-1)
    incl = jnp.tril(jnp.ones((cl, cl), bool))
    strict = jnp.tril(jnp.ones((cl, cl), bool), -1)
    seg = cum[..., :, None] - cum[..., None, :]
    decay = jnp.where(incl, jnp.exp(jnp.where(incl, seg, 0.0)), 0.0)
    kkt = jnp.einsum('bchld,bchsd->bchls', k, k)
    a_mat = jnp.where(strict, beta[..., :, None] * kkt * decay, 0.0)
    rhs = jnp.concatenate([beta[..., None] * v, (beta * jnp.exp(cum))[..., None] * k], axis=-1)
    sol = lax.linalg.triangular_solve(a_mat + jnp.eye(cl, dtype=f32), rhs, left_side=True, lower=True,
                                      unit_diagonal=True)
    u, w = sol[..., :Dv], sol[..., Dv:]
    qk = jnp.einsum('bchld,bchsd->bchls', q, k) * decay

    def step(s, inp):
        qc, kc, uc, wc, qkc, cumc = inp
        v_new = uc - jnp.einsum('bhlk,bhkv->bhlv', wc, s)
        o = jnp.einsum('bhlk,bhkv->bhlv', qc * jnp.exp(cumc)[..., None], s) + jnp.einsum('bhls,bhsv->bhlv', qkc, v_new)
        lastc = cumc[..., -1:]
        s = jnp.exp(lastc)[..., None] * s + jnp.einsum('bhlk,bhlv->bhkv', kc * jnp.exp(lastc - cumc)[..., None], v_new)
        return s, o

    s_last, o = lax.scan(step, s0.astype(f32), tuple(jnp.moveaxis(t, 1, 0) for t in (q, k, u, w, qk, cum)))
    o = jnp.transpose(o, (1, 0, 3, 2, 4)).reshape(Bsz, nc * cl, H, Dv)[:, :L]
    return o, s_last


def _t5_bucket(dist):
    n = jnp.maximum(dist, 0)
    exact = N_BUCKETS // 2
    nf = jnp.maximum(n, 1).astype(jnp.float32)
    large = exact + (jnp.log(nf / exact) / math.log(MAX_DISTANCE / exact) * (N_BUCKETS - exact)).astype(jnp.int32)
    return jnp.where(n < exact, n, jnp.minimum(large, N_BUCKETS - 1))


def _masked_softmax(logits, mask):
    p = jax.nn.softmax(jnp.where(mask, logits, NEG), axis=-1)
    return jnp.where(mask, p, 0.0)


def _attend(q, k, v, q_pos, k_pos, mask, rel_bias):
    G, R, D = q.shape[-3:]
    logits = jnp.einsum('...qgrd,...kgd->...grqk', q, k).astype(jnp.float32) * (D ** -0.5)
    bias = rel_bias[_t5_bucket(q_pos[..., :, None] - k_pos[..., None, :])]
    bias = jnp.moveaxis(bias, -1, -3)
    bias = bias.reshape(bias.shape[:-3] + (G, R) + bias.shape[-2:]).astype(jnp.float32)
    p = _masked_softmax(logits + bias, mask[..., None, None, :, :])
    return jnp.einsum('...grqk,...kgd->...qgrd', p.astype(v.dtype), v), p


def _selected_attention(q, q_pos, idx, k_blocks, v_blocks, rel_bias):
    B, Qb, G, R, D = q.shape
    n_sel = idx.shape[-1]
    b_ix = jnp.arange(B)[:, None, None, None]
    g_ix = jnp.arange(G)[None, None, :, None]
    ks = k_blocks[b_ix, g_ix, idx].reshape(B, Qb, G, n_sel * SLC_BLOCK, D)
    vs = v_blocks[b_ix, g_ix, idx].reshape(B, Qb, G, n_sel * SLC_BLOCK, D)
    k_pos = (idx[..., None] * SLC_BLOCK + jnp.arange(SLC_BLOCK)).reshape(B, Qb, G, n_sel * SLC_BLOCK)
    dist = q_pos[None, :, None, None] - k_pos
    logits = jnp.einsum('bqgrd,bqgkd->bqgrk', q, ks).astype(jnp.float32) * (D ** -0.5)
    bias = rel_bias.reshape(N_BUCKETS, G, R)[_t5_bucket(dist), g_ix]
    p = _masked_softmax(logits + jnp.moveaxis(bias, -1, -2).astype(jnp.float32), (dist >= 0)[:, :, :, None, :])
    return jnp.einsum('bqgrk,bqgkd->bqgrd', p.astype(vs.dtype), vs)


def _compress(raw, pe, w1, w2):
    B, T, G, D = raw.shape
    n_cmp = (T - CMP_LEN) // CMP_STRIDE + 1
    r = raw[:, :CMP_STRIDE * (n_cmp + 1)].reshape(B, n_cmp + 1, CMP_STRIDE, G, D)
    blocks = jnp.concatenate([r[:, :-1], r[:, 1:]], axis=2) + pe[:, None, :]
    blocks = jnp.moveaxis(blocks, 3, 2).reshape(B, n_cmp, G, CMP_LEN * D)
    return jax.nn.silu(blocks @ w1) @ w2


def _window_band(q, k, v, rel_bias):
    B, L, G, R, D = q.shape
    wb = WIN_BLOCK
    nb = L // wb
    nw = WINDOW // wb
    qb = q.reshape(B, nb, wb, G, R, D)
    band = lambda t: jnp.concatenate(
        [t[:, s:s + nb] for s in range(nw + 1)], axis=2)
    kp = jnp.pad(k, ((0, 0), (WINDOW, 0), (0, 0), (0, 0))).reshape(B, nb + nw, wb, G, D)
    vp = jnp.pad(v, ((0, 0), (WINDOW, 0), (0, 0), (0, 0))).reshape(B, nb + nw, wb, G, D)
    q_pos = jnp.arange(L).reshape(nb, wb)
    k_pos = jnp.arange(nb)[:, None] * wb - WINDOW + jnp.arange((nw + 1) * wb)[None, :]
    d = q_pos[:, :, None] - k_pos[:, None, :]
    mask = (d >= 0) & (d <= WINDOW) & (k_pos[:, None, :] >= 0)
    o, _ = _attend(qb, band(kp), band(vp), q_pos, k_pos, mask, rel_bias)
    return o.reshape(B, L, G, R, D)


def _nsa_compressed_selected(q, q_pos, k_cmp, v_cmp, k_slc, v_slc, lw, rel_bias):
    B, L = q.shape[:2]
    T = k_cmp.shape[1]
    kc = _rms(_compress(k_cmp, lw['nsa_cmp_pe'][0], lw['nsa_cmp_w1'][0], lw['nsa_cmp_w2'][0]), lw['nsa_k_norm'][0])
    vc = _compress(v_cmp, lw['nsa_cmp_pe'][1], lw['nsa_cmp_w1'][1], lw['nsa_cmp_w2'][1])
    n_cmp = kc.shape[1]
    cmp_start = jnp.arange(n_cmp) * CMP_STRIDE
    cmp_end = cmp_start + CMP_LEN - 1
    o_cmp, p_cmp = _attend(q, kc, vc, q_pos, cmp_end, cmp_end[None, :] <= q_pos[:, None], rel_bias)
    n_slc = -(-T // SLC_BLOCK)
    slc_start = jnp.arange(n_slc) * SLC_BLOCK
    overlap = ((cmp_start[:, None] < slc_start[None, :] + SLC_BLOCK)
               & (cmp_start[:, None] + CMP_LEN > slc_start[None, :])).astype(jnp.float32)
    imp = jnp.einsum('bgrqc,cj->bqgj', p_cmp, overlap)
    cur = q_pos // SLC_BLOCK
    blk = jnp.arange(n_slc)
    valid = (slc_start[None, :] <= q_pos[:, None])[None, :, None, :]
    forced = ((blk[None, :] == 0) | (blk[None, :] == cur[:, None]) | (blk[None, :] == cur[:, None] - 1))[None, :, None, :]
    score = jnp.where(valid, jnp.where(forced, 1e9, imp), -1e9)
    _, idx = lax.top_k(score, min(SLC_TOPN, n_slc))
    pad = n_slc * SLC_BLOCK - T

    def blocks(t):
        t = jnp.pad(t, ((0, 0), (0, pad), (0, 0), (0, 0))).reshape(B, n_slc, SLC_BLOCK, NSA_KV, NSA_HD)
        return jnp.transpose(t, (0, 3, 1, 2, 4))

    kb, vb = blocks(k_slc), blocks(v_slc)
    qb = min(SLC_QBLOCK, L)
    nqb = -(-L // qb)
    q_blk = jnp.moveaxis(_to_chunks(q, qb, nqb), 1, 0)
    i_blk = jnp.moveaxis(_to_chunks(idx, qb, nqb), 1, 0)
    p_blk = jnp.pad(q_pos, (0, nqb * qb - L), mode='edge').reshape(nqb, qb)
    o = lax.map(lambda a: _selected_attention(a[0], a[1], a[2], kb, vb, rel_bias), (q_blk, p_blk, i_blk))
    o_slc = jnp.moveaxis(o, 0, 1).reshape((B, nqb * qb) + o.shape[3:])[:, :L]
    return o_cmp, o_slc


def _token_mixing(u, lw, rel_bias, q_pos, ssd_h0, ssd_conv0, gdn_s0, gdn_conv0, past, win_buf):
    f32 = jnp.float32
    B, L, _ = u.shape
    proj = u @ lw['w_in']
    sizes = (D_SSD, SSD_CONV_CH, SSD_HEADS, GDN_CONV_CH, D_GDN, GDN_HEADS, GDN_HEADS,
             D_NSA, 6 * NSA_KV * NSA_HD, 3 * NSA_HEADS)
    cuts = [int(c) for c in np.cumsum(sizes)[:-1]]
    z_ssd, xbc, dt_raw, qkv, z_gdn, a_gdn, b_gdn, q_nsa, kv_nsa, g_nsa = jnp.split(proj, cuts, axis=-1)

    xbc, ssd_conv_new = _causal_conv(xbc, lw['ssd_conv_w'], ssd_conv0)
    xbc = jax.nn.silu(xbc + lw['ssd_conv_b'])
    xs, bs, cs = jnp.split(xbc, [D_SSD, D_SSD + SSD_GROUPS * SSD_STATE], axis=-1)
    xh = xs.reshape(B, L, SSD_HEADS, SSD_HD)
    dt = jax.nn.softplus((dt_raw + lw['ssd_dt_bias']).astype(f32))
    a = -jnp.exp(lw['ssd_a_log'].astype(f32))
    y, ssd_h_new = _ssd_chunked(xh, dt, a, bs.reshape(B, L, SSD_GROUPS, SSD_STATE),
                                cs.reshape(B, L, SSD_GROUPS, SSD_STATE), ssd_h0)
    y = y + lw['ssd_d'].astype(f32)[:, None] * xh.astype(f32)
    y = y.reshape(B, L, D_SSD) * jax.nn.silu(z_ssd.astype(f32))
    y_ssd = (_rms(y.reshape(B, L, SSD_GROUPS, D_SSD // SSD_GROUPS)).reshape(B, L, D_SSD) * lw['ssd_norm']).astype(u.dtype)

    qkv, gdn_conv_new = _causal_conv(qkv, lw['gdn_conv_w'], gdn_conv0)
    qkv = jax.nn.silu(qkv)
    qg, kg, vg = [t.reshape(B, L, GDN_HEADS, GDN_HD) for t in jnp.split(qkv, 3, axis=-1)]
    qg = _l2norm(qg) * (GDN_HD ** -0.5)
    kg = _l2norm(kg)
    beta = jax.nn.sigmoid(b_gdn.astype(f32))
    g = -jnp.exp(lw['gdn_a_log'].astype(f32)) * jax.nn.softplus((a_gdn + lw['gdn_dt_bias']).astype(f32))
    o, gdn_s_new = _gated_delta_chunked(qg, kg, vg, beta, g, gdn_s0)
    o = _rms(o) * lw['gdn_norm'] * jax.nn.silu(z_gdn.reshape(B, L, GDN_HEADS, GDN_HD).astype(f32))
    y_gdn = o.reshape(B, L, D_GDN).astype(u.dtype)

    q = _rms(q_nsa.reshape(B, L, NSA_HEADS, NSA_HD), lw['nsa_q_norm']).reshape(B, L, NSA_KV, NSA_REP, NSA_HD)
    k_cmp_new, v_cmp_new, k_slc_new, v_slc_new, k_win_new, v_win_new = [
        t.reshape(B, L, NSA_KV, NSA_HD) for t in jnp.split(kv_nsa, 6, axis=-1)]
    k_slc_new = _rms(k_slc_new, lw['nsa_k_norm'][1])
    k_win_new = _rms(k_win_new, lw['nsa_k_norm'][2])
    gates = jax.nn.sigmoid(g_nsa.astype(f32)).reshape(B, L, 3, NSA_KV, NSA_REP, 1)
    if past is None:
        k_cmp_all, v_cmp_all, k_slc_all, v_slc_all = k_cmp_new, v_cmp_new, k_slc_new, v_slc_new
    else:
        k_cmp_all, v_cmp_all, k_slc_all, v_slc_all = [
            jnp.concatenate([p.astype(u.dtype), n], axis=1)
            for p, n in zip(past, (k_cmp_new, v_cmp_new, k_slc_new, v_slc_new))]
    o_cmp, o_slc = _nsa_compressed_selected(q, q_pos, k_cmp_all, v_cmp_all, k_slc_all, v_slc_all, lw, rel_bias)
    if win_buf is None:
        o_win = _window_band(q, k_win_new, v_win_new, rel_bias)
        keep = min(WINDOW, L)
        win_k, win_v = k_win_new[:, L - keep:], v_win_new[:, L - keep:]
    else:
        n_buf = win_buf[0].shape[1]
        n_past = past[0].shape[1]
        wk = jnp.concatenate([win_buf[0].astype(u.dtype), k_win_new], axis=1)
        wv = jnp.concatenate([win_buf[1].astype(u.dtype), v_win_new], axis=1)
        k_pos = n_past - n_buf + jnp.arange(n_buf + L)
        d = q_pos[:, None] - k_pos[None, :]
        o_win, _ = _attend(q, wk, wv, q_pos, k_pos, (d >= 0) & (d <= WINDOW), rel_bias)
        win_k, win_v = wk[:, wk.shape[1] - n_buf:], wv[:, wv.shape[1] - n_buf:]
    y_nsa = (gates[:, :, 0] * o_cmp + gates[:, :, 1] * o_slc + gates[:, :, 2] * o_win).reshape(B, L, D_NSA).astype(u.dtype)

    out = jnp.concatenate([y_ssd, y_gdn, y_nsa], axis=-1) @ lw['w_out']
    new_state = (ssd_h_new.astype(u.dtype), ssd_conv_new, gdn_s_new.astype(u.dtype), gdn_conv_new,
                 k_cmp_new, v_cmp_new, k_slc_new, v_slc_new, win_k, win_v)
    return out.astype(u.dtype), new_state


def _layer(x, lw, rel_bias, q_pos, ssd_h0, ssd_conv0, gdn_s0, gdn_conv0, past, win_buf):
    x = x + 0.5 * _swiglu(_rms(x, lw['norm_ffn1']), lw['w_ffn1_gate'], lw['w_ffn1_up'], lw['w_ffn1_down'])
    mix, new_state = _token_mixing(_rms(x, lw['norm_mix']), lw, rel_bias, q_pos, ssd_h0, ssd_conv0,
                                   gdn_s0, gdn_conv0, past, win_buf)
    x = x + mix
    x = x + 0.5 * _swiglu(_rms(x, lw['norm_ffn2']), lw['w_ffn2_gate'], lw['w_ffn2_up'], lw['w_ffn2_down'])
    return x, new_state


def _gather_pages(pool, page_table):
    rows = pool[page_table]
    return rows.reshape((rows.shape[0], rows.shape[1] * rows.shape[2]) + rows.shape[3:])


def setup_inputs(seed: int = 0) -> dict:
    key = jax.random.key(seed)
    keys = jax.random.split(key, 64)
    counter = [0]
    f32 = jnp.float32

    def nk():
        counter[0] += 1
        return keys[counter[0] - 1]

    def nrm(shape, scale):
        return jax.random.normal(nk(), shape, f32) * scale

    def gain(shape):
        return 1.0 + nrm(shape, 0.02)

    def dt_bias(shape):
        dt = jnp.exp(jax.random.uniform(nk(), shape, f32, math.log(1e-3), math.log(1e-1)))
        return dt + jnp.log(-jnp.expm1(-dt))

    def a_log(shape):
        return jnp.log(jax.random.uniform(nk(), shape, f32, 1.0, 16.0))

    n_pages = PAST_LEN // PAGE_SIZE
    n_pool = (5 * DEC_BATCH * n_pages + 3) // 4
    w_buf = min(WINDOW, PAST_LEN)
    pool_shape = (DEPTH, n_pool, PAGE_SIZE, NSA_KV, NSA_HD)
    page_table = jax.random.permutation(nk(), n_pool)[:DEC_BATCH * n_pages].reshape(DEC_BATCH, n_pages).astype(jnp.int32)
    return {
        'x_prompt': nrm((BATCH, SEQ, D_MODEL), 1.0),
        'x_sample': nrm((DEC_BATCH, DEC_SEQ, D_MODEL), 1.0),
        'state_ssd': nrm((DEPTH, DEC_BATCH, SSD_HEADS, SSD_HD, SSD_STATE), 0.1),
        'state_ssd_conv': nrm((DEPTH, DEC_BATCH, CONV_W - 1, SSD_CONV_CH), 1.0),
        'state_gdn': nrm((DEPTH, DEC_BATCH, GDN_HEADS, GDN_HD, GDN_HD), 0.1),
        'state_gdn_conv': nrm((DEPTH, DEC_BATCH, CONV_W - 1, GDN_CONV_CH), 1.0),
        'cache_cmp_k': nrm(pool_shape, 1.0),
        'cache_cmp_v': nrm(pool_shape, 1.0),
        'cache_slc_k': nrm(pool_shape, 1.0),
        'cache_slc_v': nrm(pool_shape, 1.0),
        'cache_win_k': nrm((DEPTH, DEC_BATCH, w_buf, NSA_KV, NSA_HD), 1.0),
        'cache_win_v': nrm((DEPTH, DEC_BATCH, w_buf, NSA_KV, NSA_HD), 1.0),
        'page_table': page_table,
        'rel_bias': nrm((N_BUCKETS, NSA_HEADS), 0.1),
        'norm_ffn1': gain((DEPTH, D_MODEL)),
        'w_ffn1_gate': nrm((DEPTH, D_MODEL, D_FF), D_MODEL ** -0.5),
        'w_ffn1_up': nrm((DEPTH, D_MODEL, D_FF), D_MODEL ** -0.5),
        'w_ffn1_down': nrm((DEPTH, D_FF, D_MODEL), D_FF ** -0.5),
        'norm_mix': gain((DEPTH, D_MODEL)),
        'w_in': nrm((DEPTH, D_MODEL, D_IN), D_MODEL ** -0.5),
        'ssd_conv_w': nrm((DEPTH, CONV_W, SSD_CONV_CH), 0.5),
        'ssd_conv_b': nrm((DEPTH, SSD_CONV_CH), 0.02),
        'ssd_dt_bias': dt_bias((DEPTH, SSD_HEADS)),
        'ssd_a_log': a_log((DEPTH, SSD_HEADS)),
        'ssd_d': gain((DEPTH, SSD_HEADS)),
        'ssd_norm': gain((DEPTH, D_SSD)),
        'gdn_conv_w': nrm((DEPTH, CONV_W, GDN_CONV_CH), 0.5),
        'gdn_dt_bias': dt_bias((DEPTH, GDN_HEADS)),
        'gdn_a_log': a_log((DEPTH, GDN_HEADS)),
        'gdn_norm': gain((DEPTH, GDN_HD)),
        'nsa_q_norm': gain((DEPTH, NSA_HD)),
        'nsa_k_norm': gain((DEPTH, 3, NSA_HD)),
        'nsa_cmp_pe': nrm((DEPTH, 2, CMP_LEN, NSA_HD), 0.1),
        'nsa_cmp_w1': nrm((DEPTH, 2, CMP_LEN * NSA_HD, CMP_HID), (CMP_LEN * NSA_HD) ** -0.5),
        'nsa_cmp_w2': nrm((DEPTH, 2, CMP_HID, NSA_HD), CMP_HID ** -0.5),
        'w_out': nrm((DEPTH, D_MIX, D_MODEL), D_MIX ** -0.5),
        'norm_ffn2': gain((DEPTH, D_MODEL)),
        'w_ffn2_gate': nrm((DEPTH, D_MODEL, D_FF), D_MODEL ** -0.5),
        'w_ffn2_up': nrm((DEPTH, D_MODEL, D_FF), D_MODEL ** -0.5),
        'w_ffn2_down': nrm((DEPTH, D_FF, D_MODEL), D_FF ** -0.5),
    }


def reference(x_prompt, x_sample, state_ssd, state_ssd_conv, state_gdn, state_gdn_conv,
              cache_cmp_k, cache_cmp_v, cache_slc_k, cache_slc_v, cache_win_k, cache_win_v, page_table,
              rel_bias, norm_ffn1, w_ffn1_gate, w_ffn1_up, w_ffn1_down, norm_mix, w_in,
              ssd_conv_w, ssd_conv_b, ssd_dt_bias, ssd_a_log, ssd_d, ssd_norm,
              gdn_conv_w, gdn_dt_bias, gdn_a_log, gdn_norm,
              nsa_q_norm, nsa_k_norm, nsa_cmp_pe, nsa_cmp_w1, nsa_cmp_w2, w_out,
              norm_ffn2, w_ffn2_gate, w_ffn2_up, w_ffn2_down):
    bp = x_prompt.shape[0]
    n_past = page_table.shape[1] * PAGE_SIZE
    pos_p = jnp.arange(x_prompt.shape[1])
    pos_s = n_past + jnp.arange(x_sample.shape[1])
    hp, hs = x_prompt, x_sample
    outs_p, outs_s = [], []
    for i in range(DEPTH):
        lw = dict(norm_ffn1=norm_ffn1[i], w_ffn1_gate=w_ffn1_gate[i], w_ffn1_up=w_ffn1_up[i],
                  w_ffn1_down=w_ffn1_down[i], norm_mix=norm_mix[i], w_in=w_in[i],
                  ssd_conv_w=ssd_conv_w[i], ssd_conv_b=ssd_conv_b[i], ssd_dt_bias=ssd_dt_bias[i],
                  ssd_a_log=ssd_a_log[i], ssd_d=ssd_d[i], ssd_norm=ssd_norm[i],
                  gdn_conv_w=gdn_conv_w[i], gdn_dt_bias=gdn_dt_bias[i], gdn_a_log=gdn_a_log[i],
                  gdn_norm=gdn_norm[i], nsa_q_norm=nsa_q_norm[i], nsa_k_norm=nsa_k_norm[i],
                  nsa_cmp_pe=nsa_cmp_pe[i], nsa_cmp_w1=nsa_cmp_w1[i], nsa_cmp_w2=nsa_cmp_w2[i],
                  w_out=w_out[i], norm_ffn2=norm_ffn2[i], w_ffn2_gate=w_ffn2_gate[i],
                  w_ffn2_up=w_ffn2_up[i], w_ffn2_down=w_ffn2_down[i])
        hp, st_p = _layer(hp, lw, rel_bias, pos_p,
                          jnp.zeros((bp, SSD_HEADS, SSD_HD, SSD_STATE), x_prompt.dtype),
                          jnp.zeros((bp, CONV_W - 1, SSD_CONV_CH), x_prompt.dtype),
                          jnp.zeros((bp, GDN_HEADS, GDN_HD, GDN_HD), x_prompt.dtype),
                          jnp.zeros((bp, CONV_W - 1, GDN_CONV_CH), x_prompt.dtype), None, None)
        past = tuple(_gather_pages(c[i], page_table) for c in (cache_cmp_k, cache_cmp_v, cache_slc_k, cache_slc_v))
        hs, st_s = _layer(hs, lw, rel_bias, pos_s, state_ssd[i], state_ssd_conv[i], state_gdn[i],
                          state_gdn_conv[i], past, (cache_win_k[i], cache_win_v[i]))
        outs_p.append(st_p)
        outs_s.append(st_s)
    (p_ssd, p_ssd_conv, p_gdn, p_gdn_conv, p_cmp_k, p_cmp_v, p_slc_k, p_slc_v,
     p_win_k, p_win_v) = [jnp.stack(t) for t in zip(*outs_p)]
    (s_ssd, s_ssd_conv, s_gdn, s_gdn_conv, s_cmp_k, s_cmp_v, s_slc_k, s_slc_v,
     s_win_k, s_win_v) = [jnp.stack(t) for t in zip(*outs_s)]
    return (hp, hs, p_ssd, p_ssd_conv, p_gdn, p_gdn_conv, p_cmp_k, p_cmp_v, p_slc_k, p_slc_v, p_win_k, p_win_v,
            s_ssd, s_ssd_conv, s_gdn, s_gdn_conv, s_cmp_k, s_cmp_v, s_slc_k, s_slc_v, s_win_k, s_win_v)
```

```python
import functools
import math

import numpy as np
import jax
import jax.numpy as jnp
from jax import lax
from jax.experimental import pallas as pl
from jax.experimental.pallas import tpu as pltpu

f32 = jnp.float32
bf16 = jnp.bfloat16
i32 = jnp.int32
HI = lax.Precision.HIGHEST

D_MODEL = 2048
DEPTH = 4
PAGE = 128
D_SSD = 1024
SSD_HD = 64
SSD_HEADS = 16
SSD_GROUPS = 2
SSD_STATE = 128
SSD_CHUNK = 128
D_GDN = 512
GDN_HD = 128
GDN_HEADS = 4
GDN_CHUNK = 64
D_NSA = 512
NSA_HD = 64
NSA_HEADS = 8
NSA_KV = 2
NSA_REP = 4
CMP_STRIDE = 16
CMP_LEN = 32
CMP_HID = 128
SLC_BLOCK = 64
SLC_TOPN = 16
WINDOW = 512
N_BUCKETS = 32
MAX_DISTANCE = 128
CONV_W = 4
D_FF = 5632
SSD_CONV_CH = 1536
GDN_CONV_CH = 1536
EPS = 1e-6
NEG = -1e30

XBC_OFF = 0
QKV_OFF = 1536
ZS_OFF = 3072
ZG_OFF = 4096
QN_OFF = 4608
KV_OFF = 5120
SM_OFF = 5888
PROJ_W = 6144
SM_DT, SM_A, SM_B, SM_G = 0, 16, 20, 24
KV_CK, KV_CV, KV_SK, KV_SV, KV_WK, KV_WV = 0, 128, 256, 384, 512, 640

VMEM_LIMIT = 56 * 1024 * 1024


def _bucket_thresholds():
    exact = N_BUCKETS // 2
    d = np.arange(0, 4 * MAX_DISTANCE)
    nf = np.maximum(d, 1).astype(np.float32)
    large = exact + (np.log(nf / np.float32(exact)) / np.float32(math.log(MAX_DISTANCE / exact))
                     * np.float32(N_BUCKETS - exact)).astype(np.int32)
    bucket = np.where(d < exact, d, np.minimum(large, N_BUCKETS - 1))
    return [int(np.argmax(bucket >= b)) for b in range(N_BUCKETS)]


_THR = _bucket_thresholds()
BIAS_FAR = _THR[N_BUCKETS - 1]


def _bias_chain(d, tabcols):
    v = jnp.broadcast_to(tabcols[N_BUCKETS - 1], d.shape)
    for b in range(N_BUCKETS - 2, -1, -1):
        v = jnp.where(d < _THR[b + 1], tabcols[b], v)
    return v


def _silu(x):
    return x * jax.nn.sigmoid(x)


def _softplus(x):
    return jnp.maximum(x, 0.0) + jnp.log1p(jnp.exp(-jnp.abs(x)))


def _dot(a, b, prec=None):
    if prec is None:
        a, b = a.astype(bf16), b.astype(bf16)
    return jnp.dot(a, b, preferred_element_type=f32, precision=prec)


def _dot_nt(a, b, prec=None):
    if prec is None:
        a, b = a.astype(bf16), b.astype(bf16)
    return lax.dot_general(a, b, (((1,), (1,)), ((), ())), preferred_element_type=f32, precision=prec)


def _dot_tn(a, b, prec=None):
    if prec is None:
        a, b = a.astype(bf16), b.astype(bf16)
    return lax.dot_general(a, b, (((0,), (0,)), ((), ())), preferred_element_type=f32, precision=prec)


def _iota(shape, dim):
    return lax.broadcasted_iota(i32, shape, dim)


def _group_ones(n, width):
    return (_iota((n, n), 0) // width == _iota((n, n), 1) // width).astype(f32)


def _group_rms(x, width):
    ss = _dot(x * x, _group_ones(x.shape[1], width), HI)
    return x * lax.rsqrt(ss * (1.0 / width) + EPS)


def _params(sem):
    return pltpu.CompilerParams(dimension_semantics=sem, vmem_limit_bytes=VMEM_LIMIT)


def _ffn_kernel(x_ref, g_ref, wg_ref, wu_ref, wd_ref, o_ref, h_ref):
    @pl.when(pl.program_id(1) == 0)
    def _():
        x = x_ref[...]
        h = x * lax.rsqrt(jnp.mean(x * x, axis=-1, keepdims=True) + EPS) * g_ref[...]
        h_ref[...] = h.astype(bf16)
        o_ref[...] = x

    h = h_ref[...]
    a = jnp.dot(h, wg_ref[...].astype(bf16), preferred_element_type=f32)
    u = jnp.dot(h, wu_ref[...].astype(bf16), preferred_element_type=f32)
    o_ref[...] += 0.5 * _dot(_silu(a) * u, wd_ref[...])


def _ffn(x, gain, wg, wu, wd, layer):
    n = x.shape[0]
    tm = min(n, 1024)
    tf = 256 if tm == 1024 else 512
    return pl.pallas_call(
        _ffn_kernel,
        grid=(n // tm, D_FF // tf),
        in_specs=[
            pl.BlockSpec((tm, D_MODEL), lambda i, j: (i, 0)),
            pl.BlockSpec((None, 1, D_MODEL), lambda i, j: (layer, 0, 0)),
            pl.BlockSpec((None, D_MODEL, tf), lambda i, j: (layer, 0, j)),
            pl.BlockSpec((None, D_MODEL, tf), lambda i, j: (layer, 0, j)),
            pl.BlockSpec((None, tf, D_MODEL), lambda i, j: (layer, j, 0)),
        ],
        out_specs=pl.BlockSpec((tm, D_MODEL), lambda i, j: (i, 0)),
        out_shape=jax.ShapeDtypeStruct((n, D_MODEL), f32),
        scratch_shapes=[pltpu.VMEM((tm, D_MODEL), bf16)],
        compiler_params=_params(("parallel", "arbitrary")),
        name="ffn",
    )(x, gain, wg, wu, wd)


def _inproj_kernel(x_ref, g_ref, w_ref, o_ref, h_ref):
    @pl.when(pl.program_id(1) == 0)
    def _():
        x = x_ref[...]
        h = x * lax.rsqrt(jnp.mean(x * x, axis=-1, keepdims=True) + EPS) * g_ref[...]
        h_ref[...] = h.astype(bf16)

    o_ref[...] = jnp.dot(h_ref[...], w_ref[...], preferred_element_type=f32)


def _inproj(x, gain, w_r, layer):
    n = x.shape[0]
    tm = min(n, 1024)
    tn = 512
    return pl.pallas_call(
        _inproj_kernel,
        grid=(n // tm, PROJ_W // tn),
        in_specs=[
            pl.BlockSpec((tm, D_MODEL), lambda i, j: (i, 0)),
            pl.BlockSpec((None, 1, D_MODEL), lambda i, j: (layer, 0, 0)),
            pl.BlockSpec((None, D_MODEL, tn), lambda i, j: (layer, 0, j)),
        ],
        out_specs=pl.BlockSpec((tm, tn), lambda i, j: (i, j)),
        out_shape=jax.ShapeDtypeStruct((n, PROJ_W), f32),
        scratch_shapes=[pltpu.VMEM((tm, D_MODEL), bf16)],
        compiler_params=_params(("parallel", "arbitrary")),
        name="inproj",
    )(x, gain, w_r)


def _outproj_kernel(x_ref, ys_ref, yg_ref, yn_ref, ws_ref, wg_ref, wn_ref, o_ref):
    acc = _dot(ys_ref[...], ws_ref[...])
    acc += _dot(yg_ref[...], wg_ref[...])
    acc += _dot(yn_ref[...], wn_ref[...])
    o_ref[...] = x_ref[...] + acc


def _outproj(x, ys, yg, yn, w_out, layer):
    n = x.shape[0]
    tm = min(n, 1024)
    tn = 512
    return pl.pallas_call(
        _outproj_kernel,
        grid=(n // tm, D_MODEL // tn),
        in_specs=[
            pl.BlockSpec((tm, tn), lambda i, j: (i, j)),
            pl.BlockSpec((tm, D_SSD), lambda i, j: (i, 0)),
            pl.BlockSpec((tm, D_GDN), lambda i, j: (i, 0)),
            pl.BlockSpec((tm, D_NSA), lambda i, j: (i, 0)),
            pl.BlockSpec((None, D_SSD, tn), lambda i, j: (layer, 0, j)),
            pl.BlockSpec((None, D_GDN, tn), lambda i, j: (layer, 2, j)),
            pl.BlockSpec((None, D_NSA, tn), lambda i, j: (layer, 3, j)),
        ],
        out_specs=pl.BlockSpec((tm, tn), lambda i, j: (i, j)),
        out_shape=jax.ShapeDtypeStruct((n, D_MODEL), f32),
        compiler_params=_params(("parallel", "arbitrary")),
        name="outproj",
    )(x, ys, yg, yn, w_out, w_out, w_out)


def _prep_kernel(q_ref, kv_ref, qg_ref, kg_ref, qn_ref, kv6_ref):
    for s in range(D_NSA // 128):
        sl = slice(s * 128, (s + 1) * 128)
        qn_ref[:, sl] = _group_rms(q_ref[:, sl], NSA_HD) * qg_ref[...]
    kv = kv_ref[...]
    kv6_ref[...] = kv[:, :6 * 128]
    kv6_ref[:, KV_SK:KV_SK + 128] = _group_rms(kv[:, KV_SK:KV_SK + 128], NSA_HD) * kg_ref[1:2, :]
    kv6_ref[:, KV_WK:KV_WK + 128] = _group_rms(kv[:, KV_WK:KV_WK + 128], NSA_HD) * kg_ref[2:3, :]


def _prep(proj, qgain, kgain):
    n = proj.shape[0]
    tm = min(n, 512)
    return pl.pallas_call(
        _prep_kernel,
        grid=(n // tm,),
        in_specs=[
            pl.BlockSpec((tm, D_NSA), lambda i: (i, QN_OFF // D_NSA)),
            pl.BlockSpec((tm, 1024), lambda i: (i, KV_OFF // 1024)),
            pl.BlockSpec((1, 128), lambda i: (0, 0)),
            pl.BlockSpec((3, 128), lambda i: (0, 0)),
        ],
        out_specs=[pl.BlockSpec((tm, D_NSA), lambda i: (i, 0)),
                   pl.BlockSpec((tm, 768), lambda i: (i, 0))],
        out_shape=[jax.ShapeDtypeStruct((n, D_NSA), f32), jax.ShapeDtypeStruct((n, 768), f32)],
        compiler_params=_params(("parallel",)),
        name="nsa_prep",
    )(proj, proj, qgain, kgain)


def _ssd_prompt_kernel(xbc_ref, z_ref, sm_ref, cw_ref, cb_ref, dtb_ref, alog_ref, dexp_ref, nrm_ref,
                       y_ref, st_ref, cst_ref, xbuf, hst, ybuf):
    c = pl.program_id(1)
    cl = SSD_CHUNK

    @pl.when(c == 0)
    def _():
        xbuf[0:8, :] = jnp.zeros((8, SSD_CONV_CH), f32)
        hst[...] = jnp.zeros(hst.shape, f32)

    x = xbc_ref[...]
    xbuf[8:8 + cl, :] = x
    conv = cw_ref[3:4, :] * x
    for k in range(1, CONV_W):
        conv += cw_ref[3 - k:4 - k, :] * xbuf[pl.ds(8 - k, cl), :]
    xbuf[0:8, :] = x[cl - 8:cl, :]
    xc = _silu(conv + cb_ref[...])
    xs = xc[:, :D_SSD]

    dt = _softplus(sm_ref[...] + dtb_ref[...])
    da = dt * (-jnp.exp(alog_ref[...]))
    row, col = _iota((cl, cl), 0), _iota((cl, cl), 1)
    causal = row >= col
    cum = _dot(causal.astype(f32), da, HI)
    cum_t = cum.T

    for g in range(SSD_GROUPS):
        bm = xc[:, D_SSD + g * SSD_STATE:D_SSD + (g + 1) * SSD_STATE]
        cm = xc[:, D_SSD + (SSD_GROUPS + g) * SSD_STATE:D_SSD + (SSD_GROUPS + g + 1) * SSD_STATE]
        cb = _dot_nt(cm, bm)
        for r in range(SSD_HEADS // SSD_GROUPS):
            h = g * (SSD_HEADS // SSD_GROUPS) + r
            cum_c = cum[:, h:h + 1]
            seg = cum_c - cum_t[h:h + 1, :]
            decay = jnp.where(causal, jnp.exp(jnp.where(causal, seg, 0.0)), 0.0)
            xdt = xs[:, h * SSD_HD:(h + 1) * SSD_HD] * dt[:, h:h + 1]
            last = cum[cl - 1:cl, h:h + 1]
            h0 = hst[h]
            y = _dot(cb * decay, xdt) + jnp.exp(cum_c) * _dot_nt(cm, h0)
            hst[h] = jnp.exp(last) * h0 + _dot_tn(xdt * jnp.exp(last - cum_c), bm)
            ybuf[:, h * SSD_HD:(h + 1) * SSD_HD] = y

    y = (ybuf[...] + dexp_ref[...] * xs) * _silu(z_ref[...])
    gw = D_SSD // SSD_GROUPS
    for g in range(SSD_GROUPS):
        yg = y[:, g * gw:(g + 1) * gw]
        yg = yg * lax.rsqrt(jnp.mean(yg * yg, axis=-1, keepdims=True) + EPS)
        y_ref[:, g * gw:(g + 1) * gw] = yg * nrm_ref[:, g * gw:(g + 1) * gw]

    @pl.when(c == pl.num_programs(1) - 1)
    def _():
        st_ref[...] = hst[...]
        cst_ref[...] = x[cl - (CONV_W - 1):cl, :]


def _ssd_prompt(proj3, lp):
    B, L, _ = proj3.shape
    cl = SSD_CHUNK
    vec = lambda w: pl.BlockSpec((1, w), lambda b, c: (0, 0))
    return pl.pallas_call(
        _ssd_prompt_kernel,
        grid=(B, L // cl),
        in_specs=[
            pl.BlockSpec((None, cl, SSD_CONV_CH), lambda b, c: (b, c, XBC_OFF // SSD_CONV_CH)),
            pl.BlockSpec((None, cl, D_SSD), lambda b, c: (b, c, ZS_OFF // D_SSD)),
            pl.BlockSpec((None, cl, 128), lambda b, c: (b, c, SM_OFF // 128)),
            pl.BlockSpec((CONV_W, SSD_CONV_CH), lambda b, c: (0, 0)),
            vec(SSD_CONV_CH), vec(128), vec(128), vec(D_SSD), vec(D_SSD),
        ],
        out_specs=[
            pl.BlockSpec((None, cl, D_SSD), lambda b, c: (b, c, 0)),
            pl.BlockSpec((None, SSD_HEADS, SSD_HD, SSD_STATE), lambda b, c: (b, 0, 0, 0)),
            pl.BlockSpec((None, CONV_W - 1, SSD_CONV_CH), lambda b, c: (b, 0, 0)),
        ],
        out_shape=[
            jax.ShapeDtypeStruct((B, L, D_SSD), f32),
            jax.ShapeDtypeStruct((B, SSD_HEADS, SSD_HD, SSD_STATE), f32),
            jax.ShapeDtypeStruct((B, CONV_W - 1, SSD_CONV_CH), f32),
        ],
        scratch_shapes=[
            pltpu.VMEM((8 + cl, SSD_CONV_CH), f32),
            pltpu.VMEM((SSD_HEADS, SSD_HD, SSD_STATE), f32),
            pltpu.VMEM((cl, D_SSD), f32),
        ],
        compiler_params=_params(("parallel", "arbitrary")),
        name="ssd_prompt",
    )(proj3, proj3, proj3, lp["ssd_conv_w"], lp["ssd_conv_b"], lp["ssd_dtb"], lp["ssd_alog"],
      lp["ssd_dexp"], lp["ssd_norm"])


def _gdn_prompt_kernel(qkv_ref, z_ref, sm_ref, cw_ref, dtb_ref, alog_ref, nrm_ref,
                       y_ref, st_ref, cst_ref, xbuf, sst):
    c = pl.program_id(1)
    cl = GDN_CHUNK

    @pl.when(c == 0)
    def _():
        xbuf[0:8, :] = jnp.zeros((8, GDN_CONV_CH), f32)
        sst[...] = jnp.zeros(sst.shape, f32)

    x = qkv_ref[...]
    xbuf[8:8 + cl, :] = x
    conv = cw_ref[3:4, :] * x
    for k in range(1, CONV_W):
        conv += cw_ref[3 - k:4 - k, :] * xbuf[pl.ds(8 - k, cl), :]
    xbuf[0:8, :] = x[cl - 8:cl, :]
    xc = _silu(conv)

    sm = sm_ref[...]
    beta_all = jax.nn.sigmoid(sm)
    g_all = -jnp.exp(alog_ref[...]) * _softplus(sm + dtb_ref[...])
    row, col = _iota((cl, cl), 0), _iota((cl, cl), 1)
    incl = row >= col
    strict = row > col
    eye = (row == col).astype(f32)
    cum = _dot(incl.astype(f32), g_all, HI)
    cum_t = cum.T

    for h in range(GDN_HEADS):
        sl = slice(h * GDN_HD, (h + 1) * GDN_HD)
        qh = xc[:, sl]
        kh = xc[:, D_GDN + h * GDN_HD:D_GDN + (h + 1) * GDN_HD]
        vh = xc[:, 2 * D_GDN + h * GDN_HD:2 * D_GDN + (h + 1) * GDN_HD]
        qh = qh * lax.rsqrt(jnp.sum(qh * qh, axis=-1, keepdims=True) + EPS) * (GDN_HD ** -0.5)
        kh = kh * lax.rsqrt(jnp.sum(kh * kh, axis=-1, keepdims=True) + EPS)
        beta = beta_all[:, SM_B + h:SM_B + h + 1]
        cum_c = cum[:, SM_A + h:SM_A + h + 1]
        seg = cum_c - cum_t[SM_A + h:SM_A + h + 1, :]
        decay = jnp.where(incl, jnp.exp(jnp.where(incl, seg, 0.0)), 0.0)
        a_mat = jnp.where(strict, beta * _dot_nt(kh, kh) * decay, 0.0)
        pw = -a_mat
        inv = eye + pw
        for _ in range(5):
            pw = _dot(pw, pw, HI)
            inv = inv + _dot(inv, pw, HI)
        u = _dot(inv, beta * vh, HI)
        w = _dot(inv, (beta * jnp.exp(cum_c)) * kh, HI)
        qk = _dot_nt(qh, kh) * decay
        s0 = sst[h]
        v_new = u - _dot(w, s0)
        o = _dot(qh * jnp.exp(cum_c), s0) + _dot(qk, v_new)
        last = cum[cl - 1:cl, SM_A + h:SM_A + h + 1]
        sst[h] = jnp.exp(last) * s0 + _dot_tn(kh * jnp.exp(last - cum_c), v_new)
        o = o * lax.rsqrt(jnp.mean(o * o, axis=-1, keepdims=True) + EPS)
        y_ref[:, sl] = o * nrm_ref[...] * _silu(z_ref[:, sl])

    @pl.when(c == pl.num_programs(1) - 1)
    def _():
        st_ref[...] = sst[...]
        cst_ref[...] = x[cl - (CONV_W - 1):cl, :]


def _gdn_prompt(proj3, lp):
    B, L, _ = proj3.shape
    cl = GDN_CHUNK
    vec = lambda w: pl.BlockSpec((1, w), lambda b, c: (0, 0))
    return pl.pallas_call(
        _gdn_prompt_kernel,
        grid=(B, L // cl),
        in_specs=[
            pl.BlockSpec((None, cl, GDN_CONV_CH), lambda b, c: (b, c, QKV_OFF // GDN_CONV_CH)),
            pl.BlockSpec((None, cl, D_GDN), lambda b, c: (b, c, ZG_OFF // D_GDN)),
            pl.BlockSpec((None, cl, 128), lambda b, c: (b, c, SM_OFF // 128)),
            pl.BlockSpec((CONV_W, GDN_CONV_CH), lambda b, c: (0, 0)),
            vec(128), vec(128), vec(GDN_HD),
        ],
        out_specs=[
            pl.BlockSpec((None, cl, D_GDN), lambda b, c: (b, c, 0)),
            pl.BlockSpec((None, GDN_HEADS, GDN_HD, GDN_HD), lambda b, c: (b, 0, 0, 0)),
            pl.BlockSpec((None, CONV_W - 1, GDN_CONV_CH), lambda b, c: (b, 0, 0)),
        ],
        out_shape=[
            jax.ShapeDtypeStruct((B, L, D_GDN), f32),
            jax.ShapeDtypeStruct((B, GDN_HEADS, GDN_HD, GDN_HD), f32),
            jax.ShapeDtypeStruct((B, CONV_W - 1, GDN_CONV_CH), f32),
        ],
        scratch_shapes=[
            pltpu.VMEM((8 + cl, GDN_CONV_CH), f32),
            pltpu.VMEM((GDN_HEADS, GDN_HD, GDN_HD), f32),
        ],
        compiler_params=_params(("parallel", "arbitrary")),
        name="gdn_prompt",
    )(proj3, proj3, proj3, lp["gdn_conv_w"], lp["gdn_dtb"], lp["gdn_alog"], lp["gdn_norm"])


def _cmp_prompt_kernel(k_ref, v_ref, pe_ref, w1_ref, w2_ref, kg_ref, o_ref):
    nb = o_ref.shape[0]
    for t, src in enumerate((k_ref, v_ref)):
        p1 = [jnp.zeros((nb, CMP_HID), f32) for _ in range(NSA_KV)]
        p2 = [jnp.zeros((nb, CMP_HID), f32) for _ in range(NSA_KV)]
        for l in range(CMP_STRIDE):
            xr = src[pl.ds(l, nb, stride=CMP_STRIDE), :]
            l2 = CMP_STRIDE + l
            for g in range(NSA_KV):
                xg = xr[:, g * NSA_HD:(g + 1) * NSA_HD]
                p1[g] += _dot(xg + pe_ref[t, l:l + 1, :], w1_ref[t, l * NSA_HD:(l + 1) * NSA_HD, :])
                p2[g] += _dot(xg + pe_ref[t, l2:l2 + 1, :], w1_ref[t, l2 * NSA_HD:(l2 + 1) * NSA_HD, :])
        for g in range(NSA_KV):
            hid = p1[g] + pltpu.roll(p2[g], nb - 1, 0)
            cmp = _dot(_silu(hid), w2_ref[t])
            if t == 0:
                cmp = cmp * lax.rsqrt(jnp.mean(cmp * cmp, axis=-1, keepdims=True) + EPS) * kg_ref[0:1, 0:NSA_HD]
            lo = t * 128 + g * NSA_HD
            o_ref[:, lo:lo + NSA_HD] = cmp


def _cmp_prompt(kv6_3, lp):
    B, L, _ = kv6_3.shape
    nb = L // CMP_STRIDE
    full = lambda s: pl.BlockSpec(s, lambda b: (0,) * len(s))
    return pl.pallas_call(
        _cmp_prompt_kernel,
        grid=(B,),
        in_specs=[
            pl.BlockSpec((None, L, 128), lambda b: (b, 0, KV_CK // 128)),
            pl.BlockSpec((None, L, 128), lambda b: (b, 0, KV_CV // 128)),
            full((2, CMP_LEN, NSA_HD)), full((2, CMP_LEN * NSA_HD, CMP_HID)), full((2, CMP_HID, NSA_HD)),
            full((3, 128)),
        ],
        out_specs=pl.BlockSpec((None, nb, 256), lambda b: (b, 0, 0)),
        out_shape=jax.ShapeDtypeStruct((B, nb, 256), f32),
        compiler_params=_params(("parallel",)),
        name="nsa_cmp_prompt",
    )(kv6_3, kv6_3, lp["cmp_pe"], lp["cmp_w1"], lp["cmp_w2"], lp["k_gain"])


def _nsa_prompt_kernel(rb_ref, q_ref, kv_ref, kcv_ref, sm_ref, y_ref, tb_ref, tc_ref, ex_ref, selx_ref):
    qi = pl.program_id(1)
    L = kv_ref.shape[0]
    tq = 128
    nr = NSA_REP * tq
    ncmp = kcv_ref.shape[0]

    @pl.when(qi == 0)
    def _():
        i_b = _iota((tq, 256), 0)
        j_b = _iota((tq, 256), 1)
        d_band = i_b + 128 - j_b
        i_c = _iota((tq, 128), 0)
        m_c = _iota((tq, 128), 1)
        c_rel = jnp.where(m_c < 64, m_c, m_c - 128)
        d_cmp = i_c - CMP_STRIDE * c_rel - (CMP_LEN - 1)
        for g in range(NSA_KV):
            for r in range(NSA_REP):
                tab = [rb_ref[b, g * NSA_REP + r] for b in range(N_BUCKETS)]
                rows = slice(r * tq, (r + 1) * tq)
                tb_ref[g, rows, 0:128] = jnp.full((tq, 128), tab[N_BUCKETS - 1], f32)
                tb_ref[g, rows, 128:384] = _bias_chain(d_band, tab)
                tc_ref[g, rows, :] = jnp.where(d_cmp < 0, tab[N_BUCKETS - 1], _bias_chain(d_cmp, tab))
        ex_ref[...] = (_iota((128, L), 1) // SLC_BLOCK == _iota((128, L), 0)).astype(bf16)

    q0 = qi * tq
    qpos = q0 + _iota((tq, 1), 0)
    qpos_r = jnp.concatenate([qpos] * NSA_REP, axis=0)
    lane = _iota((1, 128), 1)
    sm = jax.nn.sigmoid(sm_ref[...])

    n_slc = L // SLC_BLOCK
    jl = _iota((tq, 128), 1)
    cur = qpos // SLC_BLOCK
    valid = jl * SLC_BLOCK <= qpos
    forced = (jl == 0) | (jl == cur) | (jl == cur - 1)
    c_i = _iota((ncmp, 128), 0)
    j_i = _iota((ncmp, 128), 1)
    ovl = ((c_i * CMP_STRIDE < j_i * SLC_BLOCK + SLC_BLOCK) & (c_i * CMP_STRIDE + CMP_LEN > j_i * SLC_BLOCK)
           & (j_i < n_slc) & (c_i < ncmp - 1)).astype(f32)

    for g in range(NSA_KV):
        qg = jnp.concatenate([q_ref[:, (g * NSA_REP + r) * NSA_HD:(g * NSA_REP + r + 1) * NSA_HD]
                              for r in range(NSA_REP)], axis=0)
        qg = (qg * (NSA_HD ** -0.5)).astype(bf16)

        kc = kcv_ref[:, g * NSA_HD:(g + 1) * NSA_HD]
        vc = kcv_ref[:, 128 + g * NSA_HD:128 + (g + 1) * NSA_HD]
        lg = _dot_nt(qg, kc) + pltpu.roll(tc_ref[g], (qi * (tq // CMP_STRIDE)) % 128, 1)
        ok = (lane * CMP_STRIDE + (CMP_LEN - 1) <= qpos_r) & (lane < ncmp - 1)
        lg = jnp.where(ok, lg, NEG)
        e = jnp.exp(lg - jnp.max(lg, axis=-1, keepdims=True))
        p = jnp.where(ok, e / jnp.sum(e, axis=-1, keepdims=True), 0.0)
        o_cmp = _dot(p, vc)
        p_sum = p[0:tq] + p[tq:2 * tq] + p[2 * tq:3 * tq] + p[3 * tq:4 * tq]
        imp = _dot(p_sum, ovl, HI)
        score = jnp.where(valid, jnp.where(forced, 1e9, imp), -1e9)
        score = jnp.where(jl < n_slc, score, -3e9)
        rank = jnp.zeros((tq, 128), f32)
        for i in range(n_slc):
            si = score[:, i:i + 1]
            rank += ((si > score) | ((si == score) & (jl > i))).astype(f32)
        sel = ((rank < SLC_TOPN) & (jl < n_slc)).astype(bf16)
        selx_ref[...] = jnp.dot(sel, ex_ref[...], preferred_element_type=f32)

        def attend(kt, carry, k_off, v_off, use_sel):
            m, l, acc = carry
            ko = pl.multiple_of(kt * 128, 128)
            k = kv_ref[pl.ds(ko, 128), k_off + g * NSA_HD:k_off + (g + 1) * NSA_HD]
            v = kv_ref[pl.ds(ko, 128), v_off + g * NSA_HD:v_off + (g + 1) * NSA_HD]
            t = jnp.clip(kt - qi + 2, 0, 2)
            s = _dot_nt(qg, k) + tb_ref[g, :, pl.ds(pl.multiple_of(t * 128, 128), 128)]
            d = qpos_r - (ko + lane)
            if use_sel:
                sx = selx_ref[:, pl.ds(ko, 128)]
                ok = (jnp.concatenate([sx] * NSA_REP, axis=0) > 0.5) & (d >= 0)
            else:
                ok = (d >= 0) & (d <= WINDOW)
            s = jnp.where(ok, s, NEG)
            m_new = jnp.maximum(m, jnp.max(s, axis=-1, keepdims=True))
            alpha = jnp.exp(m - m_new)
            pr = jnp.where(ok, jnp.exp(s - m_new), 0.0)
            l = alpha * l + jnp.sum(pr, axis=-1, keepdims=True)
            acc = alpha * acc + _dot(pr, v)
            return m_new, l, acc

        init = (jnp.full((nr, 1), NEG, f32), jnp.zeros((nr, 1), f32), jnp.zeros((nr, NSA_HD), f32))
        _, l_s, a_s = lax.fori_loop(0, qi + 1, functools.partial(attend, k_off=KV_SK, v_off=KV_SV, use_sel=True), init)
        o_slc = jnp.where(l_s > 0, a_s / jnp.where(l_s > 0, l_s, 1.0), 0.0)
        _, l_w, a_w = lax.fori_loop(jnp.maximum(qi - WINDOW // 128, 0), qi + 1,
                                    functools.partial(attend, k_off=KV_WK, v_off=KV_WV, use_sel=False), init)
        o_win = jnp.where(l_w > 0, a_w / jnp.where(l_w > 0, l_w, 1.0), 0.0)

        outs = []
        for r in range(NSA_REP):
            rows = slice(r * tq, (r + 1) * tq)
            h = g * NSA_REP + r
            gate = lambda t: sm[:, SM_G + t * NSA_HEADS + h:SM_G + t * NSA_HEADS + h + 1]
            outs.append(gate(0) * o_cmp[rows] + gate(1) * o_slc[rows] + gate(2) * o_win[rows])
        y_ref[:, g * 256:(g + 1) * 256] = jnp.concatenate(outs, axis=1)


def _nsa_prompt(qn3, kv6_3, kcv, proj3, rel_bias):
    B, L, _ = qn3.shape
    tq = 128
    return pl.pallas_call(
        _nsa_prompt_kernel,
        grid=(B, L // tq),
        in_specs=[
            pl.BlockSpec(memory_space=pltpu.SMEM),
            pl.BlockSpec((None, tq, D_NSA), lambda b, i: (b, i, 0)),
            pl.BlockSpec((None, L, 768), lambda b, i: (b, 0, 0)),
            pl.BlockSpec((None, L // CMP_STRIDE, 256), lambda b, i: (b, 0, 0)),
            pl.BlockSpec((None, tq, 128), lambda b, i: (b, i, SM_OFF // 128)),
        ],
        out_specs=pl.BlockSpec((None, tq, D_NSA), lambda b, i: (b, i, 0)),
        out_shape=jax.ShapeDtypeStruct((B, L, D_NSA), f32),
        scratch_shapes=[
            pltpu.VMEM((NSA_KV, NSA_REP * tq, 384), f32),
            pltpu.VMEM((NSA_KV, NSA_REP * tq, 128), f32),
            pltpu.VMEM((128, L), bf16),
            pltpu.VMEM((tq, L), f32),
        ],
        compiler_params=_params(("parallel", "arbitrary")),
        name="nsa_prompt",
    )(rel_bias, qn3, kv6_3, kcv, proj3)


def _row0(x, rows=8):
    return jnp.where(_iota((rows, x.shape[1]), 0) == 0, x, 0.0)


def _ssd_decode_kernel(xbc_ref, z_ref, sm_ref, cprev_ref, sin_ref, cw_ref, cb_ref, dtb_ref, alog_ref, dexp_ref,
                       nrm_ref, y_ref, st_ref, cst_ref):
    x = xbc_ref[...]
    conv = cw_ref[3:4, :] * x
    for j in range(CONV_W - 1):
        conv += cw_ref[j:j + 1, :] * cprev_ref[j:j + 1, :]
    cst_ref[0:2, :] = cprev_ref[1:3, :]
    cst_ref[2:3, :] = x
    xc = _silu(conv + cb_ref[...])
    xs = xc[:, :D_SSD]
    dt = _softplus(sm_ref[...] + dtb_ref[...])
    da = dt * (-jnp.exp(alog_ref[...]))
    ys = []
    for g in range(SSD_GROUPS):
        bm = xc[:, D_SSD + g * SSD_STATE:D_SSD + (g + 1) * SSD_STATE]
        cm = xc[:, D_SSD + (SSD_GROUPS + g) * SSD_STATE:D_SSD + (SSD_GROUPS + g + 1) * SSD_STATE]
        cb = jnp.sum(cm * bm, axis=-1, keepdims=True)
        bm8, cm8 = _row0(bm), _row0(cm)
        for r in range(SSD_HEADS // SSD_GROUPS):
            h = g * (SSD_HEADS // SSD_GROUPS) + r
            xdt = xs[:, h * SSD_HD:(h + 1) * SSD_HD] * dt[:, h:h + 1]
            eda = jnp.exp(da[:, h:h + 1])
            h0 = sin_ref[h]
            ys.append(cb * xdt + eda * _dot_nt(cm8, h0, HI)[0:1])
            st_ref[h] = eda * h0 + _dot_tn(_row0(xdt), bm8, HI)
    y = (jnp.concatenate(ys, axis=1) + dexp_ref[...] * xs) * _silu(z_ref[...])
    gw = D_SSD // SSD_GROUPS
    for g in range(SSD_GROUPS):
        yg = y[:, g * gw:(g + 1) * gw]
        yg = yg * lax.rsqrt(jnp.mean(yg * yg, axis=-1, keepdims=True) + EPS)
        y_ref[:, g * gw:(g + 1) * gw] = yg * nrm_ref[:, g * gw:(g + 1) * gw]


def _ssd_decode(proj3, conv_state, state, layer, lp):
    B = proj3.shape[0]
    vec = lambda w: pl.BlockSpec((1, w), lambda b: (0, 0))
    return pl.pallas_call(
        _ssd_decode_kernel,
        grid=(B,),
        in_specs=[
            pl.BlockSpec((None, 1, SSD_CONV_CH), lambda b: (b, 0, XBC_OFF // SSD_CONV_CH)),
            pl.BlockSpec((None, 1, D_SSD), lambda b: (b, 0, ZS_OFF // D_SSD)),
            pl.BlockSpec((None, 1, 128), lambda b: (b, 0, SM_OFF // 128)),
            pl.BlockSpec((None, None, CONV_W - 1, SSD_CONV_CH), lambda b: (layer, b, 0, 0)),
            pl.BlockSpec((None, None, SSD_HEADS, SSD_HD, SSD_STATE), lambda b: (layer, b, 0, 0, 0)),
            pl.BlockSpec((CONV_W, SSD_CONV_CH), lambda b: (0, 0)),
            vec(SSD_CONV_CH), vec(128), vec(128), vec(D_SSD), vec(D_SSD),
        ],
        out_specs=[
            pl.BlockSpec((None, 1, D_SSD), lambda b: (b, 0, 0)),
            pl.BlockSpec((None, SSD_HEADS, SSD_HD, SSD_STATE), lambda b: (b, 0, 0, 0)),
            pl.BlockSpec((None, CONV_W - 1, SSD_CONV_CH), lambda b: (b, 0, 0)),
        ],
        out_shape=[
            jax.ShapeDtypeStruct((B, 1, D_SSD), f32),
            jax.ShapeDtypeStruct((B, SSD_HEADS, SSD_HD, SSD_STATE), f32),
            jax.ShapeDtypeStruct((B, CONV_W - 1, SSD_CONV_CH), f32),
        ],
        compiler_params=_params(("parallel",)),
        name="ssd_decode",
    )(proj3, proj3, proj3, conv_state, state, lp["ssd_conv_w"], lp["ssd_conv_b"], lp["ssd_dtb"], lp["ssd_alog"],
      lp["ssd_dexp"], lp["ssd_norm"])


def _gdn_decode_kernel(qkv_ref, z_ref, sm_ref, cprev_ref, sin_ref, cw_ref, dtb_ref, alog_ref, nrm_ref,
                       y_ref, st_ref, cst_ref):
    x = qkv_ref[...]
    conv = cw_ref[3:4, :] * x
    for j in range(CONV_W - 1):
        conv += cw_ref[j:j + 1, :] * cprev_ref[j:j + 1, :]
    cst_ref[0:2, :] = cprev_ref[1:3, :]
    cst_ref[2:3, :] = x
    xc = _silu(conv)
    sm = sm_ref[...]
    beta_all = jax.nn.sigmoid(sm)
    g_all = -jnp.exp(alog_ref[...]) * _softplus(sm + dtb_ref[...])
    for h in range(GDN_HEADS):
        sl = slice(h * GDN_HD, (h + 1) * GDN_HD)
        qh = xc[:, sl]
        kh = xc[:, D_GDN + h * GDN_HD:D_GDN + (h + 1) * GDN_HD]
        vh = xc[:, 2 * D_GDN + h * GDN_HD:2 * D_GDN + (h + 1) * GDN_HD]
        qh = qh * lax.rsqrt(jnp.sum(qh * qh, axis=-1, keepdims=True) + EPS) * (GDN_HD ** -0.5)
        kh = kh * lax.rsqrt(jnp.sum(kh * kh, axis=-1, keepdims=True) + EPS)
        beta = beta_all[:, SM_B + h:SM_B + h + 1]
        eg = jnp.exp(g_all[:, SM_A + h:SM_A + h + 1])
        s0 = sin_ref[h]
        k8 = _row0(kh)
        v_new = beta * vh - (beta * eg) * _dot(k8, s0, HI)[0:1]
        o = eg * _dot(_row0(qh), s0, HI)[0:1] + jnp.sum(qh * kh, axis=-1, keepdims=True) * v_new
        st_ref[h] = eg * s0 + _dot_tn(k8, _row0(v_new), HI)
        o = o * lax.rsqrt(jnp.mean(o * o, axis=-1, keepdims=True) + EPS)
        y_ref[:, sl] = o * nrm_ref[...] * _silu(z_ref[:, sl])


def _gdn_decode(proj3, conv_state, state, layer, lp):
    B = proj3.shape[0]
    vec = lambda w: pl.BlockSpec((1, w), lambda b: (0, 0))
    return pl.pallas_call(
        _gdn_decode_kernel,
        grid=(B,),
        in_specs=[
            pl.BlockSpec((None, 1, GDN_CONV_CH), lambda b: (b, 0, QKV_OFF // GDN_CONV_CH)),
            pl.BlockSpec((None, 1, D_GDN), lambda b: (b, 0, ZG_OFF // D_GDN)),
            pl.BlockSpec((None, 1, 128), lambda b: (b, 0, SM_OFF // 128)),
            pl.BlockSpec((None, None, CONV_W - 1, GDN_CONV_CH), lambda b: (layer, b, 0, 0)),
            pl.BlockSpec((None, None, GDN_HEADS, GDN_HD, GDN_HD), lambda b: (layer, b, 0, 0, 0)),
            pl.BlockSpec((CONV_W, GDN_CONV_CH), lambda b: (0, 0)),
            vec(128), vec(128), vec(GDN_HD),
        ],
        out_specs=[
            pl.BlockSpec((None, 1, D_GDN), lambda b: (b, 0, 0)),
            pl.BlockSpec((None, GDN_HEADS, GDN_HD, GDN_HD), lambda b: (b, 0, 0, 0)),
            pl.BlockSpec((None, CONV_W - 1, GDN_CONV_CH), lambda b: (b, 0, 0)),
        ],
        out_shape=[
            jax.ShapeDtypeStruct((B, 1, D_GDN), f32),
            jax.ShapeDtypeStruct((B, GDN_HEADS, GDN_HD, GDN_HD), f32),
            jax.ShapeDtypeStruct((B, CONV_W - 1, GDN_CONV_CH), f32),
        ],
        compiler_params=_params(("parallel",)),
        name="gdn_decode",
    )(proj3, proj3, proj3, conv_state, state, lp["gdn_conv_w"], lp["gdn_dtb"], lp["gdn_alog"], lp["gdn_norm"])


def _query_rows(q, g):
    rows = [q[:, (g * NSA_REP + r) * NSA_HD:(g * NSA_REP + r + 1) * NSA_HD] for r in range(NSA_REP)]
    return jnp.concatenate(rows + [jnp.zeros((8 - NSA_REP, NSA_HD), f32)], axis=0) * (NSA_HD ** -0.5)


def _masked_softmax(lg, ok):
    lg = jnp.where(ok, lg, NEG)
    e = jnp.exp(lg - jnp.max(lg, axis=-1, keepdims=True))
    return jnp.where(ok, e / jnp.sum(e, axis=-1, keepdims=True), 0.0)


def _head_lanes(parts):
    return jnp.concatenate([o[r:r + 1, :] for o in parts for r in range(NSA_REP)], axis=1)


def _nsa_dec_cmp_kernel(pt_ref, q_ref, pe_ref, w1_ref, w2_ref, kg_ref, tab_ref, pk_ref, pv_ref,
                        ocmp_ref, idx_ref, kbuf, vbuf, sem, *, layer, n_pages):
    b = pl.program_id(0)
    n_past = n_pages * PAGE
    q_pos = n_past
    n_blk = n_past // CMP_STRIDE
    n_cmp = (n_past + 1 - CMP_LEN) // CMP_STRIDE + 1
    n_slc = -(-(n_past + 1) // SLC_BLOCK)
    nj = -(-n_slc // 128) * 128

    def page_copy(pool, buf, p, s):
        return pltpu.make_async_copy(pool.at[layer, pt_ref[b, p]], buf.at[pl.ds(p * PAGE, PAGE)], sem.at[s])

    def start(p, c):
        page_copy(pk_ref, kbuf, p, 0).start()
        page_copy(pv_ref, vbuf, p, 1).start()
        return c

    def wait(p, c):
        page_copy(pk_ref, kbuf, p, 0).wait()
        page_copy(pv_ref, vbuf, p, 1).wait()
        return c

    lax.fori_loop(0, n_pages, start, 0)
    lax.fori_loop(0, n_pages, wait, 0)

    cmps = []
    for t, buf in enumerate((kbuf, vbuf)):
        p1 = jnp.zeros((n_blk, 2 * CMP_HID), f32)
        p2 = jnp.zeros((n_blk, 2 * CMP_HID), f32)
        for l in range(CMP_STRIDE):
            xr = buf[pl.ds(l, n_blk, stride=CMP_STRIDE), :]
            l2 = CMP_STRIDE + l
            p1 += jnp.dot((xr + pe_ref[t, l:l + 1, :]).astype(bf16), w1_ref[t, l], preferred_element_type=f32)
            p2 += jnp.dot((xr + pe_ref[t, l2:l2 + 1, :]).astype(bf16), w1_ref[t, l2], preferred_element_type=f32)
        hid = p1 + pltpu.roll(p2, n_blk - 1, 0)
        c = jnp.dot(_silu(hid).astype(bf16), w2_ref[t], preferred_element_type=f32)
        if t == 0:
            c = _group_rms(c, NSA_HD) * kg_ref[0:1, :]
        cmps.append(c)
    kc, vc = cmps

    q = q_ref[...]
    lane = _iota((8, n_blk), 1)
    cend = lane * CMP_STRIDE + (CMP_LEN - 1)
    ok = (cend <= q_pos) & (lane < n_cmp)
    c_i = _iota((n_blk, nj), 0)
    j_i = _iota((n_blk, nj), 1)
    ovl = ((c_i * CMP_STRIDE < j_i * SLC_BLOCK + SLC_BLOCK) & (c_i * CMP_STRIDE + CMP_LEN > j_i * SLC_BLOCK)
           & (j_i < n_slc) & (c_i < n_cmp)).astype(f32)
    jl = _iota((1, nj), 1)
    cur = q_pos // SLC_BLOCK
    valid = jl * SLC_BLOCK <= q_pos
    forced = (jl == 0) | (jl == cur) | (jl == cur - 1)
    ii = _iota((nj, nj), 0)
    jj = _iota((nj, nj), 1)
    kk = _iota((SLC_TOPN, nj), 0)
    j16 = _iota((SLC_TOPN, nj), 1)
    o_parts, idx_cols = [], []
    for g in range(NSA_KV):
        tabcols = [tab_ref[g, :, bk:bk + 1] for bk in range(N_BUCKETS)]
        lg = _dot_nt(_query_rows(q, g), kc[:, g * NSA_HD:(g + 1) * NSA_HD]) + _bias_chain(q_pos - cend, tabcols)
        p = _masked_softmax(lg, ok)
        o_parts.append(_dot(p, vc[:, g * NSA_HD:(g + 1) * NSA_HD]))
        p_sum = p[0:1] + p[1:2] + p[2:3] + p[3:4]
        imp = _dot(_row0(p_sum), ovl, HI)[0:1]
        score = jnp.where(valid, jnp.where(forced, 1e9, imp), -1e9)
        score = jnp.where(jl < n_slc, score, -3e9)
        s_c = jnp.sum(jnp.where(ii == jj, score, 0.0), axis=1, keepdims=True)
        rank_r = jnp.sum(((s_c > score) | ((s_c == score) & (ii < jj))).astype(f32), axis=0, keepdims=True)
        rank_c = jnp.sum(((score > s_c) | ((score == s_c) & (jj < ii))).astype(f32), axis=1, keepdims=True)
        sel_r = (rank_r < SLC_TOPN) & (jl < n_slc)
        sel_c = (rank_c < SLC_TOPN) & (ii[:, 0:1] < n_slc)
        pos_r = jnp.sum((sel_c & (ii < jj)).astype(f32), axis=0, keepdims=True)
        hit = (pos_r == kk.astype(f32)) & sel_r
        idx_cols.append(jnp.sum(jnp.where(hit, j16.astype(f32), 0.0), axis=1, keepdims=True))
    ocmp_ref[...] = _head_lanes(o_parts)
    l16 = _iota((SLC_TOPN, 128), 1)
    idx_ref[...] = (jnp.where(l16 == 0, idx_cols[0], 0.0) + jnp.where(l16 == 1, idx_cols[1], 0.0)).astype(i32)


def _nsa_dec_cmp(qn3, page_table, pool_k, pool_v, layer, lp):
    B = qn3.shape[0]
    n_pages = page_table.shape[1]
    full = lambda s: pl.BlockSpec(s, lambda b, pt: (0,) * len(s))
    return pl.pallas_call(
        functools.partial(_nsa_dec_cmp_kernel, layer=layer, n_pages=n_pages),
        grid_spec=pltpu.PrefetchScalarGridSpec(
            num_scalar_prefetch=1,
            grid=(B,),
            in_specs=[
                pl.BlockSpec((None, 1, D_NSA), lambda b, pt: (b, 0, 0)),
                full((2, CMP_LEN, 128)), full((2, CMP_LEN, 128, 2 * CMP_HID)), full((2, 2 * CMP_HID, 128)),
                full((3, 128)), full((NSA_KV, 8, N_BUCKETS)),
                pl.BlockSpec(memory_space=pl.ANY), pl.BlockSpec(memory_space=pl.ANY),
            ],
            out_specs=[pl.BlockSpec((None, 1, D_NSA), lambda b, pt: (b, 0, 0)),
                       pl.BlockSpec((None, SLC_TOPN, 128), lambda b, pt: (b, 0, 0))],
            scratch_shapes=[pltpu.VMEM((n_pages * PAGE, 128), f32), pltpu.VMEM((n_pages * PAGE, 128), f32),
                            pltpu.SemaphoreType.DMA((2,))],
        ),
        out_shape=[jax.ShapeDtypeStruct((B, 1, D_NSA), f32), jax.ShapeDtypeStruct((B, SLC_TOPN, 128), i32)],
        compiler_params=_params(("arbitrary",)),
        name="nsa_dec_cmp",
    )(page_table, qn3, lp["cmp_pe2"], lp["cmp_w1bd"], lp["cmp_w2bd"], lp["k_gain"], lp["tab8"], pool_k, pool_v)


def _nsa_dec_sel_kernel(pt_ref, idx_ref, q_ref, kv_ref, sm_ref, ocmp_ref, tab_ref, wk_ref, wv_ref, pk_ref, pv_ref,
                        y_ref, wko_ref, wvo_ref, kg, vg, kw, vw, sem, *, layer, n_pages):
    b = pl.program_id(0)
    n_past = n_pages * PAGE
    q_pos = n_past
    n_buf = wk_ref.shape[0]
    n_sel = SLC_TOPN * SLC_BLOCK
    new_blk = n_past // SLC_BLOCK
    per_page = PAGE // SLC_BLOCK

    def blk_copy(pool, buf, g, s, sm_i):
        j = jnp.minimum(idx_ref[b, s, g], new_blk - 1)
        src = pool.at[layer, pt_ref[b, j // per_page], pl.ds((j % per_page) * SLC_BLOCK, SLC_BLOCK)]
        return pltpu.make_async_copy(src, buf.at[g, pl.ds(s * SLC_BLOCK, SLC_BLOCK)], sem.at[sm_i])

    for g in range(NSA_KV):
        for s in range(SLC_TOPN):
            blk_copy(pk_ref, kg, g, s, 0).start()
            blk_copy(pv_ref, vg, g, s, 1).start()

    kv = kv_ref[...]
    tail = lambda off, rows: _row0(kv[:, off:off + 128], rows)
    kw[0:n_buf, :] = wk_ref[...]
    vw[0:n_buf, :] = wv_ref[...]
    kw[n_buf:n_buf + 128, :] = tail(KV_WK, 128)
    vw[n_buf:n_buf + 128, :] = tail(KV_WV, 128)
    wko_ref[...] = kw[pl.ds(1, n_buf), :]
    wvo_ref[...] = vw[pl.ds(1, n_buf), :]
    for g in range(NSA_KV):
        kg[g, n_sel:n_sel + 128, :] = tail(KV_SK, 128)
        vg[g, n_sel:n_sel + 128, :] = tail(KV_SV, 128)

    for g in range(NSA_KV):
        for s in range(SLC_TOPN):
            blk_copy(pk_ref, kg, g, s, 0).wait()
            blk_copy(pv_ref, vg, g, s, 1).wait()

    q = q_ref[...]
    gate = jax.nn.sigmoid(sm_ref[...])
    ocmp = ocmp_ref[...]
    lw = _iota((8, n_buf + 128), 1)
    d_w = q_pos - (n_past - n_buf + lw)
    ok_w = (d_w >= 0) & (d_w <= WINDOW) & (lw <= n_buf)
    ls = _iota((8, n_sel + 128), 1)
    slot = ls // SLC_BLOCK
    o_slc, o_win = [], []
    for g in range(NSA_KV):
        tabcols = [tab_ref[g, :, bk:bk + 1] for bk in range(N_BUCKETS)]
        qg = _query_rows(q, g)
        gs = slice(g * NSA_HD, (g + 1) * NSA_HD)
        blk = jnp.zeros(ls.shape, i32)
        n_new = jnp.zeros(ls.shape, i32)
        for s in range(SLC_TOPN):
            j = idx_ref[b, s, g]
            blk = jnp.where(slot == s, j, blk)
            n_new = jnp.where(j == new_blk, n_new + 1, n_new)
        k_pos = jnp.where(ls < n_sel, blk * SLC_BLOCK + ls % SLC_BLOCK, q_pos)
        ok_s = ((ls < n_sel) & (blk < new_blk)) | ((ls == n_sel) & (n_new > 0))
        lg = _dot_nt(qg, kg[g, :, gs]) + _bias_chain(q_pos - k_pos, tabcols)
        o_slc.append(_dot(_masked_softmax(lg, ok_s), vg[g, :, gs]))
        lg = _dot_nt(qg, kw[:, gs]) + _bias_chain(d_w, tabcols)
        o_win.append(_dot(_masked_softmax(lg, ok_w), vw[:, gs]))
    gl = lambda t: gate[:, SM_G + t * NSA_HEADS:SM_G + (t + 1) * NSA_HEADS]
    wide = lambda gt: jnp.concatenate([jnp.broadcast_to(gt[:, h:h + 1], (1, NSA_HD)) for h in range(NSA_HEADS)], axis=1)
    y_ref[...] = wide(gl(0)) * ocmp + wide(gl(1)) * _head_lanes(o_slc) + wide(gl(2)) * _head_lanes(o_win)


def _nsa_dec_sel(qn3, kv6_3, proj3, ocmp, idx, page_table, win_k, win_v, pool_k, pool_v, layer, lp):
    B = qn3.shape[0]
    n_pages = page_table.shape[1]
    n_buf = win_k.shape[2]
    n_sel = SLC_TOPN * SLC_BLOCK
    full = lambda s: pl.BlockSpec(s, lambda b, pt, ix: (0,) * len(s))
    row = lambda w, j: pl.BlockSpec((None, 1, w), lambda b, pt, ix: (b, 0, j))
    win = pl.BlockSpec((None, None, n_buf, 128), lambda b, pt, ix: (layer, b, 0, 0))
    return pl.pallas_call(
        functools.partial(_nsa_dec_sel_kernel, layer=layer, n_pages=n_pages),
        grid_spec=pltpu.PrefetchScalarGridSpec(
            num_scalar_prefetch=2,
            grid=(B,),
            in_specs=[row(D_NSA, 0), row(768, 0), row(128, SM_OFF // 128), row(D_NSA, 0),
                      full((NSA_KV, 8, N_BUCKETS)), win, win,
                      pl.BlockSpec(memory_space=pl.ANY), pl.BlockSpec(memory_space=pl.ANY)],
            out_specs=[row(D_NSA, 0),
                       pl.BlockSpec((None, n_buf, 128), lambda b, pt, ix: (b, 0, 0)),
                       pl.BlockSpec((None, n_buf, 128), lambda b, pt, ix: (b, 0, 0))],
            scratch_shapes=[pltpu.VMEM((NSA_KV, n_sel + 128, 128), f32), pltpu.VMEM((NSA_KV, n_sel + 128, 128), f32),
                            pltpu.VMEM((n_buf + 128, 128), f32), pltpu.VMEM((n_buf + 128, 128), f32),
                            pltpu.SemaphoreType.DMA((2,))],
        ),
        out_shape=[jax.ShapeDtypeStruct((B, 1, D_NSA), f32),
                   jax.ShapeDtypeStruct((B, n_buf, 128), f32), jax.ShapeDtypeStruct((B, n_buf, 128), f32)],
        compiler_params=_params(("arbitrary",)),
        name="nsa_dec_sel",
    )(page_table, idx, qn3, kv6_3, proj3, ocmp, lp["tab8"], win_k, win_v, pool_k, pool_v)


def _layer_params(i, p):
    pad128 = lambda v, off: jnp.zeros((1, 128), f32).at[0, off:off + v.shape[0]].set(v)
    return dict(
        ssd_conv_w=p["ssd_conv_w"][i], ssd_conv_b=p["ssd_conv_b"][i][None],
        ssd_dtb=pad128(p["ssd_dt_bias"][i], SM_DT), ssd_alog=pad128(p["ssd_a_log"][i], SM_DT),
        ssd_dexp=jnp.repeat(p["ssd_d"][i], SSD_HD)[None], ssd_norm=p["ssd_norm"][i][None],
        gdn_conv_w=p["gdn_conv_w"][i],
        gdn_dtb=pad128(p["gdn_dt_bias"][i], SM_A), gdn_alog=pad128(p["gdn_a_log"][i], SM_A),
        gdn_norm=p["gdn_norm"][i][None],
        q_gain=jnp.tile(p["nsa_q_norm"][i], 2)[None], k_gain=jnp.tile(p["nsa_k_norm"][i], (1, 2)),
        cmp_pe=p["nsa_cmp_pe"][i], cmp_w1=p["nsa_cmp_w1"][i], cmp_w2=p["nsa_cmp_w2"][i],
        cmp_pe2=jnp.tile(p["nsa_cmp_pe"][i], (1, 1, 2)),
        cmp_w1bd=_block_diag2(p["nsa_cmp_w1"][i].reshape(2, CMP_LEN, NSA_HD, CMP_HID)).astype(bf16),
        cmp_w2bd=_block_diag2(p["nsa_cmp_w2"][i]).astype(bf16),
        tab8=jnp.pad(p["rel_bias"].T.reshape(NSA_KV, NSA_REP, N_BUCKETS), ((0, 0), (0, 8 - NSA_REP), (0, 0))),
    )


def _block_diag2(w):
    z = jnp.zeros_like(w)
    return jnp.concatenate([jnp.concatenate([w, z], axis=-1), jnp.concatenate([z, w], axis=-1)], axis=-2)


def _reorder_w_in(w_in):
    cuts = [(1024, 2560), (2576, 4112), (0, 1024), (4112, 4624), (4632, 5144), (5144, 5912),
            (2560, 2576), (4624, 4632), (5912, 5936)]
    parts = [w_in[:, :, a:b] for a, b in cuts]
    used = sum(b - a for a, b in cuts)
    parts.append(jnp.zeros(w_in.shape[:2] + (PROJ_W - used,), w_in.dtype))
    return jnp.concatenate(parts, axis=-1).astype(bf16)


def _mix_prompt(x3, i, p, lp, w_r):
    B, L, _ = x3.shape
    n = B * L
    x = x3.reshape(n, D_MODEL)
    proj = _inproj(x, p["norm_mix"], w_r, i)
    proj3 = proj.reshape(B, L, PROJ_W)
    y_ssd, st_ssd, cst_ssd = _ssd_prompt(proj3, lp)
    y_gdn, st_gdn, cst_gdn = _gdn_prompt(proj3, lp)
    qn, kv6 = _prep(proj, lp["q_gain"], lp["k_gain"])
    kv6_3 = kv6.reshape(B, L, 768)
    kcv = _cmp_prompt(kv6_3, lp)
    y_nsa = _nsa_prompt(qn.reshape(B, L, D_NSA), kv6_3, kcv, proj3, p["rel_bias"])
    x = _outproj(x, y_ssd.reshape(n, D_SSD), y_gdn.reshape(n, D_GDN), y_nsa.reshape(n, D_NSA), p["w_out"], i)
    rows = lambda off: kv6_3[:, :, off:off + 128].reshape(B, L, NSA_KV, NSA_HD)
    keep = min(WINDOW, L)
    state = (st_ssd, cst_ssd, st_gdn, cst_gdn, rows(KV_CK), rows(KV_CV), rows(KV_SK), rows(KV_SV),
             rows(KV_WK)[:, L - keep:], rows(KV_WV)[:, L - keep:])
    return x.reshape(B, L, D_MODEL), state


def _mix_decode(x, i, p, lp, w_r, st):
    B = x.shape[0]
    proj = _inproj(x, p["norm_mix"], w_r, i)
    proj3 = proj.reshape(B, 1, PROJ_W)
    y_ssd, st_ssd, cst_ssd = _ssd_decode(proj3, st["ssd_conv"], st["ssd"], i, lp)
    y_gdn, st_gdn, cst_gdn = _gdn_decode(proj3, st["gdn_conv"], st["gdn"], i, lp)
    qn, kv6 = _prep(proj, lp["q_gain"], lp["k_gain"])
    qn3, kv6_3 = qn.reshape(B, 1, D_NSA), kv6.reshape(B, 1, 768)
    o_cmp, idx = _nsa_dec_cmp(qn3, st["page_table"], st["cmp_k"], st["cmp_v"], i, lp)
    y_nsa, win_k, win_v = _nsa_dec_sel(qn3, kv6_3, proj3, o_cmp, idx[:, :, :NSA_KV], st["page_table"],
                                       st["win_k"], st["win_v"], st["slc_k"], st["slc_v"], i, lp)
    x = _outproj(x, y_ssd.reshape(B, D_SSD), y_gdn.reshape(B, D_GDN), y_nsa.reshape(B, D_NSA), p["w_out"], i)
    rows = lambda off: kv6[:, off:off + 128].reshape(B, 1, NSA_KV, NSA_HD)
    n_buf = win_k.shape[1]
    state = (st_ssd, cst_ssd, st_gdn, cst_gdn, rows(KV_CK), rows(KV_CV), rows(KV_SK), rows(KV_SV),
             win_k.reshape(B, n_buf, NSA_KV, NSA_HD), win_v.reshape(B, n_buf, NSA_KV, NSA_HD))
    return x, state


def kernel(x_prompt, x_sample, state_ssd, state_ssd_conv, state_gdn, state_gdn_conv, cache_cmp_k, cache_cmp_v,
           cache_slc_k, cache_slc_v, cache_win_k, cache_win_v, page_table, rel_bias, norm_ffn1, w_ffn1_gate,
           w_ffn1_up, w_ffn1_down, norm_mix, w_in, ssd_conv_w, ssd_conv_b, ssd_dt_bias, ssd_a_log, ssd_d, ssd_norm,
           gdn_conv_w, gdn_dt_bias, gdn_a_log, gdn_norm, nsa_q_norm, nsa_k_norm, nsa_cmp_pe, nsa_cmp_w1,
           nsa_cmp_w2, w_out, norm_ffn2, w_ffn2_gate, w_ffn2_up, w_ffn2_down):
    bp, lp_len, _ = x_prompt.shape
    bs = x_sample.shape[0]
    gain3 = lambda g: g.reshape(DEPTH, 1, D_MODEL)
    p = dict(rel_bias=rel_bias, norm_mix=gain3(norm_mix), w_out=w_out, ssd_conv_w=ssd_conv_w, ssd_conv_b=ssd_conv_b,
             ssd_dt_bias=ssd_dt_bias, ssd_a_log=ssd_a_log, ssd_d=ssd_d, ssd_norm=ssd_norm, gdn_conv_w=gdn_conv_w,
             gdn_dt_bias=gdn_dt_bias, gdn_a_log=gdn_a_log, gdn_norm=gdn_norm, nsa_q_norm=nsa_q_norm,
             nsa_k_norm=nsa_k_norm, nsa_cmp_pe=nsa_cmp_pe, nsa_cmp_w1=nsa_cmp_w1, nsa_cmp_w2=nsa_cmp_w2)
    n1, n2 = gain3(norm_ffn1), gain3(norm_ffn2)
    pool = lambda c: c.reshape(c.shape[0], c.shape[1], PAGE, NSA_KV * NSA_HD)
    st = dict(ssd=state_ssd, ssd_conv=state_ssd_conv, gdn=state_gdn, gdn_conv=state_gdn_conv,
              cmp_k=pool(cache_cmp_k), cmp_v=pool(cache_cmp_v), slc_k=pool(cache_slc_k), slc_v=pool(cache_slc_v),
              win_k=cache_win_k.reshape(cache_win_k.shape[:3] + (NSA_KV * NSA_HD,)),
              win_v=cache_win_v.reshape(cache_win_v.shape[:3] + (NSA_KV * NSA_HD,)), page_table=page_table)
    w_r = _reorder_w_in(w_in)

    hp = x_prompt.reshape(bp * lp_len, D_MODEL)
    hs = x_sample.reshape(bs, D_MODEL)
    outs_p, outs_s = [], []
    for i in range(DEPTH):
        lp = _layer_params(i, p)
        hp = _ffn(hp, n1, w_ffn1_gate, w_ffn1_up, w_ffn1_down, i)
        hs = _ffn(hs, n1, w_ffn1_gate, w_ffn1_up, w_ffn1_down, i)
        hp3, st_p = _mix_prompt(hp.reshape(bp, lp_len, D_MODEL), i, p, lp, w_r)
        hs, st_s = _mix_decode(hs, i, p, lp, w_r, st)
        hp = _ffn(hp3.reshape(bp * lp_len, D_MODEL), n2, w_ffn2_gate, w_ffn2_up, w_ffn2_down, i)
        hs = _ffn(hs, n2, w_ffn2_gate, w_ffn2_up, w_ffn2_down, i)
        outs_p.append(st_p)
        outs_s.append(st_s)
    stack = lambda outs: [jnp.stack(t) for t in zip(*outs)]
    return (hp.reshape(bp, lp_len, D_MODEL), hs.reshape(bs, 1, D_MODEL), *stack(outs_p), *stack(outs_s))
```

```python
import functools
import math

import numpy as np
import jax
import jax.numpy as jnp
from jax import lax
from jax.experimental import pallas as pl
from jax.experimental.pallas import tpu as pltpu

f32 = jnp.float32
bf16 = jnp.bfloat16
i32 = jnp.int32
HI = lax.Precision.HIGHEST

D_MODEL = 2048
DEPTH = 4
PAGE = 128
D_SSD = 1024
SSD_HD = 64
SSD_HEADS = 16
SSD_GROUPS = 2
SSD_STATE = 128
SSD_CHUNK = 128
D_GDN = 512
GDN_HD = 128
GDN_HEADS = 4
GDN_CHUNK = 64
GDN_STEP_CHUNKS = 2
D_NSA = 512
NSA_HD = 64
NSA_HEADS = 8
NSA_KV = 2
NSA_REP = 4
CMP_STRIDE = 16
CMP_LEN = 32
CMP_HID = 128
SLC_BLOCK = 64
SLC_TOPN = 16
WINDOW = 512
N_BUCKETS = 32
MAX_DISTANCE = 128
CONV_W = 4
D_FF = 5632
SSD_CONV_CH = 1536
GDN_CONV_CH = 1536
EPS = 1e-6
NEG = -1e30

XBC_OFF = 0
QKV_OFF = 1536
ZS_OFF = 3072
ZG_OFF = 4096
QN_OFF = 4608
KV_OFF = 5120
SM_OFF = 5888
PROJ_W = 6144
SM_DT, SM_A, SM_B, SM_G = 0, 16, 20, 24
KV_CK, KV_CV, KV_SK, KV_SV, KV_WK, KV_WV = 0, 128, 256, 384, 512, 640

VMEM_LIMIT = 56 * 1024 * 1024


def _bucket_thresholds():
    exact = N_BUCKETS // 2
    d = np.arange(0, 4 * MAX_DISTANCE)
    nf = np.maximum(d, 1).astype(np.float32)
    large = exact + (np.log(nf / np.float32(exact)) / np.float32(math.log(MAX_DISTANCE / exact))
                     * np.float32(N_BUCKETS - exact)).astype(np.int32)
    bucket = np.where(d < exact, d, np.minimum(large, N_BUCKETS - 1))
    return [int(np.argmax(bucket >= b)) for b in range(N_BUCKETS)]


_THR = _bucket_thresholds()
BIAS_FAR = _THR[N_BUCKETS - 1]


def _bias_chain(d, tabcols):
    v = jnp.broadcast_to(tabcols[N_BUCKETS - 1], d.shape)
    for b in range(N_BUCKETS - 2, -1, -1):
        v = jnp.where(d < _THR[b + 1], tabcols[b], v)
    return v


def _silu(x):
    return x * jax.nn.sigmoid(x)


def _softplus(x):
    return jnp.maximum(x, 0.0) + jnp.log1p(jnp.exp(-jnp.abs(x)))


def _dot(a, b, prec=None):
    if prec is None:
        a, b = a.astype(bf16), b.astype(bf16)
    return jnp.dot(a, b, preferred_element_type=f32, precision=prec)


def _dot_nt(a, b, prec=None):
    if prec is None:
        a, b = a.astype(bf16), b.astype(bf16)
    return lax.dot_general(a, b, (((1,), (1,)), ((), ())), preferred_element_type=f32, precision=prec)


def _dot_tn(a, b, prec=None):
    if prec is None:
        a, b = a.astype(bf16), b.astype(bf16)
    return lax.dot_general(a, b, (((0,), (0,)), ((), ())), preferred_element_type=f32, precision=prec)


def _split2(a):
    hi = a.astype(bf16)
    return hi, (a - hi.astype(f32)).astype(bf16)


def _dot3(a2, b2):
    d = lambda x, y: jnp.dot(x, y, preferred_element_type=f32)
    return d(a2[0], b2[0]) + (d(a2[0], b2[1]) + d(a2[1], b2[0]))


def _iota(shape, dim):
    return lax.broadcasted_iota(i32, shape, dim)


def _group_ones(n, width):
    return (_iota((n, n), 0) // width == _iota((n, n), 1) // width).astype(f32)


def _group_rms(x, width):
    ss = _dot(x * x, _group_ones(x.shape[1], width), HI)
    return x * lax.rsqrt(ss * (1.0 / width) + EPS)


def _params(sem):
    return pltpu.CompilerParams(dimension_semantics=sem, vmem_limit_bytes=VMEM_LIMIT)


def _ffn_kernel(x_ref, g_ref, wg_ref, wu_ref, wd_ref, o_ref, h_ref):
    @pl.when(pl.program_id(1) == 0)
    def _():
        x = x_ref[...]
        h = x * lax.rsqrt(jnp.mean(x * x, axis=-1, keepdims=True) + EPS) * g_ref[...]
        h_ref[...] = h.astype(bf16)
        o_ref[...] = x

    h = h_ref[...]
    a = jnp.dot(h, wg_ref[...].astype(bf16), preferred_element_type=f32)
    u = jnp.dot(h, wu_ref[...].astype(bf16), preferred_element_type=f32)
    o_ref[...] += 0.5 * _dot(_silu(a) * u, wd_ref[...])


def _ffn(x, gain, wg, wu, wd, layer):
    n = x.shape[0]
    tm = min(n, 1024)
    tf = 256 if tm == 1024 else 512
    return pl.pallas_call(
        _ffn_kernel,
        grid=(n // tm, D_FF // tf),
        in_specs=[
            pl.BlockSpec((tm, D_MODEL), lambda i, j: (i, 0)),
            pl.BlockSpec((None, 1, D_MODEL), lambda i, j: (layer, 0, 0)),
            pl.BlockSpec((None, D_MODEL, tf), lambda i, j: (layer, 0, j)),
            pl.BlockSpec((None, D_MODEL, tf), lambda i, j: (layer, 0, j)),
            pl.BlockSpec((None, tf, D_MODEL), lambda i, j: (layer, j, 0)),
        ],
        out_specs=pl.BlockSpec((tm, D_MODEL), lambda i, j: (i, 0)),
        out_shape=jax.ShapeDtypeStruct((n, D_MODEL), f32),
        scratch_shapes=[pltpu.VMEM((tm, D_MODEL), bf16)],
        compiler_params=_params(("parallel", "arbitrary")),
        name="ffn",
    )(x, gain, wg, wu, wd)


def _inproj_kernel(x_ref, g_ref, w_ref, o_ref, h_ref):
    @pl.when(pl.program_id(1) == 0)
    def _():
        x = x_ref[...]
        h = x * lax.rsqrt(jnp.mean(x * x, axis=-1, keepdims=True) + EPS) * g_ref[...]
        h_ref[...] = h.astype(bf16)

    o_ref[...] = jnp.dot(h_ref[...], w_ref[...], preferred_element_type=f32)


def _inproj(x, gain, w_r, layer):
    n = x.shape[0]
    tm = min(n, 1024)
    tn = 512
    return pl.pallas_call(
        _inproj_kernel,
        grid=(n // tm, PROJ_W // tn),
        in_specs=[
            pl.BlockSpec((tm, D_MODEL), lambda i, j: (i, 0)),
            pl.BlockSpec((None, 1, D_MODEL), lambda i, j: (layer, 0, 0)),
            pl.BlockSpec((None, D_MODEL, tn), lambda i, j: (layer, 0, j)),
        ],
        out_specs=pl.BlockSpec((tm, tn), lambda i, j: (i, j)),
        out_shape=jax.ShapeDtypeStruct((n, PROJ_W), f32),
        scratch_shapes=[pltpu.VMEM((tm, D_MODEL), bf16)],
        compiler_params=_params(("parallel", "arbitrary")),
        name="inproj",
    )(x, gain, w_r)


def _outproj_kernel(x_ref, ys_ref, yg_ref, yn_ref, ws_ref, wg_ref, wn_ref, o_ref):
    acc = _dot(ys_ref[...], ws_ref[...])
    acc += _dot(yg_ref[...], wg_ref[...])
    acc += _dot(yn_ref[...], wn_ref[...])
    o_ref[...] = x_ref[...] + acc


def _outproj(x, ys, yg, yn, w_out, layer):
    n = x.shape[0]
    tm = min(n, 1024)
    tn = 512
    return pl.pallas_call(
        _outproj_kernel,
        grid=(n // tm, D_MODEL // tn),
        in_specs=[
            pl.BlockSpec((tm, tn), lambda i, j: (i, j)),
            pl.BlockSpec((tm, D_SSD), lambda i, j: (i, 0)),
            pl.BlockSpec((tm, D_GDN), lambda i, j: (i, 0)),
            pl.BlockSpec((tm, D_NSA), lambda i, j: (i, 0)),
            pl.BlockSpec((None, D_SSD, tn), lambda i, j: (layer, 0, j)),
            pl.BlockSpec((None, D_GDN, tn), lambda i, j: (layer, 2, j)),
            pl.BlockSpec((None, D_NSA, tn), lambda i, j: (layer, 3, j)),
        ],
        out_specs=pl.BlockSpec((tm, tn), lambda i, j: (i, j)),
        out_shape=jax.ShapeDtypeStruct((n, D_MODEL), f32),
        compiler_params=_params(("parallel", "arbitrary")),
        name="outproj",
    )(x, ys, yg, yn, w_out, w_out, w_out)


def _prep_kernel(q_ref, kv_ref, qg_ref, kg_ref, qn_ref, kv6_ref):
    for s in range(D_NSA // 128):
        sl = slice(s * 128, (s + 1) * 128)
        qn_ref[:, sl] = _group_rms(q_ref[:, sl], NSA_HD) * qg_ref[...]
    kv = kv_ref[...]
    kv6_ref[...] = kv[:, :6 * 128]
    kv6_ref[:, KV_SK:KV_SK + 128] = _group_rms(kv[:, KV_SK:KV_SK + 128], NSA_HD) * kg_ref[1:2, :]
    kv6_ref[:, KV_WK:KV_WK + 128] = _group_rms(kv[:, KV_WK:KV_WK + 128], NSA_HD) * kg_ref[2:3, :]


def _prep(proj, qgain, kgain):
    n = proj.shape[0]
    tm = min(n, 512)
    return pl.pallas_call(
        _prep_kernel,
        grid=(n // tm,),
        in_specs=[
            pl.BlockSpec((tm, D_NSA), lambda i: (i, QN_OFF // D_NSA)),
            pl.BlockSpec((tm, 1024), lambda i: (i, KV_OFF // 1024)),
            pl.BlockSpec((1, 128), lambda i: (0, 0)),
            pl.BlockSpec((3, 128), lambda i: (0, 0)),
        ],
        out_specs=[pl.BlockSpec((tm, D_NSA), lambda i: (i, 0)),
                   pl.BlockSpec((tm, 768), lambda i: (i, 0))],
        out_shape=[jax.ShapeDtypeStruct((n, D_NSA), f32), jax.ShapeDtypeStruct((n, 768), f32)],
        compiler_params=_params(("parallel",)),
        name="nsa_prep",
    )(proj, proj, qgain, kgain)


def _ssd_prompt_kernel(xbc_ref, z_ref, sm_ref, cw_ref, cb_ref, dtb_ref, alog_ref, dexp_ref, nrm_ref,
                       y_ref, st_ref, cst_ref, xbuf, hst, ybuf):
    c = pl.program_id(1)
    cl = SSD_CHUNK

    @pl.when(c == 0)
    def _():
        xbuf[0:8, :] = jnp.zeros((8, SSD_CONV_CH), f32)
        hst[...] = jnp.zeros(hst.shape, f32)

    x = xbc_ref[...]
    xbuf[8:8 + cl, :] = x
    conv = cw_ref[3:4, :] * x
    for k in range(1, CONV_W):
        conv += cw_ref[3 - k:4 - k, :] * xbuf[pl.ds(8 - k, cl), :]
    xbuf[0:8, :] = x[cl - 8:cl, :]
    xc = _silu(conv + cb_ref[...])
    xs = xc[:, :D_SSD]

    dt = _softplus(sm_ref[...] + dtb_ref[...])
    da = dt * (-jnp.exp(alog_ref[...]))
    row, col = _iota((cl, cl), 0), _iota((cl, cl), 1)
    causal = row >= col
    cum = _dot(causal.astype(f32), da, HI)
    cum_t = cum.T

    for g in range(SSD_GROUPS):
        bm = xc[:, D_SSD + g * SSD_STATE:D_SSD + (g + 1) * SSD_STATE]
        cm = xc[:, D_SSD + (SSD_GROUPS + g) * SSD_STATE:D_SSD + (SSD_GROUPS + g + 1) * SSD_STATE]
        cb = _dot_nt(cm, bm)
        for r in range(SSD_HEADS // SSD_GROUPS):
            h = g * (SSD_HEADS // SSD_GROUPS) + r
            cum_c = cum[:, h:h + 1]
            seg = cum_c - cum_t[h:h + 1, :]
            decay = jnp.where(causal, jnp.exp(jnp.where(causal, seg, 0.0)), 0.0)
            xdt = xs[:, h * SSD_HD:(h + 1) * SSD_HD] * dt[:, h:h + 1]
            last = cum[cl - 1:cl, h:h + 1]
            h0 = hst[h]
            y = _dot(cb * decay, xdt) + jnp.exp(cum_c) * _dot_nt(cm, h0)
            hst[h] = jnp.exp(last) * h0 + _dot_tn(xdt * jnp.exp(last - cum_c), bm)
            ybuf[:, h * SSD_HD:(h + 1) * SSD_HD] = y

    y = (ybuf[...] + dexp_ref[...] * xs) * _silu(z_ref[...])
    gw = D_SSD // SSD_GROUPS
    for g in range(SSD_GROUPS):
        yg = y[:, g * gw:(g + 1) * gw]
        yg = yg * lax.rsqrt(jnp.mean(yg * yg, axis=-1, keepdims=True) + EPS)
        y_ref[:, g * gw:(g + 1) * gw] = yg * nrm_ref[:, g * gw:(g + 1) * gw]

    @pl.when(c == pl.num_programs(1) - 1)
    def _():
        st_ref[...] = hst[...]
        cst_ref[...] = x[cl - (CONV_W - 1):cl, :]


def _ssd_prompt(proj3, lp):
    B, L, _ = proj3.shape
    cl = SSD_CHUNK
    vec = lambda w: pl.BlockSpec((1, w), lambda b, c: (0, 0))
    return pl.pallas_call(
        _ssd_prompt_kernel,
        grid=(B, L // cl),
        in_specs=[
            pl.BlockSpec((None, cl, SSD_CONV_CH), lambda b, c: (b, c, XBC_OFF // SSD_CONV_CH)),
            pl.BlockSpec((None, cl, D_SSD), lambda b, c: (b, c, ZS_OFF // D_SSD)),
            pl.BlockSpec((None, cl, 128), lambda b, c: (b, c, SM_OFF // 128)),
            pl.BlockSpec((CONV_W, SSD_CONV_CH), lambda b, c: (0, 0)),
            vec(SSD_CONV_CH), vec(128), vec(128), vec(D_SSD), vec(D_SSD),
        ],
        out_specs=[
            pl.BlockSpec((None, cl, D_SSD), lambda b, c: (b, c, 0)),
            pl.BlockSpec((None, SSD_HEADS, SSD_HD, SSD_STATE), lambda b, c: (b, 0, 0, 0)),
            pl.BlockSpec((None, CONV_W - 1, SSD_CONV_CH), lambda b, c: (b, 0, 0)),
        ],
        out_shape=[
            jax.ShapeDtypeStruct((B, L, D_SSD), f32),
            jax.ShapeDtypeStruct((B, SSD_HEADS, SSD_HD, SSD_STATE), f32),
            jax.ShapeDtypeStruct((B, CONV_W - 1, SSD_CONV_CH), f32),
        ],
        scratch_shapes=[
            pltpu.VMEM((8 + cl, SSD_CONV_CH), f32),
            pltpu.VMEM((SSD_HEADS, SSD_HD, SSD_STATE), f32),
            pltpu.VMEM((cl, D_SSD), f32),
        ],
        compiler_params=_params(("parallel", "arbitrary")),
        name="ssd_prompt",
    )(proj3, proj3, proj3, lp["ssd_conv_w"], lp["ssd_conv_b"], lp["ssd_dtb"], lp["ssd_alog"],
      lp["ssd_dexp"], lp["ssd_norm"])


def _gdn_prompt_kernel(qkv_ref, z_ref, sm_ref, cw_ref, dtb_ref, alog_ref, nrm_ref,
                       y_ref, st_ref, cst_ref, xbuf, sst):
    c = pl.program_id(1)
    cl = GDN_CHUNK
    nch = GDN_STEP_CHUNKS
    tl = nch * cl

    @pl.when(c == 0)
    def _():
        xbuf[0:8, :] = jnp.zeros((8, GDN_CONV_CH), f32)
        sst[...] = jnp.zeros(sst.shape, f32)

    x = qkv_ref[...]
    xbuf[8:8 + tl, :] = x
    conv = cw_ref[3:4, :] * x
    for k in range(1, CONV_W):
        conv += cw_ref[3 - k:4 - k, :] * xbuf[pl.ds(8 - k, tl), :]
    xbuf[0:8, :] = x[tl - 8:tl, :]
    xc = _silu(conv)

    sm = sm_ref[...]
    beta_all = jax.nn.sigmoid(sm)
    g_all = -jnp.exp(alog_ref[...]) * _softplus(sm + dtb_ref[...])
    row, col = _iota((cl, cl), 0), _iota((cl, cl), 1)
    incl = row >= col
    strict = row > col
    eye = (row == col).astype(f32)
    rt, ct = _iota((tl, tl), 0), _iota((tl, tl), 1)
    cum = _dot(((rt >= ct) & (rt // cl == ct // cl)).astype(f32), g_all, HI)
    cum_t = cum.T

    units = [(ci, h) for ci in range(nch) for h in range(GDN_HEADS)]
    q, k, v, beta, cum_c, decay = {}, {}, {}, {}, {}, {}
    for u in units:
        ci, h = u
        rows = slice(ci * cl, (ci + 1) * cl)
        qh = xc[rows, h * GDN_HD:(h + 1) * GDN_HD]
        kh = xc[rows, D_GDN + h * GDN_HD:D_GDN + (h + 1) * GDN_HD]
        v[u] = xc[rows, 2 * D_GDN + h * GDN_HD:2 * D_GDN + (h + 1) * GDN_HD]
        q[u] = qh * lax.rsqrt(jnp.sum(qh * qh, axis=-1, keepdims=True) + EPS) * (GDN_HD ** -0.5)
        k[u] = kh * lax.rsqrt(jnp.sum(kh * kh, axis=-1, keepdims=True) + EPS)
        beta[u] = beta_all[rows, SM_B + h:SM_B + h + 1]
        cum_c[u] = cum[rows, SM_A + h:SM_A + h + 1]
        seg = cum_c[u] - cum_t[SM_A + h:SM_A + h + 1, rows]
        decay[u] = jnp.where(incl, jnp.exp(jnp.where(incl, seg, 0.0)), 0.0)
    pw = {u: -jnp.where(strict, beta[u] * _dot_nt(k[u], k[u]) * decay[u], 0.0) for u in units}
    inv = {u: eye + pw[u] for u in units}
    pw2 = {u: _split2(pw[u]) for u in units}
    for _ in range(5):
        pw2 = {u: _split2(_dot3(pw2[u], pw2[u])) for u in units}
        inv = {u: inv[u] + _dot3(_split2(inv[u]), pw2[u]) for u in units}
    rhs = {u: jnp.concatenate([beta[u] * v[u], (beta[u] * jnp.exp(cum_c[u])) * k[u]], axis=1) for u in units}
    sol = {u: _dot3(_split2(inv[u]), _split2(rhs[u])) for u in units}
    uu = {u: sol[u][:, :GDN_HD] for u in units}
    ww = {u: sol[u][:, GDN_HD:] for u in units}
    qk = {u: _dot_nt(q[u], k[u]) * decay[u] for u in units}

    s = [sst[h] for h in range(GDN_HEADS)]
    for ci in range(nch):
        rows = slice(ci * cl, (ci + 1) * cl)
        hs = [(ci, h) for h in range(GDN_HEADS)]
        v_new = {u: uu[u] - _dot(ww[u], s[u[1]]) for u in hs}
        o = {u: _dot(q[u] * jnp.exp(cum_c[u]), s[u[1]]) + _dot(qk[u], v_new[u]) for u in hs}
        for u in hs:
            h = u[1]
            last = cum[ci * cl + cl - 1:ci * cl + cl, SM_A + h:SM_A + h + 1]
            s[h] = jnp.exp(last) * s[h] + _dot_tn(k[u] * jnp.exp(last - cum_c[u]), v_new[u])
        for u in hs:
            sl = slice(u[1] * GDN_HD, (u[1] + 1) * GDN_HD)
            on = o[u] * lax.rsqrt(jnp.mean(o[u] * o[u], axis=-1, keepdims=True) + EPS)
            y_ref[rows, sl] = on * nrm_ref[...] * _silu(z_ref[rows, sl])
    for h in range(GDN_HEADS):
        sst[h] = s[h]

    @pl.when(c == pl.num_programs(1) - 1)
    def _():
        st_ref[...] = sst[...]
        cst_ref[...] = x[tl - (CONV_W - 1):tl, :]


def _gdn_prompt(proj3, lp):
    B, L, _ = proj3.shape
    cl = GDN_CHUNK * GDN_STEP_CHUNKS
    vec = lambda w: pl.BlockSpec((1, w), lambda b, c: (0, 0))
    return pl.pallas_call(
        _gdn_prompt_kernel,
        grid=(B, L // cl),
        in_specs=[
            pl.BlockSpec((None, cl, GDN_CONV_CH), lambda b, c: (b, c, QKV_OFF // GDN_CONV_CH)),
            pl.BlockSpec((None, cl, D_GDN), lambda b, c: (b, c, ZG_OFF // D_GDN)),
            pl.BlockSpec((None, cl, 128), lambda b, c: (b, c, SM_OFF // 128)),
            pl.BlockSpec((CONV_W, GDN_CONV_CH), lambda b, c: (0, 0)),
            vec(128), vec(128), vec(GDN_HD),
        ],
        out_specs=[
            pl.BlockSpec((None, cl, D_GDN), lambda b, c: (b, c, 0)),
            pl.BlockSpec((None, GDN_HEADS, GDN_HD, GDN_HD), lambda b, c: (b, 0, 0, 0)),
            pl.BlockSpec((None, CONV_W - 1, GDN_CONV_CH), lambda b, c: (b, 0, 0)),
        ],
        out_shape=[
            jax.ShapeDtypeStruct((B, L, D_GDN), f32),
            jax.ShapeDtypeStruct((B, GDN_HEADS, GDN_HD, GDN_HD), f32),
            jax.ShapeDtypeStruct((B, CONV_W - 1, GDN_CONV_CH), f32),
        ],
        scratch_shapes=[
            pltpu.VMEM((8 + cl, GDN_CONV_CH), f32),
            pltpu.VMEM((GDN_HEADS, GDN_HD, GDN_HD), f32),
        ],
        compiler_params=_params(("parallel", "arbitrary")),
        name="gdn_prompt",
    )(proj3, proj3, proj3, lp["gdn_conv_w"], lp["gdn_dtb"], lp["gdn_alog"], lp["gdn_norm"])


def _cmp_prompt_kernel(k_ref, v_ref, pe_ref, w1_ref, w2_ref, kg_ref, kc_ref, vct_ref):
    nb = kc_ref.shape[0]
    for t, src in enumerate((k_ref, v_ref)):
        outs = []
        p1 = [jnp.zeros((nb, CMP_HID), f32) for _ in range(NSA_KV)]
        p2 = [jnp.zeros((nb, CMP_HID), f32) for _ in range(NSA_KV)]
        for l in range(CMP_STRIDE):
            xr = src[pl.ds(l, nb, stride=CMP_STRIDE), :]
            l2 = CMP_STRIDE + l
            for g in range(NSA_KV):
                xg = xr[:, g * NSA_HD:(g + 1) * NSA_HD]
                p1[g] += _dot(xg + pe_ref[t, l:l + 1, :], w1_ref[t, l * NSA_HD:(l + 1) * NSA_HD, :])
                p2[g] += _dot(xg + pe_ref[t, l2:l2 + 1, :], w1_ref[t, l2 * NSA_HD:(l2 + 1) * NSA_HD, :])
        for g in range(NSA_KV):
            hid = p1[g] + pltpu.roll(p2[g], nb - 1, 0)
            cmp = _dot(_silu(hid), w2_ref[t])
            if t == 0:
                cmp = cmp * lax.rsqrt(jnp.mean(cmp * cmp, axis=-1, keepdims=True) + EPS) * kg_ref[0:1, 0:NSA_HD]
            outs.append(cmp)
        both = jnp.concatenate(outs, axis=1)
        if t == 0:
            kc_ref[...] = both
        else:
            vct_ref[...] = both.T


def _cmp_prompt(kv6_3, lp):
    B, L, _ = kv6_3.shape
    nb = L // CMP_STRIDE
    full = lambda s: pl.BlockSpec(s, lambda b: (0,) * len(s))
    return pl.pallas_call(
        _cmp_prompt_kernel,
        grid=(B,),
        in_specs=[
            pl.BlockSpec((None, L, 128), lambda b: (b, 0, KV_CK // 128)),
            pl.BlockSpec((None, L, 128), lambda b: (b, 0, KV_CV // 128)),
            full((2, CMP_LEN, NSA_HD)), full((2, CMP_LEN * NSA_HD, CMP_HID)), full((2, CMP_HID, NSA_HD)),
            full((3, 128)),
        ],
        out_specs=[pl.BlockSpec((None, nb, 128), lambda b: (b, 0, 0)),
                   pl.BlockSpec((None, 128, nb), lambda b: (b, 0, 0))],
        out_shape=[jax.ShapeDtypeStruct((B, nb, 128), f32), jax.ShapeDtypeStruct((B, 128, nb), f32)],
        compiler_params=_params(("parallel",)),
        name="nsa_cmp_prompt",
    )(kv6_3, kv6_3, lp["cmp_pe"], lp["cmp_w1"], lp["cmp_w2"], lp["k_gain"])


def _nsa_prompt_kernel(rb_ref, qt_ref, ks_ref, kw_ref, vst_ref, vwt_ref, kc_ref, vct_ref, gt_ref,
                       y_ref, tb_ref, tc_ref, ext_ref, selx_ref):
    qi = pl.program_id(1)
    L = ks_ref.shape[0]
    tq = 128
    nl = NSA_REP * tq
    ncmp = kc_ref.shape[0]
    n_slc = L // SLC_BLOCK
    last = N_BUCKETS - 1

    @pl.when(qi == 0)
    def _():
        sub = _iota((128, tq), 0)
        qo = _iota((128, tq), 1)
        d_diag = qo - sub
        c_rel = jnp.where(sub < 64, sub, sub - 128)
        d_cmp = qo - CMP_STRIDE * c_rel - (CMP_LEN - 1)
        for g in range(NSA_KV):
            for r in range(NSA_REP):
                h = g * NSA_REP + r
                tab = [rb_ref[b, h] - rb_ref[last, h] for b in range(N_BUCKETS)]
                lanes = slice(r * tq, (r + 1) * tq)
                tb_ref[g, 0, :, lanes] = jnp.zeros((128, tq), f32)
                tb_ref[g, 1, :, lanes] = _bias_chain(d_diag + 128, tab)
                tb_ref[g, 2, :, lanes] = _bias_chain(d_diag, tab)
                t_c = jnp.where(d_cmp < 0, 0.0, _bias_chain(d_cmp, tab))
                tc_ref[g, 0:128, lanes] = t_c
                tc_ref[g, 128:256, lanes] = t_c
        ext_ref[...] = (_iota((L, 128), 0) // SLC_BLOCK == _iota((L, 128), 1)).astype(bf16)

    q0 = qi * tq
    qpos = q0 + _iota((1, nl), 1) % tq
    qpos1 = q0 + _iota((1, tq), 1)
    sub = _iota((128, 1), 0)
    gt = gt_ref[...]

    jr = _iota((n_slc, tq), 0)
    cur = qpos1 // SLC_BLOCK
    valid = jr * SLC_BLOCK <= qpos1
    forced = (jr == 0) | (jr == cur) | (jr == cur - 1)
    j_i = _iota((n_slc, ncmp), 0)
    c_i = _iota((n_slc, ncmp), 1)
    ovl = ((c_i * CMP_STRIDE < j_i * SLC_BLOCK + SLC_BLOCK) & (c_i * CMP_STRIDE + CMP_LEN > j_i * SLC_BLOCK)
           & (c_i < ncmp - 1)).astype(f32)
    cmp_off = pl.multiple_of((128 - qi * (tq // CMP_STRIDE)) % 128, 8)

    y_parts = []
    for g in range(NSA_KV):
        gs = slice(g * NSA_HD, (g + 1) * NSA_HD)
        qt = jnp.concatenate([qt_ref[(g * NSA_REP + r) * NSA_HD:(g * NSA_REP + r + 1) * NSA_HD, :]
                              for r in range(NSA_REP)], axis=1).astype(bf16)

        st = _dot(kc_ref[:, gs], qt) + tc_ref[g, pl.ds(cmp_off, 128), :]
        ok = (sub * CMP_STRIDE + (CMP_LEN - 1) <= qpos) & (sub < ncmp - 1)
        st = jnp.where(ok, st, NEG)
        e = jnp.exp(st - jnp.max(st, axis=0, keepdims=True))
        p = jnp.where(ok, e / jnp.sum(e, axis=0, keepdims=True), 0.0)
        o_cmp = _dot(vct_ref[gs, :], p)
        p_sum = p[:, 0:tq] + p[:, tq:2 * tq] + p[:, 2 * tq:3 * tq] + p[:, 3 * tq:4 * tq]
        imp = _dot(ovl, p_sum, HI)
        score = jnp.where(valid, jnp.where(forced, 1e9, imp), -1e9)
        rank = jnp.zeros((n_slc, tq), f32)
        for i in range(n_slc):
            si = score[i:i + 1, :]
            rank += ((si > score) | ((si == score) & (jr > i))).astype(f32)
        sel = (rank < SLC_TOPN).astype(bf16)
        sel = jnp.concatenate([sel, jnp.zeros((128 - n_slc, tq), bf16)], axis=0)
        selx_ref[...] = jnp.dot(ext_ref[...], sel, preferred_element_type=f32)

        def tile(kt, carry, k_ref, vt_ref, table, selected, diagonal):
            m, l, acc = carry
            ko = pl.multiple_of(kt * 128, 128)
            s = _dot(k_ref[pl.ds(ko, 128), gs], qt) + tb_ref[g, table]
            kpos = ko + sub
            if selected:
                sx = selx_ref[pl.ds(ko, 128), :]
                ok = jnp.concatenate([sx] * NSA_REP, axis=1) > 0.5
                if diagonal:
                    ok = ok & (kpos <= qpos)
            elif diagonal:
                ok = kpos <= qpos
            else:
                ok = qpos - kpos <= WINDOW
            s = jnp.where(ok, s, NEG)
            m_new = jnp.maximum(m, jnp.max(s, axis=0, keepdims=True))
            alpha = jnp.exp(m - m_new)
            pr = jnp.exp(s - m_new)
            l = alpha * l + jnp.sum(pr, axis=0, keepdims=True)
            acc = alpha * acc + _dot(vt_ref[gs, pl.ds(ko, 128)], pr)
            return m_new, l, acc

        def branch(k_ref, vt_ref, lo, selected):
            body = lambda kt, c: tile(kt, c, k_ref, vt_ref, jnp.where(kt == qi - 1, 1, 0), selected, False)
            init = (jnp.full((1, nl), NEG, f32), jnp.zeros((1, nl), f32), jnp.zeros((NSA_HD, nl), f32))
            _, l, acc = tile(qi, lax.fori_loop(lo, qi, body, init), k_ref, vt_ref, 2, selected, True)
            return acc / l

        o_slc = branch(ks_ref, vst_ref, 0, True)
        o_win = branch(kw_ref, vwt_ref, jnp.maximum(qi - WINDOW // 128, 0), False)

        for r in range(NSA_REP):
            lanes = slice(r * tq, (r + 1) * tq)
            h = g * NSA_REP + r
            gate = lambda t: gt[SM_G + t * NSA_HEADS + h:SM_G + t * NSA_HEADS + h + 1, :]
            y_parts.append(gate(0) * o_cmp[:, lanes] + gate(1) * o_slc[:, lanes] + gate(2) * o_win[:, lanes])
    y_ref[...] = jnp.concatenate(y_parts, axis=0).T


def _nsa_prompt(qt, kv6_3, kv6t, kc, vct, gt, rel_bias):
    B, L, _ = kv6_3.shape
    tq = 128
    nb = L // CMP_STRIDE
    return pl.pallas_call(
        _nsa_prompt_kernel,
        grid=(B, L // tq),
        in_specs=[
            pl.BlockSpec(memory_space=pltpu.SMEM),
            pl.BlockSpec((None, D_NSA, tq), lambda b, i: (b, 0, i)),
            pl.BlockSpec((None, L, 128), lambda b, i: (b, 0, KV_SK // 128)),
            pl.BlockSpec((None, L, 128), lambda b, i: (b, 0, KV_WK // 128)),
            pl.BlockSpec((None, 128, L), lambda b, i: (b, KV_SV // 128, 0)),
            pl.BlockSpec((None, 128, L), lambda b, i: (b, KV_WV // 128, 0)),
            pl.BlockSpec((None, nb, 128), lambda b, i: (b, 0, 0)),
            pl.BlockSpec((None, 128, nb), lambda b, i: (b, 0, 0)),
            pl.BlockSpec((None, 128, tq), lambda b, i: (b, 0, i)),
        ],
        out_specs=pl.BlockSpec((None, tq, D_NSA), lambda b, i: (b, i, 0)),
        out_shape=jax.ShapeDtypeStruct((B, L, D_NSA), f32),
        scratch_shapes=[
            pltpu.VMEM((NSA_KV, 3, 128, NSA_REP * tq), f32),
            pltpu.VMEM((NSA_KV, 256, NSA_REP * tq), f32),
            pltpu.VMEM((L, 128), bf16),
            pltpu.VMEM((L, tq), f32),
        ],
        compiler_params=_params(("parallel", "arbitrary")),
        name="nsa_prompt",
    )(rel_bias, qt, kv6_3, kv6_3, kv6t, kv6t, kc, vct, gt)


def _prep_t_kernel(q_ref, kv_ref, qg_ref, kg_ref, qt_ref, kv6_ref, kv6t_ref, gt_ref):
    for s in range(D_NSA // 128):
        sl = slice(s * 128, (s + 1) * 128)
        qt_ref[sl, :] = (_group_rms(q_ref[:, sl], NSA_HD) * qg_ref[...] * (NSA_HD ** -0.5)).T
    kv = kv_ref[...]
    for j, off in enumerate((KV_CK, KV_CV, KV_SK, KV_SV, KV_WK, KV_WV)):
        x = kv[:, off:off + 128]
        if off == KV_SK:
            x = _group_rms(x, NSA_HD) * kg_ref[1:2, :]
        if off == KV_WK:
            x = _group_rms(x, NSA_HD) * kg_ref[2:3, :]
        kv6_ref[:, off:off + 128] = x
        kv6t_ref[off:off + 128, :] = x.T
    gt_ref[...] = jax.nn.sigmoid(kv[:, SM_OFF - KV_OFF:SM_OFF - KV_OFF + 128]).T


def _prep_t(proj3, qgain, kgain):
    B, L, _ = proj3.shape
    tm = min(L, 512)
    return pl.pallas_call(
        _prep_t_kernel,
        grid=(B, L // tm),
        in_specs=[
            pl.BlockSpec((None, tm, D_NSA), lambda b, i: (b, i, QN_OFF // D_NSA)),
            pl.BlockSpec((None, tm, 1024), lambda b, i: (b, i, KV_OFF // 1024)),
            pl.BlockSpec((1, 128), lambda b, i: (0, 0)),
            pl.BlockSpec((3, 128), lambda b, i: (0, 0)),
        ],
        out_specs=[pl.BlockSpec((None, D_NSA, tm), lambda b, i: (b, 0, i)),
                   pl.BlockSpec((None, tm, 768), lambda b, i: (b, i, 0)),
                   pl.BlockSpec((None, 768, tm), lambda b, i: (b, 0, i)),
                   pl.BlockSpec((None, 128, tm), lambda b, i: (b, 0, i))],
        out_shape=[jax.ShapeDtypeStruct((B, D_NSA, L), f32), jax.ShapeDtypeStruct((B, L, 768), f32),
                   jax.ShapeDtypeStruct((B, 768, L), f32), jax.ShapeDtypeStruct((B, 128, L), f32)],
        compiler_params=_params(("parallel", "parallel")),
        name="nsa_prep_t",
    )(proj3, proj3, qgain, kgain)


def _row0(x, rows=8):
    return jnp.where(_iota((rows, x.shape[1]), 0) == 0, x, 0.0)


def _ssd_decode_kernel(xbc_ref, z_ref, sm_ref, cprev_ref, sin_ref, cw_ref, cb_ref, dtb_ref, alog_ref, dexp_ref,
                       nrm_ref, y_ref, st_ref, cst_ref):
    x = xbc_ref[...]
    conv = cw_ref[3:4, :] * x
    for j in range(CONV_W - 1):
        conv += cw_ref[j:j + 1, :] * cprev_ref[j:j + 1, :]
    cst_ref[0:2, :] = cprev_ref[1:3, :]
    cst_ref[2:3, :] = x
    xc = _silu(conv + cb_ref[...])
    xs = xc[:, :D_SSD]
    dt = _softplus(sm_ref[...] + dtb_ref[...])
    da = dt * (-jnp.exp(alog_ref[...]))
    ys = []
    for g in range(SSD_GROUPS):
        bm = xc[:, D_SSD + g * SSD_STATE:D_SSD + (g + 1) * SSD_STATE]
        cm = xc[:, D_SSD + (SSD_GROUPS + g) * SSD_STATE:D_SSD + (SSD_GROUPS + g + 1) * SSD_STATE]
        cb = jnp.sum(cm * bm, axis=-1, keepdims=True)
        bm8, cm8 = _row0(bm), _row0(cm)
        for r in range(SSD_HEADS // SSD_GROUPS):
            h = g * (SSD_HEADS // SSD_GROUPS) + r
            xdt = xs[:, h * SSD_HD:(h + 1) * SSD_HD] * dt[:, h:h + 1]
            eda = jnp.exp(da[:, h:h + 1])
            h0 = sin_ref[h]
            ys.append(cb * xdt + eda * _dot_nt(cm8, h0, HI)[0:1])
            st_ref[h] = eda * h0 + _dot_tn(_row0(xdt), bm8, HI)
    y = (jnp.concatenate(ys, axis=1) + dexp_ref[...] * xs) * _silu(z_ref[...])
    gw = D_SSD // SSD_GROUPS
    for g in range(SSD_GROUPS):
        yg = y[:, g * gw:(g + 1) * gw]
        yg = yg * lax.rsqrt(jnp.mean(yg * yg, axis=-1, keepdims=True) + EPS)
        y_ref[:, g * gw:(g + 1) * gw] = yg * nrm_ref[:, g * gw:(g + 1) * gw]


def _ssd_decode(proj3, conv_state, state, layer, lp):
    B = proj3.shape[0]
    vec = lambda w: pl.BlockSpec((1, w), lambda b: (0, 0))
    return pl.pallas_call(
        _ssd_decode_kernel,
        grid=(B,),
        in_specs=[
            pl.BlockSpec((None, 1, SSD_CONV_CH), lambda b: (b, 0, XBC_OFF // SSD_CONV_CH)),
            pl.BlockSpec((None, 1, D_SSD), lambda b: (b, 0, ZS_OFF // D_SSD)),
            pl.BlockSpec((None, 1, 128), lambda b: (b, 0, SM_OFF // 128)),
            pl.BlockSpec((None, None, CONV_W - 1, SSD_CONV_CH), lambda b: (layer, b, 0, 0)),
            pl.BlockSpec((None, None, SSD_HEADS, SSD_HD, SSD_STATE), lambda b: (layer, b, 0, 0, 0)),
            pl.BlockSpec((CONV_W, SSD_CONV_CH), lambda b: (0, 0)),
            vec(SSD_CONV_CH), vec(128), vec(128), vec(D_SSD), vec(D_SSD),
        ],
        out_specs=[
            pl.BlockSpec((None, 1, D_SSD), lambda b: (b, 0, 0)),
            pl.BlockSpec((None, SSD_HEADS, SSD_HD, SSD_STATE), lambda b: (b, 0, 0, 0)),
            pl.BlockSpec((None, CONV_W - 1, SSD_CONV_CH), lambda b: (b, 0, 0)),
        ],
        out_shape=[
            jax.ShapeDtypeStruct((B, 1, D_SSD), f32),
            jax.ShapeDtypeStruct((B, SSD_HEADS, SSD_HD, SSD_STATE), f32),
            jax.ShapeDtypeStruct((B, CONV_W - 1, SSD_CONV_CH), f32),
        ],
        compiler_params=_params(("parallel",)),
        name="ssd_decode",
    )(proj3, proj3, proj3, conv_state, state, lp["ssd_conv_w"], lp["ssd_conv_b"], lp["ssd_dtb"], lp["ssd_alog"],
      lp["ssd_dexp"], lp["ssd_norm"])


def _gdn_decode_kernel(qkv_ref, z_ref, sm_ref, cprev_ref, sin_ref, cw_ref, dtb_ref, alog_ref, nrm_ref,
                       y_ref, st_ref, cst_ref):
    x = qkv_ref[...]
    conv = cw_ref[3:4, :] * x
    for j in range(CONV_W - 1):
        conv += cw_ref[j:j + 1, :] * cprev_ref[j:j + 1, :]
    cst_ref[0:2, :] = cprev_ref[1:3, :]
    cst_ref[2:3, :] = x
    xc = _silu(conv)
    sm = sm_ref[...]
    beta_all = jax.nn.sigmoid(sm)
    g_all = -jnp.exp(alog_ref[...]) * _softplus(sm + dtb_ref[...])
    for h in range(GDN_HEADS):
        sl = slice(h * GDN_HD, (h + 1) * GDN_HD)
        qh = xc[:, sl]
        kh = xc[:, D_GDN + h * GDN_HD:D_GDN + (h + 1) * GDN_HD]
        vh = xc[:, 2 * D_GDN + h * GDN_HD:2 * D_GDN + (h + 1) * GDN_HD]
        qh = qh * lax.rsqrt(jnp.sum(qh * qh, axis=-1, keepdims=True) + EPS) * (GDN_HD ** -0.5)
        kh = kh * lax.rsqrt(jnp.sum(kh * kh, axis=-1, keepdims=True) + EPS)
        beta = beta_all[:, SM_B + h:SM_B + h + 1]
        eg = jnp.exp(g_all[:, SM_A + h:SM_A + h + 1])
        s0 = sin_ref[h]
        k8 = _row0(kh)
        v_new = beta * vh - (beta * eg) * _dot(k8, s0, HI)[0:1]
        o = eg * _dot(_row0(qh), s0, HI)[0:1] + jnp.sum(qh * kh, axis=-1, keepdims=True) * v_new
        st_ref[h] = eg * s0 + _dot_tn(k8, _row0(v_new), HI)
        o = o * lax.rsqrt(jnp.mean(o * o, axis=-1, keepdims=True) + EPS)
        y_ref[:, sl] = o * nrm_ref[...] * _silu(z_ref[:, sl])


def _gdn_decode(proj3, conv_state, state, layer, lp):
    B = proj3.shape[0]
    vec = lambda w: pl.BlockSpec((1, w), lambda b: (0, 0))
    return pl.pallas_call(
        _gdn_decode_kernel,
        grid=(B,),
        in_specs=[
            pl.BlockSpec((None, 1, GDN_CONV_CH), lambda b: (b, 0, QKV_OFF // GDN_CONV_CH)),
            pl.BlockSpec((None, 1, D_GDN), lambda b: (b, 0, ZG_OFF // D_GDN)),
            pl.BlockSpec((None, 1, 128), lambda b: (b, 0, SM_OFF // 128)),
            pl.BlockSpec((None, None, CONV_W - 1, GDN_CONV_CH), lambda b: (layer, b, 0, 0)),
            pl.BlockSpec((None, None, GDN_HEADS, GDN_HD, GDN_HD), lambda b: (layer, b, 0, 0, 0)),
            pl.BlockSpec((CONV_W, GDN_CONV_CH), lambda b: (0, 0)),
            vec(128), vec(128), vec(GDN_HD),
        ],
        out_specs=[
            pl.BlockSpec((None, 1, D_GDN), lambda b: (b, 0, 0)),
            pl.BlockSpec((None, GDN_HEADS, GDN_HD, GDN_HD), lambda b: (b, 0, 0, 0)),
            pl.BlockSpec((None, CONV_W - 1, GDN_CONV_CH), lambda b: (b, 0, 0)),
        ],
        out_shape=[
            jax.ShapeDtypeStruct((B, 1, D_GDN), f32),
            jax.ShapeDtypeStruct((B, GDN_HEADS, GDN_HD, GDN_HD), f32),
            jax.ShapeDtypeStruct((B, CONV_W - 1, GDN_CONV_CH), f32),
        ],
        compiler_params=_params(("parallel",)),
        name="gdn_decode",
    )(proj3, proj3, proj3, conv_state, state, lp["gdn_conv_w"], lp["gdn_dtb"], lp["gdn_alog"], lp["gdn_norm"])


def _query_rows(q, g):
    rows = [q[:, (g * NSA_REP + r) * NSA_HD:(g * NSA_REP + r + 1) * NSA_HD] for r in range(NSA_REP)]
    return jnp.concatenate(rows + [jnp.zeros((8 - NSA_REP, NSA_HD), f32)], axis=0) * (NSA_HD ** -0.5)


def _masked_softmax(lg, ok):
    lg = jnp.where(ok, lg, NEG)
    e = jnp.exp(lg - jnp.max(lg, axis=-1, keepdims=True))
    return jnp.where(ok, e / jnp.sum(e, axis=-1, keepdims=True), 0.0)


def _head_lanes(parts):
    return jnp.concatenate([o[r:r + 1, :] for o in parts for r in range(NSA_REP)], axis=1)


def _nsa_dec_cmp_kernel(pt_ref, q_ref, pe_ref, w1_ref, w2_ref, kg_ref, tab_ref, pk_ref, pv_ref,
                        ocmp_ref, idx_ref, kbuf, vbuf, sem, *, layer, n_pages):
    b = pl.program_id(0)
    n_past = n_pages * PAGE
    q_pos = n_past
    n_blk = n_past // CMP_STRIDE
    n_cmp = (n_past + 1 - CMP_LEN) // CMP_STRIDE + 1
    n_slc = -(-(n_past + 1) // SLC_BLOCK)
    nj = -(-n_slc // 128) * 128

    def page_copy(pool, buf, p, s):
        return pltpu.make_async_copy(pool.at[layer, pt_ref[b, p]], buf.at[pl.ds(p * PAGE, PAGE)], sem.at[s])

    def start(p, c):
        page_copy(pk_ref, kbuf, p, 0).start()
        page_copy(pv_ref, vbuf, p, 1).start()
        return c

    def wait(p, c):
        page_copy(pk_ref, kbuf, p, 0).wait()
        page_copy(pv_ref, vbuf, p, 1).wait()
        return c

    lax.fori_loop(0, n_pages, start, 0)
    lax.fori_loop(0, n_pages, wait, 0)

    cmps = []
    for t, buf in enumerate((kbuf, vbuf)):
        p1 = jnp.zeros((n_blk, 2 * CMP_HID), f32)
        p2 = jnp.zeros((n_blk, 2 * CMP_HID), f32)
        for l in range(CMP_STRIDE):
            xr = buf[pl.ds(l, n_blk, stride=CMP_STRIDE), :]
            l2 = CMP_STRIDE + l
            p1 += jnp.dot((xr + pe_ref[t, l:l + 1, :]).astype(bf16), w1_ref[t, l], preferred_element_type=f32)
            p2 += jnp.dot((xr + pe_ref[t, l2:l2 + 1, :]).astype(bf16), w1_ref[t, l2], preferred_element_type=f32)
        hid = p1 + pltpu.roll(p2, n_blk - 1, 0)
        c = jnp.dot(_silu(hid).astype(bf16), w2_ref[t], preferred_element_type=f32)
        if t == 0:
            c = _group_rms(c, NSA_HD) * kg_ref[0:1, :]
        cmps.append(c)
    kc, vc = cmps

    q = q_ref[...]
    lane = _iota((8, n_blk), 1)
    cend = lane * CMP_STRIDE + (CMP_LEN - 1)
    ok = (cend <= q_pos) & (lane < n_cmp)
    c_i = _iota((n_blk, nj), 0)
    j_i = _iota((n_blk, nj), 1)
    ovl = ((c_i * CMP_STRIDE < j_i * SLC_BLOCK + SLC_BLOCK) & (c_i * CMP_STRIDE + CMP_LEN > j_i * SLC_BLOCK)
           & (j_i < n_slc) & (c_i < n_cmp)).astype(f32)
    jl = _iota((1, nj), 1)
    cur = q_pos // SLC_BLOCK
    valid = jl * SLC_BLOCK <= q_pos
    forced = (jl == 0) | (jl == cur) | (jl == cur - 1)
    ii = _iota((nj, nj), 0)
    jj = _iota((nj, nj), 1)
    kk = _iota((SLC_TOPN, nj), 0)
    j16 = _iota((SLC_TOPN, nj), 1)
    o_parts, idx_cols = [], []
    for g in range(NSA_KV):
        tabcols = [tab_ref[g, :, bk:bk + 1] for bk in range(N_BUCKETS)]
        lg = _dot_nt(_query_rows(q, g), kc[:, g * NSA_HD:(g + 1) * NSA_HD]) + _bias_chain(q_pos - cend, tabcols)
        p = _masked_softmax(lg, ok)
        o_parts.append(_dot(p, vc[:, g * NSA_HD:(g + 1) * NSA_HD]))
        p_sum = p[0:1] + p[1:2] + p[2:3] + p[3:4]
        imp = _dot(_row0(p_sum), ovl, HI)[0:1]
        score = jnp.where(valid, jnp.where(forced, 1e9, imp), -1e9)
        score = jnp.where(jl < n_slc, score, -3e9)
        s_c = jnp.sum(jnp.where(ii == jj, score, 0.0), axis=1, keepdims=True)
        rank_r = jnp.sum(((s_c > score) | ((s_c == score) & (ii < jj))).astype(f32), axis=0, keepdims=True)
        rank_c = jnp.sum(((score > s_c) | ((score == s_c) & (jj < ii))).astype(f32), axis=1, keepdims=True)
        sel_r = (rank_r < SLC_TOPN) & (jl < n_slc)
        sel_c = (rank_c < SLC_TOPN) & (ii[:, 0:1] < n_slc)
        pos_r = jnp.sum((sel_c & (ii < jj)).astype(f32), axis=0, keepdims=True)
        hit = (pos_r == kk.astype(f32)) & sel_r
        idx_cols.append(jnp.sum(jnp.where(hit, j16.astype(f32), 0.0), axis=1, keepdims=True))
    ocmp_ref[...] = _head_lanes(o_parts)
    l16 = _iota((SLC_TOPN, 128), 1)
    idx_ref[...] = (jnp.where(l16 == 0, idx_cols[0], 0.0) + jnp.where(l16 == 1, idx_cols[1], 0.0)).astype(i32)


def _nsa_dec_cmp(qn3, page_table, pool_k, pool_v, layer, lp):
    B = qn3.shape[0]
    n_pages = page_table.shape[1]
    full = lambda s: pl.BlockSpec(s, lambda b, pt: (0,) * len(s))
    return pl.pallas_call(
        functools.partial(_nsa_dec_cmp_kernel, layer=layer, n_pages=n_pages),
        grid_spec=pltpu.PrefetchScalarGridSpec(
            num_scalar_prefetch=1,
            grid=(B,),
            in_specs=[
                pl.BlockSpec((None, 1, D_NSA), lambda b, pt: (b, 0, 0)),
                full((2, CMP_LEN, 128)), full((2, CMP_LEN, 128, 2 * CMP_HID)), full((2, 2 * CMP_HID, 128)),
                full((3, 128)), full((NSA_KV, 8, N_BUCKETS)),
                pl.BlockSpec(memory_space=pl.ANY), pl.BlockSpec(memory_space=pl.ANY),
            ],
            out_specs=[pl.BlockSpec((None, 1, D_NSA), lambda b, pt: (b, 0, 0)),
                       pl.BlockSpec((None, SLC_TOPN, 128), lambda b, pt: (b, 0, 0))],
            scratch_shapes=[pltpu.VMEM((n_pages * PAGE, 128), f32), pltpu.VMEM((n_pages * PAGE, 128), f32),
                            pltpu.SemaphoreType.DMA((2,))],
        ),
        out_shape=[jax.ShapeDtypeStruct((B, 1, D_NSA), f32), jax.ShapeDtypeStruct((B, SLC_TOPN, 128), i32)],
        compiler_params=_params(("arbitrary",)),
        name="nsa_dec_cmp",
    )(page_table, qn3, lp["cmp_pe2"], lp["cmp_w1bd"], lp["cmp_w2bd"], lp["k_gain"], lp["tab8"], pool_k, pool_v)


def _nsa_dec_sel_kernel(pt_ref, idx_ref, q_ref, kv_ref, sm_ref, ocmp_ref, tab_ref, wk_ref, wv_ref, pk_ref, pv_ref,
                        y_ref, wko_ref, wvo_ref, kg, vg, kw, vw, sem, *, layer, n_pages):
    b = pl.program_id(0)
    n_past = n_pages * PAGE
    q_pos = n_past
    n_buf = wk_ref.shape[0]
    n_sel = SLC_TOPN * SLC_BLOCK
    new_blk = n_past // SLC_BLOCK
    per_page = PAGE // SLC_BLOCK

    def blk_copy(pool, buf, g, s, sm_i):
        j = jnp.minimum(idx_ref[b, s, g], new_blk - 1)
        src = pool.at[layer, pt_ref[b, j // per_page], pl.ds((j % per_page) * SLC_BLOCK, SLC_BLOCK)]
        return pltpu.make_async_copy(src, buf.at[g, pl.ds(s * SLC_BLOCK, SLC_BLOCK)], sem.at[sm_i])

    for g in range(NSA_KV):
        for s in range(SLC_TOPN):
            blk_copy(pk_ref, kg, g, s, 0).start()
            blk_copy(pv_ref, vg, g, s, 1).start()

    kv = kv_ref[...]
    tail = lambda off, rows: _row0(kv[:, off:off + 128], rows)
    kw[0:n_buf, :] = wk_ref[...]
    vw[0:n_buf, :] = wv_ref[...]
    kw[n_buf:n_buf + 128, :] = tail(KV_WK, 128)
    vw[n_buf:n_buf + 128, :] = tail(KV_WV, 128)
    wko_ref[...] = kw[pl.ds(1, n_buf), :]
    wvo_ref[...] = vw[pl.ds(1, n_buf), :]
    for g in range(NSA_KV):
        kg[g, n_sel:n_sel + 128, :] = tail(KV_SK, 128)
        vg[g, n_sel:n_sel + 128, :] = tail(KV_SV, 128)

    for g in range(NSA_KV):
        for s in range(SLC_TOPN):
            blk_copy(pk_ref, kg, g, s, 0).wait()
            blk_copy(pv_ref, vg, g, s, 1).wait()

    q = q_ref[...]
    gate = jax.nn.sigmoid(sm_ref[...])
    ocmp = ocmp_ref[...]
    lw = _iota((8, n_buf + 128), 1)
    d_w = q_pos - (n_past - n_buf + lw)
    ok_w = (d_w >= 0) & (d_w <= WINDOW) & (lw <= n_buf)
    ls = _iota((8, n_sel + 128), 1)
    slot = ls // SLC_BLOCK
    o_slc, o_win = [], []
    for g in range(NSA_KV):
        tabcols = [tab_ref[g, :, bk:bk + 1] for bk in range(N_BUCKETS)]
        qg = _query_rows(q, g)
        gs = slice(g * NSA_HD, (g + 1) * NSA_HD)
        blk = jnp.zeros(ls.shape, i32)
        n_new = jnp.zeros(ls.shape, i32)
        for s in range(SLC_TOPN):
            j = idx_ref[b, s, g]
            blk = jnp.where(slot == s, j, blk)
            n_new = jnp.where(j == new_blk, n_new + 1, n_new)
        k_pos = jnp.where(ls < n_sel, blk * SLC_BLOCK + ls % SLC_BLOCK, q_pos)
        ok_s = ((ls < n_sel) & (blk < new_blk)) | ((ls == n_sel) & (n_new > 0))
        lg = _dot_nt(qg, kg[g, :, gs]) + _bias_chain(q_pos - k_pos, tabcols)
        o_slc.append(_dot(_masked_softmax(lg, ok_s), vg[g, :, gs]))
        lg = _dot_nt(qg, kw[:, gs]) + _bias_chain(d_w, tabcols)
        o_win.append(_dot(_masked_softmax(lg, ok_w), vw[:, gs]))
    gl = lambda t: gate[:, SM_G + t * NSA_HEADS:SM_G + (t + 1) * NSA_HEADS]
    wide = lambda gt: jnp.concatenate([jnp.broadcast_to(gt[:, h:h + 1], (1, NSA_HD)) for h in range(NSA_HEADS)], axis=1)
    y_ref[...] = wide(gl(0)) * ocmp + wide(gl(1)) * _head_lanes(o_slc) + wide(gl(2)) * _head_lanes(o_win)


def _nsa_dec_sel(qn3, kv6_3, proj3, ocmp, idx, page_table, win_k, win_v, pool_k, pool_v, layer, lp):
    B = qn3.shape[0]
    n_pages = page_table.shape[1]
    n_buf = win_k.shape[2]
    n_sel = SLC_TOPN * SLC_BLOCK
    full = lambda s: pl.BlockSpec(s, lambda b, pt, ix: (0,) * len(s))
    row = lambda w, j: pl.BlockSpec((None, 1, w), lambda b, pt, ix: (b, 0, j))
    win = pl.BlockSpec((None, None, n_buf, 128), lambda b, pt, ix: (layer, b, 0, 0))
    return pl.pallas_call(
        functools.partial(_nsa_dec_sel_kernel, layer=layer, n_pages=n_pages),
        grid_spec=pltpu.PrefetchScalarGridSpec(
            num_scalar_prefetch=2,
            grid=(B,),
            in_specs=[row(D_NSA, 0), row(768, 0), row(128, SM_OFF // 128), row(D_NSA, 0),
                      full((NSA_KV, 8, N_BUCKETS)), win, win,
                      pl.BlockSpec(memory_space=pl.ANY), pl.BlockSpec(memory_space=pl.ANY)],
            out_specs=[row(D_NSA, 0),
                       pl.BlockSpec((None, n_buf, 128), lambda b, pt, ix: (b, 0, 0)),
                       pl.BlockSpec((None, n_buf, 128), lambda b, pt, ix: (b, 0, 0))],
            scratch_shapes=[pltpu.VMEM((NSA_KV, n_sel + 128, 128), f32), pltpu.VMEM((NSA_KV, n_sel + 128, 128), f32),
                            pltpu.VMEM((n_buf + 128, 128), f32), pltpu.VMEM((n_buf + 128, 128), f32),
                            pltpu.SemaphoreType.DMA((2,))],
        ),
        out_shape=[jax.ShapeDtypeStruct((B, 1, D_NSA), f32),
                   jax.ShapeDtypeStruct((B, n_buf, 128), f32), jax.ShapeDtypeStruct((B, n_buf, 128), f32)],
        compiler_params=_params(("arbitrary",)),
        name="nsa_dec_sel",
    )(page_table, idx, qn3, kv6_3, proj3, ocmp, lp["tab8"], win_k, win_v, pool_k, pool_v)


def _layer_params(i, p):
    pad128 = lambda v, off: jnp.zeros((1, 128), f32).at[0, off:off + v.shape[0]].set(v)
    return dict(
        ssd_conv_w=p["ssd_conv_w"][i], ssd_conv_b=p["ssd_conv_b"][i][None],
        ssd_dtb=pad128(p["ssd_dt_bias"][i], SM_DT), ssd_alog=pad128(p["ssd_a_log"][i], SM_DT),
        ssd_dexp=jnp.repeat(p["ssd_d"][i], SSD_HD)[None], ssd_norm=p["ssd_norm"][i][None],
        gdn_conv_w=p["gdn_conv_w"][i],
        gdn_dtb=pad128(p["gdn_dt_bias"][i], SM_A), gdn_alog=pad128(p["gdn_a_log"][i], SM_A),
        gdn_norm=p["gdn_norm"][i][None],
        q_gain=jnp.tile(p["nsa_q_norm"][i], 2)[None], k_gain=jnp.tile(p["nsa_k_norm"][i], (1, 2)),
        cmp_pe=p["nsa_cmp_pe"][i], cmp_w1=p["nsa_cmp_w1"][i], cmp_w2=p["nsa_cmp_w2"][i],
        cmp_pe2=jnp.tile(p["nsa_cmp_pe"][i], (1, 1, 2)),
        cmp_w1bd=_block_diag2(p["nsa_cmp_w1"][i].reshape(2, CMP_LEN, NSA_HD, CMP_HID)).astype(bf16),
        cmp_w2bd=_block_diag2(p["nsa_cmp_w2"][i]).astype(bf16),
        tab8=jnp.pad(p["rel_bias"].T.reshape(NSA_KV, NSA_REP, N_BUCKETS), ((0, 0), (0, 8 - NSA_REP), (0, 0))),
    )


def _block_diag2(w):
    z = jnp.zeros_like(w)
    return jnp.concatenate([jnp.concatenate([w, z], axis=-1), jnp.concatenate([z, w], axis=-1)], axis=-2)


def _reorder_w_in(w_in):
    cuts = [(1024, 2560), (2576, 4112), (0, 1024), (4112, 4624), (4632, 5144), (5144, 5912),
            (2560, 2576), (4624, 4632), (5912, 5936)]
    parts = [w_in[:, :, a:b] for a, b in cuts]
    used = sum(b - a for a, b in cuts)
    parts.append(jnp.zeros(w_in.shape[:2] + (PROJ_W - used,), w_in.dtype))
    return jnp.concatenate(parts, axis=-1).astype(bf16)


def _mix_prompt(x3, i, p, lp, w_r):
    B, L, _ = x3.shape
    n = B * L
    x = x3.reshape(n, D_MODEL)
    proj = _inproj(x, p["norm_mix"], w_r, i)
    proj3 = proj.reshape(B, L, PROJ_W)
    y_ssd, st_ssd, cst_ssd = _ssd_prompt(proj3, lp)
    y_gdn, st_gdn, cst_gdn = _gdn_prompt(proj3, lp)
    qt, kv6_3, kv6t, gt = _prep_t(proj3, lp["q_gain"], lp["k_gain"])
    kc, vct = _cmp_prompt(kv6_3, lp)
    y_nsa = _nsa_prompt(qt, kv6_3, kv6t, kc, vct, gt, p["rel_bias"])
    x = _outproj(x, y_ssd.reshape(n, D_SSD), y_gdn.reshape(n, D_GDN), y_nsa.reshape(n, D_NSA), p["w_out"], i)
    rows = lambda off: jnp.transpose(kv6t[:, off:off + 128].reshape(B, NSA_KV, NSA_HD, L), (0, 3, 1, 2))
    keep = min(WINDOW, L)
    state = (st_ssd, cst_ssd, st_gdn, cst_gdn, rows(KV_CK), rows(KV_CV), rows(KV_SK), rows(KV_SV),
             rows(KV_WK)[:, L - keep:], rows(KV_WV)[:, L - keep:])
    return x.reshape(B, L, D_MODEL), state


def _mix_decode(x, i, p, lp, w_r, st):
    B = x.shape[0]
    proj = _inproj(x, p["norm_mix"], w_r, i)
    proj3 = proj.reshape(B, 1, PROJ_W)
    y_ssd, st_ssd, cst_ssd = _ssd_decode(proj3, st["ssd_conv"], st["ssd"], i, lp)
    y_gdn, st_gdn, cst_gdn = _gdn_decode(proj3, st["gdn_conv"], st["gdn"], i, lp)
    qn, kv6 = _prep(proj, lp["q_gain"], lp["k_gain"])
    qn3, kv6_3 = qn.reshape(B, 1, D_NSA), kv6.reshape(B, 1, 768)
    o_cmp, idx = _nsa_dec_cmp(qn3, st["page_table"], st["cmp_k"], st["cmp_v"], i, lp)
    y_nsa, win_k, win_v = _nsa_dec_sel(qn3, kv6_3, proj3, o_cmp, idx[:, :, :NSA_KV], st["page_table"],
                                       st["win_k"], st["win_v"], st["slc_k"], st["slc_v"], i, lp)
    x = _outproj(x, y_ssd.reshape(B, D_SSD), y_gdn.reshape(B, D_GDN), y_nsa.reshape(B, D_NSA), p["w_out"], i)
    rows = lambda off: kv6[:, off:off + 128].reshape(B, 1, NSA_KV, NSA_HD)
    n_buf = win_k.shape[1]
    state = (st_ssd, cst_ssd, st_gdn, cst_gdn, rows(KV_CK), rows(KV_CV), rows(KV_SK), rows(KV_SV),
             win_k.reshape(B, n_buf, NSA_KV, NSA_HD), win_v.reshape(B, n_buf, NSA_KV, NSA_HD))
    return x, state


def kernel(x_prompt, x_sample, state_ssd, state_ssd_conv, state_gdn, state_gdn_conv, cache_cmp_k, cache_cmp_v,
           cache_slc_k, cache_slc_v, cache_win_k, cache_win_v, page_table, rel_bias, norm_ffn1, w_ffn1_gate,
           w_ffn1_up, w_ffn1_down, norm_mix, w_in, ssd_conv_w, ssd_conv_b, ssd_dt_bias, ssd_a_log, ssd_d, ssd_norm,
           gdn_conv_w, gdn_dt_bias, gdn_a_log, gdn_norm, nsa_q_norm, nsa_k_norm, nsa_cmp_pe, nsa_cmp_w1,
           nsa_cmp_w2, w_out, norm_ffn2, w_ffn2_gate, w_ffn2_up, w_ffn2_down):
    bp, lp_len, _ = x_prompt.shape
    bs = x_sample.shape[0]
    gain3 = lambda g: g.reshape(DEPTH, 1, D_MODEL)
    p = dict(rel_bias=rel_bias, norm_mix=gain3(norm_mix), w_out=w_out, ssd_conv_w=ssd_conv_w, ssd_conv_b=ssd_conv_b,
             ssd_dt_bias=ssd_dt_bias, ssd_a_log=ssd_a_log, ssd_d=ssd_d, ssd_norm=ssd_norm, gdn_conv_w=gdn_conv_w,
             gdn_dt_bias=gdn_dt_bias, gdn_a_log=gdn_a_log, gdn_norm=gdn_norm, nsa_q_norm=nsa_q_norm,
             nsa_k_norm=nsa_k_norm, nsa_cmp_pe=nsa_cmp_pe, nsa_cmp_w1=nsa_cmp_w1, nsa_cmp_w2=nsa_cmp_w2)
    n1, n2 = gain3(norm_ffn1), gain3(norm_ffn2)
    pool = lambda c: c.reshape(c.shape[0], c.shape[1], PAGE, NSA_KV * NSA_HD)
    st = dict(ssd=state_ssd, ssd_conv=state_ssd_conv, gdn=state_gdn, gdn_conv=state_gdn_conv,
              cmp_k=pool(cache_cmp_k), cmp_v=pool(cache_cmp_v), slc_k=pool(cache_slc_k), slc_v=pool(cache_slc_v),
              win_k=cache_win_k.reshape(cache_win_k.shape[:3] + (NSA_KV * NSA_HD,)),
              win_v=cache_win_v.reshape(cache_win_v.shape[:3] + (NSA_KV * NSA_HD,)), page_table=page_table)
    w_r = _reorder_w_in(w_in)

    hp = x_prompt.reshape(bp * lp_len, D_MODEL)
    hs = x_sample.reshape(bs, D_MODEL)
    outs_p, outs_s = [], []
    for i in range(DEPTH):
        lp = _layer_params(i, p)
        hp = _ffn(hp, n1, w_ffn1_gate, w_ffn1_up, w_ffn1_down, i)
        hs = _ffn(hs, n1, w_ffn1_gate, w_ffn1_up, w_ffn1_down, i)
        hp3, st_p = _mix_prompt(hp.reshape(bp, lp_len, D_MODEL), i, p, lp, w_r)
        hs, st_s = _mix_decode(hs, i, p, lp, w_r, st)
        hp = _ffn(hp3.reshape(bp * lp_len, D_MODEL), n2, w_ffn2_gate, w_ffn2_up, w_ffn2_down, i)
        hs = _ffn(hs, n2, w_ffn2_gate, w_ffn2_up, w_ffn2_down, i)
        outs_p.append(st_p)
        outs_s.append(st_s)
    stack = lambda outs: [jnp.stack(t) for t in zip(*outs)]
    return (hp.reshape(bp, lp_len, D_MODEL), hs.reshape(bs, 1, D_MODEL), *stack(outs_p), *stack(outs_s))
```

```python
import functools
import math

import numpy as np
import jax
import jax.numpy as jnp
from jax import lax
from jax.experimental import pallas as pl
from jax.experimental.pallas import tpu as pltpu

f32 = jnp.float32
bf16 = jnp.bfloat16
i32 = jnp.int32
HI = lax.Precision.HIGHEST

D_MODEL = 2048
DEPTH = 4
PAGE = 128
D_SSD = 1024
SSD_HD = 64
SSD_HEADS = 16
SSD_GROUPS = 2
SSD_STATE = 128
SSD_CHUNK = 128
D_GDN = 512
GDN_HD = 128
GDN_HEADS = 4
GDN_CHUNK = 64
GDN_STEP_CHUNKS = 2
D_NSA = 512
NSA_HD = 64
NSA_HEADS = 8
NSA_KV = 2
NSA_REP = 4
CMP_STRIDE = 16
CMP_LEN = 32
CMP_HID = 128
SLC_BLOCK = 64
SLC_TOPN = 16
WINDOW = 512
N_BUCKETS = 32
MAX_DISTANCE = 128
CONV_W = 4
D_FF = 5632
SSD_CONV_CH = 1536
GDN_CONV_CH = 1536
EPS = 1e-6
NEG = -1e30

XBC_OFF = 0
QKV_OFF = 1536
ZS_OFF = 3072
ZG_OFF = 4096
QN_OFF = 4608
KV_OFF = 5120
SM_OFF = 5888
PROJ_W = 6144
SM_DT, SM_A, SM_B, SM_G = 0, 16, 20, 24
KV_CK, KV_CV, KV_SK, KV_SV, KV_WK, KV_WV = 0, 128, 256, 384, 512, 640

VMEM_LIMIT = 56 * 1024 * 1024


def _bucket_thresholds():
    exact = N_BUCKETS // 2
    d = np.arange(0, 4 * MAX_DISTANCE)
    nf = np.maximum(d, 1).astype(np.float32)
    large = exact + (np.log(nf / np.float32(exact)) / np.float32(math.log(MAX_DISTANCE / exact))
                     * np.float32(N_BUCKETS - exact)).astype(np.int32)
    bucket = np.where(d < exact, d, np.minimum(large, N_BUCKETS - 1))
    return [int(np.argmax(bucket >= b)) for b in range(N_BUCKETS)]


_THR = _bucket_thresholds()
BIAS_FAR = _THR[N_BUCKETS - 1]


def _bias_chain(d, tabcols):
    v = jnp.broadcast_to(tabcols[N_BUCKETS - 1], d.shape)
    for b in range(N_BUCKETS - 2, -1, -1):
        v = jnp.where(d < _THR[b + 1], tabcols[b], v)
    return v


def _silu(x):
    return x * jax.nn.sigmoid(x)


def _softplus(x):
    return jnp.maximum(x, 0.0) + jnp.log1p(jnp.exp(-jnp.abs(x)))


def _dot(a, b, prec=None):
    if prec is None:
        a, b = a.astype(bf16), b.astype(bf16)
    return jnp.dot(a, b, preferred_element_type=f32, precision=prec)


def _dot_nt(a, b, prec=None):
    if prec is None:
        a, b = a.astype(bf16), b.astype(bf16)
    return lax.dot_general(a, b, (((1,), (1,)), ((), ())), preferred_element_type=f32, precision=prec)


def _dot_tn(a, b, prec=None):
    if prec is None:
        a, b = a.astype(bf16), b.astype(bf16)
    return lax.dot_general(a, b, (((0,), (0,)), ((), ())), preferred_element_type=f32, precision=prec)


def _split2(a):
    hi = a.astype(bf16)
    return hi, (a - hi.astype(f32)).astype(bf16)


def _dot3(a2, b2):
    d = lambda x, y: jnp.dot(x, y, preferred_element_type=f32)
    return d(a2[0], b2[0]) + (d(a2[0], b2[1]) + d(a2[1], b2[0]))


def _iota(shape, dim):
    return lax.broadcasted_iota(i32, shape, dim)


def _group_ones(n, width):
    return (_iota((n, n), 0) // width == _iota((n, n), 1) // width).astype(f32)


def _group_rms(x, width):
    ss = _dot(x * x, _group_ones(x.shape[1], width), HI)
    return x * lax.rsqrt(ss * (1.0 / width) + EPS)


def _params(sem):
    return pltpu.CompilerParams(dimension_semantics=sem, vmem_limit_bytes=VMEM_LIMIT)


def _ffn_kernel(x_ref, g_ref, wg_ref, wu_ref, wd_ref, o_ref, h_ref):
    @pl.when(pl.program_id(1) == 0)
    def _():
        x = x_ref[...]
        h = x * lax.rsqrt(jnp.mean(x * x, axis=-1, keepdims=True) + EPS) * g_ref[...]
        h_ref[...] = h.astype(bf16)
        o_ref[...] = x

    h = h_ref[...]
    a = jnp.dot(h, wg_ref[...].astype(bf16), preferred_element_type=f32)
    u = jnp.dot(h, wu_ref[...].astype(bf16), preferred_element_type=f32)
    o_ref[...] += 0.5 * _dot(_silu(a) * u, wd_ref[...])


def _ffn(x, gain, wg, wu, wd, layer):
    n = x.shape[0]
    tm = min(n, 1024)
    tf = 256 if tm == 1024 else 512
    return pl.pallas_call(
        _ffn_kernel,
        grid=(n // tm, D_FF // tf),
        in_specs=[
            pl.BlockSpec((tm, D_MODEL), lambda i, j: (i, 0)),
            pl.BlockSpec((None, 1, D_MODEL), lambda i, j: (layer, 0, 0)),
            pl.BlockSpec((None, D_MODEL, tf), lambda i, j: (layer, 0, j)),
            pl.BlockSpec((None, D_MODEL, tf), lambda i, j: (layer, 0, j)),
            pl.BlockSpec((None, tf, D_MODEL), lambda i, j: (layer, j, 0)),
        ],
        out_specs=pl.BlockSpec((tm, D_MODEL), lambda i, j: (i, 0)),
        out_shape=jax.ShapeDtypeStruct((n, D_MODEL), f32),
        scratch_shapes=[pltpu.VMEM((tm, D_MODEL), bf16)],
        compiler_params=_params(("parallel", "arbitrary")),
        name="ffn",
    )(x, gain, wg, wu, wd)


def _inproj_kernel(x_ref, g_ref, w_ref, o_ref, h_ref):
    @pl.when(pl.program_id(1) == 0)
    def _():
        x = x_ref[...]
        h = x * lax.rsqrt(jnp.mean(x * x, axis=-1, keepdims=True) + EPS) * g_ref[...]
        h_ref[...] = h.astype(bf16)

    o_ref[...] = jnp.dot(h_ref[...], w_ref[...], preferred_element_type=f32)


def _inproj(x, gain, w_r, layer):
    n = x.shape[0]
    tm = min(n, 1024)
    tn = 512
    return pl.pallas_call(
        _inproj_kernel,
        grid=(n // tm, PROJ_W // tn),
        in_specs=[
            pl.BlockSpec((tm, D_MODEL), lambda i, j: (i, 0)),
            pl.BlockSpec((None, 1, D_MODEL), lambda i, j: (layer, 0, 0)),
            pl.BlockSpec((None, D_MODEL, tn), lambda i, j: (layer, 0, j)),
        ],
        out_specs=pl.BlockSpec((tm, tn), lambda i, j: (i, j)),
        out_shape=jax.ShapeDtypeStruct((n, PROJ_W), f32),
        scratch_shapes=[pltpu.VMEM((tm, D_MODEL), bf16)],
        compiler_params=_params(("parallel", "arbitrary")),
        name="inproj",
    )(x, gain, w_r)


def _outproj_kernel(x_ref, ys_ref, yg_ref, yn_ref, ws_ref, wg_ref, wn_ref, o_ref):
    acc = _dot(ys_ref[...], ws_ref[...])
    acc += _dot(yg_ref[...], wg_ref[...])
    acc += _dot(yn_ref[...], wn_ref[...])
    o_ref[...] = x_ref[...] + acc


def _outproj(x, ys, yg, yn, w_out, layer):
    n = x.shape[0]
    tm = min(n, 1024)
    tn = 512
    return pl.pallas_call(
        _outproj_kernel,
        grid=(n // tm, D_MODEL // tn),
        in_specs=[
            pl.BlockSpec((tm, tn), lambda i, j: (i, j)),
            pl.BlockSpec((tm, D_SSD), lambda i, j: (i, 0)),
            pl.BlockSpec((tm, D_GDN), lambda i, j: (i, 0)),
            pl.BlockSpec((tm, D_NSA), lambda i, j: (i, 0)),
            pl.BlockSpec((None, D_SSD, tn), lambda i, j: (layer, 0, j)),
            pl.BlockSpec((None, D_GDN, tn), lambda i, j: (layer, 2, j)),
            pl.BlockSpec((None, D_NSA, tn), lambda i, j: (layer, 3, j)),
        ],
        out_specs=pl.BlockSpec((tm, tn), lambda i, j: (i, j)),
        out_shape=jax.ShapeDtypeStruct((n, D_MODEL), f32),
        compiler_params=_params(("parallel", "arbitrary")),
        name="outproj",
    )(x, ys, yg, yn, w_out, w_out, w_out)


def _prep_kernel(q_ref, kv_ref, qg_ref, kg_ref, qn_ref, kv6_ref):
    for s in range(D_NSA // 128):
        sl = slice(s * 128, (s + 1) * 128)
        qn_ref[:, sl] = _group_rms(q_ref[:, sl], NSA_HD) * qg_ref[...]
    kv = kv_ref[...]
    kv6_ref[...] = kv[:, :6 * 128]
    kv6_ref[:, KV_SK:KV_SK + 128] = _group_rms(kv[:, KV_SK:KV_SK + 128], NSA_HD) * kg_ref[1:2, :]
    kv6_ref[:, KV_WK:KV_WK + 128] = _group_rms(kv[:, KV_WK:KV_WK + 128], NSA_HD) * kg_ref[2:3, :]


def _prep(proj, qgain, kgain):
    n = proj.shape[0]
    tm = min(n, 512)
    return pl.pallas_call(
        _prep_kernel,
        grid=(n // tm,),
        in_specs=[
            pl.BlockSpec((tm, D_NSA), lambda i: (i, QN_OFF // D_NSA)),
            pl.BlockSpec((tm, 1024), lambda i: (i, KV_OFF // 1024)),
            pl.BlockSpec((1, 128), lambda i: (0, 0)),
            pl.BlockSpec((3, 128), lambda i: (0, 0)),
        ],
        out_specs=[pl.BlockSpec((tm, D_NSA), lambda i: (i, 0)),
                   pl.BlockSpec((tm, 768), lambda i: (i, 0))],
        out_shape=[jax.ShapeDtypeStruct((n, D_NSA), f32), jax.ShapeDtypeStruct((n, 768), f32)],
        compiler_params=_params(("parallel",)),
        name="nsa_prep",
    )(proj, proj, qgain, kgain)


def _ssd_prompt_kernel(xbc_ref, z_ref, sm_ref, cw_ref, cb_ref, dtb_ref, alog_ref, dexp_ref, nrm_ref,
                       y_ref, st_ref, cst_ref, xbuf, hst, ybuf):
    c = pl.program_id(1)
    cl = SSD_CHUNK

    @pl.when(c == 0)
    def _():
        xbuf[0:8, :] = jnp.zeros((8, SSD_CONV_CH), f32)
        hst[...] = jnp.zeros(hst.shape, f32)

    x = xbc_ref[...]
    xbuf[8:8 + cl, :] = x
    conv = cw_ref[3:4, :] * x
    for k in range(1, CONV_W):
        conv += cw_ref[3 - k:4 - k, :] * xbuf[pl.ds(8 - k, cl), :]
    xbuf[0:8, :] = x[cl - 8:cl, :]
    xc = _silu(conv + cb_ref[...])
    xs = xc[:, :D_SSD]

    dt = jnp.where(_iota((cl, 128), 1) < SSD_HEADS, _softplus(sm_ref[...] + dtb_ref[...]), 0.0)
    da = dt * (-jnp.exp(alog_ref[...]))
    row, col = _iota((cl, cl), 0), _iota((cl, cl), 1)
    causal = row >= col
    cum = _dot(causal.astype(f32), da, HI)
    cum_t = cum.T
    last = cum[cl - 1:cl, :]
    e_last = jnp.exp(last)
    spread = (_iota((128, D_SSD), 1) // SSD_HD == _iota((128, D_SSD), 0)).astype(f32)
    coef = _dot(jnp.concatenate([dt, jnp.exp(last - cum), jnp.exp(cum)], axis=0), spread, HI)
    xdt = xs * coef[0:cl]
    xw = xdt * coef[cl:2 * cl]
    e_cum = coef[2 * cl:3 * cl]

    hpg = SSD_HEADS // SSD_GROUPS
    gw = hpg * SSD_HD
    for g in range(SSD_GROUPS):
        bm = xc[:, D_SSD + g * SSD_STATE:D_SSD + (g + 1) * SSD_STATE]
        cm = xc[:, D_SSD + (SSD_GROUPS + g) * SSD_STATE:D_SSD + (SSD_GROUPS + g + 1) * SSD_STATE]
        cb = _dot_nt(cm, bm)
        h0 = hst[g]
        y_off = _dot_nt(cm, h0) * e_cum[:, g * gw:(g + 1) * gw]
        s_new = _dot_tn(xw[:, g * gw:(g + 1) * gw], bm)
        for r in range(hpg):
            h = g * hpg + r
            seg = cum[:, h:h + 1] - cum_t[h:h + 1, :]
            decay = jnp.where(causal, jnp.exp(jnp.where(causal, seg, 0.0)), 0.0)
            rs = slice(r * SSD_HD, (r + 1) * SSD_HD)
            ybuf[:, h * SSD_HD:(h + 1) * SSD_HD] = (_dot(cb * decay, xdt[:, h * SSD_HD:(h + 1) * SSD_HD])
                                                    + y_off[:, rs])
            hst[g, rs, :] = e_last[:, h:h + 1] * h0[rs, :] + s_new[rs, :]

    y = (ybuf[...] + dexp_ref[...] * xs) * _silu(z_ref[...])
    gw = D_SSD // SSD_GROUPS
    for g in range(SSD_GROUPS):
        yg = y[:, g * gw:(g + 1) * gw]
        yg = yg * lax.rsqrt(jnp.mean(yg * yg, axis=-1, keepdims=True) + EPS)
        y_ref[:, g * gw:(g + 1) * gw] = yg * nrm_ref[:, g * gw:(g + 1) * gw]

    @pl.when(c == pl.num_programs(1) - 1)
    def _():
        st_ref[...] = hst[...]
        cst_ref[...] = x[cl - (CONV_W - 1):cl, :]


def _ssd_prompt(proj3, lp):
    B, L, _ = proj3.shape
    cl = SSD_CHUNK
    gh = SSD_HEADS // SSD_GROUPS * SSD_HD
    vec = lambda w: pl.BlockSpec((1, w), lambda b, c: (0, 0))
    y, st, cst = pl.pallas_call(
        _ssd_prompt_kernel,
        grid=(B, L // cl),
        in_specs=[
            pl.BlockSpec((None, cl, SSD_CONV_CH), lambda b, c: (b, c, XBC_OFF // SSD_CONV_CH)),
            pl.BlockSpec((None, cl, D_SSD), lambda b, c: (b, c, ZS_OFF // D_SSD)),
            pl.BlockSpec((None, cl, 128), lambda b, c: (b, c, SM_OFF // 128)),
            pl.BlockSpec((CONV_W, SSD_CONV_CH), lambda b, c: (0, 0)),
            vec(SSD_CONV_CH), vec(128), vec(128), vec(D_SSD), vec(D_SSD),
        ],
        out_specs=[
            pl.BlockSpec((None, cl, D_SSD), lambda b, c: (b, c, 0)),
            pl.BlockSpec((None, SSD_GROUPS, gh, SSD_STATE), lambda b, c: (b, 0, 0, 0)),
            pl.BlockSpec((None, CONV_W - 1, SSD_CONV_CH), lambda b, c: (b, 0, 0)),
        ],
        out_shape=[
            jax.ShapeDtypeStruct((B, L, D_SSD), f32),
            jax.ShapeDtypeStruct((B, SSD_GROUPS, gh, SSD_STATE), f32),
            jax.ShapeDtypeStruct((B, CONV_W - 1, SSD_CONV_CH), f32),
        ],
        scratch_shapes=[
            pltpu.VMEM((8 + cl, SSD_CONV_CH), f32),
            pltpu.VMEM((SSD_GROUPS, gh, SSD_STATE), f32),
            pltpu.VMEM((cl, D_SSD), f32),
        ],
        compiler_params=_params(("parallel", "arbitrary")),
        name="ssd_prompt",
    )(proj3, proj3, proj3, lp["ssd_conv_w"], lp["ssd_conv_b"], lp["ssd_dtb"], lp["ssd_alog"],
      lp["ssd_dexp"], lp["ssd_norm"])
    return y, st.reshape(B, SSD_HEADS, SSD_HD, SSD_STATE), cst


def _gdn_prompt_kernel(qkv_ref, z_ref, sm_ref, cw_ref, dtb_ref, alog_ref, nrm_ref,
                       y_ref, st_ref, cst_ref, xbuf, sst):
    c = pl.program_id(1)
    cl = GDN_CHUNK
    nch = GDN_STEP_CHUNKS
    tl = nch * cl

    @pl.when(c == 0)
    def _():
        xbuf[0:8, :] = jnp.zeros((8, GDN_CONV_CH), f32)
        sst[...] = jnp.zeros(sst.shape, f32)

    x = qkv_ref[...]
    xbuf[8:8 + tl, :] = x
    conv = cw_ref[3:4, :] * x
    for k in range(1, CONV_W):
        conv += cw_ref[3 - k:4 - k, :] * xbuf[pl.ds(8 - k, tl), :]
    xbuf[0:8, :] = x[tl - 8:tl, :]
    xc = _silu(conv)

    sm = sm_ref[...]
    beta_all = jax.nn.sigmoid(sm)
    g_all = -jnp.exp(alog_ref[...]) * _softplus(sm + dtb_ref[...])
    row, col = _iota((cl, cl), 0), _iota((cl, cl), 1)
    incl = row >= col
    strict = row > col
    eye = (row == col).astype(f32)
    rt, ct = _iota((tl, tl), 0), _iota((tl, tl), 1)
    cum = _dot(((rt >= ct) & (rt // cl == ct // cl)).astype(f32), g_all, HI)
    cum_t = cum.T

    units = [(ci, h) for ci in range(nch) for h in range(GDN_HEADS)]
    q, k, v, beta, cum_c, decay = {}, {}, {}, {}, {}, {}
    for u in units:
        ci, h = u
        rows = slice(ci * cl, (ci + 1) * cl)
        qh = xc[rows, h * GDN_HD:(h + 1) * GDN_HD]
        kh = xc[rows, D_GDN + h * GDN_HD:D_GDN + (h + 1) * GDN_HD]
        v[u] = xc[rows, 2 * D_GDN + h * GDN_HD:2 * D_GDN + (h + 1) * GDN_HD]
        q[u] = qh * lax.rsqrt(jnp.sum(qh * qh, axis=-1, keepdims=True) + EPS) * (GDN_HD ** -0.5)
        k[u] = kh * lax.rsqrt(jnp.sum(kh * kh, axis=-1, keepdims=True) + EPS)
        beta[u] = beta_all[rows, SM_B + h:SM_B + h + 1]
        cum_c[u] = cum[rows, SM_A + h:SM_A + h + 1]
        seg = cum_c[u] - cum_t[SM_A + h:SM_A + h + 1, rows]
        decay[u] = jnp.where(incl, jnp.exp(jnp.where(incl, seg, 0.0)), 0.0)
    pw = {u: -jnp.where(strict, beta[u] * _dot_nt(k[u], k[u]) * decay[u], 0.0) for u in units}
    inv = {u: eye + pw[u] for u in units}
    pw2 = {u: _split2(pw[u]) for u in units}
    for _ in range(5):
        pw2 = {u: _split2(_dot3(pw2[u], pw2[u])) for u in units}
        inv = {u: inv[u] + _dot3(_split2(inv[u]), pw2[u]) for u in units}
    rhs = {u: jnp.concatenate([beta[u] * v[u], (beta[u] * jnp.exp(cum_c[u])) * k[u]], axis=1) for u in units}
    sol = {u: _dot3(_split2(inv[u]), _split2(rhs[u])) for u in units}
    uu = {u: sol[u][:, :GDN_HD] for u in units}
    ww = {u: sol[u][:, GDN_HD:] for u in units}
    qk = {u: _dot_nt(q[u], k[u]) * decay[u] for u in units}

    s = [sst[h] for h in range(GDN_HEADS)]
    for ci in range(nch):
        rows = slice(ci * cl, (ci + 1) * cl)
        hs = [(ci, h) for h in range(GDN_HEADS)]
        v_new = {u: uu[u] - _dot(ww[u], s[u[1]]) for u in hs}
        o = {u: _dot(q[u] * jnp.exp(cum_c[u]), s[u[1]]) + _dot(qk[u], v_new[u]) for u in hs}
        for u in hs:
            h = u[1]
            last = cum[ci * cl + cl - 1:ci * cl + cl, SM_A + h:SM_A + h + 1]
            s[h] = jnp.exp(last) * s[h] + _dot_tn(k[u] * jnp.exp(last - cum_c[u]), v_new[u])
        for u in hs:
            sl = slice(u[1] * GDN_HD, (u[1] + 1) * GDN_HD)
            on = o[u] * lax.rsqrt(jnp.mean(o[u] * o[u], axis=-1, keepdims=True) + EPS)
            y_ref[rows, sl] = on * nrm_ref[...] * _silu(z_ref[rows, sl])
    for h in range(GDN_HEADS):
        sst[h] = s[h]

    @pl.when(c == pl.num_programs(1) - 1)
    def _():
        st_ref[...] = sst[...]
        cst_ref[...] = x[tl - (CONV_W - 1):tl, :]


def _gdn_prompt(proj3, lp):
    B, L, _ = proj3.shape
    cl = GDN_CHUNK * GDN_STEP_CHUNKS
    vec = lambda w: pl.BlockSpec((1, w), lambda b, c: (0, 0))
    return pl.pallas_call(
        _gdn_prompt_kernel,
        grid=(B, L // cl),
        in_specs=[
            pl.BlockSpec((None, cl, GDN_CONV_CH), lambda b, c: (b, c, QKV_OFF // GDN_CONV_CH)),
            pl.BlockSpec((None, cl, D_GDN), lambda b, c: (b, c, ZG_OFF // D_GDN)),
            pl.BlockSpec((None, cl, 128), lambda b, c: (b, c, SM_OFF // 128)),
            pl.BlockSpec((CONV_W, GDN_CONV_CH), lambda b, c: (0, 0)),
            vec(128), vec(128), vec(GDN_HD),
        ],
        out_specs=[
            pl.BlockSpec((None, cl, D_GDN), lambda b, c: (b, c, 0)),
            pl.BlockSpec((None, GDN_HEADS, GDN_HD, GDN_HD), lambda b, c: (b, 0, 0, 0)),
            pl.BlockSpec((None, CONV_W - 1, GDN_CONV_CH), lambda b, c: (b, 0, 0)),
        ],
        out_shape=[
            jax.ShapeDtypeStruct((B, L, D_GDN), f32),
            jax.ShapeDtypeStruct((B, GDN_HEADS, GDN_HD, GDN_HD), f32),
            jax.ShapeDtypeStruct((B, CONV_W - 1, GDN_CONV_CH), f32),
        ],
        scratch_shapes=[
            pltpu.VMEM((8 + cl, GDN_CONV_CH), f32),
            pltpu.VMEM((GDN_HEADS, GDN_HD, GDN_HD), f32),
        ],
        compiler_params=_params(("parallel", "arbitrary")),
        name="gdn_prompt",
    )(proj3, proj3, proj3, lp["gdn_conv_w"], lp["gdn_dtb"], lp["gdn_alog"], lp["gdn_norm"])


def _cmp_prompt_kernel(k_ref, v_ref, pe_ref, w1_ref, w2_ref, kg_ref, kc_ref, vct_ref):
    nb = kc_ref.shape[0]
    for t, src in enumerate((k_ref, v_ref)):
        outs = []
        p1 = [jnp.zeros((nb, CMP_HID), f32) for _ in range(NSA_KV)]
        p2 = [jnp.zeros((nb, CMP_HID), f32) for _ in range(NSA_KV)]
        for l in range(CMP_STRIDE):
            xr = src[pl.ds(l, nb, stride=CMP_STRIDE), :]
            l2 = CMP_STRIDE + l
            for g in range(NSA_KV):
                xg = xr[:, g * NSA_HD:(g + 1) * NSA_HD]
                p1[g] += _dot(xg + pe_ref[t, l:l + 1, :], w1_ref[t, l * NSA_HD:(l + 1) * NSA_HD, :])
                p2[g] += _dot(xg + pe_ref[t, l2:l2 + 1, :], w1_ref[t, l2 * NSA_HD:(l2 + 1) * NSA_HD, :])
        for g in range(NSA_KV):
            hid = p1[g] + pltpu.roll(p2[g], nb - 1, 0)
            cmp = _dot(_silu(hid), w2_ref[t])
            if t == 0:
                cmp = cmp * lax.rsqrt(jnp.mean(cmp * cmp, axis=-1, keepdims=True) + EPS) * kg_ref[0:1, 0:NSA_HD]
            outs.append(cmp)
        both = jnp.concatenate(outs, axis=1)
        if t == 0:
            kc_ref[...] = both
        else:
            vct_ref[...] = both.T


def _cmp_prompt(kv6_3, lp):
    B, L, _ = kv6_3.shape
    nb = L // CMP_STRIDE
    full = lambda s: pl.BlockSpec(s, lambda b: (0,) * len(s))
    return pl.pallas_call(
        _cmp_prompt_kernel,
        grid=(B,),
        in_specs=[
            pl.BlockSpec((None, L, 128), lambda b: (b, 0, KV_CK // 128)),
            pl.BlockSpec((None, L, 128), lambda b: (b, 0, KV_CV // 128)),
            full((2, CMP_LEN, NSA_HD)), full((2, CMP_LEN * NSA_HD, CMP_HID)), full((2, CMP_HID, NSA_HD)),
            full((3, 128)),
        ],
        out_specs=[pl.BlockSpec((None, nb, 128), lambda b: (b, 0, 0)),
                   pl.BlockSpec((None, 128, nb), lambda b: (b, 0, 0))],
        out_shape=[jax.ShapeDtypeStruct((B, nb, 128), f32), jax.ShapeDtypeStruct((B, 128, nb), f32)],
        compiler_params=_params(("parallel",)),
        name="nsa_cmp_prompt",
    )(kv6_3, kv6_3, lp["cmp_pe"], lp["cmp_w1"], lp["cmp_w2"], lp["k_gain"])


def _nsa_prompt_kernel(rb_ref, qt_ref, ks_ref, kw_ref, vst_ref, vwt_ref, kc_ref, vct_ref, gt_ref,
                       y_ref, tb_ref, tc_ref, ext_ref, selx_ref):
    qi = pl.program_id(1)
    L = ks_ref.shape[0]
    tq = 128
    nl = NSA_REP * tq
    ncmp = kc_ref.shape[0]
    n_slc = L // SLC_BLOCK
    last = N_BUCKETS - 1

    @pl.when(qi == 0)
    def _():
        sub = _iota((128, tq), 0)
        qo = _iota((128, tq), 1)
        d_diag = qo - sub
        c_rel = jnp.where(sub < 64, sub, sub - 128)
        d_cmp = qo - CMP_STRIDE * c_rel - (CMP_LEN - 1)
        for g in range(NSA_KV):
            for r in range(NSA_REP):
                h = g * NSA_REP + r
                tab = [rb_ref[b, h] - rb_ref[last, h] for b in range(N_BUCKETS)]
                lanes = slice(r * tq, (r + 1) * tq)
                tb_ref[g, 0, :, lanes] = jnp.zeros((128, tq), f32)
                tb_ref[g, 1, :, lanes] = _bias_chain(d_diag + 128, tab)
                tb_ref[g, 2, :, lanes] = _bias_chain(d_diag, tab)
                t_c = jnp.where(d_cmp < 0, 0.0, _bias_chain(d_cmp, tab))
                tc_ref[g, 0:128, lanes] = t_c
                tc_ref[g, 128:256, lanes] = t_c
        ext_ref[...] = (_iota((L, 128), 0) // SLC_BLOCK == _iota((L, 128), 1)).astype(bf16)

    q0 = qi * tq
    qpos = q0 + _iota((1, nl), 1) % tq
    qpos1 = q0 + _iota((1, tq), 1)
    sub = _iota((128, 1), 0)
    gt = gt_ref[...]

    jr = _iota((n_slc, tq), 0)
    cur = qpos1 // SLC_BLOCK
    valid = jr * SLC_BLOCK <= qpos1
    forced = (jr == 0) | (jr == cur) | (jr == cur - 1)
    j_i = _iota((n_slc, ncmp), 0)
    c_i = _iota((n_slc, ncmp), 1)
    ovl = ((c_i * CMP_STRIDE < j_i * SLC_BLOCK + SLC_BLOCK) & (c_i * CMP_STRIDE + CMP_LEN > j_i * SLC_BLOCK)
           & (c_i < ncmp - 1)).astype(f32)
    cmp_off = pl.multiple_of((128 - qi * (tq // CMP_STRIDE)) % 128, 8)

    groups = range(NSA_KV)
    gsl = [slice(g * NSA_HD, (g + 1) * NSA_HD) for g in groups]
    qt = [jnp.concatenate([qt_ref[(g * NSA_REP + r) * NSA_HD:(g * NSA_REP + r + 1) * NSA_HD, :]
                           for r in range(NSA_REP)], axis=1).astype(bf16) for g in groups]

    ok_c = (sub * CMP_STRIDE + (CMP_LEN - 1) <= qpos) & (sub < ncmp - 1)
    o_cmp = []
    for g in groups:
        st = _dot(kc_ref[:, gsl[g]], qt[g]) + tc_ref[g, pl.ds(cmp_off, 128), :]
        st = jnp.where(ok_c, st, NEG)
        e = jnp.exp(st - jnp.max(st, axis=0, keepdims=True))
        p = jnp.where(ok_c, e / jnp.sum(e, axis=0, keepdims=True), 0.0)
        o_cmp.append(_dot(vct_ref[gsl[g], :], p))
        p_sum = p[:, 0:tq] + p[:, tq:2 * tq] + p[:, 2 * tq:3 * tq] + p[:, 3 * tq:4 * tq]
        imp = _dot(ovl, p_sum, HI)
        score = jnp.where(valid, jnp.where(forced, 1e9, imp), -1e9)
        rank = jnp.zeros((n_slc, tq), f32)
        for i in range(n_slc):
            si = score[i:i + 1, :]
            rank += ((si > score) | ((si == score) & (jr > i))).astype(f32)
        sel = (rank < SLC_TOPN).astype(bf16)
        sel = jnp.concatenate([sel, jnp.zeros((128 - n_slc, tq), bf16)], axis=0)
        selx_ref[g] = jnp.dot(ext_ref[...], sel, preferred_element_type=f32)

    def attend(state, tiles, k_ref, vt_ref, mask_fn):
        s, offs = {}, []
        for ti, (kt, table) in enumerate(tiles):
            ko = pl.multiple_of(kt * 128, 128)
            offs.append(ko)
            oks = mask_fn(kt, ko)
            for g in groups:
                sg = _dot(k_ref[pl.ds(ko, 128), gsl[g]], qt[g]) + tb_ref[g, table]
                s[g, ti] = jnp.where(oks[g], sg, NEG)
        out = []
        for g in groups:
            m, l, acc = state[g]
            m_new = m
            for ti in range(len(tiles)):
                m_new = jnp.maximum(m_new, jnp.max(s[g, ti], axis=0, keepdims=True))
            alpha = jnp.exp(m - m_new)
            l, acc = alpha * l, alpha * acc
            for ti in range(len(tiles)):
                pr = jnp.exp(s[g, ti] - m_new)
                l += jnp.sum(pr, axis=0, keepdims=True)
                acc += _dot(vt_ref[gsl[g], pl.ds(offs[ti], 128)], pr)
            out.append((m_new, l, acc))
        return tuple(out)

    init = tuple((jnp.full((1, nl), NEG, f32), jnp.zeros((1, nl), f32), jnp.zeros((NSA_HD, nl), f32))
                 for _ in groups)
    far_table = lambda kt: jnp.where(kt == qi - 1, 1, 0)

    def sel_mask(kt, ko, diagonal=False):
        oks = [jnp.concatenate([selx_ref[g, pl.ds(ko, 128), :]] * NSA_REP, axis=1) > 0.5 for g in groups]
        return [ok & (ko + sub <= qpos) for ok in oks] if diagonal else oks

    pair = lambda i, st: attend(st, [(2 * i, far_table(2 * i)), (2 * i + 1, far_table(2 * i + 1))],
                                ks_ref, vst_ref, sel_mask)
    st = lax.fori_loop(0, qi // 2, pair, init)
    st = lax.fori_loop(0, qi % 2, lambda i, st: attend(st, [(qi - 1, 1)], ks_ref, vst_ref, sel_mask), st)
    st = attend(st, [(qi, 2)], ks_ref, vst_ref, functools.partial(sel_mask, diagonal=True))
    o_slc = [acc / l for _, l, acc in st]

    w0 = jnp.maximum(qi - WINDOW // 128, 0)

    def win_mask(kt, ko):
        d = qpos - (ko + sub)
        ok = (d >= 0) & (d <= WINDOW)
        return [ok, ok]

    win_tiles = [(w0 + t, jnp.clip(w0 + t - qi + 2, 0, 2)) for t in range(WINDOW // 128 + 1)]
    st = attend(init, win_tiles, kw_ref, vwt_ref, win_mask)
    o_win = [acc / l for _, l, acc in st]

    y_parts = []
    for g in groups:
        for r in range(NSA_REP):
            lanes = slice(r * tq, (r + 1) * tq)
            h = g * NSA_REP + r
            gate = lambda t: gt[SM_G + t * NSA_HEADS + h:SM_G + t * NSA_HEADS + h + 1, :]
            y_parts.append(gate(0) * o_cmp[g][:, lanes] + gate(1) * o_slc[g][:, lanes]
                           + gate(2) * o_win[g][:, lanes])
    y_ref[...] = jnp.concatenate(y_parts, axis=0).T


def _nsa_prompt(qt, kv6_3, kv6t, kc, vct, gt, rel_bias):
    B, L, _ = kv6_3.shape
    tq = 128
    nb = L // CMP_STRIDE
    return pl.pallas_call(
        _nsa_prompt_kernel,
        grid=(B, L // tq),
        in_specs=[
            pl.BlockSpec(memory_space=pltpu.SMEM),
            pl.BlockSpec((None, D_NSA, tq), lambda b, i: (b, 0, i)),
            pl.BlockSpec((None, L, 128), lambda b, i: (b, 0, KV_SK // 128)),
            pl.BlockSpec((None, L, 128), lambda b, i: (b, 0, KV_WK // 128)),
            pl.BlockSpec((None, 128, L), lambda b, i: (b, KV_SV // 128, 0)),
            pl.BlockSpec((None, 128, L), lambda b, i: (b, KV_WV // 128, 0)),
            pl.BlockSpec((None, nb, 128), lambda b, i: (b, 0, 0)),
            pl.BlockSpec((None, 128, nb), lambda b, i: (b, 0, 0)),
            pl.BlockSpec((None, 128, tq), lambda b, i: (b, 0, i)),
        ],
        out_specs=pl.BlockSpec((None, tq, D_NSA), lambda b, i: (b, i, 0)),
        out_shape=jax.ShapeDtypeStruct((B, L, D_NSA), f32),
        scratch_shapes=[
            pltpu.VMEM((NSA_KV, 3, 128, NSA_REP * tq), f32),
            pltpu.VMEM((NSA_KV, 256, NSA_REP * tq), f32),
            pltpu.VMEM((L, 128), bf16),
            pltpu.VMEM((NSA_KV, L, tq), f32),
        ],
        compiler_params=_params(("parallel", "arbitrary")),
        name="nsa_prompt",
    )(rel_bias, qt, kv6_3, kv6_3, kv6t, kv6t, kc, vct, gt)


def _prep_t_kernel(q_ref, kv_ref, qg_ref, kg_ref, qt_ref, kv6_ref, kv6t_ref, gt_ref):
    for s in range(D_NSA // 128):
        sl = slice(s * 128, (s + 1) * 128)
        qt_ref[sl, :] = (_group_rms(q_ref[:, sl], NSA_HD) * qg_ref[...] * (NSA_HD ** -0.5)).T
    kv = kv_ref[...]
    for j, off in enumerate((KV_CK, KV_CV, KV_SK, KV_SV, KV_WK, KV_WV)):
        x = kv[:, off:off + 128]
        if off == KV_SK:
            x = _group_rms(x, NSA_HD) * kg_ref[1:2, :]
        if off == KV_WK:
            x = _group_rms(x, NSA_HD) * kg_ref[2:3, :]
        kv6_ref[:, off:off + 128] = x
        kv6t_ref[off:off + 128, :] = x.T
    gt_ref[...] = jax.nn.sigmoid(kv[:, SM_OFF - KV_OFF:SM_OFF - KV_OFF + 128]).T


def _prep_t(proj3, qgain, kgain):
    B, L, _ = proj3.shape
    tm = min(L, 512)
    return pl.pallas_call(
        _prep_t_kernel,
        grid=(B, L // tm),
        in_specs=[
            pl.BlockSpec((None, tm, D_NSA), lambda b, i: (b, i, QN_OFF // D_NSA)),
            pl.BlockSpec((None, tm, 1024), lambda b, i: (b, i, KV_OFF // 1024)),
            pl.BlockSpec((1, 128), lambda b, i: (0, 0)),
            pl.BlockSpec((3, 128), lambda b, i: (0, 0)),
        ],
        out_specs=[pl.BlockSpec((None, D_NSA, tm), lambda b, i: (b, 0, i)),
                   pl.BlockSpec((None, tm, 768), lambda b, i: (b, i, 0)),
                   pl.BlockSpec((None, 768, tm), lambda b, i: (b, 0, i)),
                   pl.BlockSpec((None, 128, tm), lambda b, i: (b, 0, i))],
        out_shape=[jax.ShapeDtypeStruct((B, D_NSA, L), f32), jax.ShapeDtypeStruct((B, L, 768), f32),
                   jax.ShapeDtypeStruct((B, 768, L), f32), jax.ShapeDtypeStruct((B, 128, L), f32)],
        compiler_params=_params(("parallel", "parallel")),
        name="nsa_prep_t",
    )(proj3, proj3, qgain, kgain)


def _row0(x, rows=8):
    return jnp.where(_iota((rows, x.shape[1]), 0) == 0, x, 0.0)


def _ssd_decode_kernel(xbc_ref, z_ref, sm_ref, cprev_ref, sin_ref, cw_ref, cb_ref, dtb_ref, alog_ref, dexp_ref,
                       nrm_ref, y_ref, st_ref, cst_ref):
    x = xbc_ref[...]
    conv = cw_ref[3:4, :] * x
    for j in range(CONV_W - 1):
        conv += cw_ref[j:j + 1, :] * cprev_ref[j:j + 1, :]
    cst_ref[0:2, :] = cprev_ref[1:3, :]
    cst_ref[2:3, :] = x
    xc = _silu(conv + cb_ref[...])
    xs = xc[:, :D_SSD]
    dt = _softplus(sm_ref[...] + dtb_ref[...])
    da = dt * (-jnp.exp(alog_ref[...]))
    ys = []
    for g in range(SSD_GROUPS):
        bm = xc[:, D_SSD + g * SSD_STATE:D_SSD + (g + 1) * SSD_STATE]
        cm = xc[:, D_SSD + (SSD_GROUPS + g) * SSD_STATE:D_SSD + (SSD_GROUPS + g + 1) * SSD_STATE]
        cb = jnp.sum(cm * bm, axis=-1, keepdims=True)
        bm8, cm8 = _row0(bm), _row0(cm)
        for r in range(SSD_HEADS // SSD_GROUPS):
            h = g * (SSD_HEADS // SSD_GROUPS) + r
            xdt = xs[:, h * SSD_HD:(h + 1) * SSD_HD] * dt[:, h:h + 1]
            eda = jnp.exp(da[:, h:h + 1])
            h0 = sin_ref[h]
            ys.append(cb * xdt + eda * _dot_nt(cm8, h0, HI)[0:1])
            st_ref[h] = eda * h0 + _dot_tn(_row0(xdt), bm8, HI)
    y = (jnp.concatenate(ys, axis=1) + dexp_ref[...] * xs) * _silu(z_ref[...])
    gw = D_SSD // SSD_GROUPS
    for g in range(SSD_GROUPS):
        yg = y[:, g * gw:(g + 1) * gw]
        yg = yg * lax.rsqrt(jnp.mean(yg * yg, axis=-1, keepdims=True) + EPS)
        y_ref[:, g * gw:(g + 1) * gw] = yg * nrm_ref[:, g * gw:(g + 1) * gw]


def _ssd_decode(proj3, conv_state, state, layer, lp):
    B = proj3.shape[0]
    vec = lambda w: pl.BlockSpec((1, w), lambda b: (0, 0))
    return pl.pallas_call(
        _ssd_decode_kernel,
        grid=(B,),
        in_specs=[
            pl.BlockSpec((None, 1, SSD_CONV_CH), lambda b: (b, 0, XBC_OFF // SSD_CONV_CH)),
            pl.BlockSpec((None, 1, D_SSD), lambda b: (b, 0, ZS_OFF // D_SSD)),
            pl.BlockSpec((None, 1, 128), lambda b: (b, 0, SM_OFF // 128)),
            pl.BlockSpec((None, None, CONV_W - 1, SSD_CONV_CH), lambda b: (layer, b, 0, 0)),
            pl.BlockSpec((None, None, SSD_HEADS, SSD_HD, SSD_STATE), lambda b: (layer, b, 0, 0, 0)),
            pl.BlockSpec((CONV_W, SSD_CONV_CH), lambda b: (0, 0)),
            vec(SSD_CONV_CH), vec(128), vec(128), vec(D_SSD), vec(D_SSD),
        ],
        out_specs=[
            pl.BlockSpec((None, 1, D_SSD), lambda b: (b, 0, 0)),
            pl.BlockSpec((None, SSD_HEADS, SSD_HD, SSD_STATE), lambda b: (b, 0, 0, 0)),
            pl.BlockSpec((None, CONV_W - 1, SSD_CONV_CH), lambda b: (b, 0, 0)),
        ],
        out_shape=[
            jax.ShapeDtypeStruct((B, 1, D_SSD), f32),
            jax.ShapeDtypeStruct((B, SSD_HEADS, SSD_HD, SSD_STATE), f32),
            jax.ShapeDtypeStruct((B, CONV_W - 1, SSD_CONV_CH), f32),
        ],
        compiler_params=_params(("parallel",)),
        name="ssd_decode",
    )(proj3, proj3, proj3, conv_state, state, lp["ssd_conv_w"], lp["ssd_conv_b"], lp["ssd_dtb"], lp["ssd_alog"],
      lp["ssd_dexp"], lp["ssd_norm"])


def _gdn_decode_kernel(qkv_ref, z_ref, sm_ref, cprev_ref, sin_ref, cw_ref, dtb_ref, alog_ref, nrm_ref,
                       y_ref, st_ref, cst_ref):
    x = qkv_ref[...]
    conv = cw_ref[3:4, :] * x
    for j in range(CONV_W - 1):
        conv += cw_ref[j:j + 1, :] * cprev_ref[j:j + 1, :]
    cst_ref[0:2, :] = cprev_ref[1:3, :]
    cst_ref[2:3, :] = x
    xc = _silu(conv)
    sm = sm_ref[...]
    beta_all = jax.nn.sigmoid(sm)
    g_all = -jnp.exp(alog_ref[...]) * _softplus(sm + dtb_ref[...])
    for h in range(GDN_HEADS):
        sl = slice(h * GDN_HD, (h + 1) * GDN_HD)
        qh = xc[:, sl]
        kh = xc[:, D_GDN + h * GDN_HD:D_GDN + (h + 1) * GDN_HD]
        vh = xc[:, 2 * D_GDN + h * GDN_HD:2 * D_GDN + (h + 1) * GDN_HD]
        qh = qh * lax.rsqrt(jnp.sum(qh * qh, axis=-1, keepdims=True) + EPS) * (GDN_HD ** -0.5)
        kh = kh * lax.rsqrt(jnp.sum(kh * kh, axis=-1, keepdims=True) + EPS)
        beta = beta_all[:, SM_B + h:SM_B + h + 1]
        eg = jnp.exp(g_all[:, SM_A + h:SM_A + h + 1])
        s0 = sin_ref[h]
        k8 = _row0(kh)
        v_new = beta * vh - (beta * eg) * _dot(k8, s0, HI)[0:1]
        o = eg * _dot(_row0(qh), s0, HI)[0:1] + jnp.sum(qh * kh, axis=-1, keepdims=True) * v_new
        st_ref[h] = eg * s0 + _dot_tn(k8, _row0(v_new), HI)
        o = o * lax.rsqrt(jnp.mean(o * o, axis=-1, keepdims=True) + EPS)
        y_ref[:, sl] = o * nrm_ref[...] * _silu(z_ref[:, sl])


def _gdn_decode(proj3, conv_state, state, layer, lp):
    B = proj3.shape[0]
    vec = lambda w: pl.BlockSpec((1, w), lambda b: (0, 0))
    return pl.pallas_call(
        _gdn_decode_kernel,
        grid=(B,),
        in_specs=[
            pl.BlockSpec((None, 1, GDN_CONV_CH), lambda b: (b, 0, QKV_OFF // GDN_CONV_CH)),
            pl.BlockSpec((None, 1, D_GDN), lambda b: (b, 0, ZG_OFF // D_GDN)),
            pl.BlockSpec((None, 1, 128), lambda b: (b, 0, SM_OFF // 128)),
            pl.BlockSpec((None, None, CONV_W - 1, GDN_CONV_CH), lambda b: (layer, b, 0, 0)),
            pl.BlockSpec((None, None, GDN_HEADS, GDN_HD, GDN_HD), lambda b: (layer, b, 0, 0, 0)),
            pl.BlockSpec((CONV_W, GDN_CONV_CH), lambda b: (0, 0)),
            vec(128), vec(128), vec(GDN_HD),
        ],
        out_specs=[
            pl.BlockSpec((None, 1, D_GDN), lambda b: (b, 0, 0)),
            pl.BlockSpec((None, GDN_HEADS, GDN_HD, GDN_HD), lambda b: (b, 0, 0, 0)),
            pl.BlockSpec((None, CONV_W - 1, GDN_CONV_CH), lambda b: (b, 0, 0)),
        ],
        out_shape=[
            jax.ShapeDtypeStruct((B, 1, D_GDN), f32),
            jax.ShapeDtypeStruct((B, GDN_HEADS, GDN_HD, GDN_HD), f32),
            jax.ShapeDtypeStruct((B, CONV_W - 1, GDN_CONV_CH), f32),
        ],
        compiler_params=_params(("parallel",)),
        name="gdn_decode",
    )(proj3, proj3, proj3, conv_state, state, lp["gdn_conv_w"], lp["gdn_dtb"], lp["gdn_alog"], lp["gdn_norm"])


def _query_rows(q, g):
    rows = [q[:, (g * NSA_REP + r) * NSA_HD:(g * NSA_REP + r + 1) * NSA_HD] for r in range(NSA_REP)]
    return jnp.concatenate(rows + [jnp.zeros((8 - NSA_REP, NSA_HD), f32)], axis=0) * (NSA_HD ** -0.5)


def _masked_softmax(lg, ok):
    lg = jnp.where(ok, lg, NEG)
    e = jnp.exp(lg - jnp.max(lg, axis=-1, keepdims=True))
    return jnp.where(ok, e / jnp.sum(e, axis=-1, keepdims=True), 0.0)


def _head_lanes(parts):
    return jnp.concatenate([o[r:r + 1, :] for o in parts for r in range(NSA_REP)], axis=1)


def _nsa_dec_cmp_kernel(pt_ref, q_ref, pe_ref, w1_ref, w2_ref, kg_ref, tab_ref, pk_ref, pv_ref,
                        ocmp_ref, idx_ref, stage, rows, sem, *, layer, n_pages):
    b = pl.program_id(0)
    nb = pl.num_programs(0)
    n_past = n_pages * PAGE
    q_pos = n_past
    n_blk = n_past // CMP_STRIDE
    n_cmp = (n_past + 1 - CMP_LEN) // CMP_STRIDE + 1
    n_slc = -(-(n_past + 1) // SLC_BLOCK)
    nj = -(-n_slc // 128) * 128
    pools = (pk_ref, pv_ref)

    def page_copy(t, seq, p):
        return pltpu.make_async_copy(pools[t].at[layer, pt_ref[seq, p]], stage.at[t, p], sem.at[t])

    def start_all(t, seq):
        lax.fori_loop(0, n_pages, lambda p, c: (page_copy(t, seq, p).start(), c)[1], 0)

    def wait_all(t, seq):
        lax.fori_loop(0, n_pages, lambda p, c: (page_copy(t, seq, p).wait(), c)[1], 0)

    @pl.when(b == 0)
    def _():
        start_all(0, 0)
        start_all(1, 0)

    def to_rows(t):
        def body(p, c):
            rows[pl.ds(pl.multiple_of(p * PAGE, PAGE), PAGE), :] = stage[t, p].T
            return c
        lax.fori_loop(0, n_pages, body, 0, unroll=4)

    cmps = []
    for t in range(2):
        wait_all(t, b)
        to_rows(t)

        @pl.when(b + 1 < nb)
        def _():
            start_all(t, b + 1)

        p1 = jnp.zeros((n_blk, 2 * CMP_HID), f32)
        p2 = jnp.zeros((n_blk, 2 * CMP_HID), f32)
        for j in range(CMP_STRIDE // 2):
            xr = jnp.concatenate([rows[pl.ds(2 * j, n_blk, stride=CMP_STRIDE), :],
                                  rows[pl.ds(2 * j + 1, n_blk, stride=CMP_STRIDE), :]], axis=1)
            j2 = CMP_STRIDE // 2 + j
            p1 += jnp.dot((xr + pe_ref[t, j:j + 1, :]).astype(bf16), w1_ref[t, j], preferred_element_type=f32)
            p2 += jnp.dot((xr + pe_ref[t, j2:j2 + 1, :]).astype(bf16), w1_ref[t, j2], preferred_element_type=f32)
        hid = p1 + pltpu.roll(p2, n_blk - 1, 0)
        c = jnp.dot(_silu(hid).astype(bf16), w2_ref[t], preferred_element_type=f32)
        if t == 0:
            c = _group_rms(c, NSA_HD) * kg_ref[0:1, :]
        cmps.append(c)
    kc, vc = cmps

    q = q_ref[...]
    lane = _iota((8, n_blk), 1)
    cend = lane * CMP_STRIDE + (CMP_LEN - 1)
    ok = (cend <= q_pos) & (lane < n_cmp)
    c_i = _iota((n_blk, nj), 0)
    j_i = _iota((n_blk, nj), 1)
    ovl = ((c_i * CMP_STRIDE < j_i * SLC_BLOCK + SLC_BLOCK) & (c_i * CMP_STRIDE + CMP_LEN > j_i * SLC_BLOCK)
           & (j_i < n_slc) & (c_i < n_cmp)).astype(f32)
    jl = _iota((1, nj), 1)
    cur = q_pos // SLC_BLOCK
    valid = jl * SLC_BLOCK <= q_pos
    forced = (jl == 0) | (jl == cur) | (jl == cur - 1)
    ii = _iota((nj, nj), 0)
    jj = _iota((nj, nj), 1)
    kk = _iota((SLC_TOPN, nj), 0)
    j16 = _iota((SLC_TOPN, nj), 1)
    o_parts, idx_cols = [], []
    for g in range(NSA_KV):
        tabcols = [tab_ref[g, :, bk:bk + 1] for bk in range(N_BUCKETS)]
        lg = _dot_nt(_query_rows(q, g), kc[:, g * NSA_HD:(g + 1) * NSA_HD]) + _bias_chain(q_pos - cend, tabcols)
        p = _masked_softmax(lg, ok)
        o_parts.append(_dot(p, vc[:, g * NSA_HD:(g + 1) * NSA_HD]))
        p_sum = p[0:1] + p[1:2] + p[2:3] + p[3:4]
        imp = _dot(_row0(p_sum), ovl, HI)[0:1]
        score = jnp.where(valid, jnp.where(forced, 1e9, imp), -1e9)
        score = jnp.where(jl < n_slc, score, -3e9)
        s_c = jnp.sum(jnp.where(ii == jj, score, 0.0), axis=1, keepdims=True)
        rank_r = jnp.sum(((s_c > score) | ((s_c == score) & (ii < jj))).astype(f32), axis=0, keepdims=True)
        rank_c = jnp.sum(((score > s_c) | ((score == s_c) & (jj < ii))).astype(f32), axis=1, keepdims=True)
        sel_r = (rank_r < SLC_TOPN) & (jl < n_slc)
        sel_c = (rank_c < SLC_TOPN) & (ii[:, 0:1] < n_slc)
        pos_r = jnp.sum((sel_c & (ii < jj)).astype(f32), axis=0, keepdims=True)
        hit = (pos_r == kk.astype(f32)) & sel_r
        idx_cols.append(jnp.sum(jnp.where(hit, j16.astype(f32), 0.0), axis=1, keepdims=True))
    ocmp_ref[...] = _head_lanes(o_parts)
    l16 = _iota((SLC_TOPN, 128), 1)
    idx_ref[...] = (jnp.where(l16 == 0, idx_cols[0], 0.0) + jnp.where(l16 == 1, idx_cols[1], 0.0)).astype(i32)


def _nsa_dec_cmp(qn3, page_table, pool_k, pool_v, layer, lp):
    B = qn3.shape[0]
    n_pages = page_table.shape[1]
    full = lambda s: pl.BlockSpec(s, lambda b, pt: (0,) * len(s))
    return pl.pallas_call(
        functools.partial(_nsa_dec_cmp_kernel, layer=layer, n_pages=n_pages),
        grid_spec=pltpu.PrefetchScalarGridSpec(
            num_scalar_prefetch=1,
            grid=(B,),
            in_specs=[
                pl.BlockSpec((None, 1, D_NSA), lambda b, pt: (b, 0, 0)),
                full((2, CMP_LEN // 2, 256)), full((2, CMP_LEN // 2, 256, 2 * CMP_HID)),
                full((2, 2 * CMP_HID, 128)), full((3, 128)), full((NSA_KV, 8, N_BUCKETS)),
                pl.BlockSpec(memory_space=pl.ANY), pl.BlockSpec(memory_space=pl.ANY),
            ],
            out_specs=[pl.BlockSpec((None, 1, D_NSA), lambda b, pt: (b, 0, 0)),
                       pl.BlockSpec((None, SLC_TOPN, 128), lambda b, pt: (b, 0, 0))],
            scratch_shapes=[pltpu.VMEM((2, n_pages, PAGE, PAGE), f32), pltpu.VMEM((n_pages * PAGE, 128), f32),
                            pltpu.SemaphoreType.DMA((2,))],
        ),
        out_shape=[jax.ShapeDtypeStruct((B, 1, D_NSA), f32), jax.ShapeDtypeStruct((B, SLC_TOPN, 128), i32)],
        compiler_params=_params(("arbitrary",)),
        name="nsa_dec_cmp",
    )(page_table, qn3, lp["cmp_pe2"], lp["cmp_w1bd"], lp["cmp_w2bd"], lp["k_gain"], lp["tab8"], pool_k, pool_v)


def _nsa_dec_sel_kernel(pt_ref, idx_ref, q_ref, kv_ref, sm_ref, ocmp_ref, tab_ref, wk_ref, wv_ref, pk_ref, pv_ref,
                        y_ref, wko_ref, wvo_ref, kg, vg, sem, *, layer, n_pages):
    b = pl.program_id(0)
    n_past = n_pages * PAGE
    q_pos = n_past
    n_buf = wk_ref.shape[1]
    new_blk = n_past // SLC_BLOCK
    per_page = PAGE // SLC_BLOCK

    def blk_copy(pool, buf, g, s, sm_i):
        j = jnp.minimum(idx_ref[b, s, g], new_blk - 1)
        src = pool.at[layer, pt_ref[b, j // per_page], pl.ds(g * NSA_HD, NSA_HD)]
        return pltpu.make_async_copy(src, buf.at[g, :, pl.ds(s * PAGE, PAGE)], sem.at[sm_i])

    for g in range(NSA_KV):
        for s in range(SLC_TOPN):
            blk_copy(pk_ref, kg, g, s, 0).start()
            blk_copy(pv_ref, vg, g, s, 1).start()

    kv = kv_ref[...]
    q = q_ref[...]
    gate = jax.nn.sigmoid(sm_ref[...])
    ocmp = ocmp_ref[...]

    ii, jj = _iota((128, 128), 0), _iota((128, 128), 1)
    column = lambda off: jnp.sum(jnp.where(ii == jj, kv[:, off:off + 128], 0.0), axis=1, keepdims=True)
    last_lane = _iota((128, n_buf), 1) == n_buf - 1
    wk, wv = wk_ref[...], wv_ref[...]
    wko_ref[...] = jnp.where(last_lane, column(KV_WK), pltpu.roll(wk, n_buf - 1, 1))
    wvo_ref[...] = jnp.where(last_lane, column(KV_WV), pltpu.roll(wv, n_buf - 1, 1))

    def attend(qg, kt, vt, bias, ok, k_new, v_new, bias_new, ok_new):
        lg = jnp.where(ok, _dot(qg, kt) + bias, NEG)
        lg_new = jnp.where(ok_new, jnp.sum(qg * k_new, axis=-1, keepdims=True) + bias_new, NEG)
        m = jnp.maximum(jnp.max(lg, axis=-1, keepdims=True), lg_new)
        e = jnp.where(ok, jnp.exp(lg - m), 0.0)
        e_new = jnp.where(ok_new, jnp.exp(lg_new - m), 0.0)
        den = jnp.sum(e, axis=-1, keepdims=True) + e_new
        p, p_new = e / den, e_new / den
        return _dot_nt(p, vt) + p_new * v_new

    for g in range(NSA_KV):
        for s in range(SLC_TOPN):
            blk_copy(pk_ref, kg, g, s, 0).wait()
            blk_copy(pv_ref, vg, g, s, 1).wait()

    lw = _iota((8, n_buf), 1)
    d_w = q_pos - (n_past - n_buf + lw)
    ok_w = (d_w >= 0) & (d_w <= WINDOW)
    ls = _iota((8, SLC_TOPN * PAGE), 1)
    slot = ls // PAGE
    zero = jnp.zeros((8, 1), i32)
    o_slc, o_win = [], []
    for g in range(NSA_KV):
        tabcols = [tab_ref[g, :, bk:bk + 1] for bk in range(N_BUCKETS)]
        qg = _query_rows(q, g)
        gs = slice(g * NSA_HD, (g + 1) * NSA_HD)
        b_new = _bias_chain(zero, tabcols)
        blk = jnp.zeros(ls.shape, i32)
        n_new = jnp.zeros((8, 1), i32)
        for s in range(SLC_TOPN):
            j = idx_ref[b, s, g]
            blk = jnp.where(slot == s, j, blk)
            n_new = jnp.where(j == new_blk, n_new + 1, n_new)
        k_pos = (jnp.minimum(blk, new_blk - 1) // per_page) * PAGE + ls % PAGE
        ok_s = (blk < new_blk) & (k_pos // SLC_BLOCK == blk)
        o_slc.append(attend(qg, kg[g], vg[g], _bias_chain(q_pos - k_pos, tabcols), ok_s,
                            kv[:, KV_SK + g * NSA_HD:KV_SK + (g + 1) * NSA_HD],
                            kv[:, KV_SV + g * NSA_HD:KV_SV + (g + 1) * NSA_HD], b_new, n_new > 0))
        o_win.append(attend(qg, wk[gs, :], wv[gs, :], _bias_chain(d_w, tabcols), ok_w,
                            kv[:, KV_WK + g * NSA_HD:KV_WK + (g + 1) * NSA_HD],
                            kv[:, KV_WV + g * NSA_HD:KV_WV + (g + 1) * NSA_HD], b_new, zero == 0))
    gl = lambda t: gate[:, SM_G + t * NSA_HEADS:SM_G + (t + 1) * NSA_HEADS]
    wide = lambda gt: jnp.concatenate([jnp.broadcast_to(gt[:, h:h + 1], (1, NSA_HD)) for h in range(NSA_HEADS)], axis=1)
    y_ref[...] = wide(gl(0)) * ocmp + wide(gl(1)) * _head_lanes(o_slc) + wide(gl(2)) * _head_lanes(o_win)


def _nsa_dec_sel(qn3, kv6_3, proj3, ocmp, idx, page_table, win_k, win_v, pool_k, pool_v, layer, lp):
    B = qn3.shape[0]
    n_pages = page_table.shape[1]
    n_buf = win_k.shape[3]
    full = lambda s: pl.BlockSpec(s, lambda b, pt, ix: (0,) * len(s))
    row = lambda w, j: pl.BlockSpec((None, 1, w), lambda b, pt, ix: (b, 0, j))
    win = pl.BlockSpec((None, None, 128, n_buf), lambda b, pt, ix: (layer, b, 0, 0))
    return pl.pallas_call(
        functools.partial(_nsa_dec_sel_kernel, layer=layer, n_pages=n_pages),
        grid_spec=pltpu.PrefetchScalarGridSpec(
            num_scalar_prefetch=2,
            grid=(B,),
            in_specs=[row(D_NSA, 0), row(768, 0), row(128, SM_OFF // 128), row(D_NSA, 0),
                      full((NSA_KV, 8, N_BUCKETS)), win, win,
                      pl.BlockSpec(memory_space=pl.ANY), pl.BlockSpec(memory_space=pl.ANY)],
            out_specs=[row(D_NSA, 0),
                       pl.BlockSpec((None, 128, n_buf), lambda b, pt, ix: (b, 0, 0)),
                       pl.BlockSpec((None, 128, n_buf), lambda b, pt, ix: (b, 0, 0))],
            scratch_shapes=[pltpu.VMEM((NSA_KV, NSA_HD, SLC_TOPN * PAGE), f32),
                            pltpu.VMEM((NSA_KV, NSA_HD, SLC_TOPN * PAGE), f32),
                            pltpu.SemaphoreType.DMA((2,))],
        ),
        out_shape=[jax.ShapeDtypeStruct((B, 1, D_NSA), f32),
                   jax.ShapeDtypeStruct((B, 128, n_buf), f32), jax.ShapeDtypeStruct((B, 128, n_buf), f32)],
        compiler_params=_params(("arbitrary",)),
        name="nsa_dec_sel",
    )(page_table, idx, qn3, kv6_3, proj3, ocmp, lp["tab8"], win_k, win_v, pool_k, pool_v)


def _layer_params(i, p):
    pad128 = lambda v, off: jnp.zeros((1, 128), f32).at[0, off:off + v.shape[0]].set(v)
    return dict(
        ssd_conv_w=p["ssd_conv_w"][i], ssd_conv_b=p["ssd_conv_b"][i][None],
        ssd_dtb=pad128(p["ssd_dt_bias"][i], SM_DT), ssd_alog=pad128(p["ssd_a_log"][i], SM_DT),
        ssd_dexp=jnp.repeat(p["ssd_d"][i], SSD_HD)[None], ssd_norm=p["ssd_norm"][i][None],
        gdn_conv_w=p["gdn_conv_w"][i],
        gdn_dtb=pad128(p["gdn_dt_bias"][i], SM_A), gdn_alog=pad128(p["gdn_a_log"][i], SM_A),
        gdn_norm=p["gdn_norm"][i][None],
        q_gain=jnp.tile(p["nsa_q_norm"][i], 2)[None], k_gain=jnp.tile(p["nsa_k_norm"][i], (1, 2)),
        cmp_pe=p["nsa_cmp_pe"][i], cmp_w1=p["nsa_cmp_w1"][i], cmp_w2=p["nsa_cmp_w2"][i],
        cmp_pe2=jnp.tile(p["nsa_cmp_pe"][i], (1, 1, 2)).reshape(2, CMP_LEN // 2, 256),
        cmp_w1bd=_block_diag2(p["nsa_cmp_w1"][i].reshape(2, CMP_LEN, NSA_HD, CMP_HID)).astype(bf16)
        .reshape(2, CMP_LEN // 2, 256, 2 * CMP_HID),
        cmp_w2bd=_block_diag2(p["nsa_cmp_w2"][i]).astype(bf16),
        tab8=jnp.pad(p["rel_bias"].T.reshape(NSA_KV, NSA_REP, N_BUCKETS), ((0, 0), (0, 8 - NSA_REP), (0, 0))),
    )


def _block_diag2(w):
    z = jnp.zeros_like(w)
    return jnp.concatenate([jnp.concatenate([w, z], axis=-1), jnp.concatenate([z, w], axis=-1)], axis=-2)


def _reorder_w_in(w_in):
    cuts = [(1024, 2560), (2576, 4112), (0, 1024), (4112, 4624), (4632, 5144), (5144, 5912),
            (2560, 2576), (4624, 4632), (5912, 5936)]
    parts = [w_in[:, :, a:b] for a, b in cuts]
    used = sum(b - a for a, b in cuts)
    parts.append(jnp.zeros(w_in.shape[:2] + (PROJ_W - used,), w_in.dtype))
    return jnp.concatenate(parts, axis=-1).astype(bf16)


def _mix_prompt(x3, i, p, lp, w_r):
    B, L, _ = x3.shape
    n = B * L
    x = x3.reshape(n, D_MODEL)
    proj = _inproj(x, p["norm_mix"], w_r, i)
    proj3 = proj.reshape(B, L, PROJ_W)
    y_ssd, st_ssd, cst_ssd = _ssd_prompt(proj3, lp)
    y_gdn, st_gdn, cst_gdn = _gdn_prompt(proj3, lp)
    qt, kv6_3, kv6t, gt = _prep_t(proj3, lp["q_gain"], lp["k_gain"])
    kc, vct = _cmp_prompt(kv6_3, lp)
    y_nsa = _nsa_prompt(qt, kv6_3, kv6t, kc, vct, gt, p["rel_bias"])
    x = _outproj(x, y_ssd.reshape(n, D_SSD), y_gdn.reshape(n, D_GDN), y_nsa.reshape(n, D_NSA), p["w_out"], i)
    rows = lambda off: jnp.transpose(kv6t[:, off:off + 128].reshape(B, NSA_KV, NSA_HD, L), (0, 3, 1, 2))
    keep = min(WINDOW, L)
    state = (st_ssd, cst_ssd, st_gdn, cst_gdn, rows(KV_CK), rows(KV_CV), rows(KV_SK), rows(KV_SV),
             rows(KV_WK)[:, L - keep:], rows(KV_WV)[:, L - keep:])
    return x.reshape(B, L, D_MODEL), state


def _mix_decode(x, i, p, lp, w_r, st):
    B = x.shape[0]
    proj = _inproj(x, p["norm_mix"], w_r, i)
    proj3 = proj.reshape(B, 1, PROJ_W)
    y_ssd, st_ssd, cst_ssd = _ssd_decode(proj3, st["ssd_conv"], st["ssd"], i, lp)
    y_gdn, st_gdn, cst_gdn = _gdn_decode(proj3, st["gdn_conv"], st["gdn"], i, lp)
    qn, kv6 = _prep(proj, lp["q_gain"], lp["k_gain"])
    qn3, kv6_3 = qn.reshape(B, 1, D_NSA), kv6.reshape(B, 1, 768)
    o_cmp, idx = _nsa_dec_cmp(qn3, st["page_table"], st["cmp_k"], st["cmp_v"], i, lp)
    y_nsa, win_k, win_v = _nsa_dec_sel(qn3, kv6_3, proj3, o_cmp, idx[:, :, :NSA_KV], st["page_table"],
                                       st["win_k"], st["win_v"], st["slc_k"], st["slc_v"], i, lp)
    x = _outproj(x, y_ssd.reshape(B, D_SSD), y_gdn.reshape(B, D_GDN), y_nsa.reshape(B, D_NSA), p["w_out"], i)
    rows = lambda off: kv6[:, off:off + 128].reshape(B, 1, NSA_KV, NSA_HD)
    n_buf = win_k.shape[2]
    buf = lambda w: jnp.transpose(w.reshape(B, NSA_KV, NSA_HD, n_buf), (0, 3, 1, 2))
    state = (st_ssd, cst_ssd, st_gdn, cst_gdn, rows(KV_CK), rows(KV_CV), rows(KV_SK), rows(KV_SV),
             buf(win_k), buf(win_v))
    return x, state


def kernel(x_prompt, x_sample, state_ssd, state_ssd_conv, state_gdn, state_gdn_conv, cache_cmp_k, cache_cmp_v,
           cache_slc_k, cache_slc_v, cache_win_k, cache_win_v, page_table, rel_bias, norm_ffn1, w_ffn1_gate,
           w_ffn1_up, w_ffn1_down, norm_mix, w_in, ssd_conv_w, ssd_conv_b, ssd_dt_bias, ssd_a_log, ssd_d, ssd_norm,
           gdn_conv_w, gdn_dt_bias, gdn_a_log, gdn_norm, nsa_q_norm, nsa_k_norm, nsa_cmp_pe, nsa_cmp_w1,
           nsa_cmp_w2, w_out, norm_ffn2, w_ffn2_gate, w_ffn2_up, w_ffn2_down):
    bp, lp_len, _ = x_prompt.shape
    bs = x_sample.shape[0]
    gain3 = lambda g: g.reshape(DEPTH, 1, D_MODEL)
    p = dict(rel_bias=rel_bias, norm_mix=gain3(norm_mix), w_out=w_out, ssd_conv_w=ssd_conv_w, ssd_conv_b=ssd_conv_b,
             ssd_dt_bias=ssd_dt_bias, ssd_a_log=ssd_a_log, ssd_d=ssd_d, ssd_norm=ssd_norm, gdn_conv_w=gdn_conv_w,
             gdn_dt_bias=gdn_dt_bias, gdn_a_log=gdn_a_log, gdn_norm=gdn_norm, nsa_q_norm=nsa_q_norm,
             nsa_k_norm=nsa_k_norm, nsa_cmp_pe=nsa_cmp_pe, nsa_cmp_w1=nsa_cmp_w1, nsa_cmp_w2=nsa_cmp_w2)
    n1, n2 = gain3(norm_ffn1), gain3(norm_ffn2)
    chan_row = lambda c: jnp.transpose(c, (0, 1, 3, 4, 2)).reshape(c.shape[:2] + (NSA_KV * NSA_HD, c.shape[2]))
    st = dict(ssd=state_ssd, ssd_conv=state_ssd_conv, gdn=state_gdn, gdn_conv=state_gdn_conv,
              cmp_k=chan_row(cache_cmp_k), cmp_v=chan_row(cache_cmp_v), slc_k=chan_row(cache_slc_k),
              slc_v=chan_row(cache_slc_v), win_k=chan_row(cache_win_k), win_v=chan_row(cache_win_v),
              page_table=page_table)
    w_r = _reorder_w_in(w_in)

    hp = x_prompt.reshape(bp * lp_len, D_MODEL)
    hs = x_sample.reshape(bs, D_MODEL)
    outs_p, outs_s = [], []
    for i in range(DEPTH):
        lp = _layer_params(i, p)
        hp = _ffn(hp, n1, w_ffn1_gate, w_ffn1_up, w_ffn1_down, i)
        hs = _ffn(hs, n1, w_ffn1_gate, w_ffn1_up, w_ffn1_down, i)
        hp3, st_p = _mix_prompt(hp.reshape(bp, lp_len, D_MODEL), i, p, lp, w_r)
        hs, st_s = _mix_decode(hs, i, p, lp, w_r, st)
        hp = _ffn(hp3.reshape(bp * lp_len, D_MODEL), n2, w_ffn2_gate, w_ffn2_up, w_ffn2_down, i)
        hs = _ffn(hs, n2, w_ffn2_gate, w_ffn2_up, w_ffn2_down, i)
        outs_p.append(st_p)
        outs_s.append(st_s)
    stack = lambda outs: [jnp.stack(t) for t in zip(*outs)]
    return (hp.reshape(bp, lp_len, D_MODEL), hs.reshape(bs, 1, D_MODEL), *stack(outs_p), *stack(outs_s))
```

```python
import functools
import math

import numpy as np
import jax
import jax.numpy as jnp
from jax import lax
from jax.experimental import pallas as pl
from jax.experimental.pallas import tpu as pltpu

f32 = jnp.float32
bf16 = jnp.bfloat16
i32 = jnp.int32
HI = lax.Precision.HIGHEST

D_MODEL = 2048
DEPTH = 4
PAGE = 128
D_SSD = 1024
SSD_HD = 64
SSD_HEADS = 16
SSD_GROUPS = 2
SSD_STATE = 128
SSD_CHUNK = 128
D_GDN = 512
GDN_HD = 128
GDN_HEADS = 4
GDN_CHUNK = 64
GDN_STEP_CHUNKS = 4
D_NSA = 512
NSA_HD = 64
NSA_HEADS = 8
NSA_KV = 2
NSA_REP = 4
CMP_STRIDE = 16
CMP_LEN = 32
CMP_HID = 128
SLC_BLOCK = 64
SLC_TOPN = 16
WINDOW = 512
N_BUCKETS = 32
MAX_DISTANCE = 128
CONV_W = 4
D_FF = 5632
SSD_CONV_CH = 1536
GDN_CONV_CH = 1536
EPS = 1e-6
NEG = -1e30

XBC_OFF = 0
QKV_OFF = 1536
ZS_OFF = 3072
ZG_OFF = 4096
QN_OFF = 4608
KV_OFF = 5120
SM_OFF = 5888
PROJ_W = 6144
SM_DT, SM_A, SM_B, SM_G = 0, 16, 20, 24
KV_CK, KV_CV, KV_SK, KV_SV, KV_WK, KV_WV = 0, 128, 256, 384, 512, 640

VMEM_LIMIT = 56 * 1024 * 1024


def _bucket_thresholds():
    exact = N_BUCKETS // 2
    d = np.arange(0, 4 * MAX_DISTANCE)
    nf = np.maximum(d, 1).astype(np.float32)
    large = exact + (np.log(nf / np.float32(exact)) / np.float32(math.log(MAX_DISTANCE / exact))
                     * np.float32(N_BUCKETS - exact)).astype(np.int32)
    bucket = np.where(d < exact, d, np.minimum(large, N_BUCKETS - 1))
    return [int(np.argmax(bucket >= b)) for b in range(N_BUCKETS)]


_THR = _bucket_thresholds()
BIAS_FAR = _THR[N_BUCKETS - 1]


def _bias_chain(d, tabcols):
    v = jnp.broadcast_to(tabcols[N_BUCKETS - 1], d.shape)
    for b in range(N_BUCKETS - 2, -1, -1):
        v = jnp.where(d < _THR[b + 1], tabcols[b], v)
    return v


def _silu(x):
    return x * jax.nn.sigmoid(x)


def _softplus(x):
    return jnp.maximum(x, 0.0) + jnp.log1p(jnp.exp(-jnp.abs(x)))


def _dot(a, b, prec=None):
    if prec is None:
        a, b = a.astype(bf16), b.astype(bf16)
    return jnp.dot(a, b, preferred_element_type=f32, precision=prec)


def _dot_nt(a, b, prec=None):
    if prec is None:
        a, b = a.astype(bf16), b.astype(bf16)
    return lax.dot_general(a, b, (((1,), (1,)), ((), ())), preferred_element_type=f32, precision=prec)


def _dot_tn(a, b, prec=None):
    if prec is None:
        a, b = a.astype(bf16), b.astype(bf16)
    return lax.dot_general(a, b, (((0,), (0,)), ((), ())), preferred_element_type=f32, precision=prec)


def _split2(a):
    hi = a.astype(bf16)
    return hi, (a - hi.astype(f32)).astype(bf16)


def _dot3(a2, b2):
    d = lambda x, y: jnp.dot(x, y, preferred_element_type=f32)
    return d(a2[0], b2[0]) + (d(a2[0], b2[1]) + d(a2[1], b2[0]))


def _iota(shape, dim):
    return lax.broadcasted_iota(i32, shape, dim)


def _group_ones(n, width):
    return (_iota((n, n), 0) // width == _iota((n, n), 1) // width).astype(f32)


def _group_rms(x, width):
    ss = _dot(x * x, _group_ones(x.shape[1], width), HI)
    return x * lax.rsqrt(ss * (1.0 / width) + EPS)


def _params(sem):
    return pltpu.CompilerParams(dimension_semantics=sem, vmem_limit_bytes=VMEM_LIMIT)


FFN_EXTRA = 16


def _ffn_kernel(x_ref, xs_ref, g_ref, wg_ref, wu_ref, wd_ref, o_ref, os_ref, h_ref):
    tm, ns = x_ref.shape[0], xs_ref.shape[0]

    @pl.when(pl.program_id(1) == 0)
    def _():
        rms = lambda x: x * lax.rsqrt(jnp.mean(x * x, axis=-1, keepdims=True) + EPS) * g_ref[...]
        x, xs = x_ref[...], xs_ref[...]
        h_ref[0:tm, :] = rms(x).astype(bf16)
        h_ref[tm:tm + FFN_EXTRA, :] = jnp.concatenate(
            [rms(xs), jnp.zeros((FFN_EXTRA - ns, D_MODEL), f32)], axis=0).astype(bf16)
        o_ref[...] = x
        os_ref[...] = xs

    h = h_ref[...]
    a = jnp.dot(h, wg_ref[...].astype(bf16), preferred_element_type=f32)
    u = jnp.dot(h, wu_ref[...].astype(bf16), preferred_element_type=f32)
    y = 0.5 * _dot(_silu(a) * u, wd_ref[...])
    o_ref[...] += y[0:tm]
    os_ref[...] += y[tm:tm + ns]


def _ffn(x, xs, gain, wg, wu, wd, layer):
    n, ns = x.shape[0], xs.shape[0]
    tm, tf = 1024, 256
    return pl.pallas_call(
        _ffn_kernel,
        grid=(n // tm, D_FF // tf),
        in_specs=[
            pl.BlockSpec((tm, D_MODEL), lambda i, j: (i, 0)),
            pl.BlockSpec((ns, D_MODEL), lambda i, j: (0, 0)),
            pl.BlockSpec((None, 1, D_MODEL), lambda i, j: (layer, 0, 0)),
            pl.BlockSpec((None, D_MODEL, tf), lambda i, j: (layer, 0, j)),
            pl.BlockSpec((None, D_MODEL, tf), lambda i, j: (layer, 0, j)),
            pl.BlockSpec((None, tf, D_MODEL), lambda i, j: (layer, j, 0)),
        ],
        out_specs=[pl.BlockSpec((tm, D_MODEL), lambda i, j: (i, 0)),
                   pl.BlockSpec((ns, D_MODEL), lambda i, j: (0, 0))],
        out_shape=[jax.ShapeDtypeStruct((n, D_MODEL), f32), jax.ShapeDtypeStruct((ns, D_MODEL), f32)],
        scratch_shapes=[pltpu.VMEM((tm + FFN_EXTRA, D_MODEL), bf16)],
        compiler_params=_params(("arbitrary", "arbitrary")),
        name="ffn",
    )(x, xs, gain, wg, wu, wd)


def _inproj_kernel(x_ref, g_ref, w_ref, o_ref, h_ref):
    @pl.when(pl.program_id(1) == 0)
    def _():
        x = x_ref[...]
        h = x * lax.rsqrt(jnp.mean(x * x, axis=-1, keepdims=True) + EPS) * g_ref[...]
        h_ref[...] = h.astype(bf16)

    o_ref[...] = _dot_nt(h_ref[...], w_ref[...])


def _inproj(x, gain, w_r, layer):
    n = x.shape[0]
    tm = min(n, 1024)
    tn = 512
    return pl.pallas_call(
        _inproj_kernel,
        grid=(n // tm, PROJ_W // tn),
        in_specs=[
            pl.BlockSpec((tm, D_MODEL), lambda i, j: (i, 0)),
            pl.BlockSpec((None, 1, D_MODEL), lambda i, j: (layer, 0, 0)),
            pl.BlockSpec((None, tn, D_MODEL), lambda i, j: (layer, j, 0)),
        ],
        out_specs=pl.BlockSpec((tm, tn), lambda i, j: (i, j)),
        out_shape=jax.ShapeDtypeStruct((n, PROJ_W), f32),
        scratch_shapes=[pltpu.VMEM((tm, D_MODEL), bf16)],
        compiler_params=_params(("parallel", "arbitrary")),
        name="inproj",
    )(x, gain, w_r)


def _outproj_kernel(x_ref, ys_ref, yg_ref, yn_ref, ws_ref, wg_ref, wn_ref, o_ref):
    acc = _dot(ys_ref[...], ws_ref[...])
    acc += _dot(yg_ref[...], wg_ref[...])
    acc += _dot(yn_ref[...], wn_ref[...])
    o_ref[...] = x_ref[...] + acc


def _outproj(x, ys, yg, yn, w_out, layer):
    n = x.shape[0]
    tm = min(n, 1024)
    tn = 512
    return pl.pallas_call(
        _outproj_kernel,
        grid=(n // tm, D_MODEL // tn),
        in_specs=[
            pl.BlockSpec((tm, tn), lambda i, j: (i, j)),
            pl.BlockSpec((tm, D_SSD), lambda i, j: (i, 0)),
            pl.BlockSpec((tm, D_GDN), lambda i, j: (i, 0)),
            pl.BlockSpec((tm, D_NSA), lambda i, j: (i, 0)),
            pl.BlockSpec((None, D_SSD, tn), lambda i, j: (layer, 0, j)),
            pl.BlockSpec((None, D_GDN, tn), lambda i, j: (layer, 2, j)),
            pl.BlockSpec((None, D_NSA, tn), lambda i, j: (layer, 3, j)),
        ],
        out_specs=pl.BlockSpec((tm, tn), lambda i, j: (i, j)),
        out_shape=jax.ShapeDtypeStruct((n, D_MODEL), f32),
        compiler_params=_params(("parallel", "arbitrary")),
        name="outproj",
    )(x, ys, yg, yn, w_out, w_out, w_out)


def _prep_kernel(q_ref, kv_ref, qg_ref, kg_ref, qn_ref, kv6_ref):
    for s in range(D_NSA // 128):
        sl = slice(s * 128, (s + 1) * 128)
        qn_ref[:, sl] = _group_rms(q_ref[:, sl], NSA_HD) * qg_ref[...]
    kv = kv_ref[...]
    kv6_ref[...] = kv[:, :6 * 128]
    kv6_ref[:, KV_SK:KV_SK + 128] = _group_rms(kv[:, KV_SK:KV_SK + 128], NSA_HD) * kg_ref[1:2, :]
    kv6_ref[:, KV_WK:KV_WK + 128] = _group_rms(kv[:, KV_WK:KV_WK + 128], NSA_HD) * kg_ref[2:3, :]


def _prep(proj, qgain, kgain):
    n = proj.shape[0]
    tm = min(n, 512)
    return pl.pallas_call(
        _prep_kernel,
        grid=(n // tm,),
        in_specs=[
            pl.BlockSpec((tm, D_NSA), lambda i: (i, QN_OFF // D_NSA)),
            pl.BlockSpec((tm, 1024), lambda i: (i, KV_OFF // 1024)),
            pl.BlockSpec((1, 128), lambda i: (0, 0)),
            pl.BlockSpec((3, 128), lambda i: (0, 0)),
        ],
        out_specs=[pl.BlockSpec((tm, D_NSA), lambda i: (i, 0)),
                   pl.BlockSpec((tm, 768), lambda i: (i, 0))],
        out_shape=[jax.ShapeDtypeStruct((n, D_NSA), f32), jax.ShapeDtypeStruct((n, 768), f32)],
        compiler_params=_params(("parallel",)),
        name="nsa_prep",
    )(proj, proj, qgain, kgain)


def _ssd_prompt_kernel(xbc_ref, z_ref, sm_ref, cw_ref, cb_ref, dtb_ref, alog_ref, dexp_ref, nrm_ref,
                       y_ref, st_ref, cst_ref, xbuf, hst, ybuf):
    c = pl.program_id(1)
    cl = SSD_CHUNK

    @pl.when(c == 0)
    def _():
        xbuf[0:8, :] = jnp.zeros((8, SSD_CONV_CH), f32)
        hst[...] = jnp.zeros(hst.shape, f32)

    x = xbc_ref[...]
    xbuf[8:8 + cl, :] = x
    conv = cw_ref[3:4, :] * x
    for k in range(1, CONV_W):
        conv += cw_ref[3 - k:4 - k, :] * xbuf[pl.ds(8 - k, cl), :]
    xbuf[0:8, :] = x[cl - 8:cl, :]
    xc = _silu(conv + cb_ref[...])
    xs = xc[:, :D_SSD]

    dt = jnp.where(_iota((cl, 128), 1) < SSD_HEADS, _softplus(sm_ref[...] + dtb_ref[...]), 0.0)
    da = dt * (-jnp.exp(alog_ref[...]))
    row, col = _iota((cl, cl), 0), _iota((cl, cl), 1)
    causal = row >= col
    cum = _dot(causal.astype(f32), da, HI)
    cum_t = cum.T
    last = cum[cl - 1:cl, :]
    e_last = jnp.exp(last)
    spread = (_iota((128, D_SSD), 1) // SSD_HD == _iota((128, D_SSD), 0)).astype(f32)
    coef = _dot(jnp.concatenate([dt, jnp.exp(last - cum), jnp.exp(cum)], axis=0), spread, HI)
    xdt = xs * coef[0:cl]
    xw = xdt * coef[cl:2 * cl]
    e_cum = coef[2 * cl:3 * cl]

    hpg = SSD_HEADS // SSD_GROUPS
    gw = hpg * SSD_HD
    for g in range(SSD_GROUPS):
        bm = xc[:, D_SSD + g * SSD_STATE:D_SSD + (g + 1) * SSD_STATE]
        cm = xc[:, D_SSD + (SSD_GROUPS + g) * SSD_STATE:D_SSD + (SSD_GROUPS + g + 1) * SSD_STATE]
        cb = _dot_nt(cm, bm)
        h0 = hst[g]
        y_off = _dot_nt(cm, h0) * e_cum[:, g * gw:(g + 1) * gw]
        s_new = _dot_tn(xw[:, g * gw:(g + 1) * gw], bm)
        for r in range(hpg):
            h = g * hpg + r
            seg = cum[:, h:h + 1] - cum_t[h:h + 1, :]
            decay = jnp.where(causal, jnp.exp(jnp.where(causal, seg, 0.0)), 0.0)
            rs = slice(r * SSD_HD, (r + 1) * SSD_HD)
            ybuf[:, h * SSD_HD:(h + 1) * SSD_HD] = (_dot(cb * decay, xdt[:, h * SSD_HD:(h + 1) * SSD_HD])
                                                    + y_off[:, rs])
            hst[g, rs, :] = e_last[:, h:h + 1] * h0[rs, :] + s_new[rs, :]

    y = (ybuf[...] + dexp_ref[...] * xs) * _silu(z_ref[...])
    gw = D_SSD // SSD_GROUPS
    for g in range(SSD_GROUPS):
        yg = y[:, g * gw:(g + 1) * gw]
        yg = yg * lax.rsqrt(jnp.mean(yg * yg, axis=-1, keepdims=True) + EPS)
        y_ref[:, g * gw:(g + 1) * gw] = (yg * nrm_ref[:, g * gw:(g + 1) * gw]).astype(y_ref.dtype)

    @pl.when(c == pl.num_programs(1) - 1)
    def _():
        st_ref[...] = hst[...]
        cst_ref[...] = x[cl - (CONV_W - 1):cl, :]


def _ssd_prompt(proj3, lp):
    B, L, _ = proj3.shape
    cl = SSD_CHUNK
    gh = SSD_HEADS // SSD_GROUPS * SSD_HD
    vec = lambda w: pl.BlockSpec((1, w), lambda b, c: (0, 0))
    y, st, cst = pl.pallas_call(
        _ssd_prompt_kernel,
        grid=(B, L // cl),
        in_specs=[
            pl.BlockSpec((None, cl, SSD_CONV_CH), lambda b, c: (b, c, XBC_OFF // SSD_CONV_CH)),
            pl.BlockSpec((None, cl, D_SSD), lambda b, c: (b, c, ZS_OFF // D_SSD)),
            pl.BlockSpec((None, cl, 128), lambda b, c: (b, c, SM_OFF // 128)),
            pl.BlockSpec((CONV_W, SSD_CONV_CH), lambda b, c: (0, 0)),
            vec(SSD_CONV_CH), vec(128), vec(128), vec(D_SSD), vec(D_SSD),
        ],
        out_specs=[
            pl.BlockSpec((None, cl, D_SSD), lambda b, c: (b, c, 0)),
            pl.BlockSpec((None, SSD_GROUPS, gh, SSD_STATE), lambda b, c: (b, 0, 0, 0)),
            pl.BlockSpec((None, CONV_W - 1, SSD_CONV_CH), lambda b, c: (b, 0, 0)),
        ],
        out_shape=[
            jax.ShapeDtypeStruct((B, L, D_SSD), bf16),
            jax.ShapeDtypeStruct((B, SSD_GROUPS, gh, SSD_STATE), f32),
            jax.ShapeDtypeStruct((B, CONV_W - 1, SSD_CONV_CH), f32),
        ],
        scratch_shapes=[
            pltpu.VMEM((8 + cl, SSD_CONV_CH), f32),
            pltpu.VMEM((SSD_GROUPS, gh, SSD_STATE), f32),
            pltpu.VMEM((cl, D_SSD), f32),
        ],
        compiler_params=_params(("parallel", "arbitrary")),
        name="ssd_prompt",
    )(proj3, proj3, proj3, lp["ssd_conv_w"], lp["ssd_conv_b"], lp["ssd_dtb"], lp["ssd_alog"],
      lp["ssd_dexp"], lp["ssd_norm"])
    return y, st.reshape(B, SSD_HEADS, SSD_HD, SSD_STATE), cst


def _gdn_prompt_kernel(qkv_ref, z_ref, sm_ref, cw_ref, dtb_ref, alog_ref, nrm_ref,
                       y_ref, st_ref, cst_ref, xbuf, sst):
    c = pl.program_id(1)
    cl = GDN_CHUNK
    nch = GDN_STEP_CHUNKS
    tl = nch * cl

    @pl.when(c == 0)
    def _():
        xbuf[0:8, :] = jnp.zeros((8, GDN_CONV_CH), f32)
        sst[...] = jnp.zeros(sst.shape, f32)

    x = qkv_ref[...]
    xbuf[8:8 + tl, :] = x
    conv = cw_ref[3:4, :] * x
    for k in range(1, CONV_W):
        conv += cw_ref[3 - k:4 - k, :] * xbuf[pl.ds(8 - k, tl), :]
    xbuf[0:8, :] = x[tl - 8:tl, :]
    xc = _silu(conv)

    sm = sm_ref[...]
    beta_all = jax.nn.sigmoid(sm)
    g_all = -jnp.exp(alog_ref[...]) * _softplus(sm + dtb_ref[...])
    row, col = _iota((cl, cl), 0), _iota((cl, cl), 1)
    incl = row >= col
    strict = row > col
    eye = (row == col).astype(f32)
    rt, ct = _iota((tl, tl), 0), _iota((tl, tl), 1)
    cum = _dot(((rt >= ct) & (rt // cl == ct // cl)).astype(f32), g_all, HI)
    cum_t = cum.T

    units = [(ci, h) for ci in range(nch) for h in range(GDN_HEADS)]
    q, k, v, beta, cum_c, decay = {}, {}, {}, {}, {}, {}
    for u in units:
        ci, h = u
        rows = slice(ci * cl, (ci + 1) * cl)
        qh = xc[rows, h * GDN_HD:(h + 1) * GDN_HD]
        kh = xc[rows, D_GDN + h * GDN_HD:D_GDN + (h + 1) * GDN_HD]
        v[u] = xc[rows, 2 * D_GDN + h * GDN_HD:2 * D_GDN + (h + 1) * GDN_HD]
        q[u] = qh * lax.rsqrt(jnp.sum(qh * qh, axis=-1, keepdims=True) + EPS) * (GDN_HD ** -0.5)
        k[u] = kh * lax.rsqrt(jnp.sum(kh * kh, axis=-1, keepdims=True) + EPS)
        beta[u] = beta_all[rows, SM_B + h:SM_B + h + 1]
        cum_c[u] = cum[rows, SM_A + h:SM_A + h + 1]
        seg = cum_c[u] - cum_t[SM_A + h:SM_A + h + 1, rows]
        decay[u] = jnp.where(incl, jnp.exp(jnp.where(incl, seg, 0.0)), 0.0)
    pw = {u: -jnp.where(strict, beta[u] * _dot_nt(k[u], k[u]) * decay[u], 0.0) for u in units}
    inv = {u: eye + pw[u] for u in units}
    pw2 = {u: _split2(pw[u]) for u in units}
    for _ in range(5):
        pw2 = {u: _split2(_dot3(pw2[u], pw2[u])) for u in units}
        inv = {u: inv[u] + _dot3(_split2(inv[u]), pw2[u]) for u in units}
    rhs = {u: jnp.concatenate([beta[u] * v[u], (beta[u] * jnp.exp(cum_c[u])) * k[u]], axis=1) for u in units}
    sol = {u: _dot3(_split2(inv[u]), _split2(rhs[u])) for u in units}
    uu = {u: sol[u][:, :GDN_HD] for u in units}
    ww = {u: sol[u][:, GDN_HD:] for u in units}
    qk = {u: _dot_nt(q[u], k[u]) * decay[u] for u in units}

    s = [sst[h] for h in range(GDN_HEADS)]
    for ci in range(nch):
        rows = slice(ci * cl, (ci + 1) * cl)
        hs = [(ci, h) for h in range(GDN_HEADS)]
        v_new = {u: uu[u] - _dot(ww[u], s[u[1]]) for u in hs}
        o = {u: _dot(q[u] * jnp.exp(cum_c[u]), s[u[1]]) + _dot(qk[u], v_new[u]) for u in hs}
        for u in hs:
            h = u[1]
            last = cum[ci * cl + cl - 1:ci * cl + cl, SM_A + h:SM_A + h + 1]
            s[h] = jnp.exp(last) * s[h] + _dot_tn(k[u] * jnp.exp(last - cum_c[u]), v_new[u])
        for u in hs:
            sl = slice(u[1] * GDN_HD, (u[1] + 1) * GDN_HD)
            on = o[u] * lax.rsqrt(jnp.mean(o[u] * o[u], axis=-1, keepdims=True) + EPS)
            y_ref[rows, sl] = (on * nrm_ref[...] * _silu(z_ref[rows, sl])).astype(y_ref.dtype)
    for h in range(GDN_HEADS):
        sst[h] = s[h]

    @pl.when(c == pl.num_programs(1) - 1)
    def _():
        st_ref[...] = sst[...]
        cst_ref[...] = x[tl - (CONV_W - 1):tl, :]


def _gdn_prompt(proj3, lp):
    B, L, _ = proj3.shape
    cl = GDN_CHUNK * GDN_STEP_CHUNKS
    vec = lambda w: pl.BlockSpec((1, w), lambda b, c: (0, 0))
    return pl.pallas_call(
        _gdn_prompt_kernel,
        grid=(B, L // cl),
        in_specs=[
            pl.BlockSpec((None, cl, GDN_CONV_CH), lambda b, c: (b, c, QKV_OFF // GDN_CONV_CH)),
            pl.BlockSpec((None, cl, D_GDN), lambda b, c: (b, c, ZG_OFF // D_GDN)),
            pl.BlockSpec((None, cl, 128), lambda b, c: (b, c, SM_OFF // 128)),
            pl.BlockSpec((CONV_W, GDN_CONV_CH), lambda b, c: (0, 0)),
            vec(128), vec(128), vec(GDN_HD),
        ],
        out_specs=[
            pl.BlockSpec((None, cl, D_GDN), lambda b, c: (b, c, 0)),
            pl.BlockSpec((None, GDN_HEADS, GDN_HD, GDN_HD), lambda b, c: (b, 0, 0, 0)),
            pl.BlockSpec((None, CONV_W - 1, GDN_CONV_CH), lambda b, c: (b, 0, 0)),
        ],
        out_shape=[
            jax.ShapeDtypeStruct((B, L, D_GDN), bf16),
            jax.ShapeDtypeStruct((B, GDN_HEADS, GDN_HD, GDN_HD), f32),
            jax.ShapeDtypeStruct((B, CONV_W - 1, GDN_CONV_CH), f32),
        ],
        scratch_shapes=[
            pltpu.VMEM((8 + cl, GDN_CONV_CH), f32),
            pltpu.VMEM((GDN_HEADS, GDN_HD, GDN_HD), f32),
        ],
        compiler_params=_params(("parallel", "arbitrary")),
        name="gdn_prompt",
    )(proj3, proj3, proj3, lp["gdn_conv_w"], lp["gdn_dtb"], lp["gdn_alog"], lp["gdn_norm"])


def _cmp_prompt_kernel(k_ref, v_ref, pe_ref, w1_ref, w2_ref, kg_ref, kc_ref, vct_ref):
    nb = kc_ref.shape[0]
    for t, src in enumerate((k_ref, v_ref)):
        outs = []
        p1 = [jnp.zeros((nb, CMP_HID), f32) for _ in range(NSA_KV)]
        p2 = [jnp.zeros((nb, CMP_HID), f32) for _ in range(NSA_KV)]
        for l in range(CMP_STRIDE):
            xr = src[pl.ds(l, nb, stride=CMP_STRIDE), :]
            l2 = CMP_STRIDE + l
            for g in range(NSA_KV):
                xg = xr[:, g * NSA_HD:(g + 1) * NSA_HD]
                p1[g] += _dot(xg + pe_ref[t, l:l + 1, :], w1_ref[t, l * NSA_HD:(l + 1) * NSA_HD, :])
                p2[g] += _dot(xg + pe_ref[t, l2:l2 + 1, :], w1_ref[t, l2 * NSA_HD:(l2 + 1) * NSA_HD, :])
        for g in range(NSA_KV):
            hid = p1[g] + pltpu.roll(p2[g], nb - 1, 0)
            cmp = _dot(_silu(hid), w2_ref[t])
            if t == 0:
                cmp = cmp * lax.rsqrt(jnp.mean(cmp * cmp, axis=-1, keepdims=True) + EPS) * kg_ref[0:1, 0:NSA_HD]
            outs.append(cmp)
        both = jnp.concatenate(outs, axis=1)
        if t == 0:
            kc_ref[...] = both
        else:
            vct_ref[...] = both.T


def _cmp_prompt(kv6_3, lp):
    B, L, _ = kv6_3.shape
    nb = L // CMP_STRIDE
    full = lambda s: pl.BlockSpec(s, lambda b: (0,) * len(s))
    return pl.pallas_call(
        _cmp_prompt_kernel,
        grid=(B,),
        in_specs=[
            pl.BlockSpec((None, L, 128), lambda b: (b, 0, KV_CK // 128)),
            pl.BlockSpec((None, L, 128), lambda b: (b, 0, KV_CV // 128)),
            full((2, CMP_LEN, NSA_HD)), full((2, CMP_LEN * NSA_HD, CMP_HID)), full((2, CMP_HID, NSA_HD)),
            full((3, 128)),
        ],
        out_specs=[pl.BlockSpec((None, nb, 128), lambda b: (b, 0, 0)),
                   pl.BlockSpec((None, 128, nb), lambda b: (b, 0, 0))],
        out_shape=[jax.ShapeDtypeStruct((B, nb, 128), f32), jax.ShapeDtypeStruct((B, 128, nb), f32)],
        compiler_params=_params(("parallel",)),
        name="nsa_cmp_prompt",
    )(kv6_3, kv6_3, lp["cmp_pe"], lp["cmp_w1"], lp["cmp_w2"], lp["k_gain"])


def _nsa_prompt_kernel(rb_ref, qt_ref, ks_ref, kw_ref, vst_ref, vwt_ref, kc_ref, vct_ref, gt_ref,
                       y_ref, tb_ref, tc_ref, ext_ref, selx_ref):
    qi = pl.program_id(1)
    L = ks_ref.shape[0]
    tq = 128
    nl = NSA_REP * tq
    ncmp = kc_ref.shape[0]
    n_slc = L // SLC_BLOCK
    last = N_BUCKETS - 1

    @pl.when(qi == 0)
    def _():
        sub = _iota((128, tq), 0)
        qo = _iota((128, tq), 1)
        d_diag = qo - sub
        c_rel = jnp.where(sub < 64, sub, sub - 128)
        d_cmp = qo - CMP_STRIDE * c_rel - (CMP_LEN - 1)
        for g in range(NSA_KV):
            for r in range(NSA_REP):
                h = g * NSA_REP + r
                tab = [rb_ref[b, h] - rb_ref[last, h] for b in range(N_BUCKETS)]
                lanes = slice(r * tq, (r + 1) * tq)
                tb_ref[g, 0, :, lanes] = jnp.zeros((128, tq), f32)
                tb_ref[g, 1, :, lanes] = _bias_chain(d_diag + 128, tab)
                tb_ref[g, 2, :, lanes] = _bias_chain(d_diag, tab)
                t_c = jnp.where(d_cmp < 0, 0.0, _bias_chain(d_cmp, tab))
                tc_ref[g, 0:128, lanes] = t_c
                tc_ref[g, 128:256, lanes] = t_c
        ext_ref[...] = (_iota((L, 128), 0) // SLC_BLOCK == _iota((L, 128), 1)).astype(bf16)

    q0 = qi * tq
    qpos = q0 + _iota((1, nl), 1) % tq
    qpos1 = q0 + _iota((1, tq), 1)
    sub = _iota((128, 1), 0)
    gt = gt_ref[...]

    jr = _iota((n_slc, tq), 0)
    cur = qpos1 // SLC_BLOCK
    valid = jr * SLC_BLOCK <= qpos1
    forced = (jr == 0) | (jr == cur) | (jr == cur - 1)
    j_i = _iota((n_slc, ncmp), 0)
    c_i = _iota((n_slc, ncmp), 1)
    ovl = ((c_i * CMP_STRIDE < j_i * SLC_BLOCK + SLC_BLOCK) & (c_i * CMP_STRIDE + CMP_LEN > j_i * SLC_BLOCK)
           & (c_i < ncmp - 1)).astype(f32)
    cmp_off = pl.multiple_of((128 - qi * (tq // CMP_STRIDE)) % 128, 8)

    groups = range(NSA_KV)
    gsl = [slice(g * NSA_HD, (g + 1) * NSA_HD) for g in groups]
    qt = [jnp.concatenate([qt_ref[(g * NSA_REP + r) * NSA_HD:(g * NSA_REP + r + 1) * NSA_HD, :]
                           for r in range(NSA_REP)], axis=1).astype(bf16) for g in groups]

    ok_c = (sub * CMP_STRIDE + (CMP_LEN - 1) <= qpos) & (sub < ncmp - 1)
    o_cmp = []
    for g in groups:
        st = _dot(kc_ref[:, gsl[g]], qt[g]) + tc_ref[g, pl.ds(cmp_off, 128), :]
        st = jnp.where(ok_c, st, NEG)
        e = jnp.exp(st - jnp.max(st, axis=0, keepdims=True))
        p = jnp.where(ok_c, e / jnp.sum(e, axis=0, keepdims=True), 0.0)
        o_cmp.append(_dot(vct_ref[gsl[g], :], p))
        p_sum = p[:, 0:tq] + p[:, tq:2 * tq] + p[:, 2 * tq:3 * tq] + p[:, 3 * tq:4 * tq]
        imp = _dot(ovl, p_sum, HI)
        score = jnp.where(valid, jnp.where(forced, 1e9, imp), -1e9)
        rank = jnp.zeros((n_slc, tq), f32)
        for i in range(n_slc):
            si = score[i:i + 1, :]
            rank += ((si > score) | ((si == score) & (jr > i))).astype(f32)
        sel = (rank < SLC_TOPN).astype(bf16)
        sel = jnp.concatenate([sel, jnp.zeros((128 - n_slc, tq), bf16)], axis=0)
        selx_ref[g] = jnp.dot(ext_ref[...], sel, preferred_element_type=f32)

    def attend(state, tiles, k_ref, vt_ref, mask_fn):
        s, offs = {}, []
        for ti, (kt, table) in enumerate(tiles):
            ko = pl.multiple_of(kt * 128, 128)
            offs.append(ko)
            oks = mask_fn(kt, ko)
            for g in groups:
                sg = _dot(k_ref[pl.ds(ko, 128), gsl[g]], qt[g]) + tb_ref[g, table]
                s[g, ti] = jnp.where(oks[g], sg, NEG)
        out = []
        for g in groups:
            m, l, acc = state[g]
            m_new = m
            for ti in range(len(tiles)):
                m_new = jnp.maximum(m_new, jnp.max(s[g, ti], axis=0, keepdims=True))
            alpha = jnp.exp(m - m_new)
            l, acc = alpha * l, alpha * acc
            for ti in range(len(tiles)):
                pr = jnp.exp(s[g, ti] - m_new)
                l += jnp.sum(pr, axis=0, keepdims=True)
                acc += _dot(vt_ref[gsl[g], pl.ds(offs[ti], 128)], pr)
            out.append((m_new, l, acc))
        return tuple(out)

    init = tuple((jnp.full((1, nl), NEG, f32), jnp.zeros((1, nl), f32), jnp.zeros((NSA_HD, nl), f32))
                 for _ in groups)
    far_table = lambda kt: jnp.where(kt == qi - 1, 1, 0)

    def sel_mask(kt, ko, diagonal=False):
        oks = [jnp.concatenate([selx_ref[g, pl.ds(ko, 128), :]] * NSA_REP, axis=1) > 0.5 for g in groups]
        return [ok & (ko + sub <= qpos) for ok in oks] if diagonal else oks

    pair = lambda i, st: attend(st, [(2 * i, far_table(2 * i)), (2 * i + 1, far_table(2 * i + 1))],
                                ks_ref, vst_ref, sel_mask)
    st = lax.fori_loop(0, qi // 2, pair, init)
    st = lax.fori_loop(0, qi % 2, lambda i, st: attend(st, [(qi - 1, 1)], ks_ref, vst_ref, sel_mask), st)
    st = attend(st, [(qi, 2)], ks_ref, vst_ref, functools.partial(sel_mask, diagonal=True))
    o_slc = [acc / l for _, l, acc in st]

    w0 = jnp.maximum(qi - WINDOW // 128, 0)

    def win_mask(kt, ko):
        d = qpos - (ko + sub)
        ok = (d >= 0) & (d <= WINDOW)
        return [ok, ok]

    win_tiles = [(w0 + t, jnp.clip(w0 + t - qi + 2, 0, 2)) for t in range(WINDOW // 128 + 1)]
    st = attend(init, win_tiles, kw_ref, vwt_ref, win_mask)
    o_win = [acc / l for _, l, acc in st]

    y_parts = []
    for g in groups:
        for r in range(NSA_REP):
            lanes = slice(r * tq, (r + 1) * tq)
            h = g * NSA_REP + r
            gate = lambda t: gt[SM_G + t * NSA_HEADS + h:SM_G + t * NSA_HEADS + h + 1, :]
            y_parts.append(gate(0) * o_cmp[g][:, lanes] + gate(1) * o_slc[g][:, lanes]
                           + gate(2) * o_win[g][:, lanes])
    y_ref[...] = jnp.concatenate(y_parts, axis=0).T.astype(y_ref.dtype)


def _nsa_prompt(qt, kv6_3, kv6t, kc, vct, gt, rel_bias):
    B, L, _ = kv6_3.shape
    tq = 128
    nb = L // CMP_STRIDE
    return pl.pallas_call(
        _nsa_prompt_kernel,
        grid=(B, L // tq),
        in_specs=[
            pl.BlockSpec(memory_space=pltpu.SMEM),
            pl.BlockSpec((None, D_NSA, tq), lambda b, i: (b, 0, i)),
            pl.BlockSpec((None, L, 128), lambda b, i: (b, 0, KV_SK // 128)),
            pl.BlockSpec((None, L, 128), lambda b, i: (b, 0, KV_WK // 128)),
            pl.BlockSpec((None, 128, L), lambda b, i: (b, KV_SV // 128, 0)),
            pl.BlockSpec((None, 128, L), lambda b, i: (b, KV_WV // 128, 0)),
            pl.BlockSpec((None, nb, 128), lambda b, i: (b, 0, 0)),
            pl.BlockSpec((None, 128, nb), lambda b, i: (b, 0, 0)),
            pl.BlockSpec((None, 128, tq), lambda b, i: (b, 0, i)),
        ],
        out_specs=pl.BlockSpec((None, tq, D_NSA), lambda b, i: (b, i, 0)),
        out_shape=jax.ShapeDtypeStruct((B, L, D_NSA), bf16),
        scratch_shapes=[
            pltpu.VMEM((NSA_KV, 3, 128, NSA_REP * tq), f32),
            pltpu.VMEM((NSA_KV, 256, NSA_REP * tq), f32),
            pltpu.VMEM((L, 128), bf16),
            pltpu.VMEM((NSA_KV, L, tq), f32),
        ],
        compiler_params=_params(("parallel", "arbitrary")),
        name="nsa_prompt",
    )(rel_bias, qt, kv6_3, kv6_3, kv6t, kv6t, kc, vct, gt)


def _prep_t_kernel(q_ref, kv_ref, qg_ref, kg_ref, qt_ref, kv6_ref, kv6t_ref, gt_ref):
    for s in range(D_NSA // 128):
        sl = slice(s * 128, (s + 1) * 128)
        qt_ref[sl, :] = (_group_rms(q_ref[:, sl], NSA_HD) * qg_ref[...] * (NSA_HD ** -0.5)).T
    kv = kv_ref[...]
    for j, off in enumerate((KV_CK, KV_CV, KV_SK, KV_SV, KV_WK, KV_WV)):
        x = kv[:, off:off + 128]
        if off == KV_SK:
            x = _group_rms(x, NSA_HD) * kg_ref[1:2, :]
        if off == KV_WK:
            x = _group_rms(x, NSA_HD) * kg_ref[2:3, :]
        kv6_ref[:, off:off + 128] = x
        kv6t_ref[off:off + 128, :] = x.T
    gt_ref[...] = jax.nn.sigmoid(kv[:, SM_OFF - KV_OFF:SM_OFF - KV_OFF + 128]).T


def _prep_t(proj3, qgain, kgain):
    B, L, _ = proj3.shape
    tm = min(L, 512)
    return pl.pallas_call(
        _prep_t_kernel,
        grid=(B, L // tm),
        in_specs=[
            pl.BlockSpec((None, tm, D_NSA), lambda b, i: (b, i, QN_OFF // D_NSA)),
            pl.BlockSpec((None, tm, 1024), lambda b, i: (b, i, KV_OFF // 1024)),
            pl.BlockSpec((1, 128), lambda b, i: (0, 0)),
            pl.BlockSpec((3, 128), lambda b, i: (0, 0)),
        ],
        out_specs=[pl.BlockSpec((None, D_NSA, tm), lambda b, i: (b, 0, i)),
                   pl.BlockSpec((None, tm, 768), lambda b, i: (b, i, 0)),
                   pl.BlockSpec((None, 768, tm), lambda b, i: (b, 0, i)),
                   pl.BlockSpec((None, 128, tm), lambda b, i: (b, 0, i))],
        out_shape=[jax.ShapeDtypeStruct((B, D_NSA, L), f32), jax.ShapeDtypeStruct((B, L, 768), f32),
                   jax.ShapeDtypeStruct((B, 768, L), f32), jax.ShapeDtypeStruct((B, 128, L), f32)],
        compiler_params=_params(("parallel", "parallel")),
        name="nsa_prep_t",
    )(proj3, proj3, qgain, kgain)


def _row0(x, rows=8):
    return jnp.where(_iota((rows, x.shape[1]), 0) == 0, x, 0.0)


def _ssd_decode_kernel(xbc_ref, z_ref, sm_ref, cprev_ref, sin_ref, cw_ref, cb_ref, dtb_ref, alog_ref, dexp_ref,
                       nrm_ref, y_ref, st_ref, cst_ref):
    x = xbc_ref[...]
    conv = cw_ref[3:4, :] * x
    for j in range(CONV_W - 1):
        conv += cw_ref[j:j + 1, :] * cprev_ref[j:j + 1, :]
    cst_ref[0:2, :] = cprev_ref[1:3, :]
    cst_ref[2:3, :] = x
    xc = _silu(conv + cb_ref[...])
    xs = xc[:, :D_SSD]
    dt = _softplus(sm_ref[...] + dtb_ref[...])
    da = dt * (-jnp.exp(alog_ref[...]))
    ys = []
    for g in range(SSD_GROUPS):
        bm = xc[:, D_SSD + g * SSD_STATE:D_SSD + (g + 1) * SSD_STATE]
        cm = xc[:, D_SSD + (SSD_GROUPS + g) * SSD_STATE:D_SSD + (SSD_GROUPS + g + 1) * SSD_STATE]
        cb = jnp.sum(cm * bm, axis=-1, keepdims=True)
        bm8, cm8 = _row0(bm), _row0(cm)
        for r in range(SSD_HEADS // SSD_GROUPS):
            h = g * (SSD_HEADS // SSD_GROUPS) + r
            xdt = xs[:, h * SSD_HD:(h + 1) * SSD_HD] * dt[:, h:h + 1]
            eda = jnp.exp(da[:, h:h + 1])
            h0 = sin_ref[h]
            ys.append(cb * xdt + eda * _dot_nt(cm8, h0, HI)[0:1])
            st_ref[h] = eda * h0 + _dot_tn(_row0(xdt), bm8, HI)
    y = (jnp.concatenate(ys, axis=1) + dexp_ref[...] * xs) * _silu(z_ref[...])
    gw = D_SSD // SSD_GROUPS
    for g in range(SSD_GROUPS):
        yg = y[:, g * gw:(g + 1) * gw]
        yg = yg * lax.rsqrt(jnp.mean(yg * yg, axis=-1, keepdims=True) + EPS)
        y_ref[:, g * gw:(g + 1) * gw] = yg * nrm_ref[:, g * gw:(g + 1) * gw]


def _ssd_decode(proj3, conv_state, state, layer, lp):
    B = proj3.shape[0]
    vec = lambda w: pl.BlockSpec((1, w), lambda b: (0, 0))
    return pl.pallas_call(
        _ssd_decode_kernel,
        grid=(B,),
        in_specs=[
            pl.BlockSpec((None, 1, SSD_CONV_CH), lambda b: (b, 0, XBC_OFF // SSD_CONV_CH)),
            pl.BlockSpec((None, 1, D_SSD), lambda b: (b, 0, ZS_OFF // D_SSD)),
            pl.BlockSpec((None, 1, 128), lambda b: (b, 0, SM_OFF // 128)),
            pl.BlockSpec((None, None, CONV_W - 1, SSD_CONV_CH), lambda b: (layer, b, 0, 0)),
            pl.BlockSpec((None, None, SSD_HEADS, SSD_HD, SSD_STATE), lambda b: (layer, b, 0, 0, 0)),
            pl.BlockSpec((CONV_W, SSD_CONV_CH), lambda b: (0, 0)),
            vec(SSD_CONV_CH), vec(128), vec(128), vec(D_SSD), vec(D_SSD),
        ],
        out_specs=[
            pl.BlockSpec((None, 1, D_SSD), lambda b: (b, 0, 0)),
            pl.BlockSpec((None, SSD_HEADS, SSD_HD, SSD_STATE), lambda b: (b, 0, 0, 0)),
            pl.BlockSpec((None, CONV_W - 1, SSD_CONV_CH), lambda b: (b, 0, 0)),
        ],
        out_shape=[
            jax.ShapeDtypeStruct((B, 1, D_SSD), f32),
            jax.ShapeDtypeStruct((B, SSD_HEADS, SSD_HD, SSD_STATE), f32),
            jax.ShapeDtypeStruct((B, CONV_W - 1, SSD_CONV_CH), f32),
        ],
        compiler_params=_params(("parallel",)),
        name="ssd_decode",
    )(proj3, proj3, proj3, conv_state, state, lp["ssd_conv_w"], lp["ssd_conv_b"], lp["ssd_dtb"], lp["ssd_alog"],
      lp["ssd_dexp"], lp["ssd_norm"])


def _gdn_decode_kernel(qkv_ref, z_ref, sm_ref, cprev_ref, sin_ref, cw_ref, dtb_ref, alog_ref, nrm_ref,
                       y_ref, st_ref, cst_ref):
    x = qkv_ref[...]
    conv = cw_ref[3:4, :] * x
    for j in range(CONV_W - 1):
        conv += cw_ref[j:j + 1, :] * cprev_ref[j:j + 1, :]
    cst_ref[0:2, :] = cprev_ref[1:3, :]
    cst_ref[2:3, :] = x
    xc = _silu(conv)
    sm = sm_ref[...]
    beta_all = jax.nn.sigmoid(sm)
    g_all = -jnp.exp(alog_ref[...]) * _softplus(sm + dtb_ref[...])
    for h in range(GDN_HEADS):
        sl = slice(h * GDN_HD, (h + 1) * GDN_HD)
        qh = xc[:, sl]
        kh = xc[:, D_GDN + h * GDN_HD:D_GDN + (h + 1) * GDN_HD]
        vh = xc[:, 2 * D_GDN + h * GDN_HD:2 * D_GDN + (h + 1) * GDN_HD]
        qh = qh * lax.rsqrt(jnp.sum(qh * qh, axis=-1, keepdims=True) + EPS) * (GDN_HD ** -0.5)
        kh = kh * lax.rsqrt(jnp.sum(kh * kh, axis=-1, keepdims=True) + EPS)
        beta = beta_all[:, SM_B + h:SM_B + h + 1]
        eg = jnp.exp(g_all[:, SM_A + h:SM_A + h + 1])
        s0 = sin_ref[h]
        k8 = _row0(kh)
        v_new = beta * vh - (beta * eg) * _dot(k8, s0, HI)[0:1]
        o = eg * _dot(_row0(qh), s0, HI)[0:1] + jnp.sum(qh * kh, axis=-1, keepdims=True) * v_new
        st_ref[h] = eg * s0 + _dot_tn(k8, _row0(v_new), HI)
        o = o * lax.rsqrt(jnp.mean(o * o, axis=-1, keepdims=True) + EPS)
        y_ref[:, sl] = o * nrm_ref[...] * _silu(z_ref[:, sl])


def _gdn_decode(proj3, conv_state, state, layer, lp):
    B = proj3.shape[0]
    vec = lambda w: pl.BlockSpec((1, w), lambda b: (0, 0))
    return pl.pallas_call(
        _gdn_decode_kernel,
        grid=(B,),
        in_specs=[
            pl.BlockSpec((None, 1, GDN_CONV_CH), lambda b: (b, 0, QKV_OFF // GDN_CONV_CH)),
            pl.BlockSpec((None, 1, D_GDN), lambda b: (b, 0, ZG_OFF // D_GDN)),
            pl.BlockSpec((None, 1, 128), lambda b: (b, 0, SM_OFF // 128)),
            pl.BlockSpec((None, None, CONV_W - 1, GDN_CONV_CH), lambda b: (layer, b, 0, 0)),
            pl.BlockSpec((None, None, GDN_HEADS, GDN_HD, GDN_HD), lambda b: (layer, b, 0, 0, 0)),
            pl.BlockSpec((CONV_W, GDN_CONV_CH), lambda b: (0, 0)),
            vec(128), vec(128), vec(GDN_HD),
        ],
        out_specs=[
            pl.BlockSpec((None, 1, D_GDN), lambda b: (b, 0, 0)),
            pl.BlockSpec((None, GDN_HEADS, GDN_HD, GDN_HD), lambda b: (b, 0, 0, 0)),
            pl.BlockSpec((None, CONV_W - 1, GDN_CONV_CH), lambda b: (b, 0, 0)),
        ],
        out_shape=[
            jax.ShapeDtypeStruct((B, 1, D_GDN), f32),
            jax.ShapeDtypeStruct((B, GDN_HEADS, GDN_HD, GDN_HD), f32),
            jax.ShapeDtypeStruct((B, CONV_W - 1, GDN_CONV_CH), f32),
        ],
        compiler_params=_params(("parallel",)),
        name="gdn_decode",
    )(proj3, proj3, proj3, conv_state, state, lp["gdn_conv_w"], lp["gdn_dtb"], lp["gdn_alog"], lp["gdn_norm"])


def _query_rows(q, g):
    rows = [q[:, (g * NSA_REP + r) * NSA_HD:(g * NSA_REP + r + 1) * NSA_HD] for r in range(NSA_REP)]
    return jnp.concatenate(rows + [jnp.zeros((8 - NSA_REP, NSA_HD), f32)], axis=0) * (NSA_HD ** -0.5)


def _masked_softmax(lg, ok):
    lg = jnp.where(ok, lg, NEG)
    e = jnp.exp(lg - jnp.max(lg, axis=-1, keepdims=True))
    return jnp.where(ok, e / jnp.sum(e, axis=-1, keepdims=True), 0.0)


def _head_lanes(parts):
    return jnp.concatenate([o[r:r + 1, :] for o in parts for r in range(NSA_REP)], axis=1)


def _nsa_dec_cmp_kernel(pt_ref, q_ref, pe_ref, w1_ref, w2_ref, kg_ref, tab_ref, pk_ref, pv_ref,
                        ocmp_ref, idx_ref, stage, rows, sem, *, layer, n_pages):
    b = pl.program_id(0)
    nb = pl.num_programs(0)
    n_past = n_pages * PAGE
    q_pos = n_past
    n_blk = n_past // CMP_STRIDE
    n_cmp = (n_past + 1 - CMP_LEN) // CMP_STRIDE + 1
    n_slc = -(-(n_past + 1) // SLC_BLOCK)
    nj = -(-n_slc // 128) * 128
    pools = (pk_ref, pv_ref)

    def page_copy(t, seq, p):
        return pltpu.make_async_copy(pools[t].at[layer, pt_ref[seq, p]], stage.at[t, p], sem.at[t])

    def start_all(t, seq):
        lax.fori_loop(0, n_pages, lambda p, c: (page_copy(t, seq, p).start(), c)[1], 0)

    def wait_all(t, seq):
        lax.fori_loop(0, n_pages, lambda p, c: (page_copy(t, seq, p).wait(), c)[1], 0)

    @pl.when(b == 0)
    def _():
        start_all(0, 0)
        start_all(1, 0)

    def to_rows(t):
        def body(p, c):
            rows[pl.ds(pl.multiple_of(p * PAGE, PAGE), PAGE), :] = stage[t, p].T
            return c
        lax.fori_loop(0, n_pages, body, 0, unroll=4)

    cmps = []
    for t in range(2):
        wait_all(t, b)
        to_rows(t)

        @pl.when(b + 1 < nb)
        def _():
            start_all(t, b + 1)

        p1 = jnp.zeros((n_blk, 2 * CMP_HID), f32)
        p2 = jnp.zeros((n_blk, 2 * CMP_HID), f32)
        for j in range(CMP_STRIDE // 2):
            xr = jnp.concatenate([rows[pl.ds(2 * j, n_blk, stride=CMP_STRIDE), :],
                                  rows[pl.ds(2 * j + 1, n_blk, stride=CMP_STRIDE), :]], axis=1)
            j2 = CMP_STRIDE // 2 + j
            p1 += jnp.dot((xr + pe_ref[t, j:j + 1, :]).astype(bf16), w1_ref[t, j], preferred_element_type=f32)
            p2 += jnp.dot((xr + pe_ref[t, j2:j2 + 1, :]).astype(bf16), w1_ref[t, j2], preferred_element_type=f32)
        hid = p1 + pltpu.roll(p2, n_blk - 1, 0)
        c = jnp.dot(_silu(hid).astype(bf16), w2_ref[t], preferred_element_type=f32)
        if t == 0:
            c = _group_rms(c, NSA_HD) * kg_ref[0:1, :]
        cmps.append(c)
    kc, vc = cmps

    q = q_ref[...]
    lane = _iota((8, n_blk), 1)
    cend = lane * CMP_STRIDE + (CMP_LEN - 1)
    ok = (cend <= q_pos) & (lane < n_cmp)
    c_i = _iota((n_blk, nj), 0)
    j_i = _iota((n_blk, nj), 1)
    ovl = ((c_i * CMP_STRIDE < j_i * SLC_BLOCK + SLC_BLOCK) & (c_i * CMP_STRIDE + CMP_LEN > j_i * SLC_BLOCK)
           & (j_i < n_slc) & (c_i < n_cmp)).astype(f32)
    jl = _iota((1, nj), 1)
    cur = q_pos // SLC_BLOCK
    valid = jl * SLC_BLOCK <= q_pos
    forced = (jl == 0) | (jl == cur) | (jl == cur - 1)
    ii = _iota((nj, nj), 0)
    jj = _iota((nj, nj), 1)
    kk = _iota((SLC_TOPN, nj), 0)
    j16 = _iota((SLC_TOPN, nj), 1)
    o_parts, idx_cols = [], []
    for g in range(NSA_KV):
        tabcols = [tab_ref[g, :, bk:bk + 1] for bk in range(N_BUCKETS)]
        lg = _dot_nt(_query_rows(q, g), kc[:, g * NSA_HD:(g + 1) * NSA_HD]) + _bias_chain(q_pos - cend, tabcols)
        p = _masked_softmax(lg, ok)
        o_parts.append(_dot(p, vc[:, g * NSA_HD:(g + 1) * NSA_HD]))
        p_sum = p[0:1] + p[1:2] + p[2:3] + p[3:4]
        imp = _dot(_row0(p_sum), ovl, HI)[0:1]
        score = jnp.where(valid, jnp.where(forced, 1e9, imp), -1e9)
        score = jnp.where(jl < n_slc, score, -3e9)
        s_c = jnp.sum(jnp.where(ii == jj, score, 0.0), axis=1, keepdims=True)
        rank_r = jnp.sum(((s_c > score) | ((s_c == score) & (ii < jj))).astype(f32), axis=0, keepdims=True)
        rank_c = jnp.sum(((score > s_c) | ((score == s_c) & (jj < ii))).astype(f32), axis=1, keepdims=True)
        sel_r = (rank_r < SLC_TOPN) & (jl < n_slc)
        sel_c = (rank_c < SLC_TOPN) & (ii[:, 0:1] < n_slc)
        pos_r = jnp.sum((sel_c & (ii < jj)).astype(f32), axis=0, keepdims=True)
        hit = (pos_r == kk.astype(f32)) & sel_r
        idx_cols.append(jnp.sum(jnp.where(hit, j16.astype(f32), 0.0), axis=1, keepdims=True))
    ocmp_ref[...] = _head_lanes(o_parts)
    l16 = _iota((SLC_TOPN, 128), 1)
    idx_ref[...] = (jnp.where(l16 == 0, idx_cols[0], 0.0) + jnp.where(l16 == 1, idx_cols[1], 0.0)).astype(i32)


def _nsa_dec_cmp(qn3, page_table, pool_k, pool_v, layer, lp):
    B = qn3.shape[0]
    n_pages = page_table.shape[1]
    full = lambda s: pl.BlockSpec(s, lambda b, pt: (0,) * len(s))
    return pl.pallas_call(
        functools.partial(_nsa_dec_cmp_kernel, layer=layer, n_pages=n_pages),
        grid_spec=pltpu.PrefetchScalarGridSpec(
            num_scalar_prefetch=1,
            grid=(B,),
            in_specs=[
                pl.BlockSpec((None, 1, D_NSA), lambda b, pt: (b, 0, 0)),
                full((2, CMP_LEN // 2, 256)), full((2, CMP_LEN // 2, 256, 2 * CMP_HID)),
                full((2, 2 * CMP_HID, 128)), full((3, 128)), full((NSA_KV, 8, N_BUCKETS)),
                pl.BlockSpec(memory_space=pl.ANY), pl.BlockSpec(memory_space=pl.ANY),
            ],
            out_specs=[pl.BlockSpec((None, 1, D_NSA), lambda b, pt: (b, 0, 0)),
                       pl.BlockSpec((None, SLC_TOPN, 128), lambda b, pt: (b, 0, 0))],
            scratch_shapes=[pltpu.VMEM((2, n_pages, NSA_KV * NSA_HD, PAGE), f32),
                            pltpu.VMEM((n_pages * PAGE, NSA_KV * NSA_HD), f32),
                            pltpu.SemaphoreType.DMA((2,))],
        ),
        out_shape=[jax.ShapeDtypeStruct((B, 1, D_NSA), f32), jax.ShapeDtypeStruct((B, SLC_TOPN, 128), i32)],
        compiler_params=_params(("arbitrary",)),
        name="nsa_dec_cmp",
    )(page_table, qn3, lp["cmp_pe2"], lp["cmp_w1bd"], lp["cmp_w2bd"], lp["k_gain"], lp["tab8"], pool_k, pool_v)


def _nsa_dec_sel_kernel(pt_ref, idx_ref, q_ref, kv_ref, sm_ref, ocmp_ref, tab_ref, wk_ref, wv_ref, pk_ref, pv_ref,
                        y_ref, wko_ref, wvo_ref, kg, vg, sem, *, layer, n_pages):
    b = pl.program_id(0)
    n_past = n_pages * PAGE
    q_pos = n_past
    n_buf = wk_ref.shape[1]
    new_blk = n_past // SLC_BLOCK
    per_page = PAGE // SLC_BLOCK

    def blk_copy(pool, buf, g, s, sm_i):
        j = jnp.minimum(idx_ref[b, s, g], new_blk - 1)
        src = pool.at[layer, pt_ref[b, j // per_page], pl.ds(g * NSA_HD, NSA_HD)]
        return pltpu.make_async_copy(src, buf.at[g, :, pl.ds(s * PAGE, PAGE)], sem.at[sm_i])

    for g in range(NSA_KV):
        for s in range(SLC_TOPN):
            blk_copy(pk_ref, kg, g, s, 0).start()
            blk_copy(pv_ref, vg, g, s, 1).start()

    kv = kv_ref[...]
    q = q_ref[...]
    gate = jax.nn.sigmoid(sm_ref[...])
    ocmp = ocmp_ref[...]

    ii, jj = _iota((128, 128), 0), _iota((128, 128), 1)
    column = lambda off: jnp.sum(jnp.where(ii == jj, kv[:, off:off + 128], 0.0), axis=1, keepdims=True)
    last_lane = _iota((128, n_buf), 1) == n_buf - 1
    wk, wv = wk_ref[...], wv_ref[...]
    wko_ref[...] = jnp.where(last_lane, column(KV_WK), pltpu.roll(wk, n_buf - 1, 1))
    wvo_ref[...] = jnp.where(last_lane, column(KV_WV), pltpu.roll(wv, n_buf - 1, 1))

    def attend(qg, kt, vt, bias, ok, k_new, v_new, bias_new, ok_new):
        lg = jnp.where(ok, _dot(qg, kt) + bias, NEG)
        lg_new = jnp.where(ok_new, jnp.sum(qg * k_new, axis=-1, keepdims=True) + bias_new, NEG)
        m = jnp.maximum(jnp.max(lg, axis=-1, keepdims=True), lg_new)
        e = jnp.where(ok, jnp.exp(lg - m), 0.0)
        e_new = jnp.where(ok_new, jnp.exp(lg_new - m), 0.0)
        den = jnp.sum(e, axis=-1, keepdims=True) + e_new
        p, p_new = e / den, e_new / den
        return _dot_nt(p, vt) + p_new * v_new

    for g in range(NSA_KV):
        for s in range(SLC_TOPN):
            blk_copy(pk_ref, kg, g, s, 0).wait()
            blk_copy(pv_ref, vg, g, s, 1).wait()

    lw = _iota((8, n_buf), 1)
    d_w = q_pos - (n_past - n_buf + lw)
    ok_w = (d_w >= 0) & (d_w <= WINDOW)
    ls = _iota((8, SLC_TOPN * PAGE), 1)
    slot = ls // PAGE
    zero = jnp.zeros((8, 1), i32)
    o_slc, o_win = [], []
    for g in range(NSA_KV):
        tabcols = [tab_ref[g, :, bk:bk + 1] for bk in range(N_BUCKETS)]
        qg = _query_rows(q, g)
        gs = slice(g * NSA_HD, (g + 1) * NSA_HD)
        b_new = _bias_chain(zero, tabcols)
        blk = jnp.zeros(ls.shape, i32)
        n_new = jnp.zeros((8, 1), i32)
        for s in range(SLC_TOPN):
            j = idx_ref[b, s, g]
            blk = jnp.where(slot == s, j, blk)
            n_new = jnp.where(j == new_blk, n_new + 1, n_new)
        k_pos = (jnp.minimum(blk, new_blk - 1) // per_page) * PAGE + ls % PAGE
        ok_s = (blk < new_blk) & (k_pos // SLC_BLOCK == blk)
        o_slc.append(attend(qg, kg[g], vg[g], _bias_chain(q_pos - k_pos, tabcols), ok_s,
                            kv[:, KV_SK + g * NSA_HD:KV_SK + (g + 1) * NSA_HD],
                            kv[:, KV_SV + g * NSA_HD:KV_SV + (g + 1) * NSA_HD], b_new, n_new > 0))
        o_win.append(attend(qg, wk[gs, :], wv[gs, :], _bias_chain(d_w, tabcols), ok_w,
                            kv[:, KV_WK + g * NSA_HD:KV_WK + (g + 1) * NSA_HD],
                            kv[:, KV_WV + g * NSA_HD:KV_WV + (g + 1) * NSA_HD], b_new, zero == 0))
    gl = lambda t: gate[:, SM_G + t * NSA_HEADS:SM_G + (t + 1) * NSA_HEADS]
    wide = lambda gt: jnp.concatenate([jnp.broadcast_to(gt[:, h:h + 1], (1, NSA_HD)) for h in range(NSA_HEADS)], axis=1)
    y_ref[...] = wide(gl(0)) * ocmp + wide(gl(1)) * _head_lanes(o_slc) + wide(gl(2)) * _head_lanes(o_win)


def _nsa_dec_sel(qn3, kv6_3, proj3, ocmp, idx, page_table, win_k, win_v, pool_k, pool_v, layer, lp):
    B = qn3.shape[0]
    n_pages = page_table.shape[1]
    n_buf = win_k.shape[3]
    full = lambda s: pl.BlockSpec(s, lambda b, pt, ix: (0,) * len(s))
    row = lambda w, j: pl.BlockSpec((None, 1, w), lambda b, pt, ix: (b, 0, j))
    win = pl.BlockSpec((None, None, 128, n_buf), lambda b, pt, ix: (layer, b, 0, 0))
    return pl.pallas_call(
        functools.partial(_nsa_dec_sel_kernel, layer=layer, n_pages=n_pages),
        grid_spec=pltpu.PrefetchScalarGridSpec(
            num_scalar_prefetch=2,
            grid=(B,),
            in_specs=[row(D_NSA, 0), row(768, 0), row(128, SM_OFF // 128), row(D_NSA, 0),
                      full((NSA_KV, 8, N_BUCKETS)), win, win,
                      pl.BlockSpec(memory_space=pl.ANY), pl.BlockSpec(memory_space=pl.ANY)],
            out_specs=[row(D_NSA, 0),
                       pl.BlockSpec((None, 128, n_buf), lambda b, pt, ix: (b, 0, 0)),
                       pl.BlockSpec((None, 128, n_buf), lambda b, pt, ix: (b, 0, 0))],
            scratch_shapes=[pltpu.VMEM((NSA_KV, NSA_HD, SLC_TOPN * PAGE), f32),
                            pltpu.VMEM((NSA_KV, NSA_HD, SLC_TOPN * PAGE), f32),
                            pltpu.SemaphoreType.DMA((2,))],
        ),
        out_shape=[jax.ShapeDtypeStruct((B, 1, D_NSA), f32),
                   jax.ShapeDtypeStruct((B, 128, n_buf), f32), jax.ShapeDtypeStruct((B, 128, n_buf), f32)],
        compiler_params=_params(("arbitrary",)),
        name="nsa_dec_sel",
    )(page_table, idx, qn3, kv6_3, proj3, ocmp, lp["tab8"], win_k, win_v, pool_k, pool_v)


def _layer_params(i, p):
    pad128 = lambda v, off: jnp.zeros((1, 128), f32).at[0, off:off + v.shape[0]].set(v)
    return dict(
        ssd_conv_w=p["ssd_conv_w"][i], ssd_conv_b=p["ssd_conv_b"][i][None],
        ssd_dtb=pad128(p["ssd_dt_bias"][i], SM_DT), ssd_alog=pad128(p["ssd_a_log"][i], SM_DT),
        ssd_dexp=jnp.repeat(p["ssd_d"][i], SSD_HD)[None], ssd_norm=p["ssd_norm"][i][None],
        gdn_conv_w=p["gdn_conv_w"][i],
        gdn_dtb=pad128(p["gdn_dt_bias"][i], SM_A), gdn_alog=pad128(p["gdn_a_log"][i], SM_A),
        gdn_norm=p["gdn_norm"][i][None],
        q_gain=jnp.tile(p["nsa_q_norm"][i], 2)[None], k_gain=jnp.tile(p["nsa_k_norm"][i], (1, 2)),
        cmp_pe=p["nsa_cmp_pe"][i], cmp_w1=p["nsa_cmp_w1"][i], cmp_w2=p["nsa_cmp_w2"][i],
        cmp_pe2=jnp.tile(p["nsa_cmp_pe"][i], (1, 1, 2)).reshape(2, CMP_LEN // 2, 256),
        cmp_w1bd=_block_diag2(p["nsa_cmp_w1"][i].reshape(2, CMP_LEN, NSA_HD, CMP_HID)).astype(bf16)
        .reshape(2, CMP_LEN // 2, 256, 2 * CMP_HID),
        cmp_w2bd=_block_diag2(p["nsa_cmp_w2"][i]).astype(bf16),
        tab8=jnp.pad(p["rel_bias"].T.reshape(NSA_KV, NSA_REP, N_BUCKETS), ((0, 0), (0, 8 - NSA_REP), (0, 0))),
    )


def _block_diag2(w):
    z = jnp.zeros_like(w)
    return jnp.concatenate([jnp.concatenate([w, z], axis=-1), jnp.concatenate([z, w], axis=-1)], axis=-2)


def _reorder_w_in(w_in):
    cuts = [(1024, 2560), (2576, 4112), (0, 1024), (4112, 4624), (4632, 5144), (5144, 5912),
            (2560, 2576), (4624, 4632), (5912, 5936)]
    wt = jnp.transpose(w_in, (0, 2, 1))
    parts = [wt[:, a:b] for a, b in cuts]
    used = sum(b - a for a, b in cuts)
    parts.append(jnp.zeros((w_in.shape[0], PROJ_W - used, w_in.shape[1]), w_in.dtype))
    return jnp.concatenate(parts, axis=1).astype(bf16)


def _mix_prompt(x3, i, p, lp, w_r):
    B, L, _ = x3.shape
    n = B * L
    x = x3.reshape(n, D_MODEL)
    proj = _inproj(x, p["norm_mix"], w_r, i)
    proj3 = proj.reshape(B, L, PROJ_W)
    y_ssd, st_ssd, cst_ssd = _ssd_prompt(proj3, lp)
    y_gdn, st_gdn, cst_gdn = _gdn_prompt(proj3, lp)
    qt, kv6_3, kv6t, gt = _prep_t(proj3, lp["q_gain"], lp["k_gain"])
    kc, vct = _cmp_prompt(kv6_3, lp)
    y_nsa = _nsa_prompt(qt, kv6_3, kv6t, kc, vct, gt, p["rel_bias"])
    x = _outproj(x, y_ssd.reshape(n, D_SSD), y_gdn.reshape(n, D_GDN), y_nsa.reshape(n, D_NSA), p["w_out"], i)
    rows = lambda off: jnp.transpose(kv6t[:, off:off + 128].reshape(B, NSA_KV, NSA_HD, L), (0, 3, 1, 2))
    keep = min(WINDOW, L)
    state = (st_ssd, cst_ssd, st_gdn, cst_gdn, rows(KV_CK), rows(KV_CV), rows(KV_SK), rows(KV_SV),
             rows(KV_WK)[:, L - keep:], rows(KV_WV)[:, L - keep:])
    return x.reshape(B, L, D_MODEL), state


def _mix_decode(x, i, p, lp, w_r, st):
    B = x.shape[0]
    proj = _inproj(x, p["norm_mix"], w_r, i)
    proj3 = proj.reshape(B, 1, PROJ_W)
    y_ssd, st_ssd, cst_ssd = _ssd_decode(proj3, st["ssd_conv"], st["ssd"], i, lp)
    y_gdn, st_gdn, cst_gdn = _gdn_decode(proj3, st["gdn_conv"], st["gdn"], i, lp)
    qn, kv6 = _prep(proj, lp["q_gain"], lp["k_gain"])
    qn3, kv6_3 = qn.reshape(B, 1, D_NSA), kv6.reshape(B, 1, 768)
    o_cmp, idx = _nsa_dec_cmp(qn3, st["page_table"], st["cmp_k"], st["cmp_v"], i, lp)
    y_nsa, win_k, win_v = _nsa_dec_sel(qn3, kv6_3, proj3, o_cmp, idx[:, :, :NSA_KV], st["page_table"],
                                       st["win_k"], st["win_v"], st["slc_k"], st["slc_v"], i, lp)
    x = _outproj(x, y_ssd.reshape(B, D_SSD), y_gdn.reshape(B, D_GDN), y_nsa.reshape(B, D_NSA), p["w_out"], i)
    rows = lambda off: kv6[:, off:off + 128].reshape(B, 1, NSA_KV, NSA_HD)
    n_buf = win_k.shape[2]
    buf = lambda w: jnp.transpose(w.reshape(B, NSA_KV, NSA_HD, n_buf), (0, 3, 1, 2))
    state = (st_ssd, cst_ssd, st_gdn, cst_gdn, rows(KV_CK), rows(KV_CV), rows(KV_SK), rows(KV_SV),
             buf(win_k), buf(win_v))
    return x, state


def kernel(x_prompt, x_sample, state_ssd, state_ssd_conv, state_gdn, state_gdn_conv, cache_cmp_k, cache_cmp_v,
           cache_slc_k, cache_slc_v, cache_win_k, cache_win_v, page_table, rel_bias, norm_ffn1, w_ffn1_gate,
           w_ffn1_up, w_ffn1_down, norm_mix, w_in, ssd_conv_w, ssd_conv_b, ssd_dt_bias, ssd_a_log, ssd_d, ssd_norm,
           gdn_conv_w, gdn_dt_bias, gdn_a_log, gdn_norm, nsa_q_norm, nsa_k_norm, nsa_cmp_pe, nsa_cmp_w1,
           nsa_cmp_w2, w_out, norm_ffn2, w_ffn2_gate, w_ffn2_up, w_ffn2_down):
    bp, lp_len, _ = x_prompt.shape
    bs = x_sample.shape[0]
    gain3 = lambda g: g.reshape(DEPTH, 1, D_MODEL)
    p = dict(rel_bias=rel_bias, norm_mix=gain3(norm_mix), w_out=w_out, ssd_conv_w=ssd_conv_w, ssd_conv_b=ssd_conv_b,
             ssd_dt_bias=ssd_dt_bias, ssd_a_log=ssd_a_log, ssd_d=ssd_d, ssd_norm=ssd_norm, gdn_conv_w=gdn_conv_w,
             gdn_dt_bias=gdn_dt_bias, gdn_a_log=gdn_a_log, gdn_norm=gdn_norm, nsa_q_norm=nsa_q_norm,
             nsa_k_norm=nsa_k_norm, nsa_cmp_pe=nsa_cmp_pe, nsa_cmp_w1=nsa_cmp_w1, nsa_cmp_w2=nsa_cmp_w2)
    n1, n2 = gain3(norm_ffn1), gain3(norm_ffn2)
    chan_row = lambda c: jnp.transpose(c, (0, 1, 3, 4, 2)).reshape(c.shape[:2] + (NSA_KV * NSA_HD, c.shape[2]))
    st = dict(ssd=state_ssd, ssd_conv=state_ssd_conv, gdn=state_gdn, gdn_conv=state_gdn_conv,
              cmp_k=chan_row(cache_cmp_k), cmp_v=chan_row(cache_cmp_v), slc_k=chan_row(cache_slc_k),
              slc_v=chan_row(cache_slc_v), win_k=chan_row(cache_win_k), win_v=chan_row(cache_win_v),
              page_table=page_table)
    w_r = _reorder_w_in(w_in)

    hp = x_prompt.reshape(bp * lp_len, D_MODEL)
    hs = x_sample.reshape(bs, D_MODEL)
    outs_p, outs_s = [], []
    for i in range(DEPTH):
        lp = _layer_params(i, p)
        hp, hs = _ffn(hp, hs, n1, w_ffn1_gate, w_ffn1_up, w_ffn1_down, i)
        hp3, st_p = _mix_prompt(hp.reshape(bp, lp_len, D_MODEL), i, p, lp, w_r)
        hs, st_s = _mix_decode(hs, i, p, lp, w_r, st)
        hp, hs = _ffn(hp3.reshape(bp * lp_len, D_MODEL), hs, n2, w_ffn2_gate, w_ffn2_up, w_ffn2_down, i)
        outs_p.append(st_p)
        outs_s.append(st_s)
    stack = lambda outs: [jnp.stack(t) for t in zip(*outs)]
    return (hp.reshape(bp, lp_len, D_MODEL), hs.reshape(bs, 1, D_MODEL), *stack(outs_p), *stack(outs_s))
```

```python
import functools
import math

import numpy as np
import jax
import jax.numpy as jnp
from jax import lax
from jax.experimental import pallas as pl
from jax.experimental.pallas import tpu as pltpu

f32 = jnp.float32
bf16 = jnp.bfloat16
i32 = jnp.int32
HI = lax.Precision.HIGHEST

D_MODEL = 2048
DEPTH = 4
PAGE = 128
D_SSD = 1024
SSD_HD = 64
SSD_HEADS = 16
SSD_GROUPS = 2
SSD_STATE = 128
SSD_CHUNK = 128
D_GDN = 512
GDN_HD = 128
GDN_HEADS = 4
GDN_CHUNK = 64
GDN_STEP_CHUNKS = 4
D_NSA = 512
NSA_HD = 64
NSA_HEADS = 8
NSA_KV = 2
NSA_REP = 4
CMP_STRIDE = 16
CMP_LEN = 32
CMP_HID = 128
SLC_BLOCK = 64
SLC_TOPN = 16
WINDOW = 512
N_BUCKETS = 32
MAX_DISTANCE = 128
CONV_W = 4
D_FF = 5632
SSD_CONV_CH = 1536
GDN_CONV_CH = 1536
EPS = 1e-6
NEG = -1e30

XBC_OFF = 0
QKV_OFF = 1536
ZS_OFF = 3072
ZG_OFF = 4096
QN_OFF = 4608
KV_OFF = 5120
SM_OFF = 5888
PROJ_W = 6144
SM_DT, SM_A, SM_B, SM_G = 0, 16, 20, 24
KV_CK, KV_CV, KV_SK, KV_SV, KV_WK, KV_WV = 0, 128, 256, 384, 512, 640

VMEM_LIMIT = 56 * 1024 * 1024
TB_FAR, TB_PREV, TB_DIAG, TB_WIN0, TB_NONE, TB_KINDS = 0, 1, 2, 3, 4, 5


def _bucket_thresholds():
    exact = N_BUCKETS // 2
    d = np.arange(0, 4 * MAX_DISTANCE)
    nf = np.maximum(d, 1).astype(np.float32)
    large = exact + (np.log(nf / np.float32(exact)) / np.float32(math.log(MAX_DISTANCE / exact))
                     * np.float32(N_BUCKETS - exact)).astype(np.int32)
    bucket = np.where(d < exact, d, np.minimum(large, N_BUCKETS - 1))
    return [int(np.argmax(bucket >= b)) for b in range(N_BUCKETS)]


_THR = _bucket_thresholds()
BIAS_FAR = _THR[N_BUCKETS - 1]


def _bias_chain(d, tabcols):
    v = jnp.broadcast_to(tabcols[N_BUCKETS - 1], d.shape)
    for b in range(N_BUCKETS - 2, -1, -1):
        v = jnp.where(d < _THR[b + 1], tabcols[b], v)
    return v


def _silu(x):
    return x * jax.nn.sigmoid(x)


def _softplus(x):
    return jnp.maximum(x, 0.0) + jnp.log1p(jnp.exp(-jnp.abs(x)))


def _dot(a, b, prec=None):
    if prec is None:
        a, b = a.astype(bf16), b.astype(bf16)
    return jnp.dot(a, b, preferred_element_type=f32, precision=prec)


def _dot_nt(a, b, prec=None):
    if prec is None:
        a, b = a.astype(bf16), b.astype(bf16)
    return lax.dot_general(a, b, (((1,), (1,)), ((), ())), preferred_element_type=f32, precision=prec)


def _dot_tn(a, b, prec=None):
    if prec is None:
        a, b = a.astype(bf16), b.astype(bf16)
    return lax.dot_general(a, b, (((0,), (0,)), ((), ())), preferred_element_type=f32, precision=prec)


def _split2(a):
    hi = a.astype(bf16)
    return hi, (a - hi.astype(f32)).astype(bf16)


def _dot3(a2, b2):
    d = lambda x, y: jnp.dot(x, y, preferred_element_type=f32)
    return d(a2[0], b2[0]) + (d(a2[0], b2[1]) + d(a2[1], b2[0]))


def _iota(shape, dim):
    return lax.broadcasted_iota(i32, shape, dim)


def _group_ones(n, width):
    return (_iota((n, n), 0) // width == _iota((n, n), 1) // width).astype(f32)


def _group_rms(x, width):
    ss = _dot(x * x, _group_ones(x.shape[1], width), HI)
    return x * lax.rsqrt(ss * (1.0 / width) + EPS)


def _params(sem):
    return pltpu.CompilerParams(dimension_semantics=sem, vmem_limit_bytes=VMEM_LIMIT)


FFN_EXTRA = 16


def _ffn_kernel(x_ref, xs_ref, g_ref, wg_ref, wu_ref, wd_ref, o_ref, os_ref, h_ref):
    tm, ns = x_ref.shape[0], xs_ref.shape[0]

    @pl.when(pl.program_id(1) == 0)
    def _():
        rms = lambda x: x * lax.rsqrt(jnp.mean(x * x, axis=-1, keepdims=True) + EPS) * g_ref[...]
        x, xs = x_ref[...], xs_ref[...]
        h_ref[0:tm, :] = rms(x).astype(bf16)
        h_ref[tm:tm + FFN_EXTRA, :] = jnp.concatenate(
            [rms(xs), jnp.zeros((FFN_EXTRA - ns, D_MODEL), f32)], axis=0).astype(bf16)
        o_ref[...] = x
        os_ref[...] = xs

    h = h_ref[...]
    a = jnp.dot(h, wg_ref[...].astype(bf16), preferred_element_type=f32)
    u = jnp.dot(h, wu_ref[...].astype(bf16), preferred_element_type=f32)
    y = 0.5 * _dot(_silu(a) * u, wd_ref[...])
    o_ref[...] += y[0:tm]
    os_ref[...] += y[tm:tm + ns]


def _ffn(x, xs, gain, wg, wu, wd, layer):
    n, ns = x.shape[0], xs.shape[0]
    tm, tf = 1024, 256
    return pl.pallas_call(
        _ffn_kernel,
        grid=(n // tm, D_FF // tf),
        in_specs=[
            pl.BlockSpec((tm, D_MODEL), lambda i, j: (i, 0)),
            pl.BlockSpec((ns, D_MODEL), lambda i, j: (0, 0)),
            pl.BlockSpec((None, 1, D_MODEL), lambda i, j: (layer, 0, 0)),
            pl.BlockSpec((None, D_MODEL, tf), lambda i, j: (layer, 0, j)),
            pl.BlockSpec((None, D_MODEL, tf), lambda i, j: (layer, 0, j)),
            pl.BlockSpec((None, tf, D_MODEL), lambda i, j: (layer, j, 0)),
        ],
        out_specs=[pl.BlockSpec((tm, D_MODEL), lambda i, j: (i, 0)),
                   pl.BlockSpec((ns, D_MODEL), lambda i, j: (0, 0))],
        out_shape=[jax.ShapeDtypeStruct((n, D_MODEL), f32), jax.ShapeDtypeStruct((ns, D_MODEL), f32)],
        scratch_shapes=[pltpu.VMEM((tm + FFN_EXTRA, D_MODEL), bf16)],
        compiler_params=_params(("arbitrary", "arbitrary")),
        name="ffn",
    )(x, xs, gain, wg, wu, wd)


def _inproj_kernel(x_ref, g_ref, w_ref, o_ref, h_ref):
    @pl.when(pl.program_id(1) == 0)
    def _():
        x = x_ref[...]
        h = x * lax.rsqrt(jnp.mean(x * x, axis=-1, keepdims=True) + EPS) * g_ref[...]
        h_ref[...] = h.astype(bf16)

    o_ref[...] = _dot_nt(h_ref[...], w_ref[...])


def _inproj(x, gain, w_r, layer):
    n = x.shape[0]
    tm = min(n, 1024)
    tn = 512
    return pl.pallas_call(
        _inproj_kernel,
        grid=(n // tm, PROJ_W // tn),
        in_specs=[
            pl.BlockSpec((tm, D_MODEL), lambda i, j: (i, 0)),
            pl.BlockSpec((None, 1, D_MODEL), lambda i, j: (layer, 0, 0)),
            pl.BlockSpec((None, tn, D_MODEL), lambda i, j: (layer, j, 0)),
        ],
        out_specs=pl.BlockSpec((tm, tn), lambda i, j: (i, j)),
        out_shape=jax.ShapeDtypeStruct((n, PROJ_W), f32),
        scratch_shapes=[pltpu.VMEM((tm, D_MODEL), bf16)],
        compiler_params=_params(("parallel", "arbitrary")),
        name="inproj",
    )(x, gain, w_r)


def _outproj_kernel(x_ref, ys_ref, yg_ref, yn_ref, ws_ref, wg_ref, wn_ref, o_ref):
    acc = _dot(ys_ref[...], ws_ref[...])
    acc += _dot(yg_ref[...], wg_ref[...])
    acc += _dot(yn_ref[...], wn_ref[...])
    o_ref[...] = x_ref[...] + acc


def _outproj(x, ys, yg, yn, w_out, layer):
    n = x.shape[0]
    tm = min(n, 2048)
    tn = 512
    return pl.pallas_call(
        _outproj_kernel,
        grid=(n // tm, D_MODEL // tn),
        in_specs=[
            pl.BlockSpec((tm, tn), lambda i, j: (i, j)),
            pl.BlockSpec((tm, D_SSD), lambda i, j: (i, 0)),
            pl.BlockSpec((tm, D_GDN), lambda i, j: (i, 0)),
            pl.BlockSpec((tm, D_NSA), lambda i, j: (i, 0)),
            pl.BlockSpec((None, D_SSD, tn), lambda i, j: (layer, 0, j)),
            pl.BlockSpec((None, D_GDN, tn), lambda i, j: (layer, 2, j)),
            pl.BlockSpec((None, D_NSA, tn), lambda i, j: (layer, 3, j)),
        ],
        out_specs=pl.BlockSpec((tm, tn), lambda i, j: (i, j)),
        out_shape=jax.ShapeDtypeStruct((n, D_MODEL), f32),
        compiler_params=_params(("parallel", "arbitrary")),
        name="outproj",
    )(x, ys, yg, yn, w_out, w_out, w_out)


def _prep_kernel(q_ref, kv_ref, qg_ref, kg_ref, qn_ref, kv6_ref):
    for s in range(D_NSA // 128):
        sl = slice(s * 128, (s + 1) * 128)
        qn_ref[:, sl] = _group_rms(q_ref[:, sl], NSA_HD) * qg_ref[...]
    kv = kv_ref[...]
    kv6_ref[...] = kv[:, :6 * 128]
    kv6_ref[:, KV_SK:KV_SK + 128] = _group_rms(kv[:, KV_SK:KV_SK + 128], NSA_HD) * kg_ref[1:2, :]
    kv6_ref[:, KV_WK:KV_WK + 128] = _group_rms(kv[:, KV_WK:KV_WK + 128], NSA_HD) * kg_ref[2:3, :]


def _prep(proj, qgain, kgain):
    n = proj.shape[0]
    tm = min(n, 512)
    return pl.pallas_call(
        _prep_kernel,
        grid=(n // tm,),
        in_specs=[
            pl.BlockSpec((tm, D_NSA), lambda i: (i, QN_OFF // D_NSA)),
            pl.BlockSpec((tm, 1024), lambda i: (i, KV_OFF // 1024)),
            pl.BlockSpec((1, 128), lambda i: (0, 0)),
            pl.BlockSpec((3, 128), lambda i: (0, 0)),
        ],
        out_specs=[pl.BlockSpec((tm, D_NSA), lambda i: (i, 0)),
                   pl.BlockSpec((tm, 768), lambda i: (i, 0))],
        out_shape=[jax.ShapeDtypeStruct((n, D_NSA), f32), jax.ShapeDtypeStruct((n, 768), f32)],
        compiler_params=_params(("parallel",)),
        name="nsa_prep",
    )(proj, proj, qgain, kgain)


def _ssd_prompt_kernel(xbc_ref, z_ref, sm_ref, cw_ref, cb_ref, dtb_ref, alog_ref, dexp_ref, nrm_ref,
                       y_ref, st_ref, cst_ref, xbuf, hst, ybuf):
    c = pl.program_id(1)
    cl = SSD_CHUNK

    @pl.when(c == 0)
    def _():
        xbuf[0:8, :] = jnp.zeros((8, SSD_CONV_CH), f32)
        hst[...] = jnp.zeros(hst.shape, f32)

    x = xbc_ref[...]
    xbuf[8:8 + cl, :] = x
    conv = cw_ref[3:4, :] * x
    for k in range(1, CONV_W):
        conv += cw_ref[3 - k:4 - k, :] * xbuf[pl.ds(8 - k, cl), :]
    xbuf[0:8, :] = x[cl - 8:cl, :]
    xc = _silu(conv + cb_ref[...])
    xs = xc[:, :D_SSD]

    dt = jnp.where(_iota((cl, 128), 1) < SSD_HEADS, _softplus(sm_ref[...] + dtb_ref[...]), 0.0)
    da = dt * (-jnp.exp(alog_ref[...]))
    row, col = _iota((cl, cl), 0), _iota((cl, cl), 1)
    causal = row >= col
    cum = _dot(causal.astype(f32), da, HI)
    cum_t = cum.T
    last = cum[cl - 1:cl, :]
    e_last = jnp.exp(last)
    spread = (_iota((128, D_SSD), 1) // SSD_HD == _iota((128, D_SSD), 0)).astype(f32)
    coef = _dot(jnp.concatenate([dt, jnp.exp(last - cum), jnp.exp(cum)], axis=0), spread, HI)
    xdt = xs * coef[0:cl]
    xw = xdt * coef[cl:2 * cl]
    e_cum = coef[2 * cl:3 * cl]

    hpg = SSD_HEADS // SSD_GROUPS
    gw = hpg * SSD_HD
    for g in range(SSD_GROUPS):
        bm = xc[:, D_SSD + g * SSD_STATE:D_SSD + (g + 1) * SSD_STATE]
        cm = xc[:, D_SSD + (SSD_GROUPS + g) * SSD_STATE:D_SSD + (SSD_GROUPS + g + 1) * SSD_STATE]
        cb = _dot_nt(cm, bm)
        h0 = hst[g]
        y_off = _dot_nt(cm, h0) * e_cum[:, g * gw:(g + 1) * gw]
        s_new = _dot_tn(xw[:, g * gw:(g + 1) * gw], bm)
        for r in range(hpg):
            h = g * hpg + r
            seg = cum[:, h:h + 1] - cum_t[h:h + 1, :]
            decay = jnp.where(causal, jnp.exp(jnp.where(causal, seg, 0.0)), 0.0)
            rs = slice(r * SSD_HD, (r + 1) * SSD_HD)
            ybuf[:, h * SSD_HD:(h + 1) * SSD_HD] = (_dot(cb * decay, xdt[:, h * SSD_HD:(h + 1) * SSD_HD])
                                                    + y_off[:, rs])
            hst[g, rs, :] = e_last[:, h:h + 1] * h0[rs, :] + s_new[rs, :]

    y = (ybuf[...] + dexp_ref[...] * xs) * _silu(z_ref[...])
    gw = D_SSD // SSD_GROUPS
    for g in range(SSD_GROUPS):
        yg = y[:, g * gw:(g + 1) * gw]
        yg = yg * lax.rsqrt(jnp.mean(yg * yg, axis=-1, keepdims=True) + EPS)
        y_ref[:, g * gw:(g + 1) * gw] = (yg * nrm_ref[:, g * gw:(g + 1) * gw]).astype(y_ref.dtype)

    @pl.when(c == pl.num_programs(1) - 1)
    def _():
        st_ref[...] = hst[...]
        cst_ref[...] = x[cl - (CONV_W - 1):cl, :]


def _ssd_prompt(proj3, lp):
    B, L, _ = proj3.shape
    cl = SSD_CHUNK
    gh = SSD_HEADS // SSD_GROUPS * SSD_HD
    vec = lambda w: pl.BlockSpec((1, w), lambda b, c: (0, 0))
    y, st, cst = pl.pallas_call(
        _ssd_prompt_kernel,
        grid=(B, L // cl),
        in_specs=[
            pl.BlockSpec((None, cl, SSD_CONV_CH), lambda b, c: (b, c, XBC_OFF // SSD_CONV_CH)),
            pl.BlockSpec((None, cl, D_SSD), lambda b, c: (b, c, ZS_OFF // D_SSD)),
            pl.BlockSpec((None, cl, 128), lambda b, c: (b, c, SM_OFF // 128)),
            pl.BlockSpec((CONV_W, SSD_CONV_CH), lambda b, c: (0, 0)),
            vec(SSD_CONV_CH), vec(128), vec(128), vec(D_SSD), vec(D_SSD),
        ],
        out_specs=[
            pl.BlockSpec((None, cl, D_SSD), lambda b, c: (b, c, 0)),
            pl.BlockSpec((None, SSD_GROUPS, gh, SSD_STATE), lambda b, c: (b, 0, 0, 0)),
            pl.BlockSpec((None, CONV_W - 1, SSD_CONV_CH), lambda b, c: (b, 0, 0)),
        ],
        out_shape=[
            jax.ShapeDtypeStruct((B, L, D_SSD), bf16),
            jax.ShapeDtypeStruct((B, SSD_GROUPS, gh, SSD_STATE), f32),
            jax.ShapeDtypeStruct((B, CONV_W - 1, SSD_CONV_CH), f32),
        ],
        scratch_shapes=[
            pltpu.VMEM((8 + cl, SSD_CONV_CH), f32),
            pltpu.VMEM((SSD_GROUPS, gh, SSD_STATE), f32),
            pltpu.VMEM((cl, D_SSD), f32),
        ],
        compiler_params=_params(("parallel", "arbitrary")),
        name="ssd_prompt",
    )(proj3, proj3, proj3, lp["ssd_conv_w"], lp["ssd_conv_b"], lp["ssd_dtb"], lp["ssd_alog"],
      lp["ssd_dexp"], lp["ssd_norm"])
    return y, st.reshape(B, SSD_HEADS, SSD_HD, SSD_STATE), cst


def _gdn_prompt_kernel(qkv_ref, z_ref, sm_ref, cw_ref, dtb_ref, alog_ref, nrm_ref,
                       y_ref, st_ref, cst_ref, xbuf, sst):
    c = pl.program_id(1)
    cl = GDN_CHUNK
    nch = GDN_STEP_CHUNKS
    tl = nch * cl

    @pl.when(c == 0)
    def _():
        xbuf[0:8, :] = jnp.zeros((8, GDN_CONV_CH), f32)
        sst[...] = jnp.zeros(sst.shape, f32)

    x = qkv_ref[...]
    xbuf[8:8 + tl, :] = x
    conv = cw_ref[3:4, :] * x
    for k in range(1, CONV_W):
        conv += cw_ref[3 - k:4 - k, :] * xbuf[pl.ds(8 - k, tl), :]
    xbuf[0:8, :] = x[tl - 8:tl, :]
    xc = _silu(conv)

    sm = sm_ref[...]
    beta_all = jax.nn.sigmoid(sm)
    g_all = -jnp.exp(alog_ref[...]) * _softplus(sm + dtb_ref[...])
    row, col = _iota((cl, cl), 0), _iota((cl, cl), 1)
    incl = row >= col
    strict = row > col
    eye = (row == col).astype(f32)
    rt, ct = _iota((tl, tl), 0), _iota((tl, tl), 1)
    cum = _dot(((rt >= ct) & (rt // cl == ct // cl)).astype(f32), g_all, HI)
    cum_t = cum.T

    units = [(ci, h) for ci in range(nch) for h in range(GDN_HEADS)]
    q, k, v, beta, cum_c, decay = {}, {}, {}, {}, {}, {}
    for u in units:
        ci, h = u
        rows = slice(ci * cl, (ci + 1) * cl)
        qh = xc[rows, h * GDN_HD:(h + 1) * GDN_HD]
        kh = xc[rows, D_GDN + h * GDN_HD:D_GDN + (h + 1) * GDN_HD]
        v[u] = xc[rows, 2 * D_GDN + h * GDN_HD:2 * D_GDN + (h + 1) * GDN_HD]
        q[u] = qh * lax.rsqrt(jnp.sum(qh * qh, axis=-1, keepdims=True) + EPS) * (GDN_HD ** -0.5)
        k[u] = kh * lax.rsqrt(jnp.sum(kh * kh, axis=-1, keepdims=True) + EPS)
        beta[u] = beta_all[rows, SM_B + h:SM_B + h + 1]
        cum_c[u] = cum[rows, SM_A + h:SM_A + h + 1]
        seg = cum_c[u] - cum_t[SM_A + h:SM_A + h + 1, rows]
        decay[u] = jnp.where(incl, jnp.exp(jnp.where(incl, seg, 0.0)), 0.0)
    pw = {u: -jnp.where(strict, beta[u] * _dot_nt(k[u], k[u]) * decay[u], 0.0) for u in units}
    inv = {u: eye + pw[u] for u in units}
    pw2 = {u: _split2(pw[u]) for u in units}
    for _ in range(5):
        pw2 = {u: _split2(_dot3(pw2[u], pw2[u])) for u in units}
        inv = {u: inv[u] + _dot3(_split2(inv[u]), pw2[u]) for u in units}
    rhs = {u: jnp.concatenate([beta[u] * v[u], (beta[u] * jnp.exp(cum_c[u])) * k[u]], axis=1) for u in units}
    sol = {u: _dot3(_split2(inv[u]), _split2(rhs[u])) for u in units}
    uu = {u: sol[u][:, :GDN_HD] for u in units}
    ww = {u: sol[u][:, GDN_HD:] for u in units}
    qk = {u: _dot_nt(q[u], k[u]) * decay[u] for u in units}

    s = [sst[h] for h in range(GDN_HEADS)]
    for ci in range(nch):
        rows = slice(ci * cl, (ci + 1) * cl)
        hs = [(ci, h) for h in range(GDN_HEADS)]
        v_new = {u: uu[u] - _dot(ww[u], s[u[1]]) for u in hs}
        o = {u: _dot(q[u] * jnp.exp(cum_c[u]), s[u[1]]) + _dot(qk[u], v_new[u]) for u in hs}
        for u in hs:
            h = u[1]
            last = cum[ci * cl + cl - 1:ci * cl + cl, SM_A + h:SM_A + h + 1]
            s[h] = jnp.exp(last) * s[h] + _dot_tn(k[u] * jnp.exp(last - cum_c[u]), v_new[u])
        for u in hs:
            sl = slice(u[1] * GDN_HD, (u[1] + 1) * GDN_HD)
            on = o[u] * lax.rsqrt(jnp.mean(o[u] * o[u], axis=-1, keepdims=True) + EPS)
            y_ref[rows, sl] = (on * nrm_ref[...] * _silu(z_ref[rows, sl])).astype(y_ref.dtype)
    for h in range(GDN_HEADS):
        sst[h] = s[h]

    @pl.when(c == pl.num_programs(1) - 1)
    def _():
        st_ref[...] = sst[...]
        cst_ref[...] = x[tl - (CONV_W - 1):tl, :]


def _gdn_prompt(proj3, lp):
    B, L, _ = proj3.shape
    cl = GDN_CHUNK * GDN_STEP_CHUNKS
    vec = lambda w: pl.BlockSpec((1, w), lambda b, c: (0, 0))
    return pl.pallas_call(
        _gdn_prompt_kernel,
        grid=(B, L // cl),
        in_specs=[
            pl.BlockSpec((None, cl, GDN_CONV_CH), lambda b, c: (b, c, QKV_OFF // GDN_CONV_CH)),
            pl.BlockSpec((None, cl, D_GDN), lambda b, c: (b, c, ZG_OFF // D_GDN)),
            pl.BlockSpec((None, cl, 128), lambda b, c: (b, c, SM_OFF // 128)),
            pl.BlockSpec((CONV_W, GDN_CONV_CH), lambda b, c: (0, 0)),
            vec(128), vec(128), vec(GDN_HD),
        ],
        out_specs=[
            pl.BlockSpec((None, cl, D_GDN), lambda b, c: (b, c, 0)),
            pl.BlockSpec((None, GDN_HEADS, GDN_HD, GDN_HD), lambda b, c: (b, 0, 0, 0)),
            pl.BlockSpec((None, CONV_W - 1, GDN_CONV_CH), lambda b, c: (b, 0, 0)),
        ],
        out_shape=[
            jax.ShapeDtypeStruct((B, L, D_GDN), bf16),
            jax.ShapeDtypeStruct((B, GDN_HEADS, GDN_HD, GDN_HD), f32),
            jax.ShapeDtypeStruct((B, CONV_W - 1, GDN_CONV_CH), f32),
        ],
        scratch_shapes=[
            pltpu.VMEM((8 + cl, GDN_CONV_CH), f32),
            pltpu.VMEM((GDN_HEADS, GDN_HD, GDN_HD), f32),
        ],
        compiler_params=_params(("parallel", "arbitrary")),
        name="gdn_prompt",
    )(proj3, proj3, proj3, lp["gdn_conv_w"], lp["gdn_dtb"], lp["gdn_alog"], lp["gdn_norm"])


def _cmp_prompt_kernel(k_ref, v_ref, pe_ref, w1_ref, w2_ref, kg_ref, kc_ref, vct_ref):
    nb = kc_ref.shape[0]
    for t, src in enumerate((k_ref, v_ref)):
        outs = []
        p1 = [jnp.zeros((nb, CMP_HID), f32) for _ in range(NSA_KV)]
        p2 = [jnp.zeros((nb, CMP_HID), f32) for _ in range(NSA_KV)]
        for l in range(CMP_STRIDE):
            xr = src[pl.ds(l, nb, stride=CMP_STRIDE), :]
            l2 = CMP_STRIDE + l
            for g in range(NSA_KV):
                xg = xr[:, g * NSA_HD:(g + 1) * NSA_HD]
                p1[g] += _dot(xg + pe_ref[t, l:l + 1, :], w1_ref[t, l * NSA_HD:(l + 1) * NSA_HD, :])
                p2[g] += _dot(xg + pe_ref[t, l2:l2 + 1, :], w1_ref[t, l2 * NSA_HD:(l2 + 1) * NSA_HD, :])
        for g in range(NSA_KV):
            hid = p1[g] + pltpu.roll(p2[g], nb - 1, 0)
            cmp = _dot(_silu(hid), w2_ref[t])
            if t == 0:
                cmp = cmp * lax.rsqrt(jnp.mean(cmp * cmp, axis=-1, keepdims=True) + EPS) * kg_ref[0:1, 0:NSA_HD]
            outs.append(cmp)
        both = jnp.concatenate(outs, axis=1)
        if t == 0:
            kc_ref[...] = both
        else:
            vct_ref[...] = both.T


def _cmp_prompt(kv6_3, lp):
    B, L, _ = kv6_3.shape
    nb = L // CMP_STRIDE
    full = lambda s: pl.BlockSpec(s, lambda b: (0,) * len(s))
    return pl.pallas_call(
        _cmp_prompt_kernel,
        grid=(B,),
        in_specs=[
            pl.BlockSpec((None, L, 128), lambda b: (b, 0, KV_CK // 128)),
            pl.BlockSpec((None, L, 128), lambda b: (b, 0, KV_CV // 128)),
            full((2, CMP_LEN, NSA_HD)), full((2, CMP_LEN * NSA_HD, CMP_HID)), full((2, CMP_HID, NSA_HD)),
            full((3, 128)),
        ],
        out_specs=[pl.BlockSpec((None, nb, 128), lambda b: (b, 0, 0)),
                   pl.BlockSpec((None, 128, nb), lambda b: (b, 0, 0))],
        out_shape=[jax.ShapeDtypeStruct((B, nb, 128), f32), jax.ShapeDtypeStruct((B, 128, nb), f32)],
        compiler_params=_params(("parallel",)),
        name="nsa_cmp_prompt",
    )(kv6_3, kv6_3, lp["cmp_pe"], lp["cmp_w1"], lp["cmp_w2"], lp["k_gain"])


def _nsa_prompt_kernel(rb_ref, qt_ref, ks_ref, kw_ref, vst_ref, vwt_ref, kc_ref, vct_ref, gt_ref,
                       y_ref, tb_ref, tc_ref, ext_ref, selm_ref):
    qi = pl.program_id(1)
    L = ks_ref.shape[0]
    tq = 128
    nl = NSA_REP * tq
    ncmp = kc_ref.shape[0]
    n_slc = L // SLC_BLOCK
    last = N_BUCKETS - 1

    @pl.when(qi == 0)
    def _():
        sub = _iota((128, tq), 0)
        qo = _iota((128, tq), 1)
        d_diag = qo - sub
        c_rel = jnp.where(sub < 64, sub, sub - 128)
        d_cmp = qo - CMP_STRIDE * c_rel - (CMP_LEN - 1)
        for g in range(NSA_KV):
            for r in range(NSA_REP):
                h = g * NSA_REP + r
                tab = [rb_ref[b, h] - rb_ref[last, h] for b in range(N_BUCKETS)]
                lanes = slice(r * tq, (r + 1) * tq)
                tb_ref[g, TB_FAR, :, lanes] = jnp.zeros((128, tq), f32)
                tb_ref[g, TB_PREV, :, lanes] = _bias_chain(d_diag + 128, tab)
                tb_ref[g, TB_DIAG, :, lanes] = jnp.where(d_diag >= 0, _bias_chain(d_diag, tab), NEG)
                tb_ref[g, TB_WIN0, :, lanes] = jnp.where(d_diag <= 0, 0.0, NEG)
                tb_ref[g, TB_NONE, :, lanes] = jnp.full((128, tq), NEG, f32)
                t_c = jnp.where(d_cmp < 0, 0.0, _bias_chain(d_cmp, tab))
                tc_ref[g, 0:128, lanes] = t_c
                tc_ref[g, 128:256, lanes] = t_c
        ext_ref[...] = (_iota((L, 128), 0) // SLC_BLOCK == _iota((L, 128), 1)).astype(bf16)

    q0 = qi * tq
    qpos = q0 + _iota((1, nl), 1) % tq
    qpos1 = q0 + _iota((1, tq), 1)
    sub = _iota((128, 1), 0)
    gt = gt_ref[...]

    jr = _iota((n_slc, tq), 0)
    cur = qpos1 // SLC_BLOCK
    valid = jr * SLC_BLOCK <= qpos1
    forced = (jr == 0) | (jr == cur) | (jr == cur - 1)
    j_i = _iota((n_slc, ncmp), 0)
    c_i = _iota((n_slc, ncmp), 1)
    ovl = ((c_i * CMP_STRIDE < j_i * SLC_BLOCK + SLC_BLOCK) & (c_i * CMP_STRIDE + CMP_LEN > j_i * SLC_BLOCK)
           & (c_i < ncmp - 1)).astype(f32)
    cmp_off = pl.multiple_of((128 - qi * (tq // CMP_STRIDE)) % 128, 8)

    groups = range(NSA_KV)
    gsl = [slice(g * NSA_HD, (g + 1) * NSA_HD) for g in groups]
    qt = [jnp.concatenate([qt_ref[(g * NSA_REP + r) * NSA_HD:(g * NSA_REP + r + 1) * NSA_HD, :]
                           for r in range(NSA_REP)], axis=1).astype(bf16) for g in groups]

    ok_c = (sub * CMP_STRIDE + (CMP_LEN - 1) <= qpos) & (sub < ncmp - 1)
    o_cmp = []
    for g in groups:
        st = _dot(kc_ref[:, gsl[g]], qt[g]) + tc_ref[g, pl.ds(cmp_off, 128), :]
        st = jnp.where(ok_c, st, NEG)
        e = jnp.exp(st - jnp.max(st, axis=0, keepdims=True))
        p = jnp.where(ok_c, e / jnp.sum(e, axis=0, keepdims=True), 0.0)
        o_cmp.append(_dot(vct_ref[gsl[g], :], p))
        p_sum = p[:, 0:tq] + p[:, tq:2 * tq] + p[:, 2 * tq:3 * tq] + p[:, 3 * tq:4 * tq]
        imp = _dot(ovl, p_sum, HI)
        score = jnp.where(valid, jnp.where(forced, 1e9, imp), -1e9)
        rank = jnp.zeros((n_slc, tq), f32)
        for i in range(n_slc):
            si = score[i:i + 1, :]
            rank += ((si > score) | ((si == score) & (jr > i))).astype(f32)
        sel = (rank < SLC_TOPN).astype(bf16)
        sel = jnp.concatenate([sel, jnp.zeros((128 - n_slc, tq), bf16)], axis=0)
        selx = jnp.dot(ext_ref[...], sel, preferred_element_type=f32)
        selm_ref[g] = (selx - 1.0) * -NEG

    ones_rows = jnp.ones((16, 128), f32)

    def attend(state, tiles, k_ref, vt_ref, selected):
        s, offs = {}, []
        for ti, (kt, table) in enumerate(tiles):
            ko = pl.multiple_of(kt * 128, 128)
            offs.append(ko)
            for g in groups:
                sg = _dot(k_ref[pl.ds(ko, 128), gsl[g]], qt[g]) + tb_ref[g, table]
                if selected:
                    sg = sg + jnp.concatenate([selm_ref[g, pl.ds(ko, 128), :]] * NSA_REP, axis=1)
                s[g, ti] = sg
        out = []
        for g in groups:
            m, acc = state[g]
            m_new = m
            for ti in range(len(tiles)):
                m_new = jnp.maximum(m_new, jnp.max(s[g, ti], axis=0, keepdims=True))
            acc = jnp.exp(m - m_new) * acc
            for ti in range(len(tiles)):
                vt = jnp.concatenate([vt_ref[gsl[g], pl.ds(offs[ti], 128)], ones_rows], axis=0)
                acc += _dot(vt, jnp.exp(s[g, ti] - m_new))
            out.append((m_new, acc))
        return tuple(out)

    finish = lambda st: [acc[0:NSA_HD] / acc[NSA_HD:NSA_HD + 1] for _, acc in st]
    init = tuple((jnp.full((1, nl), NEG, f32), jnp.zeros((NSA_HD + 16, nl), f32)) for _ in groups)
    far_table = lambda kt: jnp.where(kt == qi - 1, TB_PREV, TB_FAR)

    pair = lambda i, st: attend(st, [(2 * i, far_table(2 * i)), (2 * i + 1, far_table(2 * i + 1))],
                                ks_ref, vst_ref, True)
    st = lax.fori_loop(0, qi // 2, pair, init)
    st = lax.fori_loop(0, qi % 2, lambda i, st: attend(st, [(qi - 1, TB_PREV)], ks_ref, vst_ref, True), st)
    o_slc = finish(attend(st, [(qi, TB_DIAG)], ks_ref, vst_ref, True))

    w0 = jnp.maximum(qi - WINDOW // 128, 0)

    def win_table(kt):
        t = jnp.where(kt == qi - WINDOW // 128, TB_WIN0, TB_FAR)
        t = jnp.where(kt == qi - 1, TB_PREV, t)
        return jnp.where(kt == qi, TB_DIAG, jnp.where(kt > qi, TB_NONE, t))

    win_tiles = [(w0 + t, win_table(w0 + t)) for t in range(WINDOW // 128 + 1)]
    o_win = finish(attend(init, win_tiles, kw_ref, vwt_ref, False))

    y_parts = []
    for g in groups:
        for r in range(NSA_REP):
            lanes = slice(r * tq, (r + 1) * tq)
            h = g * NSA_REP + r
            gate = lambda t: gt[SM_G + t * NSA_HEADS + h:SM_G + t * NSA_HEADS + h + 1, :]
            y_parts.append(gate(0) * o_cmp[g][:, lanes] + gate(1) * o_slc[g][:, lanes]
                           + gate(2) * o_win[g][:, lanes])
    y_ref[...] = jnp.concatenate(y_parts, axis=0).T.astype(y_ref.dtype)


def _nsa_prompt(qt, kv6_3, kv6t, kc, vct, gt, rel_bias):
    B, L, _ = kv6_3.shape
    tq = 128
    nb = L // CMP_STRIDE
    return pl.pallas_call(
        _nsa_prompt_kernel,
        grid=(B, L // tq),
        in_specs=[
            pl.BlockSpec(memory_space=pltpu.SMEM),
            pl.BlockSpec((None, D_NSA, tq), lambda b, i: (b, 0, i)),
            pl.BlockSpec((None, L, 128), lambda b, i: (b, 0, KV_SK // 128)),
            pl.BlockSpec((None, L, 128), lambda b, i: (b, 0, KV_WK // 128)),
            pl.BlockSpec((None, 128, L), lambda b, i: (b, KV_SV // 128, 0)),
            pl.BlockSpec((None, 128, L), lambda b, i: (b, KV_WV // 128, 0)),
            pl.BlockSpec((None, nb, 128), lambda b, i: (b, 0, 0)),
            pl.BlockSpec((None, 128, nb), lambda b, i: (b, 0, 0)),
            pl.BlockSpec((None, 128, tq), lambda b, i: (b, 0, i)),
        ],
        out_specs=pl.BlockSpec((None, tq, D_NSA), lambda b, i: (b, i, 0)),
        out_shape=jax.ShapeDtypeStruct((B, L, D_NSA), bf16),
        scratch_shapes=[
            pltpu.VMEM((NSA_KV, TB_KINDS, 128, NSA_REP * tq), f32),
            pltpu.VMEM((NSA_KV, 256, NSA_REP * tq), f32),
            pltpu.VMEM((L, 128), bf16),
            pltpu.VMEM((NSA_KV, L, tq), f32),
        ],
        compiler_params=_params(("parallel", "arbitrary")),
        name="nsa_prompt",
    )(rel_bias, qt, kv6_3, kv6_3, kv6t, kv6t, kc, vct, gt)


def _prep_t_kernel(q_ref, kv_ref, qg_ref, kg_ref, qt_ref, kv6_ref, kv6t_ref, gt_ref):
    for s in range(D_NSA // 128):
        sl = slice(s * 128, (s + 1) * 128)
        qt_ref[sl, :] = (_group_rms(q_ref[:, sl], NSA_HD) * qg_ref[...] * (NSA_HD ** -0.5)).T
    kv = kv_ref[...]
    for j, off in enumerate((KV_CK, KV_CV, KV_SK, KV_SV, KV_WK, KV_WV)):
        x = kv[:, off:off + 128]
        if off == KV_SK:
            x = _group_rms(x, NSA_HD) * kg_ref[1:2, :]
        if off == KV_WK:
            x = _group_rms(x, NSA_HD) * kg_ref[2:3, :]
        kv6_ref[:, off:off + 128] = x
        kv6t_ref[off:off + 128, :] = x.T
    gt_ref[...] = jax.nn.sigmoid(kv[:, SM_OFF - KV_OFF:SM_OFF - KV_OFF + 128]).T


def _prep_t(proj3, qgain, kgain):
    B, L, _ = proj3.shape
    tm = min(L, 512)
    return pl.pallas_call(
        _prep_t_kernel,
        grid=(B, L // tm),
        in_specs=[
            pl.BlockSpec((None, tm, D_NSA), lambda b, i: (b, i, QN_OFF // D_NSA)),
            pl.BlockSpec((None, tm, 1024), lambda b, i: (b, i, KV_OFF // 1024)),
            pl.BlockSpec((1, 128), lambda b, i: (0, 0)),
            pl.BlockSpec((3, 128), lambda b, i: (0, 0)),
        ],
        out_specs=[pl.BlockSpec((None, D_NSA, tm), lambda b, i: (b, 0, i)),
                   pl.BlockSpec((None, tm, 768), lambda b, i: (b, i, 0)),
                   pl.BlockSpec((None, 768, tm), lambda b, i: (b, 0, i)),
                   pl.BlockSpec((None, 128, tm), lambda b, i: (b, 0, i))],
        out_shape=[jax.ShapeDtypeStruct((B, D_NSA, L), f32), jax.ShapeDtypeStruct((B, L, 768), f32),
                   jax.ShapeDtypeStruct((B, 768, L), f32), jax.ShapeDtypeStruct((B, 128, L), f32)],
        compiler_params=_params(("parallel", "parallel")),
        name="nsa_prep_t",
    )(proj3, proj3, qgain, kgain)


def _row0(x, rows=8):
    return jnp.where(_iota((rows, x.shape[1]), 0) == 0, x, 0.0)


def _ssd_decode_kernel(xbc_ref, z_ref, sm_ref, cprev_ref, sin_ref, cw_ref, cb_ref, dtb_ref, alog_ref, dexp_ref,
                       nrm_ref, y_ref, st_ref, cst_ref):
    x = xbc_ref[...]
    conv = cw_ref[3:4, :] * x
    for j in range(CONV_W - 1):
        conv += cw_ref[j:j + 1, :] * cprev_ref[j:j + 1, :]
    cst_ref[0:2, :] = cprev_ref[1:3, :]
    cst_ref[2:3, :] = x
    xc = _silu(conv + cb_ref[...])
    xs = xc[:, :D_SSD]
    dt = _softplus(sm_ref[...] + dtb_ref[...])
    da = dt * (-jnp.exp(alog_ref[...]))
    ys = []
    for g in range(SSD_GROUPS):
        bm = xc[:, D_SSD + g * SSD_STATE:D_SSD + (g + 1) * SSD_STATE]
        cm = xc[:, D_SSD + (SSD_GROUPS + g) * SSD_STATE:D_SSD + (SSD_GROUPS + g + 1) * SSD_STATE]
        cb = jnp.sum(cm * bm, axis=-1, keepdims=True)
        bm8, cm8 = _row0(bm), _row0(cm)
        for r in range(SSD_HEADS // SSD_GROUPS):
            h = g * (SSD_HEADS // SSD_GROUPS) + r
            xdt = xs[:, h * SSD_HD:(h + 1) * SSD_HD] * dt[:, h:h + 1]
            eda = jnp.exp(da[:, h:h + 1])
            h0 = sin_ref[h]
            ys.append(cb * xdt + eda * _dot_nt(cm8, h0, HI)[0:1])
            st_ref[h] = eda * h0 + _dot_tn(_row0(xdt), bm8, HI)
    y = (jnp.concatenate(ys, axis=1) + dexp_ref[...] * xs) * _silu(z_ref[...])
    gw = D_SSD // SSD_GROUPS
    for g in range(SSD_GROUPS):
        yg = y[:, g * gw:(g + 1) * gw]
        yg = yg * lax.rsqrt(jnp.mean(yg * yg, axis=-1, keepdims=True) + EPS)
        y_ref[:, g * gw:(g + 1) * gw] = yg * nrm_ref[:, g * gw:(g + 1) * gw]


def _ssd_decode(proj3, conv_state, state, layer, lp):
    B = proj3.shape[0]
    vec = lambda w: pl.BlockSpec((1, w), lambda b: (0, 0))
    return pl.pallas_call(
        _ssd_decode_kernel,
        grid=(B,),
        in_specs=[
            pl.BlockSpec((None, 1, SSD_CONV_CH), lambda b: (b, 0, XBC_OFF // SSD_CONV_CH)),
            pl.BlockSpec((None, 1, D_SSD), lambda b: (b, 0, ZS_OFF // D_SSD)),
            pl.BlockSpec((None, 1, 128), lambda b: (b, 0, SM_OFF // 128)),
            pl.BlockSpec((None, None, CONV_W - 1, SSD_CONV_CH), lambda b: (layer, b, 0, 0)),
            pl.BlockSpec((None, None, SSD_HEADS, SSD_HD, SSD_STATE), lambda b: (layer, b, 0, 0, 0)),
            pl.BlockSpec((CONV_W, SSD_CONV_CH), lambda b: (0, 0)),
            vec(SSD_CONV_CH), vec(128), vec(128), vec(D_SSD), vec(D_SSD),
        ],
        out_specs=[
            pl.BlockSpec((None, 1, D_SSD), lambda b: (b, 0, 0)),
            pl.BlockSpec((None, SSD_HEADS, SSD_HD, SSD_STATE), lambda b: (b, 0, 0, 0)),
            pl.BlockSpec((None, CONV_W - 1, SSD_CONV_CH), lambda b: (b, 0, 0)),
        ],
        out_shape=[
            jax.ShapeDtypeStruct((B, 1, D_SSD), f32),
            jax.ShapeDtypeStruct((B, SSD_HEADS, SSD_HD, SSD_STATE), f32),
            jax.ShapeDtypeStruct((B, CONV_W - 1, SSD_CONV_CH), f32),
        ],
        compiler_params=_params(("parallel",)),
        name="ssd_decode",
    )(proj3, proj3, proj3, conv_state, state, lp["ssd_conv_w"], lp["ssd_conv_b"], lp["ssd_dtb"], lp["ssd_alog"],
      lp["ssd_dexp"], lp["ssd_norm"])


def _gdn_decode_kernel(qkv_ref, z_ref, sm_ref, cprev_ref, sin_ref, cw_ref, dtb_ref, alog_ref, nrm_ref,
                       y_ref, st_ref, cst_ref):
    x = qkv_ref[...]
    conv = cw_ref[3:4, :] * x
    for j in range(CONV_W - 1):
        conv += cw_ref[j:j + 1, :] * cprev_ref[j:j + 1, :]
    cst_ref[0:2, :] = cprev_ref[1:3, :]
    cst_ref[2:3, :] = x
    xc = _silu(conv)
    sm = sm_ref[...]
    beta_all = jax.nn.sigmoid(sm)
    g_all = -jnp.exp(alog_ref[...]) * _softplus(sm + dtb_ref[...])
    for h in range(GDN_HEADS):
        sl = slice(h * GDN_HD, (h + 1) * GDN_HD)
        qh = xc[:, sl]
        kh = xc[:, D_GDN + h * GDN_HD:D_GDN + (h + 1) * GDN_HD]
        vh = xc[:, 2 * D_GDN + h * GDN_HD:2 * D_GDN + (h + 1) * GDN_HD]
        qh = qh * lax.rsqrt(jnp.sum(qh * qh, axis=-1, keepdims=True) + EPS) * (GDN_HD ** -0.5)
        kh = kh * lax.rsqrt(jnp.sum(kh * kh, axis=-1, keepdims=True) + EPS)
        beta = beta_all[:, SM_B + h:SM_B + h + 1]
        eg = jnp.exp(g_all[:, SM_A + h:SM_A + h + 1])
        s0 = sin_ref[h]
        k8 = _row0(kh)
        v_new = beta * vh - (beta * eg) * _dot(k8, s0, HI)[0:1]
        o = eg * _dot(_row0(qh), s0, HI)[0:1] + jnp.sum(qh * kh, axis=-1, keepdims=True) * v_new
        st_ref[h] = eg * s0 + _dot_tn(k8, _row0(v_new), HI)
        o = o * lax.rsqrt(jnp.mean(o * o, axis=-1, keepdims=True) + EPS)
        y_ref[:, sl] = o * nrm_ref[...] * _silu(z_ref[:, sl])


def _gdn_decode(proj3, conv_state, state, layer, lp):
    B = proj3.shape[0]
    vec = lambda w: pl.BlockSpec((1, w), lambda b: (0, 0))
    return pl.pallas_call(
        _gdn_decode_kernel,
        grid=(B,),
        in_specs=[
            pl.BlockSpec((None, 1, GDN_CONV_CH), lambda b: (b, 0, QKV_OFF // GDN_CONV_CH)),
            pl.BlockSpec((None, 1, D_GDN), lambda b: (b, 0, ZG_OFF // D_GDN)),
            pl.BlockSpec((None, 1, 128), lambda b: (b, 0, SM_OFF // 128)),
            pl.BlockSpec((None, None, CONV_W - 1, GDN_CONV_CH), lambda b: (layer, b, 0, 0)),
            pl.BlockSpec((None, None, GDN_HEADS, GDN_HD, GDN_HD), lambda b: (layer, b, 0, 0, 0)),
            pl.BlockSpec((CONV_W, GDN_CONV_CH), lambda b: (0, 0)),
            vec(128), vec(128), vec(GDN_HD),
        ],
        out_specs=[
            pl.BlockSpec((None, 1, D_GDN), lambda b: (b, 0, 0)),
            pl.BlockSpec((None, GDN_HEADS, GDN_HD, GDN_HD), lambda b: (b, 0, 0, 0)),
            pl.BlockSpec((None, CONV_W - 1, GDN_CONV_CH), lambda b: (b, 0, 0)),
        ],
        out_shape=[
            jax.ShapeDtypeStruct((B, 1, D_GDN), f32),
            jax.ShapeDtypeStruct((B, GDN_HEADS, GDN_HD, GDN_HD), f32),
            jax.ShapeDtypeStruct((B, CONV_W - 1, GDN_CONV_CH), f32),
        ],
        compiler_params=_params(("parallel",)),
        name="gdn_decode",
    )(proj3, proj3, proj3, conv_state, state, lp["gdn_conv_w"], lp["gdn_dtb"], lp["gdn_alog"], lp["gdn_norm"])


def _query_rows(q, g):
    rows = [q[:, (g * NSA_REP + r) * NSA_HD:(g * NSA_REP + r + 1) * NSA_HD] for r in range(NSA_REP)]
    return jnp.concatenate(rows + [jnp.zeros((8 - NSA_REP, NSA_HD), f32)], axis=0) * (NSA_HD ** -0.5)


def _masked_softmax(lg, ok):
    lg = jnp.where(ok, lg, NEG)
    e = jnp.exp(lg - jnp.max(lg, axis=-1, keepdims=True))
    return jnp.where(ok, e / jnp.sum(e, axis=-1, keepdims=True), 0.0)


def _head_lanes(parts):
    return jnp.concatenate([o[r:r + 1, :] for o in parts for r in range(NSA_REP)], axis=1)


def _nsa_dec_cmp_kernel(pt_ref, q_ref, pe_ref, w1_ref, w2_ref, kg_ref, tab_ref, pk_ref, pv_ref,
                        ocmp_ref, idx_ref, stage, rows, sem, *, layer, n_pages):
    b = pl.program_id(0)
    nb = pl.num_programs(0)
    n_past = n_pages * PAGE
    q_pos = n_past
    n_blk = n_past // CMP_STRIDE
    n_cmp = (n_past + 1 - CMP_LEN) // CMP_STRIDE + 1
    n_slc = -(-(n_past + 1) // SLC_BLOCK)
    nj = -(-n_slc // 128) * 128
    pools = (pk_ref, pv_ref)

    def page_copy(t, seq, p):
        return pltpu.make_async_copy(pools[t].at[layer, pt_ref[seq, p]], stage.at[t, p], sem.at[t])

    def start_all(t, seq):
        lax.fori_loop(0, n_pages, lambda p, c: (page_copy(t, seq, p).start(), c)[1], 0)

    def wait_all(t, seq):
        lax.fori_loop(0, n_pages, lambda p, c: (page_copy(t, seq, p).wait(), c)[1], 0)

    @pl.when(b == 0)
    def _():
        start_all(0, 0)
        start_all(1, 0)

    def to_rows(t):
        def body(p, c):
            rows[pl.ds(pl.multiple_of(p * PAGE, PAGE), PAGE), :] = stage[t, p].T
            return c
        lax.fori_loop(0, n_pages, body, 0, unroll=4)

    cmps = []
    for t in range(2):
        wait_all(t, b)
        to_rows(t)

        @pl.when(b + 1 < nb)
        def _():
            start_all(t, b + 1)

        p1 = jnp.zeros((n_blk, 2 * CMP_HID), f32)
        p2 = jnp.zeros((n_blk, 2 * CMP_HID), f32)
        for j in range(CMP_STRIDE // 2):
            xr = jnp.concatenate([rows[pl.ds(2 * j, n_blk, stride=CMP_STRIDE), :],
                                  rows[pl.ds(2 * j + 1, n_blk, stride=CMP_STRIDE), :]], axis=1)
            j2 = CMP_STRIDE // 2 + j
            p1 += jnp.dot((xr + pe_ref[t, j:j + 1, :]).astype(bf16), w1_ref[t, j], preferred_element_type=f32)
            p2 += jnp.dot((xr + pe_ref[t, j2:j2 + 1, :]).astype(bf16), w1_ref[t, j2], preferred_element_type=f32)
        hid = p1 + pltpu.roll(p2, n_blk - 1, 0)
        c = jnp.dot(_silu(hid).astype(bf16), w2_ref[t], preferred_element_type=f32)
        if t == 0:
            c = _group_rms(c, NSA_HD) * kg_ref[0:1, :]
        cmps.append(c)
    kc, vc = cmps

    q = q_ref[...]
    lane = _iota((8, n_blk), 1)
    cend = lane * CMP_STRIDE + (CMP_LEN - 1)
    ok = (cend <= q_pos) & (lane < n_cmp)
    c_i = _iota((n_blk, nj), 0)
    j_i = _iota((n_blk, nj), 1)
    ovl = ((c_i * CMP_STRIDE < j_i * SLC_BLOCK + SLC_BLOCK) & (c_i * CMP_STRIDE + CMP_LEN > j_i * SLC_BLOCK)
           & (j_i < n_slc) & (c_i < n_cmp)).astype(f32)
    jl = _iota((1, nj), 1)
    cur = q_pos // SLC_BLOCK
    valid = jl * SLC_BLOCK <= q_pos
    forced = (jl == 0) | (jl == cur) | (jl == cur - 1)
    ii = _iota((nj, nj), 0)
    jj = _iota((nj, nj), 1)
    kk = _iota((SLC_TOPN, nj), 0)
    j16 = _iota((SLC_TOPN, nj), 1)
    o_parts, idx_cols = [], []
    for g in range(NSA_KV):
        tabcols = [tab_ref[g, :, bk:bk + 1] for bk in range(N_BUCKETS)]
        lg = _dot_nt(_query_rows(q, g), kc[:, g * NSA_HD:(g + 1) * NSA_HD]) + _bias_chain(q_pos - cend, tabcols)
        p = _masked_softmax(lg, ok)
        o_parts.append(_dot(p, vc[:, g * NSA_HD:(g + 1) * NSA_HD]))
        p_sum = p[0:1] + p[1:2] + p[2:3] + p[3:4]
        imp = _dot(_row0(p_sum), ovl, HI)[0:1]
        score = jnp.where(valid, jnp.where(forced, 1e9, imp), -1e9)
        score = jnp.where(jl < n_slc, score, -3e9)
        s_c = jnp.sum(jnp.where(ii == jj, score, 0.0), axis=1, keepdims=True)
        rank_r = jnp.sum(((s_c > score) | ((s_c == score) & (ii < jj))).astype(f32), axis=0, keepdims=True)
        rank_c = jnp.sum(((score > s_c) | ((score == s_c) & (jj < ii))).astype(f32), axis=1, keepdims=True)
        sel_r = (rank_r < SLC_TOPN) & (jl < n_slc)
        sel_c = (rank_c < SLC_TOPN) & (ii[:, 0:1] < n_slc)
        pos_r = jnp.sum((sel_c & (ii < jj)).astype(f32), axis=0, keepdims=True)
        hit = (pos_r == kk.astype(f32)) & sel_r
        idx_cols.append(jnp.sum(jnp.where(hit, j16.astype(f32), 0.0), axis=1, keepdims=True))
    ocmp_ref[...] = _head_lanes(o_parts)
    l16 = _iota((SLC_TOPN, 128), 1)
    idx_ref[...] = (jnp.where(l16 == 0, idx_cols[0], 0.0) + jnp.where(l16 == 1, idx_cols[1], 0.0)).astype(i32)


def _nsa_dec_cmp(qn3, page_table, pool_k, pool_v, layer, lp):
    B = qn3.shape[0]
    n_pages = page_table.shape[1]
    full = lambda s: pl.BlockSpec(s, lambda b, pt: (0,) * len(s))
    return pl.pallas_call(
        functools.partial(_nsa_dec_cmp_kernel, layer=layer, n_pages=n_pages),
        grid_spec=pltpu.PrefetchScalarGridSpec(
            num_scalar_prefetch=1,
            grid=(B,),
            in_specs=[
                pl.BlockSpec((None, 1, D_NSA), lambda b, pt: (b, 0, 0)),
                full((2, CMP_LEN // 2, 256)), full((2, CMP_LEN // 2, 256, 2 * CMP_HID)),
                full((2, 2 * CMP_HID, 128)), full((3, 128)), full((NSA_KV, 8, N_BUCKETS)),
                pl.BlockSpec(memory_space=pl.ANY), pl.BlockSpec(memory_space=pl.ANY),
            ],
            out_specs=[pl.BlockSpec((None, 1, D_NSA), lambda b, pt: (b, 0, 0)),
                       pl.BlockSpec((None, SLC_TOPN, 128), lambda b, pt: (b, 0, 0))],
            scratch_shapes=[pltpu.VMEM((2, n_pages, NSA_KV * NSA_HD, PAGE), f32),
                            pltpu.VMEM((n_pages * PAGE, NSA_KV * NSA_HD), f32),
                            pltpu.SemaphoreType.DMA((2,))],
        ),
        out_shape=[jax.ShapeDtypeStruct((B, 1, D_NSA), f32), jax.ShapeDtypeStruct((B, SLC_TOPN, 128), i32)],
        compiler_params=_params(("arbitrary",)),
        name="nsa_dec_cmp",
    )(page_table, qn3, lp["cmp_pe2"], lp["cmp_w1bd"], lp["cmp_w2bd"], lp["k_gain"], lp["tab8"], pool_k, pool_v)


def _nsa_dec_sel_kernel(pt_ref, idx_ref, q_ref, kv_ref, sm_ref, ocmp_ref, tab_ref, wk_ref, wv_ref, pk_ref, pv_ref,
                        y_ref, wko_ref, wvo_ref, kg, vg, sem, *, layer, n_pages):
    b = pl.program_id(0)
    n_past = n_pages * PAGE
    q_pos = n_past
    n_buf = wk_ref.shape[1]
    new_blk = n_past // SLC_BLOCK
    per_page = PAGE // SLC_BLOCK

    def blk_copy(pool, buf, g, s, sm_i):
        j = jnp.minimum(idx_ref[b, s, g], new_blk - 1)
        src = pool.at[layer, pt_ref[b, j // per_page], pl.ds(g * NSA_HD, NSA_HD)]
        return pltpu.make_async_copy(src, buf.at[g, :, pl.ds(s * PAGE, PAGE)], sem.at[sm_i])

    for g in range(NSA_KV):
        for s in range(SLC_TOPN):
            blk_copy(pk_ref, kg, g, s, 0).start()
            blk_copy(pv_ref, vg, g, s, 1).start()

    kv = kv_ref[...]
    q = q_ref[...]
    gate = jax.nn.sigmoid(sm_ref[...])
    ocmp = ocmp_ref[...]

    ii, jj = _iota((128, 128), 0), _iota((128, 128), 1)
    column = lambda off: jnp.sum(jnp.where(ii == jj, kv[:, off:off + 128], 0.0), axis=1, keepdims=True)
    last_lane = _iota((128, n_buf), 1) == n_buf - 1
    wk, wv = wk_ref[...], wv_ref[...]
    wko_ref[...] = jnp.where(last_lane, column(KV_WK), pltpu.roll(wk, n_buf - 1, 1))
    wvo_ref[...] = jnp.where(last_lane, column(KV_WV), pltpu.roll(wv, n_buf - 1, 1))

    def attend(qg, kt, vt, bias, ok, k_new, v_new, bias_new, ok_new):
        lg = jnp.where(ok, _dot(qg, kt) + bias, NEG)
        lg_new = jnp.where(ok_new, jnp.sum(qg * k_new, axis=-1, keepdims=True) + bias_new, NEG)
        m = jnp.maximum(jnp.max(lg, axis=-1, keepdims=True), lg_new)
        e = jnp.where(ok, jnp.exp(lg - m), 0.0)
        e_new = jnp.where(ok_new, jnp.exp(lg_new - m), 0.0)
        den = jnp.sum(e, axis=-1, keepdims=True) + e_new
        p, p_new = e / den, e_new / den
        return _dot_nt(p, vt) + p_new * v_new

    for g in range(NSA_KV):
        for s in range(SLC_TOPN):
            blk_copy(pk_ref, kg, g, s, 0).wait()
            blk_copy(pv_ref, vg, g, s, 1).wait()

    lw = _iota((8, n_buf), 1)
    d_w = q_pos - (n_past - n_buf + lw)
    ok_w = (d_w >= 0) & (d_w <= WINDOW)
    ls = _iota((8, SLC_TOPN * PAGE), 1)
    slot = ls // PAGE
    zero = jnp.zeros((8, 1), i32)
    o_slc, o_win = [], []
    for g in range(NSA_KV):
        tabcols = [tab_ref[g, :, bk:bk + 1] for bk in range(N_BUCKETS)]
        qg = _query_rows(q, g)
        gs = slice(g * NSA_HD, (g + 1) * NSA_HD)
        b_new = _bias_chain(zero, tabcols)
        blk = jnp.zeros(ls.shape, i32)
        n_new = jnp.zeros((8, 1), i32)
        for s in range(SLC_TOPN):
            j = idx_ref[b, s, g]
            blk = jnp.where(slot == s, j, blk)
            n_new = jnp.where(j == new_blk, n_new + 1, n_new)
        k_pos = (jnp.minimum(blk, new_blk - 1) // per_page) * PAGE + ls % PAGE
        ok_s = (blk < new_blk) & (k_pos // SLC_BLOCK == blk)
        o_slc.append(attend(qg, kg[g], vg[g], _bias_chain(q_pos - k_pos, tabcols), ok_s,
                            kv[:, KV_SK + g * NSA_HD:KV_SK + (g + 1) * NSA_HD],
                            kv[:, KV_SV + g * NSA_HD:KV_SV + (g + 1) * NSA_HD], b_new, n_new > 0))
        o_win.append(attend(qg, wk[gs, :], wv[gs, :], _bias_chain(d_w, tabcols), ok_w,
                            kv[:, KV_WK + g * NSA_HD:KV_WK + (g + 1) * NSA_HD],
                            kv[:, KV_WV + g * NSA_HD:KV_WV + (g + 1) * NSA_HD], b_new, zero == 0))
    gl = lambda t: gate[:, SM_G + t * NSA_HEADS:SM_G + (t + 1) * NSA_HEADS]
    wide = lambda gt: jnp.concatenate([jnp.broadcast_to(gt[:, h:h + 1], (1, NSA_HD)) for h in range(NSA_HEADS)], axis=1)
    y_ref[...] = wide(gl(0)) * ocmp + wide(gl(1)) * _head_lanes(o_slc) + wide(gl(2)) * _head_lanes(o_win)


def _nsa_dec_sel(qn3, kv6_3, proj3, ocmp, idx, page_table, win_k, win_v, pool_k, pool_v, layer, lp):
    B = qn3.shape[0]
    n_pages = page_table.shape[1]
    n_buf = win_k.shape[3]
    full = lambda s: pl.BlockSpec(s, lambda b, pt, ix: (0,) * len(s))
    row = lambda w, j: pl.BlockSpec((None, 1, w), lambda b, pt, ix: (b, 0, j))
    win = pl.BlockSpec((None, None, 128, n_buf), lambda b, pt, ix: (layer, b, 0, 0))
    return pl.pallas_call(
        functools.partial(_nsa_dec_sel_kernel, layer=layer, n_pages=n_pages),
        grid_spec=pltpu.PrefetchScalarGridSpec(
            num_scalar_prefetch=2,
            grid=(B,),
            in_specs=[row(D_NSA, 0), row(768, 0), row(128, SM_OFF // 128), row(D_NSA, 0),
                      full((NSA_KV, 8, N_BUCKETS)), win, win,
                      pl.BlockSpec(memory_space=pl.ANY), pl.BlockSpec(memory_space=pl.ANY)],
            out_specs=[row(D_NSA, 0),
                       pl.BlockSpec((None, 128, n_buf), lambda b, pt, ix: (b, 0, 0)),
                       pl.BlockSpec((None, 128, n_buf), lambda b, pt, ix: (b, 0, 0))],
            scratch_shapes=[pltpu.VMEM((NSA_KV, NSA_HD, SLC_TOPN * PAGE), f32),
                            pltpu.VMEM((NSA_KV, NSA_HD, SLC_TOPN * PAGE), f32),
                            pltpu.SemaphoreType.DMA((2,))],
        ),
        out_shape=[jax.ShapeDtypeStruct((B, 1, D_NSA), f32),
                   jax.ShapeDtypeStruct((B, 128, n_buf), f32), jax.ShapeDtypeStruct((B, 128, n_buf), f32)],
        compiler_params=_params(("arbitrary",)),
        name="nsa_dec_sel",
    )(page_table, idx, qn3, kv6_3, proj3, ocmp, lp["tab8"], win_k, win_v, pool_k, pool_v)


def _layer_params(i, p):
    pad128 = lambda v, off: jnp.zeros((1, 128), f32).at[0, off:off + v.shape[0]].set(v)
    return dict(
        ssd_conv_w=p["ssd_conv_w"][i], ssd_conv_b=p["ssd_conv_b"][i][None],
        ssd_dtb=pad128(p["ssd_dt_bias"][i], SM_DT), ssd_alog=pad128(p["ssd_a_log"][i], SM_DT),
        ssd_dexp=jnp.repeat(p["ssd_d"][i], SSD_HD)[None], ssd_norm=p["ssd_norm"][i][None],
        gdn_conv_w=p["gdn_conv_w"][i],
        gdn_dtb=pad128(p["gdn_dt_bias"][i], SM_A), gdn_alog=pad128(p["gdn_a_log"][i], SM_A),
        gdn_norm=p["gdn_norm"][i][None],
        q_gain=jnp.tile(p["nsa_q_norm"][i], 2)[None], k_gain=jnp.tile(p["nsa_k_norm"][i], (1, 2)),
        cmp_pe=p["nsa_cmp_pe"][i], cmp_w1=p["nsa_cmp_w1"][i], cmp_w2=p["nsa_cmp_w2"][i],
        cmp_pe2=jnp.tile(p["nsa_cmp_pe"][i], (1, 1, 2)).reshape(2, CMP_LEN // 2, 256),
        cmp_w1bd=_block_diag2(p["nsa_cmp_w1"][i].reshape(2, CMP_LEN, NSA_HD, CMP_HID)).astype(bf16)
        .reshape(2, CMP_LEN // 2, 256, 2 * CMP_HID),
        cmp_w2bd=_block_diag2(p["nsa_cmp_w2"][i]).astype(bf16),
        tab8=jnp.pad(p["rel_bias"].T.reshape(NSA_KV, NSA_REP, N_BUCKETS), ((0, 0), (0, 8 - NSA_REP), (0, 0))),
    )


def _block_diag2(w):
    z = jnp.zeros_like(w)
    return jnp.concatenate([jnp.concatenate([w, z], axis=-1), jnp.concatenate([z, w], axis=-1)], axis=-2)


def _reorder_w_in(w_in):
    cuts = [(1024, 2560), (2576, 4112), (0, 1024), (4112, 4624), (4632, 5144), (5144, 5912),
            (2560, 2576), (4624, 4632), (5912, 5936)]
    wt = jnp.transpose(w_in, (0, 2, 1))
    parts = [wt[:, a:b] for a, b in cuts]
    used = sum(b - a for a, b in cuts)
    parts.append(jnp.zeros((w_in.shape[0], PROJ_W - used, w_in.shape[1]), w_in.dtype))
    return jnp.concatenate(parts, axis=1).astype(bf16)


def _mix_prompt(x3, i, p, lp, w_r):
    B, L, _ = x3.shape
    n = B * L
    x = x3.reshape(n, D_MODEL)
    proj = _inproj(x, p["norm_mix"], w_r, i)
    proj3 = proj.reshape(B, L, PROJ_W)
    y_ssd, st_ssd, cst_ssd = _ssd_prompt(proj3, lp)
    y_gdn, st_gdn, cst_gdn = _gdn_prompt(proj3, lp)
    qt, kv6_3, kv6t, gt = _prep_t(proj3, lp["q_gain"], lp["k_gain"])
    kc, vct = _cmp_prompt(kv6_3, lp)
    y_nsa = _nsa_prompt(qt, kv6_3, kv6t, kc, vct, gt, p["rel_bias"])
    x = _outproj(x, y_ssd.reshape(n, D_SSD), y_gdn.reshape(n, D_GDN), y_nsa.reshape(n, D_NSA), p["w_out"], i)
    rows = lambda off: jnp.transpose(kv6t[:, off:off + 128].reshape(B, NSA_KV, NSA_HD, L), (0, 3, 1, 2))
    keep = min(WINDOW, L)
    state = (st_ssd, cst_ssd, st_gdn, cst_gdn, rows(KV_CK), rows(KV_CV), rows(KV_SK), rows(KV_SV),
             rows(KV_WK)[:, L - keep:], rows(KV_WV)[:, L - keep:])
    return x.reshape(B, L, D_MODEL), state


def _mix_decode(x, i, p, lp, w_r, st):
    B = x.shape[0]
    proj = _inproj(x, p["norm_mix"], w_r, i)
    proj3 = proj.reshape(B, 1, PROJ_W)
    y_ssd, st_ssd, cst_ssd = _ssd_decode(proj3, st["ssd_conv"], st["ssd"], i, lp)
    y_gdn, st_gdn, cst_gdn = _gdn_decode(proj3, st["gdn_conv"], st["gdn"], i, lp)
    qn, kv6 = _prep(proj, lp["q_gain"], lp["k_gain"])
    qn3, kv6_3 = qn.reshape(B, 1, D_NSA), kv6.reshape(B, 1, 768)
    o_cmp, idx = _nsa_dec_cmp(qn3, st["page_table"], st["cmp_k"], st["cmp_v"], i, lp)
    y_nsa, win_k, win_v = _nsa_dec_sel(qn3, kv6_3, proj3, o_cmp, idx[:, :, :NSA_KV], st["page_table"],
                                       st["win_k"], st["win_v"], st["slc_k"], st["slc_v"], i, lp)
    x = _outproj(x, y_ssd.reshape(B, D_SSD), y_gdn.reshape(B, D_GDN), y_nsa.reshape(B, D_NSA), p["w_out"], i)
    rows = lambda off: kv6[:, off:off + 128].reshape(B, 1, NSA_KV, NSA_HD)
    n_buf = win_k.shape[2]
    buf = lambda w: jnp.transpose(w.reshape(B, NSA_KV, NSA_HD, n_buf), (0, 3, 1, 2))
    state = (st_ssd, cst_ssd, st_gdn, cst_gdn, rows(KV_CK), rows(KV_CV), rows(KV_SK), rows(KV_SV),
             buf(win_k), buf(win_v))
    return x, state


def kernel(x_prompt, x_sample, state_ssd, state_ssd_conv, state_gdn, state_gdn_conv, cache_cmp_k, cache_cmp_v,
           cache_slc_k, cache_slc_v, cache_win_k, cache_win_v, page_table, rel_bias, norm_ffn1, w_ffn1_gate,
           w_ffn1_up, w_ffn1_down, norm_mix, w_in, ssd_conv_w, ssd_conv_b, ssd_dt_bias, ssd_a_log, ssd_d, ssd_norm,
           gdn_conv_w, gdn_dt_bias, gdn_a_log, gdn_norm, nsa_q_norm, nsa_k_norm, nsa_cmp_pe, nsa_cmp_w1,
           nsa_cmp_w2, w_out, norm_ffn2, w_ffn2_gate, w_ffn2_up, w_ffn2_down):
    bp, lp_len, _ = x_prompt.shape
    bs = x_sample.shape[0]
    gain3 = lambda g: g.reshape(DEPTH, 1, D_MODEL)
    p = dict(rel_bias=rel_bias, norm_mix=gain3(norm_mix), w_out=w_out.astype(bf16), ssd_conv_w=ssd_conv_w,
             ssd_conv_b=ssd_conv_b,
             ssd_dt_bias=ssd_dt_bias, ssd_a_log=ssd_a_log, ssd_d=ssd_d, ssd_norm=ssd_norm, gdn_conv_w=gdn_conv_w,
             gdn_dt_bias=gdn_dt_bias, gdn_a_log=gdn_a_log, gdn_norm=gdn_norm, nsa_q_norm=nsa_q_norm,
             nsa_k_norm=nsa_k_norm, nsa_cmp_pe=nsa_cmp_pe, nsa_cmp_w1=nsa_cmp_w1, nsa_cmp_w2=nsa_cmp_w2)
    n1, n2 = gain3(norm_ffn1), gain3(norm_ffn2)
    chan_row = lambda c: jnp.transpose(c, (0, 1, 3, 4, 2)).reshape(c.shape[:2] + (NSA_KV * NSA_HD, c.shape[2]))
    st = dict(ssd=state_ssd, ssd_conv=state_ssd_conv, gdn=state_gdn, gdn_conv=state_gdn_conv,
              cmp_k=chan_row(cache_cmp_k), cmp_v=chan_row(cache_cmp_v), slc_k=chan_row(cache_slc_k),
              slc_v=chan_row(cache_slc_v), win_k=chan_row(cache_win_k), win_v=chan_row(cache_win_v),
              page_table=page_table)
    w_r = _reorder_w_in(w_in)

    hp = x_prompt.reshape(bp * lp_len, D_MODEL)
    hs = x_sample.reshape(bs, D_MODEL)
    outs_p, outs_s = [], []
    for i in range(DEPTH):
        lp = _layer_params(i, p)
        hp, hs = _ffn(hp, hs, n1, w_ffn1_gate, w_ffn1_up, w_ffn1_down, i)
        hp3, st_p = _mix_prompt(hp.reshape(bp, lp_len, D_MODEL), i, p, lp, w_r)
        hs, st_s = _mix_decode(hs, i, p, lp, w_r, st)
        hp, hs = _ffn(hp3.reshape(bp * lp_len, D_MODEL), hs, n2, w_ffn2_gate, w_ffn2_up, w_ffn2_down, i)
        outs_p.append(st_p)
        outs_s.append(st_s)
    stack = lambda outs: [jnp.stack(t) for t in zip(*outs)]
    return (hp.reshape(bp, lp_len, D_MODEL), hs.reshape(bs, 1, D_MODEL), *stack(outs_p), *stack(outs_s))
```

```python
import functools
import math

import numpy as np
import jax
import jax.numpy as jnp
from jax import lax
from jax.experimental import pallas as pl
from jax.experimental.pallas import tpu as pltpu

f32 = jnp.float32
bf16 = jnp.bfloat16
i32 = jnp.int32
HI = lax.Precision.HIGHEST

D_MODEL = 2048
DEPTH = 4
PAGE = 128
D_SSD = 1024
SSD_HD = 64
SSD_HEADS = 16
SSD_GROUPS = 2
SSD_STATE = 128
SSD_CHUNK = 128
D_GDN = 512
GDN_HD = 128
GDN_HEADS = 4
GDN_CHUNK = 64
GDN_STEP_CHUNKS = 4
D_NSA = 512
NSA_HD = 64
NSA_HEADS = 8
NSA_KV = 2
NSA_REP = 4
CMP_STRIDE = 16
CMP_LEN = 32
CMP_HID = 128
SLC_BLOCK = 64
SLC_TOPN = 16
WINDOW = 512
N_BUCKETS = 32
MAX_DISTANCE = 128
CONV_W = 4
D_FF = 5632
SSD_CONV_CH = 1536
GDN_CONV_CH = 1536
EPS = 1e-6
NEG = -1e30

XBC_OFF = 0
QKV_OFF = 1536
ZS_OFF = 3072
ZG_OFF = 4096
QN_OFF = 4608
KV_OFF = 5120
SM_OFF = 5888
PROJ_W = 6144
SM_DT, SM_A, SM_B, SM_G = 0, 16, 20, 24
KV_CK, KV_CV, KV_SK, KV_SV, KV_WK, KV_WV = 0, 128, 256, 384, 512, 640

VMEM_LIMIT = 56 * 1024 * 1024
TB_FAR, TB_PREV, TB_DIAG, TB_WIN0, TB_NONE, TB_KINDS = 0, 1, 2, 3, 4, 5


def _bucket_thresholds():
    exact = N_BUCKETS // 2
    d = np.arange(0, 4 * MAX_DISTANCE)
    nf = np.maximum(d, 1).astype(np.float32)
    large = exact + (np.log(nf / np.float32(exact)) / np.float32(math.log(MAX_DISTANCE / exact))
                     * np.float32(N_BUCKETS - exact)).astype(np.int32)
    bucket = np.where(d < exact, d, np.minimum(large, N_BUCKETS - 1))
    return [int(np.argmax(bucket >= b)) for b in range(N_BUCKETS)]


_THR = _bucket_thresholds()
BIAS_FAR = _THR[N_BUCKETS - 1]


def _bias_chain(d, tabcols):
    v = jnp.broadcast_to(tabcols[N_BUCKETS - 1], d.shape)
    for b in range(N_BUCKETS - 2, -1, -1):
        v = jnp.where(d < _THR[b + 1], tabcols[b], v)
    return v


def _silu(x):
    return x * jax.nn.sigmoid(x)


def _softplus(x):
    return jnp.maximum(x, 0.0) + jnp.log1p(jnp.exp(-jnp.abs(x)))


def _dot(a, b, prec=None):
    if prec is None:
        a, b = a.astype(bf16), b.astype(bf16)
    return jnp.dot(a, b, preferred_element_type=f32, precision=prec)


def _dot_nt(a, b, prec=None):
    if prec is None:
        a, b = a.astype(bf16), b.astype(bf16)
    return lax.dot_general(a, b, (((1,), (1,)), ((), ())), preferred_element_type=f32, precision=prec)


def _dot_tn(a, b, prec=None):
    if prec is None:
        a, b = a.astype(bf16), b.astype(bf16)
    return lax.dot_general(a, b, (((0,), (0,)), ((), ())), preferred_element_type=f32, precision=prec)


def _split2(a):
    hi = a.astype(bf16)
    return hi, (a - hi.astype(f32)).astype(bf16)


def _dot3(a2, b2):
    d = lambda x, y: jnp.dot(x, y, preferred_element_type=f32)
    return d(a2[0], b2[0]) + (d(a2[0], b2[1]) + d(a2[1], b2[0]))


def _split3(a):
    a1 = a.astype(bf16)
    r = a - a1.astype(f32)
    a2 = r.astype(bf16)
    return a1, a2, (r - a2.astype(f32)).astype(bf16)


def _dot_01(a, m01):
    m = m01.astype(bf16)
    return sum(jnp.dot(t, m, preferred_element_type=f32) for t in _split3(a))


def _dot_01l(m01, b):
    m = m01.astype(bf16)
    return sum(jnp.dot(m, t, preferred_element_type=f32) for t in _split3(b))


def _iota(shape, dim):
    return lax.broadcasted_iota(i32, shape, dim)


def _group_ones(n, width):
    return (_iota((n, n), 0) // width == _iota((n, n), 1) // width).astype(f32)


def _group_rms(x, width):
    ss = _dot_01(x * x, _group_ones(x.shape[1], width))
    return x * lax.rsqrt(ss * (1.0 / width) + EPS)


def _params(sem):
    return pltpu.CompilerParams(dimension_semantics=sem, vmem_limit_bytes=VMEM_LIMIT)


FFN_EXTRA = 16


def _ffn_kernel(x_ref, xs_ref, g_ref, wg_ref, wu_ref, wd_ref, o_ref, os_ref, h_ref):
    tm, ns = x_ref.shape[0], xs_ref.shape[0]

    @pl.when(pl.program_id(1) == 0)
    def _():
        rms = lambda x: x * lax.rsqrt(jnp.mean(x * x, axis=-1, keepdims=True) + EPS) * g_ref[...]
        x, xs = x_ref[...], xs_ref[...]
        h_ref[0:tm, :] = rms(x).astype(bf16)
        h_ref[tm:tm + FFN_EXTRA, :] = jnp.concatenate(
            [rms(xs), jnp.zeros((FFN_EXTRA - ns, D_MODEL), f32)], axis=0).astype(bf16)
        o_ref[...] = x
        os_ref[...] = xs

    h = h_ref[...]
    a = jnp.dot(h, wg_ref[...].astype(bf16), preferred_element_type=f32)
    u = jnp.dot(h, wu_ref[...].astype(bf16), preferred_element_type=f32)
    y = 0.5 * _dot(_silu(a) * u, wd_ref[...])
    o_ref[...] += y[0:tm]
    os_ref[...] += y[tm:tm + ns]


def _ffn(x, xs, gain, wg, wu, wd, layer):
    n, ns = x.shape[0], xs.shape[0]
    tm, tf = 1024, 256
    return pl.pallas_call(
        _ffn_kernel,
        grid=(n // tm, D_FF // tf),
        in_specs=[
            pl.BlockSpec((tm, D_MODEL), lambda i, j: (i, 0)),
            pl.BlockSpec((ns, D_MODEL), lambda i, j: (0, 0)),
            pl.BlockSpec((None, 1, D_MODEL), lambda i, j: (layer, 0, 0)),
            pl.BlockSpec((None, D_MODEL, tf), lambda i, j: (layer, 0, j)),
            pl.BlockSpec((None, D_MODEL, tf), lambda i, j: (layer, 0, j)),
            pl.BlockSpec((None, tf, D_MODEL), lambda i, j: (layer, j, 0)),
        ],
        out_specs=[pl.BlockSpec((tm, D_MODEL), lambda i, j: (i, 0)),
                   pl.BlockSpec((ns, D_MODEL), lambda i, j: (0, 0))],
        out_shape=[jax.ShapeDtypeStruct((n, D_MODEL), f32), jax.ShapeDtypeStruct((ns, D_MODEL), f32)],
        scratch_shapes=[pltpu.VMEM((tm + FFN_EXTRA, D_MODEL), bf16)],
        compiler_params=_params(("arbitrary", "arbitrary")),
        name="ffn",
    )(x, xs, gain, wg, wu, wd)


def _inproj_kernel(x_ref, g_ref, w_ref, o_ref, h_ref):
    @pl.when(pl.program_id(1) == 0)
    def _():
        x = x_ref[...]
        h = x * lax.rsqrt(jnp.mean(x * x, axis=-1, keepdims=True) + EPS) * g_ref[...]
        h_ref[...] = h.astype(bf16)

    o_ref[...] = _dot_nt(h_ref[...], w_ref[...])


def _inproj(x, gain, w_r, layer):
    n = x.shape[0]
    tm = min(n, 1024)
    tn = 512
    return pl.pallas_call(
        _inproj_kernel,
        grid=(n // tm, PROJ_W // tn),
        in_specs=[
            pl.BlockSpec((tm, D_MODEL), lambda i, j: (i, 0)),
            pl.BlockSpec((None, 1, D_MODEL), lambda i, j: (layer, 0, 0)),
            pl.BlockSpec((None, tn, D_MODEL), lambda i, j: (layer, j, 0)),
        ],
        out_specs=pl.BlockSpec((tm, tn), lambda i, j: (i, j)),
        out_shape=jax.ShapeDtypeStruct((n, PROJ_W), f32),
        scratch_shapes=[pltpu.VMEM((tm, D_MODEL), bf16)],
        compiler_params=_params(("parallel", "arbitrary")),
        name="inproj",
    )(x, gain, w_r)


def _outproj_kernel(x_ref, ys_ref, yg_ref, yn_ref, ws_ref, wg_ref, wn_ref, o_ref):
    acc = _dot(ys_ref[...], ws_ref[...])
    acc += _dot(yg_ref[...], wg_ref[...])
    acc += _dot(yn_ref[...], wn_ref[...])
    o_ref[...] = x_ref[...] + acc


def _outproj(x, ys, yg, yn, w_out, layer):
    n = x.shape[0]
    tm = min(n, 2048)
    tn = 512
    return pl.pallas_call(
        _outproj_kernel,
        grid=(n // tm, D_MODEL // tn),
        in_specs=[
            pl.BlockSpec((tm, tn), lambda i, j: (i, j)),
            pl.BlockSpec((tm, D_SSD), lambda i, j: (i, 0)),
            pl.BlockSpec((tm, D_GDN), lambda i, j: (i, 0)),
            pl.BlockSpec((tm, D_NSA), lambda i, j: (i, 0)),
            pl.BlockSpec((None, D_SSD, tn), lambda i, j: (layer, 0, j)),
            pl.BlockSpec((None, D_GDN, tn), lambda i, j: (layer, 2, j)),
            pl.BlockSpec((None, D_NSA, tn), lambda i, j: (layer, 3, j)),
        ],
        out_specs=pl.BlockSpec((tm, tn), lambda i, j: (i, j)),
        out_shape=jax.ShapeDtypeStruct((n, D_MODEL), f32),
        compiler_params=_params(("parallel", "arbitrary")),
        name="outproj",
    )(x, ys, yg, yn, w_out, w_out, w_out)


def _prep_kernel(q_ref, kv_ref, qg_ref, kg_ref, qn_ref, kv6_ref):
    for s in range(D_NSA // 128):
        sl = slice(s * 128, (s + 1) * 128)
        qn_ref[:, sl] = _group_rms(q_ref[:, sl], NSA_HD) * qg_ref[...]
    kv = kv_ref[...]
    kv6_ref[...] = kv[:, :6 * 128]
    kv6_ref[:, KV_SK:KV_SK + 128] = _group_rms(kv[:, KV_SK:KV_SK + 128], NSA_HD) * kg_ref[1:2, :]
    kv6_ref[:, KV_WK:KV_WK + 128] = _group_rms(kv[:, KV_WK:KV_WK + 128], NSA_HD) * kg_ref[2:3, :]


def _prep(proj, qgain, kgain):
    n = proj.shape[0]
    tm = min(n, 512)
    return pl.pallas_call(
        _prep_kernel,
        grid=(n // tm,),
        in_specs=[
            pl.BlockSpec((tm, D_NSA), lambda i: (i, QN_OFF // D_NSA)),
            pl.BlockSpec((tm, 1024), lambda i: (i, KV_OFF // 1024)),
            pl.BlockSpec((1, 128), lambda i: (0, 0)),
            pl.BlockSpec((3, 128), lambda i: (0, 0)),
        ],
        out_specs=[pl.BlockSpec((tm, D_NSA), lambda i: (i, 0)),
                   pl.BlockSpec((tm, 768), lambda i: (i, 0))],
        out_shape=[jax.ShapeDtypeStruct((n, D_NSA), f32), jax.ShapeDtypeStruct((n, 768), f32)],
        compiler_params=_params(("parallel",)),
        name="nsa_prep",
    )(proj, proj, qgain, kgain)


def _ssd_prompt_kernel(xbc_ref, z_ref, sm_ref, cw_ref, cb_ref, dtb_ref, alog_ref, dexp_ref, nrm_ref,
                       y_ref, st_ref, cst_ref, xbuf, hst, ybuf):
    c = pl.program_id(1)
    cl = SSD_CHUNK

    @pl.when(c == 0)
    def _():
        xbuf[0:8, :] = jnp.zeros((8, SSD_CONV_CH), f32)
        hst[...] = jnp.zeros(hst.shape, f32)

    x = xbc_ref[...]
    xbuf[8:8 + cl, :] = x
    conv = cw_ref[3:4, :] * x
    for k in range(1, CONV_W):
        conv += cw_ref[3 - k:4 - k, :] * xbuf[pl.ds(8 - k, cl), :]
    xbuf[0:8, :] = x[cl - 8:cl, :]
    xc = _silu(conv + cb_ref[...])
    xs = xc[:, :D_SSD]

    dt = jnp.where(_iota((cl, 128), 1) < SSD_HEADS, _softplus(sm_ref[...] + dtb_ref[...]), 0.0)
    da = dt * (-jnp.exp(alog_ref[...]))
    row, col = _iota((cl, cl), 0), _iota((cl, cl), 1)
    causal = row >= col
    cum = _dot_01l(causal, da)
    cum_t = cum.T
    last = cum[cl - 1:cl, :]
    e_last = jnp.exp(last)
    spread = (_iota((128, D_SSD), 1) // SSD_HD == _iota((128, D_SSD), 0)).astype(f32)
    coef = _dot_01(jnp.concatenate([dt, jnp.exp(last - cum), jnp.exp(cum)], axis=0), spread)
    xdt = xs * coef[0:cl]
    xw = xdt * coef[cl:2 * cl]
    e_cum = coef[2 * cl:3 * cl]

    hpg = SSD_HEADS // SSD_GROUPS
    gw = hpg * SSD_HD
    for g in range(SSD_GROUPS):
        bm = xc[:, D_SSD + g * SSD_STATE:D_SSD + (g + 1) * SSD_STATE]
        cm = xc[:, D_SSD + (SSD_GROUPS + g) * SSD_STATE:D_SSD + (SSD_GROUPS + g + 1) * SSD_STATE]
        cb = _dot_nt(cm, bm)
        h0 = hst[g]
        y_off = _dot_nt(cm, h0) * e_cum[:, g * gw:(g + 1) * gw]
        s_new = _dot_tn(xw[:, g * gw:(g + 1) * gw], bm)
        for r in range(hpg):
            h = g * hpg + r
            seg = cum[:, h:h + 1] - cum_t[h:h + 1, :]
            decay = jnp.where(causal, jnp.exp(jnp.where(causal, seg, 0.0)), 0.0)
            rs = slice(r * SSD_HD, (r + 1) * SSD_HD)
            ybuf[:, h * SSD_HD:(h + 1) * SSD_HD] = (_dot(cb * decay, xdt[:, h * SSD_HD:(h + 1) * SSD_HD])
                                                    + y_off[:, rs])
            hst[g, rs, :] = e_last[:, h:h + 1] * h0[rs, :] + s_new[rs, :]

    y = (ybuf[...] + dexp_ref[...] * xs) * _silu(z_ref[...])
    gw = D_SSD // SSD_GROUPS
    for g in range(SSD_GROUPS):
        yg = y[:, g * gw:(g + 1) * gw]
        yg = yg * lax.rsqrt(jnp.mean(yg * yg, axis=-1, keepdims=True) + EPS)
        y_ref[:, g * gw:(g + 1) * gw] = (yg * nrm_ref[:, g * gw:(g + 1) * gw]).astype(y_ref.dtype)

    @pl.when(c == pl.num_programs(1) - 1)
    def _():
        st_ref[...] = hst[...]
        cst_ref[...] = x[cl - (CONV_W - 1):cl, :]


def _ssd_prompt(proj3, lp):
    B, L, _ = proj3.shape
    cl = SSD_CHUNK
    gh = SSD_HEADS // SSD_GROUPS * SSD_HD
    vec = lambda w: pl.BlockSpec((1, w), lambda b, c: (0, 0))
    y, st, cst = pl.pallas_call(
        _ssd_prompt_kernel,
        grid=(B, L // cl),
        in_specs=[
            pl.BlockSpec((None, cl, SSD_CONV_CH), lambda b, c: (b, c, XBC_OFF // SSD_CONV_CH)),
            pl.BlockSpec((None, cl, D_SSD), lambda b, c: (b, c, ZS_OFF // D_SSD)),
            pl.BlockSpec((None, cl, 128), lambda b, c: (b, c, SM_OFF // 128)),
            pl.BlockSpec((CONV_W, SSD_CONV_CH), lambda b, c: (0, 0)),
            vec(SSD_CONV_CH), vec(128), vec(128), vec(D_SSD), vec(D_SSD),
        ],
        out_specs=[
            pl.BlockSpec((None, cl, D_SSD), lambda b, c: (b, c, 0)),
            pl.BlockSpec((None, SSD_GROUPS, gh, SSD_STATE), lambda b, c: (b, 0, 0, 0)),
            pl.BlockSpec((None, CONV_W - 1, SSD_CONV_CH), lambda b, c: (b, 0, 0)),
        ],
        out_shape=[
            jax.ShapeDtypeStruct((B, L, D_SSD), bf16),
            jax.ShapeDtypeStruct((B, SSD_GROUPS, gh, SSD_STATE), f32),
            jax.ShapeDtypeStruct((B, CONV_W - 1, SSD_CONV_CH), f32),
        ],
        scratch_shapes=[
            pltpu.VMEM((8 + cl, SSD_CONV_CH), f32),
            pltpu.VMEM((SSD_GROUPS, gh, SSD_STATE), f32),
            pltpu.VMEM((cl, D_SSD), f32),
        ],
        compiler_params=_params(("parallel", "arbitrary")),
        name="ssd_prompt",
    )(proj3, proj3, proj3, lp["ssd_conv_w"], lp["ssd_conv_b"], lp["ssd_dtb"], lp["ssd_alog"],
      lp["ssd_dexp"], lp["ssd_norm"])
    return y, st.reshape(B, SSD_HEADS, SSD_HD, SSD_STATE), cst


def _gdn_prompt_kernel(qkv_ref, z_ref, sm_ref, cw_ref, dtb_ref, alog_ref, nrm_ref,
                       y_ref, st_ref, cst_ref, xbuf, sst):
    c = pl.program_id(1)
    cl = GDN_CHUNK
    nch = GDN_STEP_CHUNKS
    tl = nch * cl

    @pl.when(c == 0)
    def _():
        xbuf[0:8, :] = jnp.zeros((8, GDN_CONV_CH), f32)
        sst[...] = jnp.zeros(sst.shape, f32)

    x = qkv_ref[...]
    xbuf[8:8 + tl, :] = x
    conv = cw_ref[3:4, :] * x
    for k in range(1, CONV_W):
        conv += cw_ref[3 - k:4 - k, :] * xbuf[pl.ds(8 - k, tl), :]
    xbuf[0:8, :] = x[tl - 8:tl, :]
    xc = _silu(conv)

    sm = sm_ref[...]
    beta_all = jax.nn.sigmoid(sm)
    g_all = -jnp.exp(alog_ref[...]) * _softplus(sm + dtb_ref[...])
    row, col = _iota((cl, cl), 0), _iota((cl, cl), 1)
    incl = row >= col
    strict = row > col
    eye = (row == col).astype(f32)
    rt, ct = _iota((tl, tl), 0), _iota((tl, tl), 1)
    cum = _dot_01l((rt >= ct) & (rt // cl == ct // cl), g_all)
    cum_t = cum.T

    units = [(ci, h) for ci in range(nch) for h in range(GDN_HEADS)]
    q, k, v, beta, cum_c, decay = {}, {}, {}, {}, {}, {}
    for u in units:
        ci, h = u
        rows = slice(ci * cl, (ci + 1) * cl)
        qh = xc[rows, h * GDN_HD:(h + 1) * GDN_HD]
        kh = xc[rows, D_GDN + h * GDN_HD:D_GDN + (h + 1) * GDN_HD]
        v[u] = xc[rows, 2 * D_GDN + h * GDN_HD:2 * D_GDN + (h + 1) * GDN_HD]
        q[u] = qh * lax.rsqrt(jnp.sum(qh * qh, axis=-1, keepdims=True) + EPS) * (GDN_HD ** -0.5)
        k[u] = kh * lax.rsqrt(jnp.sum(kh * kh, axis=-1, keepdims=True) + EPS)
        beta[u] = beta_all[rows, SM_B + h:SM_B + h + 1]
        cum_c[u] = cum[rows, SM_A + h:SM_A + h + 1]
        seg = cum_c[u] - cum_t[SM_A + h:SM_A + h + 1, rows]
        decay[u] = jnp.where(incl, jnp.exp(jnp.where(incl, seg, 0.0)), 0.0)
    pw = {u: -jnp.where(strict, beta[u] * _dot_nt(k[u], k[u]) * decay[u], 0.0) for u in units}
    inv = {u: eye + pw[u] for u in units}
    pw2 = {u: _split2(pw[u]) for u in units}
    for _ in range(5):
        pw2 = {u: _split2(_dot3(pw2[u], pw2[u])) for u in units}
        inv = {u: inv[u] + _dot3(_split2(inv[u]), pw2[u]) for u in units}
    rhs = {u: jnp.concatenate([beta[u] * v[u], (beta[u] * jnp.exp(cum_c[u])) * k[u]], axis=1) for u in units}
    sol = {u: _dot3(_split2(inv[u]), _split2(rhs[u])) for u in units}
    uu = {u: sol[u][:, :GDN_HD] for u in units}
    ww = {u: sol[u][:, GDN_HD:] for u in units}
    qk = {u: _dot_nt(q[u], k[u]) * decay[u] for u in units}

    s = [sst[h] for h in range(GDN_HEADS)]
    for ci in range(nch):
        rows = slice(ci * cl, (ci + 1) * cl)
        hs = [(ci, h) for h in range(GDN_HEADS)]
        v_new = {u: uu[u] - _dot(ww[u], s[u[1]]) for u in hs}
        o = {u: _dot(q[u] * jnp.exp(cum_c[u]), s[u[1]]) + _dot(qk[u], v_new[u]) for u in hs}
        for u in hs:
            h = u[1]
            last = cum[ci * cl + cl - 1:ci * cl + cl, SM_A + h:SM_A + h + 1]
            s[h] = jnp.exp(last) * s[h] + _dot_tn(k[u] * jnp.exp(last - cum_c[u]), v_new[u])
        for u in hs:
            sl = slice(u[1] * GDN_HD, (u[1] + 1) * GDN_HD)
            on = o[u] * lax.rsqrt(jnp.mean(o[u] * o[u], axis=-1, keepdims=True) + EPS)
            y_ref[rows, sl] = (on * nrm_ref[...] * _silu(z_ref[rows, sl])).astype(y_ref.dtype)
    for h in range(GDN_HEADS):
        sst[h] = s[h]

    @pl.when(c == pl.num_programs(1) - 1)
    def _():
        st_ref[...] = sst[...]
        cst_ref[...] = x[tl - (CONV_W - 1):tl, :]


def _gdn_prompt(proj3, lp):
    B, L, _ = proj3.shape
    cl = GDN_CHUNK * GDN_STEP_CHUNKS
    vec = lambda w: pl.BlockSpec((1, w), lambda b, c: (0, 0))
    return pl.pallas_call(
        _gdn_prompt_kernel,
        grid=(B, L // cl),
        in_specs=[
            pl.BlockSpec((None, cl, GDN_CONV_CH), lambda b, c: (b, c, QKV_OFF // GDN_CONV_CH)),
            pl.BlockSpec((None, cl, D_GDN), lambda b, c: (b, c, ZG_OFF // D_GDN)),
            pl.BlockSpec((None, cl, 128), lambda b, c: (b, c, SM_OFF // 128)),
            pl.BlockSpec((CONV_W, GDN_CONV_CH), lambda b, c: (0, 0)),
            vec(128), vec(128), vec(GDN_HD),
        ],
        out_specs=[
            pl.BlockSpec((None, cl, D_GDN), lambda b, c: (b, c, 0)),
            pl.BlockSpec((None, GDN_HEADS, GDN_HD, GDN_HD), lambda b, c: (b, 0, 0, 0)),
            pl.BlockSpec((None, CONV_W - 1, GDN_CONV_CH), lambda b, c: (b, 0, 0)),
        ],
        out_shape=[
            jax.ShapeDtypeStruct((B, L, D_GDN), bf16),
            jax.ShapeDtypeStruct((B, GDN_HEADS, GDN_HD, GDN_HD), f32),
            jax.ShapeDtypeStruct((B, CONV_W - 1, GDN_CONV_CH), f32),
        ],
        scratch_shapes=[
            pltpu.VMEM((8 + cl, GDN_CONV_CH), f32),
            pltpu.VMEM((GDN_HEADS, GDN_HD, GDN_HD), f32),
        ],
        compiler_params=_params(("parallel", "arbitrary")),
        name="gdn_prompt",
    )(proj3, proj3, proj3, lp["gdn_conv_w"], lp["gdn_dtb"], lp["gdn_alog"], lp["gdn_norm"])


def _cmp_prompt_kernel(k_ref, v_ref, pe_ref, w1_ref, w2_ref, kg_ref, kc_ref, vct_ref):
    nb = kc_ref.shape[0]
    for t, src in enumerate((k_ref, v_ref)):
        outs = []
        p1 = [jnp.zeros((nb, CMP_HID), f32) for _ in range(NSA_KV)]
        p2 = [jnp.zeros((nb, CMP_HID), f32) for _ in range(NSA_KV)]
        for l in range(CMP_STRIDE):
            xr = src[pl.ds(l, nb, stride=CMP_STRIDE), :]
            l2 = CMP_STRIDE + l
            for g in range(NSA_KV):
                xg = xr[:, g * NSA_HD:(g + 1) * NSA_HD]
                p1[g] += _dot(xg + pe_ref[t, l:l + 1, :], w1_ref[t, l * NSA_HD:(l + 1) * NSA_HD, :])
                p2[g] += _dot(xg + pe_ref[t, l2:l2 + 1, :], w1_ref[t, l2 * NSA_HD:(l2 + 1) * NSA_HD, :])
        for g in range(NSA_KV):
            hid = p1[g] + pltpu.roll(p2[g], nb - 1, 0)
            cmp = _dot(_silu(hid), w2_ref[t])
            if t == 0:
                cmp = cmp * lax.rsqrt(jnp.mean(cmp * cmp, axis=-1, keepdims=True) + EPS) * kg_ref[0:1, 0:NSA_HD]
            outs.append(cmp)
        both = jnp.concatenate(outs, axis=1)
        if t == 0:
            kc_ref[...] = both
        else:
            vct_ref[...] = both.T


def _cmp_prompt(kv6_3, lp):
    B, L, _ = kv6_3.shape
    nb = L // CMP_STRIDE
    full = lambda s: pl.BlockSpec(s, lambda b: (0,) * len(s))
    return pl.pallas_call(
        _cmp_prompt_kernel,
        grid=(B,),
        in_specs=[
            pl.BlockSpec((None, L, 128), lambda b: (b, 0, KV_CK // 128)),
            pl.BlockSpec((None, L, 128), lambda b: (b, 0, KV_CV // 128)),
            full((2, CMP_LEN, NSA_HD)), full((2, CMP_LEN * NSA_HD, CMP_HID)), full((2, CMP_HID, NSA_HD)),
            full((3, 128)),
        ],
        out_specs=[pl.BlockSpec((None, nb, 128), lambda b: (b, 0, 0)),
                   pl.BlockSpec((None, 128, nb), lambda b: (b, 0, 0))],
        out_shape=[jax.ShapeDtypeStruct((B, nb, 128), f32), jax.ShapeDtypeStruct((B, 128, nb), f32)],
        compiler_params=_params(("parallel",)),
        name="nsa_cmp_prompt",
    )(kv6_3, kv6_3, lp["cmp_pe"], lp["cmp_w1"], lp["cmp_w2"], lp["k_gain"])


def _nsa_prompt_kernel(rb_ref, qt_ref, ks_ref, kw_ref, vst_ref, vwt_ref, kc_ref, vct_ref, gt_ref,
                       y_ref, tb_ref, tc_ref, ext_ref, selm_ref):
    qi = pl.program_id(1)
    L = ks_ref.shape[0]
    tq = 128
    nl = NSA_REP * tq
    ncmp = kc_ref.shape[0]
    n_slc = L // SLC_BLOCK
    last = N_BUCKETS - 1

    @pl.when(qi == 0)
    def _():
        sub = _iota((128, tq), 0)
        qo = _iota((128, tq), 1)
        d_diag = qo - sub
        c_rel = jnp.where(sub < 64, sub, sub - 128)
        d_cmp = qo - CMP_STRIDE * c_rel - (CMP_LEN - 1)
        for g in range(NSA_KV):
            for r in range(NSA_REP):
                h = g * NSA_REP + r
                tab = [rb_ref[b, h] - rb_ref[last, h] for b in range(N_BUCKETS)]
                lanes = slice(r * tq, (r + 1) * tq)
                tb_ref[g, TB_FAR, :, lanes] = jnp.zeros((128, tq), f32)
                tb_ref[g, TB_PREV, :, lanes] = _bias_chain(d_diag + 128, tab)
                tb_ref[g, TB_DIAG, :, lanes] = jnp.where(d_diag >= 0, _bias_chain(d_diag, tab), NEG)
                tb_ref[g, TB_WIN0, :, lanes] = jnp.where(d_diag <= 0, 0.0, NEG)
                tb_ref[g, TB_NONE, :, lanes] = jnp.full((128, tq), NEG, f32)
                t_c = jnp.where(d_cmp < 0, 0.0, _bias_chain(d_cmp, tab))
                tc_ref[g, 0:128, lanes] = t_c
                tc_ref[g, 128:256, lanes] = t_c
        ext_ref[...] = (_iota((L, 128), 0) // SLC_BLOCK == _iota((L, 128), 1)).astype(bf16)

    q0 = qi * tq
    qpos = q0 + _iota((1, nl), 1) % tq
    qpos1 = q0 + _iota((1, tq), 1)
    sub = _iota((128, 1), 0)
    gt = gt_ref[...]

    jr = _iota((n_slc, tq), 0)
    cur = qpos1 // SLC_BLOCK
    valid = jr * SLC_BLOCK <= qpos1
    forced = (jr == 0) | (jr == cur) | (jr == cur - 1)
    j_i = _iota((n_slc, ncmp), 0)
    c_i = _iota((n_slc, ncmp), 1)
    ovl = ((c_i * CMP_STRIDE < j_i * SLC_BLOCK + SLC_BLOCK) & (c_i * CMP_STRIDE + CMP_LEN > j_i * SLC_BLOCK)
           & (c_i < ncmp - 1)).astype(f32)
    cmp_off = pl.multiple_of((128 - qi * (tq // CMP_STRIDE)) % 128, 8)

    groups = range(NSA_KV)
    gsl = [slice(g * NSA_HD, (g + 1) * NSA_HD) for g in groups]
    qt = [jnp.concatenate([qt_ref[(g * NSA_REP + r) * NSA_HD:(g * NSA_REP + r + 1) * NSA_HD, :]
                           for r in range(NSA_REP)], axis=1).astype(bf16) for g in groups]

    ok_c = (sub * CMP_STRIDE + (CMP_LEN - 1) <= qpos) & (sub < ncmp - 1)
    o_cmp = []
    for g in groups:
        st = _dot(kc_ref[:, gsl[g]], qt[g]) + tc_ref[g, pl.ds(cmp_off, 128), :]
        st = jnp.where(ok_c, st, NEG)
        e = jnp.exp(st - jnp.max(st, axis=0, keepdims=True))
        p = jnp.where(ok_c, e / jnp.sum(e, axis=0, keepdims=True), 0.0)
        o_cmp.append(_dot(vct_ref[gsl[g], :], p))
        p_sum = p[:, 0:tq] + p[:, tq:2 * tq] + p[:, 2 * tq:3 * tq] + p[:, 3 * tq:4 * tq]
        imp = _dot_01l(ovl, p_sum)
        score = jnp.where(valid, jnp.where(forced, 1e9, imp), -1e9)
        rank = jnp.zeros((n_slc, tq), f32)
        for i in range(n_slc):
            si = score[i:i + 1, :]
            rank += ((si > score) | ((si == score) & (jr > i))).astype(f32)
        sel = (rank < SLC_TOPN).astype(bf16)
        sel = jnp.concatenate([sel, jnp.zeros((128 - n_slc, tq), bf16)], axis=0)
        selx = jnp.dot(ext_ref[...], sel, preferred_element_type=f32)
        selm_ref[g] = (selx - 1.0) * -NEG

    ones_rows = jnp.ones((16, 128), f32)

    def attend(state, tiles, k_ref, vt_ref, selected):
        s, offs = {}, []
        for ti, (kt, table) in enumerate(tiles):
            ko = pl.multiple_of(kt * 128, 128)
            offs.append(ko)
            for g in groups:
                sg = _dot(k_ref[pl.ds(ko, 128), gsl[g]], qt[g]) + tb_ref[g, table]
                if selected:
                    sg = sg + jnp.concatenate([selm_ref[g, pl.ds(ko, 128), :]] * NSA_REP, axis=1)
                s[g, ti] = sg
        out = []
        for g in groups:
            m, acc = state[g]
            m_new = m
            for ti in range(len(tiles)):
                m_new = jnp.maximum(m_new, jnp.max(s[g, ti], axis=0, keepdims=True))
            acc = jnp.exp(m - m_new) * acc
            for ti in range(len(tiles)):
                vt = jnp.concatenate([vt_ref[gsl[g], pl.ds(offs[ti], 128)], ones_rows], axis=0)
                acc += _dot(vt, jnp.exp(s[g, ti] - m_new))
            out.append((m_new, acc))
        return tuple(out)

    finish = lambda st: [acc[0:NSA_HD] / acc[NSA_HD:NSA_HD + 1] for _, acc in st]
    init = tuple((jnp.full((1, nl), NEG, f32), jnp.zeros((NSA_HD + 16, nl), f32)) for _ in groups)
    far_table = lambda kt: jnp.where(kt == qi - 1, TB_PREV, TB_FAR)

    pair = lambda i, st: attend(st, [(2 * i, far_table(2 * i)), (2 * i + 1, far_table(2 * i + 1))],
                                ks_ref, vst_ref, True)
    st = lax.fori_loop(0, qi // 2, pair, init)
    st = lax.fori_loop(0, qi % 2, lambda i, st: attend(st, [(qi - 1, TB_PREV)], ks_ref, vst_ref, True), st)
    o_slc = finish(attend(st, [(qi, TB_DIAG)], ks_ref, vst_ref, True))

    w0 = jnp.maximum(qi - WINDOW // 128, 0)

    def win_table(kt):
        t = jnp.where(kt == qi - WINDOW // 128, TB_WIN0, TB_FAR)
        t = jnp.where(kt == qi - 1, TB_PREV, t)
        return jnp.where(kt == qi, TB_DIAG, jnp.where(kt > qi, TB_NONE, t))

    win_tiles = [(w0 + t, win_table(w0 + t)) for t in range(WINDOW // 128 + 1)]
    o_win = finish(attend(init, win_tiles, kw_ref, vwt_ref, False))

    y_parts = []
    for g in groups:
        for r in range(NSA_REP):
            lanes = slice(r * tq, (r + 1) * tq)
            h = g * NSA_REP + r
            gate = lambda t: gt[SM_G + t * NSA_HEADS + h:SM_G + t * NSA_HEADS + h + 1, :]
            y_parts.append(gate(0) * o_cmp[g][:, lanes] + gate(1) * o_slc[g][:, lanes]
                           + gate(2) * o_win[g][:, lanes])
    y_ref[...] = jnp.concatenate(y_parts, axis=0).T.astype(y_ref.dtype)


def _nsa_prompt(qt, kv6_3, kv6t, kc, vct, gt, rel_bias):
    B, L, _ = kv6_3.shape
    tq = 128
    nb = L // CMP_STRIDE
    return pl.pallas_call(
        _nsa_prompt_kernel,
        grid=(B, L // tq),
        in_specs=[
            pl.BlockSpec(memory_space=pltpu.SMEM),
            pl.BlockSpec((None, D_NSA, tq), lambda b, i: (b, 0, i)),
            pl.BlockSpec((None, L, 128), lambda b, i: (b, 0, KV_SK // 128)),
            pl.BlockSpec((None, L, 128), lambda b, i: (b, 0, KV_WK // 128)),
            pl.BlockSpec((None, 128, L), lambda b, i: (b, KV_SV // 128, 0)),
            pl.BlockSpec((None, 128, L), lambda b, i: (b, KV_WV // 128, 0)),
            pl.BlockSpec((None, nb, 128), lambda b, i: (b, 0, 0)),
            pl.BlockSpec((None, 128, nb), lambda b, i: (b, 0, 0)),
            pl.BlockSpec((None, 128, tq), lambda b, i: (b, 0, i)),
        ],
        out_specs=pl.BlockSpec((None, tq, D_NSA), lambda b, i: (b, i, 0)),
        out_shape=jax.ShapeDtypeStruct((B, L, D_NSA), bf16),
        scratch_shapes=[
            pltpu.VMEM((NSA_KV, TB_KINDS, 128, NSA_REP * tq), f32),
            pltpu.VMEM((NSA_KV, 256, NSA_REP * tq), f32),
            pltpu.VMEM((L, 128), bf16),
            pltpu.VMEM((NSA_KV, L, tq), f32),
        ],
        compiler_params=_params(("parallel", "arbitrary")),
        name="nsa_prompt",
    )(rel_bias, qt, kv6_3, kv6_3, kv6t, kv6t, kc, vct, gt)


def _prep_t_kernel(q_ref, kv_ref, qg_ref, kg_ref, qt_ref, kv6_ref, kv6t_ref, gt_ref):
    for s in range(D_NSA // 128):
        sl = slice(s * 128, (s + 1) * 128)
        qt_ref[sl, :] = (_group_rms(q_ref[:, sl], NSA_HD) * qg_ref[...] * (NSA_HD ** -0.5)).T
    kv = kv_ref[...]
    for j, off in enumerate((KV_CK, KV_CV, KV_SK, KV_SV, KV_WK, KV_WV)):
        x = kv[:, off:off + 128]
        if off == KV_SK:
            x = _group_rms(x, NSA_HD) * kg_ref[1:2, :]
        if off == KV_WK:
            x = _group_rms(x, NSA_HD) * kg_ref[2:3, :]
        kv6_ref[:, off:off + 128] = x
        kv6t_ref[off:off + 128, :] = x.T
    gt_ref[...] = jax.nn.sigmoid(kv[:, SM_OFF - KV_OFF:SM_OFF - KV_OFF + 128]).T


def _prep_t(proj3, qgain, kgain):
    B, L, _ = proj3.shape
    tm = min(L, 512)
    return pl.pallas_call(
        _prep_t_kernel,
        grid=(B, L // tm),
        in_specs=[
            pl.BlockSpec((None, tm, D_NSA), lambda b, i: (b, i, QN_OFF // D_NSA)),
            pl.BlockSpec((None, tm, 1024), lambda b, i: (b, i, KV_OFF // 1024)),
            pl.BlockSpec((1, 128), lambda b, i: (0, 0)),
            pl.BlockSpec((3, 128), lambda b, i: (0, 0)),
        ],
        out_specs=[pl.BlockSpec((None, D_NSA, tm), lambda b, i: (b, 0, i)),
                   pl.BlockSpec((None, tm, 768), lambda b, i: (b, i, 0)),
                   pl.BlockSpec((None, 768, tm), lambda b, i: (b, 0, i)),
                   pl.BlockSpec((None, 128, tm), lambda b, i: (b, 0, i))],
        out_shape=[jax.ShapeDtypeStruct((B, D_NSA, L), f32), jax.ShapeDtypeStruct((B, L, 768), f32),
                   jax.ShapeDtypeStruct((B, 768, L), f32), jax.ShapeDtypeStruct((B, 128, L), f32)],
        compiler_params=_params(("parallel", "parallel")),
        name="nsa_prep_t",
    )(proj3, proj3, qgain, kgain)


def _row0(x, rows=8):
    return jnp.where(_iota((rows, x.shape[1]), 0) == 0, x, 0.0)


def _ssd_decode_kernel(xbc_ref, z_ref, sm_ref, cprev_ref, sin_ref, cw_ref, cb_ref, dtb_ref, alog_ref, dexp_ref,
                       nrm_ref, y_ref, st_ref, cst_ref):
    x = xbc_ref[...]
    conv = cw_ref[3:4, :] * x
    for j in range(CONV_W - 1):
        conv += cw_ref[j:j + 1, :] * cprev_ref[j:j + 1, :]
    cst_ref[0:2, :] = cprev_ref[1:3, :]
    cst_ref[2:3, :] = x
    xc = _silu(conv + cb_ref[...])
    xs = xc[:, :D_SSD]
    dt = _softplus(sm_ref[...] + dtb_ref[...])
    da = dt * (-jnp.exp(alog_ref[...]))
    ys = []
    for g in range(SSD_GROUPS):
        bm = xc[:, D_SSD + g * SSD_STATE:D_SSD + (g + 1) * SSD_STATE]
        cm = xc[:, D_SSD + (SSD_GROUPS + g) * SSD_STATE:D_SSD + (SSD_GROUPS + g + 1) * SSD_STATE]
        cb = jnp.sum(cm * bm, axis=-1, keepdims=True)
        bm8, cm8 = _row0(bm), _row0(cm)
        for r in range(SSD_HEADS // SSD_GROUPS):
            h = g * (SSD_HEADS // SSD_GROUPS) + r
            xdt = xs[:, h * SSD_HD:(h + 1) * SSD_HD] * dt[:, h:h + 1]
            eda = jnp.exp(da[:, h:h + 1])
            h0 = sin_ref[h]
            ys.append(cb * xdt + eda * _dot_nt(cm8, h0, HI)[0:1])
            st_ref[h] = eda * h0 + _dot_tn(_row0(xdt), bm8, HI)
    y = (jnp.concatenate(ys, axis=1) + dexp_ref[...] * xs) * _silu(z_ref[...])
    gw = D_SSD // SSD_GROUPS
    for g in range(SSD_GROUPS):
        yg = y[:, g * gw:(g + 1) * gw]
        yg = yg * lax.rsqrt(jnp.mean(yg * yg, axis=-1, keepdims=True) + EPS)
        y_ref[:, g * gw:(g + 1) * gw] = yg * nrm_ref[:, g * gw:(g + 1) * gw]


def _ssd_decode(proj3, conv_state, state, layer, lp):
    B = proj3.shape[0]
    vec = lambda w: pl.BlockSpec((1, w), lambda b: (0, 0))
    return pl.pallas_call(
        _ssd_decode_kernel,
        grid=(B,),
        in_specs=[
            pl.BlockSpec((None, 1, SSD_CONV_CH), lambda b: (b, 0, XBC_OFF // SSD_CONV_CH)),
            pl.BlockSpec((None, 1, D_SSD), lambda b: (b, 0, ZS_OFF // D_SSD)),
            pl.BlockSpec((None, 1, 128), lambda b: (b, 0, SM_OFF // 128)),
            pl.BlockSpec((None, None, CONV_W - 1, SSD_CONV_CH), lambda b: (layer, b, 0, 0)),
            pl.BlockSpec((None, None, SSD_HEADS, SSD_HD, SSD_STATE), lambda b: (layer, b, 0, 0, 0)),
            pl.BlockSpec((CONV_W, SSD_CONV_CH), lambda b: (0, 0)),
            vec(SSD_CONV_CH), vec(128), vec(128), vec(D_SSD), vec(D_SSD),
        ],
        out_specs=[
            pl.BlockSpec((None, 1, D_SSD), lambda b: (b, 0, 0)),
            pl.BlockSpec((None, SSD_HEADS, SSD_HD, SSD_STATE), lambda b: (b, 0, 0, 0)),
            pl.BlockSpec((None, CONV_W - 1, SSD_CONV_CH), lambda b: (b, 0, 0)),
        ],
        out_shape=[
            jax.ShapeDtypeStruct((B, 1, D_SSD), f32),
            jax.ShapeDtypeStruct((B, SSD_HEADS, SSD_HD, SSD_STATE), f32),
            jax.ShapeDtypeStruct((B, CONV_W - 1, SSD_CONV_CH), f32),
        ],
        compiler_params=_params(("parallel",)),
        name="ssd_decode",
    )(proj3, proj3, proj3, conv_state, state, lp["ssd_conv_w"], lp["ssd_conv_b"], lp["ssd_dtb"], lp["ssd_alog"],
      lp["ssd_dexp"], lp["ssd_norm"])


def _gdn_decode_kernel(qkv_ref, z_ref, sm_ref, cprev_ref, sin_ref, cw_ref, dtb_ref, alog_ref, nrm_ref,
                       y_ref, st_ref, cst_ref):
    x = qkv_ref[...]
    conv = cw_ref[3:4, :] * x
    for j in range(CONV_W - 1):
        conv += cw_ref[j:j + 1, :] * cprev_ref[j:j + 1, :]
    cst_ref[0:2, :] = cprev_ref[1:3, :]
    cst_ref[2:3, :] = x
    xc = _silu(conv)
    sm = sm_ref[...]
    beta_all = jax.nn.sigmoid(sm)
    g_all = -jnp.exp(alog_ref[...]) * _softplus(sm + dtb_ref[...])
    for h in range(GDN_HEADS):
        sl = slice(h * GDN_HD, (h + 1) * GDN_HD)
        qh = xc[:, sl]
        kh = xc[:, D_GDN + h * GDN_HD:D_GDN + (h + 1) * GDN_HD]
        vh = xc[:, 2 * D_GDN + h * GDN_HD:2 * D_GDN + (h + 1) * GDN_HD]
        qh = qh * lax.rsqrt(jnp.sum(qh * qh, axis=-1, keepdims=True) + EPS) * (GDN_HD ** -0.5)
        kh = kh * lax.rsqrt(jnp.sum(kh * kh, axis=-1, keepdims=True) + EPS)
        beta = beta_all[:, SM_B + h:SM_B + h + 1]
        eg = jnp.exp(g_all[:, SM_A + h:SM_A + h + 1])
        s0 = sin_ref[h]
        k8 = _row0(kh)
        v_new = beta * vh - (beta * eg) * _dot(k8, s0, HI)[0:1]
        o = eg * _dot(_row0(qh), s0, HI)[0:1] + jnp.sum(qh * kh, axis=-1, keepdims=True) * v_new
        st_ref[h] = eg * s0 + _dot_tn(k8, _row0(v_new), HI)
        o = o * lax.rsqrt(jnp.mean(o * o, axis=-1, keepdims=True) + EPS)
        y_ref[:, sl] = o * nrm_ref[...] * _silu(z_ref[:, sl])


def _gdn_decode(proj3, conv_state, state, layer, lp):
    B = proj3.shape[0]
    vec = lambda w: pl.BlockSpec((1, w), lambda b: (0, 0))
    return pl.pallas_call(
        _gdn_decode_kernel,
        grid=(B,),
        in_specs=[
            pl.BlockSpec((None, 1, GDN_CONV_CH), lambda b: (b, 0, QKV_OFF // GDN_CONV_CH)),
            pl.BlockSpec((None, 1, D_GDN), lambda b: (b, 0, ZG_OFF // D_GDN)),
            pl.BlockSpec((None, 1, 128), lambda b: (b, 0, SM_OFF // 128)),
            pl.BlockSpec((None, None, CONV_W - 1, GDN_CONV_CH), lambda b: (layer, b, 0, 0)),
            pl.BlockSpec((None, None, GDN_HEADS, GDN_HD, GDN_HD), lambda b: (layer, b, 0, 0, 0)),
            pl.BlockSpec((CONV_W, GDN_CONV_CH), lambda b: (0, 0)),
            vec(128), vec(128), vec(GDN_HD),
        ],
        out_specs=[
            pl.BlockSpec((None, 1, D_GDN), lambda b: (b, 0, 0)),
            pl.BlockSpec((None, GDN_HEADS, GDN_HD, GDN_HD), lambda b: (b, 0, 0, 0)),
            pl.BlockSpec((None, CONV_W - 1, GDN_CONV_CH), lambda b: (b, 0, 0)),
        ],
        out_shape=[
            jax.ShapeDtypeStruct((B, 1, D_GDN), f32),
            jax.ShapeDtypeStruct((B, GDN_HEADS, GDN_HD, GDN_HD), f32),
            jax.ShapeDtypeStruct((B, CONV_W - 1, GDN_CONV_CH), f32),
        ],
        compiler_params=_params(("parallel",)),
        name="gdn_decode",
    )(proj3, proj3, proj3, conv_state, state, lp["gdn_conv_w"], lp["gdn_dtb"], lp["gdn_alog"], lp["gdn_norm"])


def _query_rows(q, g):
    rows = [q[:, (g * NSA_REP + r) * NSA_HD:(g * NSA_REP + r + 1) * NSA_HD] for r in range(NSA_REP)]
    return jnp.concatenate(rows + [jnp.zeros((8 - NSA_REP, NSA_HD), f32)], axis=0) * (NSA_HD ** -0.5)


def _masked_softmax(lg, ok):
    lg = jnp.where(ok, lg, NEG)
    e = jnp.exp(lg - jnp.max(lg, axis=-1, keepdims=True))
    return jnp.where(ok, e / jnp.sum(e, axis=-1, keepdims=True), 0.0)


def _head_lanes(parts):
    return jnp.concatenate([o[r:r + 1, :] for o in parts for r in range(NSA_REP)], axis=1)


def _nsa_dec_cmp_kernel(pt_ref, q_ref, pe_ref, w1_ref, w2_ref, kg_ref, tab_ref, pk_ref, pv_ref,
                        ocmp_ref, idx_ref, stage, rows, sem, *, layer, n_pages):
    b = pl.program_id(0)
    nb = pl.num_programs(0)
    n_past = n_pages * PAGE
    q_pos = n_past
    n_blk = n_past // CMP_STRIDE
    n_cmp = (n_past + 1 - CMP_LEN) // CMP_STRIDE + 1
    n_slc = -(-(n_past + 1) // SLC_BLOCK)
    nj = -(-n_slc // 128) * 128
    pools = (pk_ref, pv_ref)

    def page_copy(t, seq, p):
        return pltpu.make_async_copy(pools[t].at[layer, pt_ref[seq, p]], stage.at[t, p], sem.at[t])

    def start_all(t, seq):
        lax.fori_loop(0, n_pages, lambda p, c: (page_copy(t, seq, p).start(), c)[1], 0)

    def wait_all(t, seq):
        lax.fori_loop(0, n_pages, lambda p, c: (page_copy(t, seq, p).wait(), c)[1], 0)

    @pl.when(b == 0)
    def _():
        start_all(0, 0)
        start_all(1, 0)

    def to_rows(t):
        def body(p, c):
            rows[pl.ds(pl.multiple_of(p * PAGE, PAGE), PAGE), :] = stage[t, p].T
            return c
        lax.fori_loop(0, n_pages, body, 0, unroll=4)

    cmps = []
    for t in range(2):
        wait_all(t, b)
        to_rows(t)

        @pl.when(b + 1 < nb)
        def _():
            start_all(t, b + 1)

        p1 = jnp.zeros((n_blk, 2 * CMP_HID), f32)
        p2 = jnp.zeros((n_blk, 2 * CMP_HID), f32)
        for j in range(CMP_STRIDE // 2):
            xr = jnp.concatenate([rows[pl.ds(2 * j, n_blk, stride=CMP_STRIDE), :],
                                  rows[pl.ds(2 * j + 1, n_blk, stride=CMP_STRIDE), :]], axis=1)
            j2 = CMP_STRIDE // 2 + j
            p1 += jnp.dot((xr + pe_ref[t, j:j + 1, :]).astype(bf16), w1_ref[t, j], preferred_element_type=f32)
            p2 += jnp.dot((xr + pe_ref[t, j2:j2 + 1, :]).astype(bf16), w1_ref[t, j2], preferred_element_type=f32)
        hid = p1 + pltpu.roll(p2, n_blk - 1, 0)
        c = jnp.dot(_silu(hid).astype(bf16), w2_ref[t], preferred_element_type=f32)
        if t == 0:
            c = _group_rms(c, NSA_HD) * kg_ref[0:1, :]
        cmps.append(c)
    kc, vc = cmps

    q = q_ref[...]
    lane = _iota((8, n_blk), 1)
    cend = lane * CMP_STRIDE + (CMP_LEN - 1)
    ok = (cend <= q_pos) & (lane < n_cmp)
    c_i = _iota((n_blk, nj), 0)
    j_i = _iota((n_blk, nj), 1)
    ovl = ((c_i * CMP_STRIDE < j_i * SLC_BLOCK + SLC_BLOCK) & (c_i * CMP_STRIDE + CMP_LEN > j_i * SLC_BLOCK)
           & (j_i < n_slc) & (c_i < n_cmp)).astype(f32)
    jl = _iota((1, nj), 1)
    cur = q_pos // SLC_BLOCK
    valid = jl * SLC_BLOCK <= q_pos
    forced = (jl == 0) | (jl == cur) | (jl == cur - 1)
    ii = _iota((nj, nj), 0)
    jj = _iota((nj, nj), 1)
    kk = _iota((SLC_TOPN, nj), 0)
    j16 = _iota((SLC_TOPN, nj), 1)
    o_parts, idx_cols = [], []
    for g in range(NSA_KV):
        tabcols = [tab_ref[g, :, bk:bk + 1] for bk in range(N_BUCKETS)]
        lg = _dot_nt(_query_rows(q, g), kc[:, g * NSA_HD:(g + 1) * NSA_HD]) + _bias_chain(q_pos - cend, tabcols)
        p = _masked_softmax(lg, ok)
        o_parts.append(_dot(p, vc[:, g * NSA_HD:(g + 1) * NSA_HD]))
        p_sum = p[0:1] + p[1:2] + p[2:3] + p[3:4]
        imp = _dot_01(_row0(p_sum), ovl)[0:1]
        score = jnp.where(valid, jnp.where(forced, 1e9, imp), -1e9)
        score = jnp.where(jl < n_slc, score, -3e9)
        s_c = jnp.sum(jnp.where(ii == jj, score, 0.0), axis=1, keepdims=True)
        rank_r = jnp.sum(((s_c > score) | ((s_c == score) & (ii < jj))).astype(f32), axis=0, keepdims=True)
        rank_c = jnp.sum(((score > s_c) | ((score == s_c) & (jj < ii))).astype(f32), axis=1, keepdims=True)
        sel_r = (rank_r < SLC_TOPN) & (jl < n_slc)
        sel_c = (rank_c < SLC_TOPN) & (ii[:, 0:1] < n_slc)
        pos_r = jnp.sum((sel_c & (ii < jj)).astype(f32), axis=0, keepdims=True)
        hit = (pos_r == kk.astype(f32)) & sel_r
        idx_cols.append(jnp.sum(jnp.where(hit, j16.astype(f32), 0.0), axis=1, keepdims=True))
    ocmp_ref[...] = _head_lanes(o_parts)
    l16 = _iota((SLC_TOPN, 128), 1)
    idx_ref[...] = (jnp.where(l16 == 0, idx_cols[0], 0.0) + jnp.where(l16 == 1, idx_cols[1], 0.0)).astype(i32)


def _nsa_dec_cmp(qn3, page_table, pool_k, pool_v, layer, lp):
    B = qn3.shape[0]
    n_pages = page_table.shape[1]
    full = lambda s: pl.BlockSpec(s, lambda b, pt: (0,) * len(s))
    return pl.pallas_call(
        functools.partial(_nsa_dec_cmp_kernel, layer=layer, n_pages=n_pages),
        grid_spec=pltpu.PrefetchScalarGridSpec(
            num_scalar_prefetch=1,
            grid=(B,),
            in_specs=[
                pl.BlockSpec((None, 1, D_NSA), lambda b, pt: (b, 0, 0)),
                full((2, CMP_LEN // 2, 256)), full((2, CMP_LEN // 2, 256, 2 * CMP_HID)),
                full((2, 2 * CMP_HID, 128)), full((3, 128)), full((NSA_KV, 8, N_BUCKETS)),
                pl.BlockSpec(memory_space=pl.ANY), pl.BlockSpec(memory_space=pl.ANY),
            ],
            out_specs=[pl.BlockSpec((None, 1, D_NSA), lambda b, pt: (b, 0, 0)),
                       pl.BlockSpec((None, SLC_TOPN, 128), lambda b, pt: (b, 0, 0))],
            scratch_shapes=[pltpu.VMEM((2, n_pages, NSA_KV * NSA_HD, PAGE), f32),
                            pltpu.VMEM((n_pages * PAGE, NSA_KV * NSA_HD), f32),
                            pltpu.SemaphoreType.DMA((2,))],
        ),
        out_shape=[jax.ShapeDtypeStruct((B, 1, D_NSA), f32), jax.ShapeDtypeStruct((B, SLC_TOPN, 128), i32)],
        compiler_params=_params(("arbitrary",)),
        name="nsa_dec_cmp",
    )(page_table, qn3, lp["cmp_pe2"], lp["cmp_w1bd"], lp["cmp_w2bd"], lp["k_gain"], lp["tab8"], pool_k, pool_v)


def _nsa_dec_sel_kernel(pt_ref, idx_ref, q_ref, kv_ref, sm_ref, ocmp_ref, tab_ref, wk_ref, wv_ref, pk_ref, pv_ref,
                        y_ref, wko_ref, wvo_ref, kg, vg, sem, *, layer, n_pages):
    b = pl.program_id(0)
    n_past = n_pages * PAGE
    q_pos = n_past
    n_buf = wk_ref.shape[1]
    new_blk = n_past // SLC_BLOCK
    per_page = PAGE // SLC_BLOCK

    def blk_copy(pool, buf, g, s, sm_i):
        j = jnp.minimum(idx_ref[b, s, g], new_blk - 1)
        src = pool.at[layer, pt_ref[b, j // per_page], pl.ds(g * NSA_HD, NSA_HD)]
        return pltpu.make_async_copy(src, buf.at[g, :, pl.ds(s * PAGE, PAGE)], sem.at[sm_i])

    for g in range(NSA_KV):
        for s in range(SLC_TOPN):
            blk_copy(pk_ref, kg, g, s, 0).start()
            blk_copy(pv_ref, vg, g, s, 1).start()

    kv = kv_ref[...]
    q = q_ref[...]
    gate = jax.nn.sigmoid(sm_ref[...])
    ocmp = ocmp_ref[...]

    ii, jj = _iota((128, 128), 0), _iota((128, 128), 1)
    column = lambda off: jnp.sum(jnp.where(ii == jj, kv[:, off:off + 128], 0.0), axis=1, keepdims=True)
    last_lane = _iota((128, n_buf), 1) == n_buf - 1
    wk, wv = wk_ref[...], wv_ref[...]
    wko_ref[...] = jnp.where(last_lane, column(KV_WK), pltpu.roll(wk, n_buf - 1, 1))
    wvo_ref[...] = jnp.where(last_lane, column(KV_WV), pltpu.roll(wv, n_buf - 1, 1))

    def attend(qg, kt, vt, bias, ok, k_new, v_new, bias_new, ok_new):
        lg = jnp.where(ok, _dot(qg, kt) + bias, NEG)
        lg_new = jnp.where(ok_new, jnp.sum(qg * k_new, axis=-1, keepdims=True) + bias_new, NEG)
        m = jnp.maximum(jnp.max(lg, axis=-1, keepdims=True), lg_new)
        e = jnp.where(ok, jnp.exp(lg - m), 0.0)
        e_new = jnp.where(ok_new, jnp.exp(lg_new - m), 0.0)
        den = jnp.sum(e, axis=-1, keepdims=True) + e_new
        p, p_new = e / den, e_new / den
        return _dot_nt(p, vt) + p_new * v_new

    for g in range(NSA_KV):
        for s in range(SLC_TOPN):
            blk_copy(pk_ref, kg, g, s, 0).wait()
            blk_copy(pv_ref, vg, g, s, 1).wait()

    lw = _iota((8, n_buf), 1)
    d_w = q_pos - (n_past - n_buf + lw)
    ok_w = (d_w >= 0) & (d_w <= WINDOW)
    ls = _iota((8, SLC_TOPN * PAGE), 1)
    slot = ls // PAGE
    zero = jnp.zeros((8, 1), i32)
    o_slc, o_win = [], []
    for g in range(NSA_KV):
        tabcols = [tab_ref[g, :, bk:bk + 1] for bk in range(N_BUCKETS)]
        qg = _query_rows(q, g)
        gs = slice(g * NSA_HD, (g + 1) * NSA_HD)
        b_new = _bias_chain(zero, tabcols)
        blk = jnp.zeros(ls.shape, i32)
        n_new = jnp.zeros((8, 1), i32)
        for s in range(SLC_TOPN):
            j = idx_ref[b, s, g]
            blk = jnp.where(slot == s, j, blk)
            n_new = jnp.where(j == new_blk, n_new + 1, n_new)
        k_pos = (jnp.minimum(blk, new_blk - 1) // per_page) * PAGE + ls % PAGE
        ok_s = (blk < new_blk) & (k_pos // SLC_BLOCK == blk)
        o_slc.append(attend(qg, kg[g], vg[g], _bias_chain(q_pos - k_pos, tabcols), ok_s,
                            kv[:, KV_SK + g * NSA_HD:KV_SK + (g + 1) * NSA_HD],
                            kv[:, KV_SV + g * NSA_HD:KV_SV + (g + 1) * NSA_HD], b_new, n_new > 0))
        o_win.append(attend(qg, wk[gs, :], wv[gs, :], _bias_chain(d_w, tabcols), ok_w,
                            kv[:, KV_WK + g * NSA_HD:KV_WK + (g + 1) * NSA_HD],
                            kv[:, KV_WV + g * NSA_HD:KV_WV + (g + 1) * NSA_HD], b_new, zero == 0))
    gl = lambda t: gate[:, SM_G + t * NSA_HEADS:SM_G + (t + 1) * NSA_HEADS]
    wide = lambda gt: jnp.concatenate([jnp.broadcast_to(gt[:, h:h + 1], (1, NSA_HD)) for h in range(NSA_HEADS)], axis=1)
    y_ref[...] = wide(gl(0)) * ocmp + wide(gl(1)) * _head_lanes(o_slc) + wide(gl(2)) * _head_lanes(o_win)


def _nsa_dec_sel(qn3, kv6_3, proj3, ocmp, idx, page_table, win_k, win_v, pool_k, pool_v, layer, lp):
    B = qn3.shape[0]
    n_pages = page_table.shape[1]
    n_buf = win_k.shape[3]
    full = lambda s: pl.BlockSpec(s, lambda b, pt, ix: (0,) * len(s))
    row = lambda w, j: pl.BlockSpec((None, 1, w), lambda b, pt, ix: (b, 0, j))
    win = pl.BlockSpec((None, None, 128, n_buf), lambda b, pt, ix: (layer, b, 0, 0))
    return pl.pallas_call(
        functools.partial(_nsa_dec_sel_kernel, layer=layer, n_pages=n_pages),
        grid_spec=pltpu.PrefetchScalarGridSpec(
            num_scalar_prefetch=2,
            grid=(B,),
            in_specs=[row(D_NSA, 0), row(768, 0), row(128, SM_OFF // 128), row(D_NSA, 0),
                      full((NSA_KV, 8, N_BUCKETS)), win, win,
                      pl.BlockSpec(memory_space=pl.ANY), pl.BlockSpec(memory_space=pl.ANY)],
            out_specs=[row(D_NSA, 0),
                       pl.BlockSpec((None, 128, n_buf), lambda b, pt, ix: (b, 0, 0)),
                       pl.BlockSpec((None, 128, n_buf), lambda b, pt, ix: (b, 0, 0))],
            scratch_shapes=[pltpu.VMEM((NSA_KV, NSA_HD, SLC_TOPN * PAGE), f32),
                            pltpu.VMEM((NSA_KV, NSA_HD, SLC_TOPN * PAGE), f32),
                            pltpu.SemaphoreType.DMA((2,))],
        ),
        out_shape=[jax.ShapeDtypeStruct((B, 1, D_NSA), f32),
                   jax.ShapeDtypeStruct((B, 128, n_buf), f32), jax.ShapeDtypeStruct((B, 128, n_buf), f32)],
        compiler_params=_params(("arbitrary",)),
        name="nsa_dec_sel",
    )(page_table, idx, qn3, kv6_3, proj3, ocmp, lp["tab8"], win_k, win_v, pool_k, pool_v)


def _layer_params(i, p):
    pad128 = lambda v, off: jnp.zeros((1, 128), f32).at[0, off:off + v.shape[0]].set(v)
    return dict(
        ssd_conv_w=p["ssd_conv_w"][i], ssd_conv_b=p["ssd_conv_b"][i][None],
        ssd_dtb=pad128(p["ssd_dt_bias"][i], SM_DT), ssd_alog=pad128(p["ssd_a_log"][i], SM_DT),
        ssd_dexp=jnp.repeat(p["ssd_d"][i], SSD_HD)[None], ssd_norm=p["ssd_norm"][i][None],
        gdn_conv_w=p["gdn_conv_w"][i],
        gdn_dtb=pad128(p["gdn_dt_bias"][i], SM_A), gdn_alog=pad128(p["gdn_a_log"][i], SM_A),
        gdn_norm=p["gdn_norm"][i][None],
        q_gain=jnp.tile(p["nsa_q_norm"][i], 2)[None], k_gain=jnp.tile(p["nsa_k_norm"][i], (1, 2)),
        cmp_pe=p["nsa_cmp_pe"][i], cmp_w1=p["nsa_cmp_w1"][i], cmp_w2=p["nsa_cmp_w2"][i],
        cmp_pe2=jnp.tile(p["nsa_cmp_pe"][i], (1, 1, 2)).reshape(2, CMP_LEN // 2, 256),
        cmp_w1bd=_block_diag2(p["nsa_cmp_w1"][i].reshape(2, CMP_LEN, NSA_HD, CMP_HID)).astype(bf16)
        .reshape(2, CMP_LEN // 2, 256, 2 * CMP_HID),
        cmp_w2bd=_block_diag2(p["nsa_cmp_w2"][i]).astype(bf16),
        tab8=jnp.pad(p["rel_bias"].T.reshape(NSA_KV, NSA_REP, N_BUCKETS), ((0, 0), (0, 8 - NSA_REP), (0, 0))),
    )


def _block_diag2(w):
    z = jnp.zeros_like(w)
    return jnp.concatenate([jnp.concatenate([w, z], axis=-1), jnp.concatenate([z, w], axis=-1)], axis=-2)


def _reorder_w_in(w_in):
    cuts = [(1024, 2560), (2576, 4112), (0, 1024), (4112, 4624), (4632, 5144), (5144, 5912),
            (2560, 2576), (4624, 4632), (5912, 5936)]
    wt = jnp.transpose(w_in, (0, 2, 1))
    parts = [wt[:, a:b] for a, b in cuts]
    used = sum(b - a for a, b in cuts)
    parts.append(jnp.zeros((w_in.shape[0], PROJ_W - used, w_in.shape[1]), w_in.dtype))
    return jnp.concatenate(parts, axis=1).astype(bf16)


def _mix_prompt(x3, i, p, lp, w_r):
    B, L, _ = x3.shape
    n = B * L
    x = x3.reshape(n, D_MODEL)
    proj = _inproj(x, p["norm_mix"], w_r, i)
    proj3 = proj.reshape(B, L, PROJ_W)
    y_ssd, st_ssd, cst_ssd = _ssd_prompt(proj3, lp)
    y_gdn, st_gdn, cst_gdn = _gdn_prompt(proj3, lp)
    qt, kv6_3, kv6t, gt = _prep_t(proj3, lp["q_gain"], lp["k_gain"])
    kc, vct = _cmp_prompt(kv6_3, lp)
    y_nsa = _nsa_prompt(qt, kv6_3, kv6t, kc, vct, gt, p["rel_bias"])
    x = _outproj(x, y_ssd.reshape(n, D_SSD), y_gdn.reshape(n, D_GDN), y_nsa.reshape(n, D_NSA), p["w_out"], i)
    rows = lambda off: jnp.transpose(kv6t[:, off:off + 128].reshape(B, NSA_KV, NSA_HD, L), (0, 3, 1, 2))
    keep = min(WINDOW, L)
    state = (st_ssd, cst_ssd, st_gdn, cst_gdn, rows(KV_CK), rows(KV_CV), rows(KV_SK), rows(KV_SV),
             rows(KV_WK)[:, L - keep:], rows(KV_WV)[:, L - keep:])
    return x.reshape(B, L, D_MODEL), state


def _mix_decode(x, i, p, lp, w_r, st):
    B = x.shape[0]
    proj = _inproj(x, p["norm_mix"], w_r, i)
    proj3 = proj.reshape(B, 1, PROJ_W)
    y_ssd, st_ssd, cst_ssd = _ssd_decode(proj3, st["ssd_conv"], st["ssd"], i, lp)
    y_gdn, st_gdn, cst_gdn = _gdn_decode(proj3, st["gdn_conv"], st["gdn"], i, lp)
    qn, kv6 = _prep(proj, lp["q_gain"], lp["k_gain"])
    qn3, kv6_3 = qn.reshape(B, 1, D_NSA), kv6.reshape(B, 1, 768)
    o_cmp, idx = _nsa_dec_cmp(qn3, st["page_table"], st["cmp_k"], st["cmp_v"], i, lp)
    y_nsa, win_k, win_v = _nsa_dec_sel(qn3, kv6_3, proj3, o_cmp, idx[:, :, :NSA_KV], st["page_table"],
                                       st["win_k"], st["win_v"], st["slc_k"], st["slc_v"], i, lp)
    x = _outproj(x, y_ssd.reshape(B, D_SSD), y_gdn.reshape(B, D_GDN), y_nsa.reshape(B, D_NSA), p["w_out"], i)
    rows = lambda off: kv6[:, off:off + 128].reshape(B, 1, NSA_KV, NSA_HD)
    n_buf = win_k.shape[2]
    buf = lambda w: jnp.transpose(w.reshape(B, NSA_KV, NSA_HD, n_buf), (0, 3, 1, 2))
    state = (st_ssd, cst_ssd, st_gdn, cst_gdn, rows(KV_CK), rows(KV_CV), rows(KV_SK), rows(KV_SV),
             buf(win_k), buf(win_v))
    return x, state


def kernel(x_prompt, x_sample, state_ssd, state_ssd_conv, state_gdn, state_gdn_conv, cache_cmp_k, cache_cmp_v,
           cache_slc_k, cache_slc_v, cache_win_k, cache_win_v, page_table, rel_bias, norm_ffn1, w_ffn1_gate,
           w_ffn1_up, w_ffn1_down, norm_mix, w_in, ssd_conv_w, ssd_conv_b, ssd_dt_bias, ssd_a_log, ssd_d, ssd_norm,
           gdn_conv_w, gdn_dt_bias, gdn_a_log, gdn_norm, nsa_q_norm, nsa_k_norm, nsa_cmp_pe, nsa_cmp_w1,
           nsa_cmp_w2, w_out, norm_ffn2, w_ffn2_gate, w_ffn2_up, w_ffn2_down):
    bp, lp_len, _ = x_prompt.shape
    bs = x_sample.shape[0]
    gain3 = lambda g: g.reshape(DEPTH, 1, D_MODEL)
    p = dict(rel_bias=rel_bias, norm_mix=gain3(norm_mix), w_out=w_out.astype(bf16), ssd_conv_w=ssd_conv_w,
             ssd_conv_b=ssd_conv_b,
             ssd_dt_bias=ssd_dt_bias, ssd_a_log=ssd_a_log, ssd_d=ssd_d, ssd_norm=ssd_norm, gdn_conv_w=gdn_conv_w,
             gdn_dt_bias=gdn_dt_bias, gdn_a_log=gdn_a_log, gdn_norm=gdn_norm, nsa_q_norm=nsa_q_norm,
             nsa_k_norm=nsa_k_norm, nsa_cmp_pe=nsa_cmp_pe, nsa_cmp_w1=nsa_cmp_w1, nsa_cmp_w2=nsa_cmp_w2)
    n1, n2 = gain3(norm_ffn1), gain3(norm_ffn2)
    chan_row = lambda c: jnp.transpose(c, (0, 1, 3, 4, 2)).reshape(c.shape[:2] + (NSA_KV * NSA_HD, c.shape[2]))
    st = dict(ssd=state_ssd, ssd_conv=state_ssd_conv, gdn=state_gdn, gdn_conv=state_gdn_conv,
              cmp_k=chan_row(cache_cmp_k), cmp_v=chan_row(cache_cmp_v), slc_k=chan_row(cache_slc_k),
              slc_v=chan_row(cache_slc_v), win_k=chan_row(cache_win_k), win_v=chan_row(cache_win_v),
              page_table=page_table)
    w_r = _reorder_w_in(w_in)

    hp = x_prompt.reshape(bp * lp_len, D_MODEL)
    hs = x_sample.reshape(bs, D_MODEL)
    outs_p, outs_s = [], []
    for i in range(DEPTH):
        lp = _layer_params(i, p)
        hp, hs = _ffn(hp, hs, n1, w_ffn1_gate, w_ffn1_up, w_ffn1_down, i)
        hp3, st_p = _mix_prompt(hp.reshape(bp, lp_len, D_MODEL), i, p, lp, w_r)
        hs, st_s = _mix_decode(hs, i, p, lp, w_r, st)
        hp, hs = _ffn(hp3.reshape(bp * lp_len, D_MODEL), hs, n2, w_ffn2_gate, w_ffn2_up, w_ffn2_down, i)
        outs_p.append(st_p)
        outs_s.append(st_s)
    stack = lambda outs: [jnp.stack(t) for t in zip(*outs)]
    return (hp.reshape(bp, lp_len, D_MODEL), hs.reshape(bs, 1, D_MODEL), *stack(outs_p), *stack(outs_s))
```

```python
import functools
import math

import numpy as np
import jax
import jax.numpy as jnp
from jax import lax
from jax.experimental import pallas as pl
from jax.experimental.pallas import tpu as pltpu

f32 = jnp.float32
bf16 = jnp.bfloat16
i32 = jnp.int32
HI = lax.Precision.HIGHEST

D_MODEL = 2048
DEPTH = 4
PAGE = 128
D_SSD = 1024
SSD_HD = 64
SSD_HEADS = 16
SSD_GROUPS = 2
SSD_STATE = 128
SSD_CHUNK = 128
D_GDN = 512
GDN_HD = 128
GDN_HEADS = 4
GDN_CHUNK = 64
GDN_STEP_CHUNKS = 4
D_NSA = 512
NSA_HD = 64
NSA_HEADS = 8
NSA_KV = 2
NSA_REP = 4
CMP_STRIDE = 16
CMP_LEN = 32
CMP_HID = 128
SLC_BLOCK = 64
SLC_TOPN = 16
WINDOW = 512
N_BUCKETS = 32
MAX_DISTANCE = 128
CONV_W = 4
D_FF = 5632
SSD_CONV_CH = 1536
GDN_CONV_CH = 1536
EPS = 1e-6
NEG = -1e30

XBC_OFF = 0
QKV_OFF = 1536
ZS_OFF = 3072
ZG_OFF = 4096
QN_OFF = 4608
KV_OFF = 5120
SM_OFF = 5888
PROJ_W = 6144
SM_DT, SM_A, SM_B, SM_G = 0, 16, 20, 24
KV_CK, KV_CV, KV_SK, KV_SV, KV_WK, KV_WV = 0, 128, 256, 384, 512, 640

VMEM_LIMIT = 56 * 1024 * 1024
ROW_TILE = 1024
FFN_COL_TILE = 256
PROJ_COL_TILE = 512
OUT_ROW_TILE = 2048
PREP_ROW_TILE = 512
NSA_Q_TILE = 128
TB_FAR, TB_PREV, TB_DIAG, TB_WIN0, TB_NONE, TB_KINDS = 0, 1, 2, 3, 4, 5


def _bucket_thresholds():
    exact = N_BUCKETS // 2
    d = np.arange(0, 4 * MAX_DISTANCE)
    nf = np.maximum(d, 1).astype(np.float32)
    large = exact + (np.log(nf / np.float32(exact)) / np.float32(math.log(MAX_DISTANCE / exact))
                     * np.float32(N_BUCKETS - exact)).astype(np.int32)
    bucket = np.where(d < exact, d, np.minimum(large, N_BUCKETS - 1))
    return [int(np.argmax(bucket >= b)) for b in range(N_BUCKETS)]


_THR = _bucket_thresholds()
BIAS_FAR = _THR[N_BUCKETS - 1]


def _bias_chain(d, tabcols):
    v = jnp.broadcast_to(tabcols[N_BUCKETS - 1], d.shape)
    for b in range(N_BUCKETS - 2, -1, -1):
        v = jnp.where(d < _THR[b + 1], tabcols[b], v)
    return v


def _silu(x):
    return x * jax.nn.sigmoid(x)


def _softplus(x):
    return jnp.maximum(x, 0.0) + jnp.log1p(jnp.exp(-jnp.abs(x)))


def _dot(a, b, prec=None):
    if prec is None:
        a, b = a.astype(bf16), b.astype(bf16)
    return jnp.dot(a, b, preferred_element_type=f32, precision=prec)


def _dot_nt(a, b, prec=None):
    if prec is None:
        a, b = a.astype(bf16), b.astype(bf16)
    return lax.dot_general(a, b, (((1,), (1,)), ((), ())), preferred_element_type=f32, precision=prec)


def _dot_tn(a, b, prec=None):
    if prec is None:
        a, b = a.astype(bf16), b.astype(bf16)
    return lax.dot_general(a, b, (((0,), (0,)), ((), ())), preferred_element_type=f32, precision=prec)


def _split2(a):
    hi = a.astype(bf16)
    return hi, (a - hi.astype(f32)).astype(bf16)


def _dot3(a2, b2):
    d = lambda x, y: jnp.dot(x, y, preferred_element_type=f32)
    return d(a2[0], b2[0]) + (d(a2[0], b2[1]) + d(a2[1], b2[0]))


def _split3(a):
    a1 = a.astype(bf16)
    r = a - a1.astype(f32)
    a2 = r.astype(bf16)
    return a1, a2, (r - a2.astype(f32)).astype(bf16)


def _dot_01(a, m01):
    m = m01.astype(bf16)
    return sum(jnp.dot(t, m, preferred_element_type=f32) for t in _split3(a))


def _dot_01l(m01, b):
    m = m01.astype(bf16)
    return sum(jnp.dot(m, t, preferred_element_type=f32) for t in _split3(b))


def _iota(shape, dim):
    return lax.broadcasted_iota(i32, shape, dim)


def _group_ones(n, width):
    return (_iota((n, n), 0) // width == _iota((n, n), 1) // width).astype(f32)


def _group_rms(x, width):
    ss = _dot_01(x * x, _group_ones(x.shape[1], width))
    return x * lax.rsqrt(ss * (1.0 / width) + EPS)


def _params(sem):
    return pltpu.CompilerParams(dimension_semantics=sem, vmem_limit_bytes=VMEM_LIMIT)


FFN_EXTRA = 16


def _ffn_kernel(x_ref, xs_ref, g_ref, wg_ref, wu_ref, wd_ref, o_ref, os_ref, h_ref):
    tm, ns = x_ref.shape[0], xs_ref.shape[0]

    @pl.when(pl.program_id(1) == 0)
    def _():
        rms = lambda x: x * lax.rsqrt(jnp.mean(x * x, axis=-1, keepdims=True) + EPS) * g_ref[...]
        x, xs = x_ref[...], xs_ref[...]
        h_ref[0:tm, :] = rms(x).astype(bf16)
        h_ref[tm:tm + FFN_EXTRA, :] = jnp.concatenate(
            [rms(xs), jnp.zeros((FFN_EXTRA - ns, D_MODEL), f32)], axis=0).astype(bf16)
        o_ref[...] = x
        os_ref[...] = xs

    h = h_ref[...]
    a = jnp.dot(h, wg_ref[...].astype(bf16), preferred_element_type=f32)
    u = jnp.dot(h, wu_ref[...].astype(bf16), preferred_element_type=f32)
    y = 0.5 * _dot(_silu(a) * u, wd_ref[...])
    o_ref[...] += y[0:tm]
    os_ref[...] += y[tm:tm + ns]


def _ffn(x, xs, gain, wg, wu, wd, layer):
    n, ns = x.shape[0], xs.shape[0]
    tm, tf = ROW_TILE, FFN_COL_TILE
    return pl.pallas_call(
        _ffn_kernel,
        grid=(n // tm, D_FF // tf),
        in_specs=[
            pl.BlockSpec((tm, D_MODEL), lambda i, j: (i, 0)),
            pl.BlockSpec((ns, D_MODEL), lambda i, j: (0, 0)),
            pl.BlockSpec((None, 1, D_MODEL), lambda i, j: (layer, 0, 0)),
            pl.BlockSpec((None, D_MODEL, tf), lambda i, j: (layer, 0, j)),
            pl.BlockSpec((None, D_MODEL, tf), lambda i, j: (layer, 0, j)),
            pl.BlockSpec((None, tf, D_MODEL), lambda i, j: (layer, j, 0)),
        ],
        out_specs=[pl.BlockSpec((tm, D_MODEL), lambda i, j: (i, 0)),
                   pl.BlockSpec((ns, D_MODEL), lambda i, j: (0, 0))],
        out_shape=[jax.ShapeDtypeStruct((n, D_MODEL), f32), jax.ShapeDtypeStruct((ns, D_MODEL), f32)],
        scratch_shapes=[pltpu.VMEM((tm + FFN_EXTRA, D_MODEL), bf16)],
        compiler_params=_params(("arbitrary", "arbitrary")),
        name="ffn",
    )(x, xs, gain, wg, wu, wd)


def _inproj_kernel(x_ref, g_ref, w_ref, o_ref, h_ref):
    @pl.when(pl.program_id(1) == 0)
    def _():
        x = x_ref[...]
        h = x * lax.rsqrt(jnp.mean(x * x, axis=-1, keepdims=True) + EPS) * g_ref[...]
        h_ref[...] = h.astype(bf16)

    o_ref[...] = _dot_nt(h_ref[...], w_ref[...])


def _inproj(x, gain, w_r, layer):
    n = x.shape[0]
    tm = min(n, ROW_TILE)
    tn = PROJ_COL_TILE
    return pl.pallas_call(
        _inproj_kernel,
        grid=(n // tm, PROJ_W // tn),
        in_specs=[
            pl.BlockSpec((tm, D_MODEL), lambda i, j: (i, 0)),
            pl.BlockSpec((None, 1, D_MODEL), lambda i, j: (layer, 0, 0)),
            pl.BlockSpec((None, tn, D_MODEL), lambda i, j: (layer, j, 0)),
        ],
        out_specs=pl.BlockSpec((tm, tn), lambda i, j: (i, j)),
        out_shape=jax.ShapeDtypeStruct((n, PROJ_W), f32),
        scratch_shapes=[pltpu.VMEM((tm, D_MODEL), bf16)],
        compiler_params=_params(("parallel", "arbitrary")),
        name="inproj",
    )(x, gain, w_r)


def _outproj_kernel(x_ref, ys_ref, yg_ref, yn_ref, ws_ref, wg_ref, wn_ref, o_ref):
    acc = _dot(ys_ref[...], ws_ref[...])
    acc += _dot(yg_ref[...], wg_ref[...])
    acc += _dot(yn_ref[...], wn_ref[...])
    o_ref[...] = x_ref[...] + acc


def _outproj(x, ys, yg, yn, w_out, layer):
    n = x.shape[0]
    tm = min(n, OUT_ROW_TILE)
    tn = PROJ_COL_TILE
    return pl.pallas_call(
        _outproj_kernel,
        grid=(n // tm, D_MODEL // tn),
        in_specs=[
            pl.BlockSpec((tm, tn), lambda i, j: (i, j)),
            pl.BlockSpec((tm, D_SSD), lambda i, j: (i, 0)),
            pl.BlockSpec((tm, D_GDN), lambda i, j: (i, 0)),
            pl.BlockSpec((tm, D_NSA), lambda i, j: (i, 0)),
            pl.BlockSpec((None, D_SSD, tn), lambda i, j: (layer, 0, j)),
            pl.BlockSpec((None, D_GDN, tn), lambda i, j: (layer, 2, j)),
            pl.BlockSpec((None, D_NSA, tn), lambda i, j: (layer, 3, j)),
        ],
        out_specs=pl.BlockSpec((tm, tn), lambda i, j: (i, j)),
        out_shape=jax.ShapeDtypeStruct((n, D_MODEL), f32),
        compiler_params=_params(("parallel", "arbitrary")),
        name="outproj",
    )(x, ys, yg, yn, w_out, w_out, w_out)


def _prep_kernel(q_ref, kv_ref, qg_ref, kg_ref, qn_ref, kv6_ref):
    for s in range(D_NSA // 128):
        sl = slice(s * 128, (s + 1) * 128)
        qn_ref[:, sl] = _group_rms(q_ref[:, sl], NSA_HD) * qg_ref[...]
    kv = kv_ref[...]
    kv6_ref[...] = kv[:, :6 * 128]
    kv6_ref[:, KV_SK:KV_SK + 128] = _group_rms(kv[:, KV_SK:KV_SK + 128], NSA_HD) * kg_ref[1:2, :]
    kv6_ref[:, KV_WK:KV_WK + 128] = _group_rms(kv[:, KV_WK:KV_WK + 128], NSA_HD) * kg_ref[2:3, :]


def _prep(proj, qgain, kgain):
    n = proj.shape[0]
    tm = min(n, PREP_ROW_TILE)
    return pl.pallas_call(
        _prep_kernel,
        grid=(n // tm,),
        in_specs=[
            pl.BlockSpec((tm, D_NSA), lambda i: (i, QN_OFF // D_NSA)),
            pl.BlockSpec((tm, 1024), lambda i: (i, KV_OFF // 1024)),
            pl.BlockSpec((1, 128), lambda i: (0, 0)),
            pl.BlockSpec((3, 128), lambda i: (0, 0)),
        ],
        out_specs=[pl.BlockSpec((tm, D_NSA), lambda i: (i, 0)),
                   pl.BlockSpec((tm, 768), lambda i: (i, 0))],
        out_shape=[jax.ShapeDtypeStruct((n, D_NSA), f32), jax.ShapeDtypeStruct((n, 768), f32)],
        compiler_params=_params(("parallel",)),
        name="nsa_prep",
    )(proj, proj, qgain, kgain)


def _ssd_prompt_kernel(xbc_ref, z_ref, sm_ref, cw_ref, cb_ref, dtb_ref, alog_ref, dexp_ref, nrm_ref,
                       y_ref, st_ref, cst_ref, xbuf, hst, ybuf):
    c = pl.program_id(1)
    cl = SSD_CHUNK

    @pl.when(c == 0)
    def _():
        xbuf[0:8, :] = jnp.zeros((8, SSD_CONV_CH), f32)
        hst[...] = jnp.zeros(hst.shape, f32)

    x = xbc_ref[...]
    xbuf[8:8 + cl, :] = x
    conv = cw_ref[3:4, :] * x
    for k in range(1, CONV_W):
        conv += cw_ref[3 - k:4 - k, :] * xbuf[pl.ds(8 - k, cl), :]
    xbuf[0:8, :] = x[cl - 8:cl, :]
    xc = _silu(conv + cb_ref[...])
    xs = xc[:, :D_SSD]

    dt = jnp.where(_iota((cl, 128), 1) < SSD_HEADS, _softplus(sm_ref[...] + dtb_ref[...]), 0.0)
    da = dt * (-jnp.exp(alog_ref[...]))
    row, col = _iota((cl, cl), 0), _iota((cl, cl), 1)
    causal = row >= col
    cum = _dot_01l(causal, da)
    cum_t = cum.T
    last = cum[cl - 1:cl, :]
    e_last = jnp.exp(last)
    spread = (_iota((128, D_SSD), 1) // SSD_HD == _iota((128, D_SSD), 0)).astype(f32)
    coef = _dot_01(jnp.concatenate([dt, jnp.exp(last - cum), jnp.exp(cum)], axis=0), spread)
    xdt = xs * coef[0:cl]
    xw = xdt * coef[cl:2 * cl]
    e_cum = coef[2 * cl:3 * cl]

    hpg = SSD_HEADS // SSD_GROUPS
    gw = hpg * SSD_HD
    for g in range(SSD_GROUPS):
        bm = xc[:, D_SSD + g * SSD_STATE:D_SSD + (g + 1) * SSD_STATE]
        cm = xc[:, D_SSD + (SSD_GROUPS + g) * SSD_STATE:D_SSD + (SSD_GROUPS + g + 1) * SSD_STATE]
        cb = _dot_nt(cm, bm)
        h0 = hst[g]
        y_off = _dot_nt(cm, h0) * e_cum[:, g * gw:(g + 1) * gw]
        s_new = _dot_tn(xw[:, g * gw:(g + 1) * gw], bm)
        for r in range(hpg):
            h = g * hpg + r
            seg = cum[:, h:h + 1] - cum_t[h:h + 1, :]
            decay = jnp.where(causal, jnp.exp(jnp.where(causal, seg, 0.0)), 0.0)
            rs = slice(r * SSD_HD, (r + 1) * SSD_HD)
            ybuf[:, h * SSD_HD:(h + 1) * SSD_HD] = (_dot(cb * decay, xdt[:, h * SSD_HD:(h + 1) * SSD_HD])
                                                    + y_off[:, rs])
            hst[g, rs, :] = e_last[:, h:h + 1] * h0[rs, :] + s_new[rs, :]

    y = (ybuf[...] + dexp_ref[...] * xs) * _silu(z_ref[...])
    gw = D_SSD // SSD_GROUPS
    for g in range(SSD_GROUPS):
        yg = y[:, g * gw:(g + 1) * gw]
        yg = yg * lax.rsqrt(jnp.mean(yg * yg, axis=-1, keepdims=True) + EPS)
        y_ref[:, g * gw:(g + 1) * gw] = (yg * nrm_ref[:, g * gw:(g + 1) * gw]).astype(y_ref.dtype)

    @pl.when(c == pl.num_programs(1) - 1)
    def _():
        st_ref[...] = hst[...]
        cst_ref[...] = x[cl - (CONV_W - 1):cl, :]


def _ssd_prompt(proj3, lp):
    B, L, _ = proj3.shape
    cl = SSD_CHUNK
    gh = SSD_HEADS // SSD_GROUPS * SSD_HD
    vec = lambda w: pl.BlockSpec((1, w), lambda b, c: (0, 0))
    y, st, cst = pl.pallas_call(
        _ssd_prompt_kernel,
        grid=(B, L // cl),
        in_specs=[
            pl.BlockSpec((None, cl, SSD_CONV_CH), lambda b, c: (b, c, XBC_OFF // SSD_CONV_CH)),
            pl.BlockSpec((None, cl, D_SSD), lambda b, c: (b, c, ZS_OFF // D_SSD)),
            pl.BlockSpec((None, cl, 128), lambda b, c: (b, c, SM_OFF // 128)),
            pl.BlockSpec((CONV_W, SSD_CONV_CH), lambda b, c: (0, 0)),
            vec(SSD_CONV_CH), vec(128), vec(128), vec(D_SSD), vec(D_SSD),
        ],
        out_specs=[
            pl.BlockSpec((None, cl, D_SSD), lambda b, c: (b, c, 0)),
            pl.BlockSpec((None, SSD_GROUPS, gh, SSD_STATE), lambda b, c: (b, 0, 0, 0)),
            pl.BlockSpec((None, CONV_W - 1, SSD_CONV_CH), lambda b, c: (b, 0, 0)),
        ],
        out_shape=[
            jax.ShapeDtypeStruct((B, L, D_SSD), bf16),
            jax.ShapeDtypeStruct((B, SSD_GROUPS, gh, SSD_STATE), f32),
            jax.ShapeDtypeStruct((B, CONV_W - 1, SSD_CONV_CH), f32),
        ],
        scratch_shapes=[
            pltpu.VMEM((8 + cl, SSD_CONV_CH), f32),
            pltpu.VMEM((SSD_GROUPS, gh, SSD_STATE), f32),
            pltpu.VMEM((cl, D_SSD), f32),
        ],
        compiler_params=_params(("parallel", "arbitrary")),
        name="ssd_prompt",
    )(proj3, proj3, proj3, lp["ssd_conv_w"], lp["ssd_conv_b"], lp["ssd_dtb"], lp["ssd_alog"],
      lp["ssd_dexp"], lp["ssd_norm"])
    return y, st.reshape(B, SSD_HEADS, SSD_HD, SSD_STATE), cst


def _gdn_prompt_kernel(qkv_ref, z_ref, sm_ref, cw_ref, dtb_ref, alog_ref, nrm_ref,
                       y_ref, st_ref, cst_ref, xbuf, sst):
    c = pl.program_id(1)
    cl = GDN_CHUNK
    nch = GDN_STEP_CHUNKS
    tl = nch * cl

    @pl.when(c == 0)
    def _():
        xbuf[0:8, :] = jnp.zeros((8, GDN_CONV_CH), f32)
        sst[...] = jnp.zeros(sst.shape, f32)

    x = qkv_ref[...]
    xbuf[8:8 + tl, :] = x
    conv = cw_ref[3:4, :] * x
    for k in range(1, CONV_W):
        conv += cw_ref[3 - k:4 - k, :] * xbuf[pl.ds(8 - k, tl), :]
    xbuf[0:8, :] = x[tl - 8:tl, :]
    xc = _silu(conv)

    sm = sm_ref[...]
    beta_all = jax.nn.sigmoid(sm)
    g_all = -jnp.exp(alog_ref[...]) * _softplus(sm + dtb_ref[...])
    row, col = _iota((cl, cl), 0), _iota((cl, cl), 1)
    incl = row >= col
    strict = row > col
    eye = (row == col).astype(f32)
    rt, ct = _iota((tl, tl), 0), _iota((tl, tl), 1)
    cum = _dot_01l((rt >= ct) & (rt // cl == ct // cl), g_all)
    cum_t = cum.T

    units = [(ci, h) for ci in range(nch) for h in range(GDN_HEADS)]
    q, k, v, beta, cum_c, decay = {}, {}, {}, {}, {}, {}
    for u in units:
        ci, h = u
        rows = slice(ci * cl, (ci + 1) * cl)
        qh = xc[rows, h * GDN_HD:(h + 1) * GDN_HD]
        kh = xc[rows, D_GDN + h * GDN_HD:D_GDN + (h + 1) * GDN_HD]
        v[u] = xc[rows, 2 * D_GDN + h * GDN_HD:2 * D_GDN + (h + 1) * GDN_HD]
        q[u] = qh * lax.rsqrt(jnp.sum(qh * qh, axis=-1, keepdims=True) + EPS) * (GDN_HD ** -0.5)
        k[u] = kh * lax.rsqrt(jnp.sum(kh * kh, axis=-1, keepdims=True) + EPS)
        beta[u] = beta_all[rows, SM_B + h:SM_B + h + 1]
        cum_c[u] = cum[rows, SM_A + h:SM_A + h + 1]
        seg = cum_c[u] - cum_t[SM_A + h:SM_A + h + 1, rows]
        decay[u] = jnp.where(incl, jnp.exp(jnp.where(incl, seg, 0.0)), 0.0)
    pw = {u: -jnp.where(strict, beta[u] * _dot_nt(k[u], k[u]) * decay[u], 0.0) for u in units}
    inv = {u: eye + pw[u] for u in units}
    pw2 = {u: _split2(pw[u]) for u in units}
    for _ in range(5):
        pw2 = {u: _split2(_dot3(pw2[u], pw2[u])) for u in units}
        inv = {u: inv[u] + _dot3(_split2(inv[u]), pw2[u]) for u in units}
    rhs = {u: jnp.concatenate([beta[u] * v[u], (beta[u] * jnp.exp(cum_c[u])) * k[u]], axis=1) for u in units}
    sol = {u: _dot3(_split2(inv[u]), _split2(rhs[u])) for u in units}
    uu = {u: sol[u][:, :GDN_HD] for u in units}
    ww = {u: sol[u][:, GDN_HD:] for u in units}
    qk = {u: _dot_nt(q[u], k[u]) * decay[u] for u in units}

    s = [sst[h] for h in range(GDN_HEADS)]
    for ci in range(nch):
        rows = slice(ci * cl, (ci + 1) * cl)
        hs = [(ci, h) for h in range(GDN_HEADS)]
        v_new = {u: uu[u] - _dot(ww[u], s[u[1]]) for u in hs}
        o = {u: _dot(q[u] * jnp.exp(cum_c[u]), s[u[1]]) + _dot(qk[u], v_new[u]) for u in hs}
        for u in hs:
            h = u[1]
            last = cum[ci * cl + cl - 1:ci * cl + cl, SM_A + h:SM_A + h + 1]
            s[h] = jnp.exp(last) * s[h] + _dot_tn(k[u] * jnp.exp(last - cum_c[u]), v_new[u])
        for u in hs:
            sl = slice(u[1] * GDN_HD, (u[1] + 1) * GDN_HD)
            on = o[u] * lax.rsqrt(jnp.mean(o[u] * o[u], axis=-1, keepdims=True) + EPS)
            y_ref[rows, sl] = (on * nrm_ref[...] * _silu(z_ref[rows, sl])).astype(y_ref.dtype)
    for h in range(GDN_HEADS):
        sst[h] = s[h]

    @pl.when(c == pl.num_programs(1) - 1)
    def _():
        st_ref[...] = sst[...]
        cst_ref[...] = x[tl - (CONV_W - 1):tl, :]


def _gdn_prompt(proj3, lp):
    B, L, _ = proj3.shape
    cl = GDN_CHUNK * GDN_STEP_CHUNKS
    vec = lambda w: pl.BlockSpec((1, w), lambda b, c: (0, 0))
    return pl.pallas_call(
        _gdn_prompt_kernel,
        grid=(B, L // cl),
        in_specs=[
            pl.BlockSpec((None, cl, GDN_CONV_CH), lambda b, c: (b, c, QKV_OFF // GDN_CONV_CH)),
            pl.BlockSpec((None, cl, D_GDN), lambda b, c: (b, c, ZG_OFF // D_GDN)),
            pl.BlockSpec((None, cl, 128), lambda b, c: (b, c, SM_OFF // 128)),
            pl.BlockSpec((CONV_W, GDN_CONV_CH), lambda b, c: (0, 0)),
            vec(128), vec(128), vec(GDN_HD),
        ],
        out_specs=[
            pl.BlockSpec((None, cl, D_GDN), lambda b, c: (b, c, 0)),
            pl.BlockSpec((None, GDN_HEADS, GDN_HD, GDN_HD), lambda b, c: (b, 0, 0, 0)),
            pl.BlockSpec((None, CONV_W - 1, GDN_CONV_CH), lambda b, c: (b, 0, 0)),
        ],
        out_shape=[
            jax.ShapeDtypeStruct((B, L, D_GDN), bf16),
            jax.ShapeDtypeStruct((B, GDN_HEADS, GDN_HD, GDN_HD), f32),
            jax.ShapeDtypeStruct((B, CONV_W - 1, GDN_CONV_CH), f32),
        ],
        scratch_shapes=[
            pltpu.VMEM((8 + cl, GDN_CONV_CH), f32),
            pltpu.VMEM((GDN_HEADS, GDN_HD, GDN_HD), f32),
        ],
        compiler_params=_params(("parallel", "arbitrary")),
        name="gdn_prompt",
    )(proj3, proj3, proj3, lp["gdn_conv_w"], lp["gdn_dtb"], lp["gdn_alog"], lp["gdn_norm"])


def _cmp_prompt_kernel(k_ref, v_ref, pe_ref, w1_ref, w2_ref, kg_ref, kc_ref, vct_ref):
    nb = kc_ref.shape[0]
    for t, src in enumerate((k_ref, v_ref)):
        outs = []
        p1 = [jnp.zeros((nb, CMP_HID), f32) for _ in range(NSA_KV)]
        p2 = [jnp.zeros((nb, CMP_HID), f32) for _ in range(NSA_KV)]
        for l in range(CMP_STRIDE):
            xr = src[pl.ds(l, nb, stride=CMP_STRIDE), :]
            l2 = CMP_STRIDE + l
            for g in range(NSA_KV):
                xg = xr[:, g * NSA_HD:(g + 1) * NSA_HD]
                p1[g] += _dot(xg + pe_ref[t, l:l + 1, :], w1_ref[t, l * NSA_HD:(l + 1) * NSA_HD, :])
                p2[g] += _dot(xg + pe_ref[t, l2:l2 + 1, :], w1_ref[t, l2 * NSA_HD:(l2 + 1) * NSA_HD, :])
        for g in range(NSA_KV):
            hid = p1[g] + pltpu.roll(p2[g], nb - 1, 0)
            cmp = _dot(_silu(hid), w2_ref[t])
            if t == 0:
                cmp = cmp * lax.rsqrt(jnp.mean(cmp * cmp, axis=-1, keepdims=True) + EPS) * kg_ref[0:1, 0:NSA_HD]
            outs.append(cmp)
        both = jnp.concatenate(outs, axis=1)
        if t == 0:
            kc_ref[...] = both
        else:
            vct_ref[...] = both.T


def _cmp_prompt(kv6_3, lp):
    B, L, _ = kv6_3.shape
    nb = L // CMP_STRIDE
    full = lambda s: pl.BlockSpec(s, lambda b: (0,) * len(s))
    return pl.pallas_call(
        _cmp_prompt_kernel,
        grid=(B,),
        in_specs=[
            pl.BlockSpec((None, L, 128), lambda b: (b, 0, KV_CK // 128)),
            pl.BlockSpec((None, L, 128), lambda b: (b, 0, KV_CV // 128)),
            full((2, CMP_LEN, NSA_HD)), full((2, CMP_LEN * NSA_HD, CMP_HID)), full((2, CMP_HID, NSA_HD)),
            full((3, 128)),
        ],
        out_specs=[pl.BlockSpec((None, nb, 128), lambda b: (b, 0, 0)),
                   pl.BlockSpec((None, 128, nb), lambda b: (b, 0, 0))],
        out_shape=[jax.ShapeDtypeStruct((B, nb, 128), f32), jax.ShapeDtypeStruct((B, 128, nb), f32)],
        compiler_params=_params(("parallel",)),
        name="nsa_cmp_prompt",
    )(kv6_3, kv6_3, lp["cmp_pe"], lp["cmp_w1"], lp["cmp_w2"], lp["k_gain"])


def _nsa_prompt_kernel(rb_ref, qt_ref, ks_ref, kw_ref, vst_ref, vwt_ref, kc_ref, vct_ref, gt_ref,
                       y_ref, tb_ref, tc_ref, ext_ref, selm_ref, s_ref):
    qi = pl.program_id(1)
    L = ks_ref.shape[0]
    tq = NSA_Q_TILE
    nl = NSA_REP * tq
    ncmp = kc_ref.shape[0]
    n_slc = L // SLC_BLOCK
    last = N_BUCKETS - 1

    @pl.when(qi == 0)
    def _():
        sub = _iota((128, tq), 0)
        qo = _iota((128, tq), 1)
        d_diag = qo - sub
        c_rel = jnp.where(sub < 64, sub, sub - 128)
        d_cmp = qo - CMP_STRIDE * c_rel - (CMP_LEN - 1)
        for g in range(NSA_KV):
            for r in range(NSA_REP):
                h = g * NSA_REP + r
                tab = [rb_ref[b, h] - rb_ref[last, h] for b in range(N_BUCKETS)]
                lanes = slice(r * tq, (r + 1) * tq)
                tb_ref[g, TB_FAR, :, lanes] = jnp.zeros((128, tq), f32)
                tb_ref[g, TB_PREV, :, lanes] = _bias_chain(d_diag + 128, tab)
                tb_ref[g, TB_DIAG, :, lanes] = jnp.where(d_diag >= 0, _bias_chain(d_diag, tab), NEG)
                tb_ref[g, TB_WIN0, :, lanes] = jnp.where(d_diag <= 0, 0.0, NEG)
                tb_ref[g, TB_NONE, :, lanes] = jnp.full((128, tq), NEG, f32)
                t_c = jnp.where(d_cmp < 0, 0.0, _bias_chain(d_cmp, tab))
                tc_ref[g, 0:128, lanes] = t_c
                tc_ref[g, 128:256, lanes] = t_c
        ext_ref[...] = (_iota((L, 128), 0) // SLC_BLOCK == _iota((L, 128), 1)).astype(bf16)

    q0 = qi * tq
    qpos = q0 + _iota((1, nl), 1) % tq
    qpos1 = q0 + _iota((1, tq), 1)
    sub = _iota((128, 1), 0)
    gt = gt_ref[...]

    jr = _iota((n_slc, tq), 0)
    cur = qpos1 // SLC_BLOCK
    valid = jr * SLC_BLOCK <= qpos1
    forced = (jr == 0) | (jr == cur) | (jr == cur - 1)
    j_i = _iota((n_slc, ncmp), 0)
    c_i = _iota((n_slc, ncmp), 1)
    ovl = ((c_i * CMP_STRIDE < j_i * SLC_BLOCK + SLC_BLOCK) & (c_i * CMP_STRIDE + CMP_LEN > j_i * SLC_BLOCK)
           & (c_i < ncmp - 1)).astype(f32)
    cmp_off = pl.multiple_of((128 - qi * (tq // CMP_STRIDE)) % 128, 8)

    groups = range(NSA_KV)
    gsl = [slice(g * NSA_HD, (g + 1) * NSA_HD) for g in groups]
    qt = [jnp.concatenate([qt_ref[(g * NSA_REP + r) * NSA_HD:(g * NSA_REP + r + 1) * NSA_HD, :]
                           for r in range(NSA_REP)], axis=1).astype(bf16) for g in groups]

    ok_c = (sub * CMP_STRIDE + (CMP_LEN - 1) <= qpos) & (sub < ncmp - 1)
    o_cmp = []
    for g in groups:
        st = _dot(kc_ref[:, gsl[g]], qt[g]) + tc_ref[g, pl.ds(cmp_off, 128), :]
        st = jnp.where(ok_c, st, NEG)
        e = jnp.exp(st - jnp.max(st, axis=0, keepdims=True))
        p = jnp.where(ok_c, e / jnp.sum(e, axis=0, keepdims=True), 0.0)
        o_cmp.append(_dot(vct_ref[gsl[g], :], p))
        p_sum = p[:, 0:tq] + p[:, tq:2 * tq] + p[:, 2 * tq:3 * tq] + p[:, 3 * tq:4 * tq]
        imp = _dot_01l(ovl, p_sum)
        score = jnp.where(valid, jnp.where(forced, 1e9, imp), -1e9)
        rank = jnp.zeros((n_slc, tq), f32)
        for i in range(n_slc):
            si = score[i:i + 1, :]
            rank += ((si > score) | ((si == score) & (jr > i))).astype(f32)
        sel = (rank < SLC_TOPN).astype(bf16)
        sel = jnp.concatenate([sel, jnp.zeros((128 - n_slc, tq), bf16)], axis=0)
        selx = jnp.dot(ext_ref[...], sel, preferred_element_type=f32)
        selm_ref[g] = (selx - 1.0) * -NEG

    ones_rows = jnp.ones((16, 128), f32)

    def attend(state, tiles, k_ref, vt_ref):
        s, offs = {}, []
        for ti, (kt, table) in enumerate(tiles):
            ko = pl.multiple_of(kt * 128, 128)
            offs.append(ko)
            for g in groups:
                sg = _dot(k_ref[pl.ds(ko, 128), gsl[g]], qt[g]) + tb_ref[g, table]
                s[g, ti] = sg
        out = []
        for g in groups:
            m, acc = state[g]
            m_new = m
            for ti in range(len(tiles)):
                m_new = jnp.maximum(m_new, jnp.max(s[g, ti], axis=0, keepdims=True))
            acc = jnp.exp(m - m_new) * acc
            for ti in range(len(tiles)):
                vt = jnp.concatenate([vt_ref[gsl[g], pl.ds(offs[ti], 128)], ones_rows], axis=0)
                acc += _dot(vt, jnp.exp(s[g, ti] - m_new))
            out.append((m_new, acc))
        return tuple(out)

    finish = lambda st: [acc[0:NSA_HD] / acc[NSA_HD:NSA_HD + 1] for _, acc in st]
    init = tuple((jnp.full((1, nl), NEG, f32), jnp.zeros((NSA_HD + 16, nl), f32)) for _ in groups)

    def sel_table(kt):
        return jnp.where(kt == qi, TB_DIAG, jnp.where(kt == qi - 1, TB_PREV, TB_FAR))

    def score_tiles(kts, mx):
        mx = list(mx)
        for kt in kts:
            ko = pl.multiple_of(kt * 128, 128)
            for g in groups:
                sg = (_dot(ks_ref[pl.ds(ko, 128), gsl[g]], qt[g]) + tb_ref[g, sel_table(kt)]
                      + jnp.concatenate([selm_ref[g, pl.ds(ko, 128), :]] * NSA_REP, axis=1))
                s_ref[g, kt] = sg
                parts = [sg[8 * k:8 * k + 8, :] for k in range(16)]
                while len(parts) > 1:
                    parts = [jnp.maximum(a, b) for a, b in zip(parts[0::2], parts[1::2])]
                mx[g] = jnp.maximum(mx[g], parts[0])
        return tuple(mx)

    def value_tiles(kts, accs, m):
        accs = list(accs)
        for kt in kts:
            ko = pl.multiple_of(kt * 128, 128)
            for g in groups:
                vt = jnp.concatenate([vst_ref[gsl[g], pl.ds(ko, 128)], ones_rows], axis=0)
                accs[g] = accs[g] + _dot(vt, jnp.exp(s_ref[g, kt] - m[g]))
        return tuple(accs)

    n_pair, n_odd = (qi + 1) // 2, (qi + 1) % 2
    mx = tuple(jnp.full((8, nl), NEG, f32) for _ in groups)
    mx = lax.fori_loop(0, n_pair, lambda i, c: score_tiles([2 * i, 2 * i + 1], c), mx)
    mx = lax.fori_loop(0, n_odd, lambda i, c: score_tiles([qi], c), mx)
    m_sel = [jnp.max(x, axis=0, keepdims=True) for x in mx]
    accs = tuple(jnp.zeros((NSA_HD + 16, nl), f32) for _ in groups)
    accs = lax.fori_loop(0, n_pair, lambda i, c: value_tiles([2 * i, 2 * i + 1], c, m_sel), accs)
    accs = lax.fori_loop(0, n_odd, lambda i, c: value_tiles([qi], c, m_sel), accs)
    o_slc = [acc[0:NSA_HD] / acc[NSA_HD:NSA_HD + 1] for acc in accs]

    w0 = jnp.maximum(qi - WINDOW // 128, 0)

    def win_table(kt):
        t = jnp.where(kt == qi - WINDOW // 128, TB_WIN0, TB_FAR)
        t = jnp.where(kt == qi - 1, TB_PREV, t)
        return jnp.where(kt == qi, TB_DIAG, jnp.where(kt > qi, TB_NONE, t))

    win_tiles = [(w0 + t, win_table(w0 + t)) for t in range(WINDOW // 128 + 1)]
    o_win = finish(attend(init, win_tiles, kw_ref, vwt_ref))

    y_parts = []
    for g in groups:
        for r in range(NSA_REP):
            lanes = slice(r * tq, (r + 1) * tq)
            h = g * NSA_REP + r
            gate = lambda t: gt[SM_G + t * NSA_HEADS + h:SM_G + t * NSA_HEADS + h + 1, :]
            y_parts.append(gate(0) * o_cmp[g][:, lanes] + gate(1) * o_slc[g][:, lanes]
                           + gate(2) * o_win[g][:, lanes])
    y_ref[...] = jnp.concatenate(y_parts, axis=0).T.astype(y_ref.dtype)


def _nsa_prompt(qt, kv6_3, kv6t, kc, vct, gt, rel_bias):
    B, L, _ = kv6_3.shape
    tq = NSA_Q_TILE
    nb = L // CMP_STRIDE
    return pl.pallas_call(
        _nsa_prompt_kernel,
        grid=(B, L // tq),
        in_specs=[
            pl.BlockSpec(memory_space=pltpu.SMEM),
            pl.BlockSpec((None, D_NSA, tq), lambda b, i: (b, 0, i)),
            pl.BlockSpec((None, L, 128), lambda b, i: (b, 0, KV_SK // 128)),
            pl.BlockSpec((None, L, 128), lambda b, i: (b, 0, KV_WK // 128)),
            pl.BlockSpec((None, 128, L), lambda b, i: (b, KV_SV // 128, 0)),
            pl.BlockSpec((None, 128, L), lambda b, i: (b, KV_WV // 128, 0)),
            pl.BlockSpec((None, nb, 128), lambda b, i: (b, 0, 0)),
            pl.BlockSpec((None, 128, nb), lambda b, i: (b, 0, 0)),
            pl.BlockSpec((None, 128, tq), lambda b, i: (b, 0, i)),
        ],
        out_specs=pl.BlockSpec((None, tq, D_NSA), lambda b, i: (b, i, 0)),
        out_shape=jax.ShapeDtypeStruct((B, L, D_NSA), bf16),
        scratch_shapes=[
            pltpu.VMEM((NSA_KV, TB_KINDS, 128, NSA_REP * tq), f32),
            pltpu.VMEM((NSA_KV, 256, NSA_REP * tq), f32),
            pltpu.VMEM((L, 128), bf16),
            pltpu.VMEM((NSA_KV, L, tq), f32),
            pltpu.VMEM((NSA_KV, L // 128, 128, NSA_REP * tq), f32),
        ],
        compiler_params=_params(("parallel", "arbitrary")),
        name="nsa_prompt",
    )(rel_bias, qt, kv6_3, kv6_3, kv6t, kv6t, kc, vct, gt)


def _prep_t_kernel(q_ref, kv_ref, qg_ref, kg_ref, qt_ref, kv6_ref, kv6t_ref, gt_ref):
    for s in range(D_NSA // 128):
        sl = slice(s * 128, (s + 1) * 128)
        qt_ref[sl, :] = (_group_rms(q_ref[:, sl], NSA_HD) * qg_ref[...] * (NSA_HD ** -0.5)).T
    kv = kv_ref[...]
    for j, off in enumerate((KV_CK, KV_CV, KV_SK, KV_SV, KV_WK, KV_WV)):
        x = kv[:, off:off + 128]
        if off == KV_SK:
            x = _group_rms(x, NSA_HD) * kg_ref[1:2, :]
        if off == KV_WK:
            x = _group_rms(x, NSA_HD) * kg_ref[2:3, :]
        kv6_ref[:, off:off + 128] = x
        kv6t_ref[off:off + 128, :] = x.T
    gt_ref[...] = jax.nn.sigmoid(kv[:, SM_OFF - KV_OFF:SM_OFF - KV_OFF + 128]).T


def _prep_t(proj3, qgain, kgain):
    B, L, _ = proj3.shape
    tm = min(L, PREP_ROW_TILE)
    return pl.pallas_call(
        _prep_t_kernel,
        grid=(B, L // tm),
        in_specs=[
            pl.BlockSpec((None, tm, D_NSA), lambda b, i: (b, i, QN_OFF // D_NSA)),
            pl.BlockSpec((None, tm, 1024), lambda b, i: (b, i, KV_OFF // 1024)),
            pl.BlockSpec((1, 128), lambda b, i: (0, 0)),
            pl.BlockSpec((3, 128), lambda b, i: (0, 0)),
        ],
        out_specs=[pl.BlockSpec((None, D_NSA, tm), lambda b, i: (b, 0, i)),
                   pl.BlockSpec((None, tm, 768), lambda b, i: (b, i, 0)),
                   pl.BlockSpec((None, 768, tm), lambda b, i: (b, 0, i)),
                   pl.BlockSpec((None, 128, tm), lambda b, i: (b, 0, i))],
        out_shape=[jax.ShapeDtypeStruct((B, D_NSA, L), f32), jax.ShapeDtypeStruct((B, L, 768), f32),
                   jax.ShapeDtypeStruct((B, 768, L), f32), jax.ShapeDtypeStruct((B, 128, L), f32)],
        compiler_params=_params(("parallel", "parallel")),
        name="nsa_prep_t",
    )(proj3, proj3, qgain, kgain)


def _row0(x, rows=8):
    return jnp.where(_iota((rows, x.shape[1]), 0) == 0, x, 0.0)


def _ssd_decode_kernel(xbc_ref, z_ref, sm_ref, cprev_ref, sin_ref, cw_ref, cb_ref, dtb_ref, alog_ref, dexp_ref,
                       nrm_ref, y_ref, st_ref, cst_ref):
    x = xbc_ref[...]
    conv = cw_ref[3:4, :] * x
    for j in range(CONV_W - 1):
        conv += cw_ref[j:j + 1, :] * cprev_ref[j:j + 1, :]
    cst_ref[0:2, :] = cprev_ref[1:3, :]
    cst_ref[2:3, :] = x
    xc = _silu(conv + cb_ref[...])
    xs = xc[:, :D_SSD]
    dt = _softplus(sm_ref[...] + dtb_ref[...])
    da = dt * (-jnp.exp(alog_ref[...]))
    ys = []
    for g in range(SSD_GROUPS):
        bm = xc[:, D_SSD + g * SSD_STATE:D_SSD + (g + 1) * SSD_STATE]
        cm = xc[:, D_SSD + (SSD_GROUPS + g) * SSD_STATE:D_SSD + (SSD_GROUPS + g + 1) * SSD_STATE]
        cb = jnp.sum(cm * bm, axis=-1, keepdims=True)
        bm8, cm8 = _row0(bm), _row0(cm)
        for r in range(SSD_HEADS // SSD_GROUPS):
            h = g * (SSD_HEADS // SSD_GROUPS) + r
            xdt = xs[:, h * SSD_HD:(h + 1) * SSD_HD] * dt[:, h:h + 1]
            eda = jnp.exp(da[:, h:h + 1])
            h0 = sin_ref[h]
            ys.append(cb * xdt + eda * _dot_nt(cm8, h0, HI)[0:1])
            st_ref[h] = eda * h0 + _dot_tn(_row0(xdt), bm8, HI)
    y = (jnp.concatenate(ys, axis=1) + dexp_ref[...] * xs) * _silu(z_ref[...])
    gw = D_SSD // SSD_GROUPS
    for g in range(SSD_GROUPS):
        yg = y[:, g * gw:(g + 1) * gw]
        yg = yg * lax.rsqrt(jnp.mean(yg * yg, axis=-1, keepdims=True) + EPS)
        y_ref[:, g * gw:(g + 1) * gw] = yg * nrm_ref[:, g * gw:(g + 1) * gw]


def _ssd_decode(proj3, conv_state, state, layer, lp):
    B = proj3.shape[0]
    vec = lambda w: pl.BlockSpec((1, w), lambda b: (0, 0))
    return pl.pallas_call(
        _ssd_decode_kernel,
        grid=(B,),
        in_specs=[
            pl.BlockSpec((None, 1, SSD_CONV_CH), lambda b: (b, 0, XBC_OFF // SSD_CONV_CH)),
            pl.BlockSpec((None, 1, D_SSD), lambda b: (b, 0, ZS_OFF // D_SSD)),
            pl.BlockSpec((None, 1, 128), lambda b: (b, 0, SM_OFF // 128)),
            pl.BlockSpec((None, None, CONV_W - 1, SSD_CONV_CH), lambda b: (layer, b, 0, 0)),
            pl.BlockSpec((None, None, SSD_HEADS, SSD_HD, SSD_STATE), lambda b: (layer, b, 0, 0, 0)),
            pl.BlockSpec((CONV_W, SSD_CONV_CH), lambda b: (0, 0)),
            vec(SSD_CONV_CH), vec(128), vec(128), vec(D_SSD), vec(D_SSD),
        ],
        out_specs=[
            pl.BlockSpec((None, 1, D_SSD), lambda b: (b, 0, 0)),
            pl.BlockSpec((None, SSD_HEADS, SSD_HD, SSD_STATE), lambda b: (b, 0, 0, 0)),
            pl.BlockSpec((None, CONV_W - 1, SSD_CONV_CH), lambda b: (b, 0, 0)),
        ],
        out_shape=[
            jax.ShapeDtypeStruct((B, 1, D_SSD), f32),
            jax.ShapeDtypeStruct((B, SSD_HEADS, SSD_HD, SSD_STATE), f32),
            jax.ShapeDtypeStruct((B, CONV_W - 1, SSD_CONV_CH), f32),
        ],
        compiler_params=_params(("parallel",)),
        name="ssd_decode",
    )(proj3, proj3, proj3, conv_state, state, lp["ssd_conv_w"], lp["ssd_conv_b"], lp["ssd_dtb"], lp["ssd_alog"],
      lp["ssd_dexp"], lp["ssd_norm"])


def _gdn_decode_kernel(qkv_ref, z_ref, sm_ref, cprev_ref, sin_ref, cw_ref, dtb_ref, alog_ref, nrm_ref,
                       y_ref, st_ref, cst_ref):
    x = qkv_ref[...]
    conv = cw_ref[3:4, :] * x
    for j in range(CONV_W - 1):
        conv += cw_ref[j:j + 1, :] * cprev_ref[j:j + 1, :]
    cst_ref[0:2, :] = cprev_ref[1:3, :]
    cst_ref[2:3, :] = x
    xc = _silu(conv)
    sm = sm_ref[...]
    beta_all = jax.nn.sigmoid(sm)
    g_all = -jnp.exp(alog_ref[...]) * _softplus(sm + dtb_ref[...])
    for h in range(GDN_HEADS):
        sl = slice(h * GDN_HD, (h + 1) * GDN_HD)
        qh = xc[:, sl]
        kh = xc[:, D_GDN + h * GDN_HD:D_GDN + (h + 1) * GDN_HD]
        vh = xc[:, 2 * D_GDN + h * GDN_HD:2 * D_GDN + (h + 1) * GDN_HD]
        qh = qh * lax.rsqrt(jnp.sum(qh * qh, axis=-1, keepdims=True) + EPS) * (GDN_HD ** -0.5)
        kh = kh * lax.rsqrt(jnp.sum(kh * kh, axis=-1, keepdims=True) + EPS)
        beta = beta_all[:, SM_B + h:SM_B + h + 1]
        eg = jnp.exp(g_all[:, SM_A + h:SM_A + h + 1])
        s0 = sin_ref[h]
        k8 = _row0(kh)
        v_new = beta * vh - (beta * eg) * _dot(k8, s0, HI)[0:1]
        o = eg * _dot(_row0(qh), s0, HI)[0:1] + jnp.sum(qh * kh, axis=-1, keepdims=True) * v_new
        st_ref[h] = eg * s0 + _dot_tn(k8, _row0(v_new), HI)
        o = o * lax.rsqrt(jnp.mean(o * o, axis=-1, keepdims=True) + EPS)
        y_ref[:, sl] = o * nrm_ref[...] * _silu(z_ref[:, sl])


def _gdn_decode(proj3, conv_state, state, layer, lp):
    B = proj3.shape[0]
    vec = lambda w: pl.BlockSpec((1, w), lambda b: (0, 0))
    return pl.pallas_call(
        _gdn_decode_kernel,
        grid=(B,),
        in_specs=[
            pl.BlockSpec((None, 1, GDN_CONV_CH), lambda b: (b, 0, QKV_OFF // GDN_CONV_CH)),
            pl.BlockSpec((None, 1, D_GDN), lambda b: (b, 0, ZG_OFF // D_GDN)),
            pl.BlockSpec((None, 1, 128), lambda b: (b, 0, SM_OFF // 128)),
            pl.BlockSpec((None, None, CONV_W - 1, GDN_CONV_CH), lambda b: (layer, b, 0, 0)),
            pl.BlockSpec((None, None, GDN_HEADS, GDN_HD, GDN_HD), lambda b: (layer, b, 0, 0, 0)),
            pl.BlockSpec((CONV_W, GDN_CONV_CH), lambda b: (0, 0)),
            vec(128), vec(128), vec(GDN_HD),
        ],
        out_specs=[
            pl.BlockSpec((None, 1, D_GDN), lambda b: (b, 0, 0)),
            pl.BlockSpec((None, GDN_HEADS, GDN_HD, GDN_HD), lambda b: (b, 0, 0, 0)),
            pl.BlockSpec((None, CONV_W - 1, GDN_CONV_CH), lambda b: (b, 0, 0)),
        ],
        out_shape=[
            jax.ShapeDtypeStruct((B, 1, D_GDN), f32),
            jax.ShapeDtypeStruct((B, GDN_HEADS, GDN_HD, GDN_HD), f32),
            jax.ShapeDtypeStruct((B, CONV_W - 1, GDN_CONV_CH), f32),
        ],
        compiler_params=_params(("parallel",)),
        name="gdn_decode",
    )(proj3, proj3, proj3, conv_state, state, lp["gdn_conv_w"], lp["gdn_dtb"], lp["gdn_alog"], lp["gdn_norm"])


def _query_rows(q, g):
    rows = [q[:, (g * NSA_REP + r) * NSA_HD:(g * NSA_REP + r + 1) * NSA_HD] for r in range(NSA_REP)]
    return jnp.concatenate(rows + [jnp.zeros((8 - NSA_REP, NSA_HD), f32)], axis=0) * (NSA_HD ** -0.5)


def _masked_softmax(lg, ok):
    lg = jnp.where(ok, lg, NEG)
    e = jnp.exp(lg - jnp.max(lg, axis=-1, keepdims=True))
    return jnp.where(ok, e / jnp.sum(e, axis=-1, keepdims=True), 0.0)


def _head_lanes(parts):
    return jnp.concatenate([o[r:r + 1, :] for o in parts for r in range(NSA_REP)], axis=1)


def _nsa_dec_cmp_kernel(pt_ref, q_ref, pe_ref, w1_ref, w2_ref, kg_ref, tab_ref, pk_ref, pv_ref,
                        ocmp_ref, idx_ref, stage, rows, sem, *, layer, n_pages):
    b = pl.program_id(0)
    nb = pl.num_programs(0)
    n_past = n_pages * PAGE
    q_pos = n_past
    n_blk = n_past // CMP_STRIDE
    n_cmp = (n_past + 1 - CMP_LEN) // CMP_STRIDE + 1
    n_slc = -(-(n_past + 1) // SLC_BLOCK)
    nj = -(-n_slc // 128) * 128
    pools = (pk_ref, pv_ref)

    def page_copy(t, seq, p):
        return pltpu.make_async_copy(pools[t].at[layer, pt_ref[seq, p]], stage.at[t, p], sem.at[t])

    def start_all(t, seq):
        lax.fori_loop(0, n_pages, lambda p, c: (page_copy(t, seq, p).start(), c)[1], 0)

    def wait_all(t, seq):
        lax.fori_loop(0, n_pages, lambda p, c: (page_copy(t, seq, p).wait(), c)[1], 0)

    @pl.when(b == 0)
    def _():
        start_all(0, 0)
        start_all(1, 0)

    def to_rows(t):
        def body(p, c):
            rows[pl.ds(pl.multiple_of(p * PAGE, PAGE), PAGE), :] = stage[t, p].T
            return c
        lax.fori_loop(0, n_pages, body, 0, unroll=4)

    cmps = []
    for t in range(2):
        wait_all(t, b)
        to_rows(t)

        @pl.when(b + 1 < nb)
        def _():
            start_all(t, b + 1)

        p1 = jnp.zeros((n_blk, 2 * CMP_HID), f32)
        p2 = jnp.zeros((n_blk, 2 * CMP_HID), f32)
        for j in range(CMP_STRIDE // 2):
            xr = jnp.concatenate([rows[pl.ds(2 * j, n_blk, stride=CMP_STRIDE), :],
                                  rows[pl.ds(2 * j + 1, n_blk, stride=CMP_STRIDE), :]], axis=1)
            j2 = CMP_STRIDE // 2 + j
            p1 += jnp.dot((xr + pe_ref[t, j:j + 1, :]).astype(bf16), w1_ref[t, j], preferred_element_type=f32)
            p2 += jnp.dot((xr + pe_ref[t, j2:j2 + 1, :]).astype(bf16), w1_ref[t, j2], preferred_element_type=f32)
        hid = p1 + pltpu.roll(p2, n_blk - 1, 0)
        c = jnp.dot(_silu(hid).astype(bf16), w2_ref[t], preferred_element_type=f32)
        if t == 0:
            c = _group_rms(c, NSA_HD) * kg_ref[0:1, :]
        cmps.append(c)
    kc, vc = cmps

    q = q_ref[...]
    lane = _iota((8, n_blk), 1)
    cend = lane * CMP_STRIDE + (CMP_LEN - 1)
    ok = (cend <= q_pos) & (lane < n_cmp)
    c_i = _iota((n_blk, nj), 0)
    j_i = _iota((n_blk, nj), 1)
    ovl = ((c_i * CMP_STRIDE < j_i * SLC_BLOCK + SLC_BLOCK) & (c_i * CMP_STRIDE + CMP_LEN > j_i * SLC_BLOCK)
           & (j_i < n_slc) & (c_i < n_cmp)).astype(f32)
    jl = _iota((1, nj), 1)
    cur = q_pos // SLC_BLOCK
    valid = jl * SLC_BLOCK <= q_pos
    forced = (jl == 0) | (jl == cur) | (jl == cur - 1)
    ii = _iota((nj, nj), 0)
    jj = _iota((nj, nj), 1)
    kk = _iota((SLC_TOPN, nj), 0)
    j16 = _iota((SLC_TOPN, nj), 1)
    o_parts, idx_cols = [], []
    for g in range(NSA_KV):
        tabcols = [tab_ref[g, :, bk:bk + 1] for bk in range(N_BUCKETS)]
        lg = _dot_nt(_query_rows(q, g), kc[:, g * NSA_HD:(g + 1) * NSA_HD]) + _bias_chain(q_pos - cend, tabcols)
        p = _masked_softmax(lg, ok)
        o_parts.append(_dot(p, vc[:, g * NSA_HD:(g + 1) * NSA_HD]))
        p_sum = p[0:1] + p[1:2] + p[2:3] + p[3:4]
        imp = _dot_01(_row0(p_sum), ovl)[0:1]
        score = jnp.where(valid, jnp.where(forced, 1e9, imp), -1e9)
        score = jnp.where(jl < n_slc, score, -3e9)
        s_c = jnp.sum(jnp.where(ii == jj, score, 0.0), axis=1, keepdims=True)
        rank_r = jnp.sum(((s_c > score) | ((s_c == score) & (ii < jj))).astype(f32), axis=0, keepdims=True)
        rank_c = jnp.sum(((score > s_c) | ((score == s_c) & (jj < ii))).astype(f32), axis=1, keepdims=True)
        sel_r = (rank_r < SLC_TOPN) & (jl < n_slc)
        sel_c = (rank_c < SLC_TOPN) & (ii[:, 0:1] < n_slc)
        pos_r = jnp.sum((sel_c & (ii < jj)).astype(f32), axis=0, keepdims=True)
        hit = (pos_r == kk.astype(f32)) & sel_r
        idx_cols.append(jnp.sum(jnp.where(hit, j16.astype(f32), 0.0), axis=1, keepdims=True))
    ocmp_ref[...] = _head_lanes(o_parts)
    l16 = _iota((SLC_TOPN, 128), 1)
    idx_ref[...] = (jnp.where(l16 == 0, idx_cols[0], 0.0) + jnp.where(l16 == 1, idx_cols[1], 0.0)).astype(i32)


def _nsa_dec_cmp(qn3, page_table, pool_k, pool_v, layer, lp):
    B = qn3.shape[0]
    n_pages = page_table.shape[1]
    full = lambda s: pl.BlockSpec(s, lambda b, pt: (0,) * len(s))
    return pl.pallas_call(
        functools.partial(_nsa_dec_cmp_kernel, layer=layer, n_pages=n_pages),
        grid_spec=pltpu.PrefetchScalarGridSpec(
            num_scalar_prefetch=1,
            grid=(B,),
            in_specs=[
                pl.BlockSpec((None, 1, D_NSA), lambda b, pt: (b, 0, 0)),
                full((2, CMP_LEN // 2, 256)), full((2, CMP_LEN // 2, 256, 2 * CMP_HID)),
                full((2, 2 * CMP_HID, 128)), full((3, 128)), full((NSA_KV, 8, N_BUCKETS)),
                pl.BlockSpec(memory_space=pl.ANY), pl.BlockSpec(memory_space=pl.ANY),
            ],
            out_specs=[pl.BlockSpec((None, 1, D_NSA), lambda b, pt: (b, 0, 0)),
                       pl.BlockSpec((None, SLC_TOPN, 128), lambda b, pt: (b, 0, 0))],
            scratch_shapes=[pltpu.VMEM((2, n_pages, NSA_KV * NSA_HD, PAGE), f32),
                            pltpu.VMEM((n_pages * PAGE, NSA_KV * NSA_HD), f32),
                            pltpu.SemaphoreType.DMA((2,))],
        ),
        out_shape=[jax.ShapeDtypeStruct((B, 1, D_NSA), f32), jax.ShapeDtypeStruct((B, SLC_TOPN, 128), i32)],
        compiler_params=_params(("arbitrary",)),
        name="nsa_dec_cmp",
    )(page_table, qn3, lp["cmp_pe2"], lp["cmp_w1bd"], lp["cmp_w2bd"], lp["k_gain"], lp["tab8"], pool_k, pool_v)


def _nsa_dec_sel_kernel(pt_ref, idx_ref, q_ref, kv_ref, sm_ref, ocmp_ref, tab_ref, wk_ref, wv_ref, pk_ref, pv_ref,
                        y_ref, wko_ref, wvo_ref, kg, vg, sem, *, layer, n_pages):
    b = pl.program_id(0)
    n_past = n_pages * PAGE
    q_pos = n_past
    n_buf = wk_ref.shape[1]
    new_blk = n_past // SLC_BLOCK
    per_page = PAGE // SLC_BLOCK

    def blk_copy(pool, buf, g, s, sm_i):
        j = jnp.minimum(idx_ref[b, s, g], new_blk - 1)
        src = pool.at[layer, pt_ref[b, j // per_page], pl.ds(g * NSA_HD, NSA_HD)]
        return pltpu.make_async_copy(src, buf.at[g, :, pl.ds(s * PAGE, PAGE)], sem.at[sm_i])

    for g in range(NSA_KV):
        for s in range(SLC_TOPN):
            blk_copy(pk_ref, kg, g, s, 0).start()
            blk_copy(pv_ref, vg, g, s, 1).start()

    kv = kv_ref[...]
    q = q_ref[...]
    gate = jax.nn.sigmoid(sm_ref[...])
    ocmp = ocmp_ref[...]

    ii, jj = _iota((128, 128), 0), _iota((128, 128), 1)
    column = lambda off: jnp.sum(jnp.where(ii == jj, kv[:, off:off + 128], 0.0), axis=1, keepdims=True)
    last_lane = _iota((128, n_buf), 1) == n_buf - 1
    wk, wv = wk_ref[...], wv_ref[...]
    wko_ref[...] = jnp.where(last_lane, column(KV_WK), pltpu.roll(wk, n_buf - 1, 1))
    wvo_ref[...] = jnp.where(last_lane, column(KV_WV), pltpu.roll(wv, n_buf - 1, 1))

    def attend(qg, kt, vt, bias, ok, k_new, v_new, bias_new, ok_new):
        lg = jnp.where(ok, _dot(qg, kt) + bias, NEG)
        lg_new = jnp.where(ok_new, jnp.sum(qg * k_new, axis=-1, keepdims=True) + bias_new, NEG)
        m = jnp.maximum(jnp.max(lg, axis=-1, keepdims=True), lg_new)
        e = jnp.where(ok, jnp.exp(lg - m), 0.0)
        e_new = jnp.where(ok_new, jnp.exp(lg_new - m), 0.0)
        den = jnp.sum(e, axis=-1, keepdims=True) + e_new
        p, p_new = e / den, e_new / den
        return _dot_nt(p, vt) + p_new * v_new

    for g in range(NSA_KV):
        for s in range(SLC_TOPN):
            blk_copy(pk_ref, kg, g, s, 0).wait()
            blk_copy(pv_ref, vg, g, s, 1).wait()

    lw = _iota((8, n_buf), 1)
    d_w = q_pos - (n_past - n_buf + lw)
    ok_w = (d_w >= 0) & (d_w <= WINDOW)
    ls = _iota((8, SLC_TOPN * PAGE), 1)
    slot = ls // PAGE
    zero = jnp.zeros((8, 1), i32)
    o_slc, o_win = [], []
    for g in range(NSA_KV):
        tabcols = [tab_ref[g, :, bk:bk + 1] for bk in range(N_BUCKETS)]
        qg = _query_rows(q, g)
        gs = slice(g * NSA_HD, (g + 1) * NSA_HD)
        b_new = _bias_chain(zero, tabcols)
        blk = jnp.zeros(ls.shape, i32)
        n_new = jnp.zeros((8, 1), i32)
        for s in range(SLC_TOPN):
            j = idx_ref[b, s, g]
            blk = jnp.where(slot == s, j, blk)
            n_new = jnp.where(j == new_blk, n_new + 1, n_new)
        k_pos = (jnp.minimum(blk, new_blk - 1) // per_page) * PAGE + ls % PAGE
        ok_s = (blk < new_blk) & (k_pos // SLC_BLOCK == blk)
        o_slc.append(attend(qg, kg[g], vg[g], _bias_chain(q_pos - k_pos, tabcols), ok_s,
                            kv[:, KV_SK + g * NSA_HD:KV_SK + (g + 1) * NSA_HD],
                            kv[:, KV_SV + g * NSA_HD:KV_SV + (g + 1) * NSA_HD], b_new, n_new > 0))
        o_win.append(attend(qg, wk[gs, :], wv[gs, :], _bias_chain(d_w, tabcols), ok_w,
                            kv[:, KV_WK + g * NSA_HD:KV_WK + (g + 1) * NSA_HD],
                            kv[:, KV_WV + g * NSA_HD:KV_WV + (g + 1) * NSA_HD], b_new, zero == 0))
    gl = lambda t: gate[:, SM_G + t * NSA_HEADS:SM_G + (t + 1) * NSA_HEADS]
    wide = lambda gt: jnp.concatenate([jnp.broadcast_to(gt[:, h:h + 1], (1, NSA_HD)) for h in range(NSA_HEADS)], axis=1)
    y_ref[...] = wide(gl(0)) * ocmp + wide(gl(1)) * _head_lanes(o_slc) + wide(gl(2)) * _head_lanes(o_win)


def _nsa_dec_sel(qn3, kv6_3, proj3, ocmp, idx, page_table, win_k, win_v, pool_k, pool_v, layer, lp):
    B = qn3.shape[0]
    n_pages = page_table.shape[1]
    n_buf = win_k.shape[3]
    full = lambda s: pl.BlockSpec(s, lambda b, pt, ix: (0,) * len(s))
    row = lambda w, j: pl.BlockSpec((None, 1, w), lambda b, pt, ix: (b, 0, j))
    win = pl.BlockSpec((None, None, 128, n_buf), lambda b, pt, ix: (layer, b, 0, 0))
    return pl.pallas_call(
        functools.partial(_nsa_dec_sel_kernel, layer=layer, n_pages=n_pages),
        grid_spec=pltpu.PrefetchScalarGridSpec(
            num_scalar_prefetch=2,
            grid=(B,),
            in_specs=[row(D_NSA, 0), row(768, 0), row(128, SM_OFF // 128), row(D_NSA, 0),
                      full((NSA_KV, 8, N_BUCKETS)), win, win,
                      pl.BlockSpec(memory_space=pl.ANY), pl.BlockSpec(memory_space=pl.ANY)],
            out_specs=[row(D_NSA, 0),
                       pl.BlockSpec((None, 128, n_buf), lambda b, pt, ix: (b, 0, 0)),
                       pl.BlockSpec((None, 128, n_buf), lambda b, pt, ix: (b, 0, 0))],
            scratch_shapes=[pltpu.VMEM((NSA_KV, NSA_HD, SLC_TOPN * PAGE), f32),
                            pltpu.VMEM((NSA_KV, NSA_HD, SLC_TOPN * PAGE), f32),
                            pltpu.SemaphoreType.DMA((2,))],
        ),
        out_shape=[jax.ShapeDtypeStruct((B, 1, D_NSA), f32),
                   jax.ShapeDtypeStruct((B, 128, n_buf), f32), jax.ShapeDtypeStruct((B, 128, n_buf), f32)],
        compiler_params=_params(("arbitrary",)),
        name="nsa_dec_sel",
    )(page_table, idx, qn3, kv6_3, proj3, ocmp, lp["tab8"], win_k, win_v, pool_k, pool_v)


def _layer_params(i, p):
    pad128 = lambda v, off: jnp.zeros((1, 128), f32).at[0, off:off + v.shape[0]].set(v)
    return dict(
        ssd_conv_w=p["ssd_conv_w"][i], ssd_conv_b=p["ssd_conv_b"][i][None],
        ssd_dtb=pad128(p["ssd_dt_bias"][i], SM_DT), ssd_alog=pad128(p["ssd_a_log"][i], SM_DT),
        ssd_dexp=jnp.repeat(p["ssd_d"][i], SSD_HD)[None], ssd_norm=p["ssd_norm"][i][None],
        gdn_conv_w=p["gdn_conv_w"][i],
        gdn_dtb=pad128(p["gdn_dt_bias"][i], SM_A), gdn_alog=pad128(p["gdn_a_log"][i], SM_A),
        gdn_norm=p["gdn_norm"][i][None],
        q_gain=jnp.tile(p["nsa_q_norm"][i], 2)[None], k_gain=jnp.tile(p["nsa_k_norm"][i], (1, 2)),
        cmp_pe=p["nsa_cmp_pe"][i], cmp_w1=p["nsa_cmp_w1"][i], cmp_w2=p["nsa_cmp_w2"][i],
        cmp_pe2=jnp.tile(p["nsa_cmp_pe"][i], (1, 1, 2)).reshape(2, CMP_LEN // 2, 256),
        cmp_w1bd=_block_diag2(p["nsa_cmp_w1"][i].reshape(2, CMP_LEN, NSA_HD, CMP_HID)).astype(bf16)
        .reshape(2, CMP_LEN // 2, 256, 2 * CMP_HID),
        cmp_w2bd=_block_diag2(p["nsa_cmp_w2"][i]).astype(bf16),
        tab8=jnp.pad(p["rel_bias"].T.reshape(NSA_KV, NSA_REP, N_BUCKETS), ((0, 0), (0, 8 - NSA_REP), (0, 0))),
    )


def _block_diag2(w):
    z = jnp.zeros_like(w)
    return jnp.concatenate([jnp.concatenate([w, z], axis=-1), jnp.concatenate([z, w], axis=-1)], axis=-2)


def _reorder_w_in(w_in):
    cuts = [(1024, 2560), (2576, 4112), (0, 1024), (4112, 4624), (4632, 5144), (5144, 5912),
            (2560, 2576), (4624, 4632), (5912, 5936)]
    wt = jnp.transpose(w_in, (0, 2, 1))
    parts = [wt[:, a:b] for a, b in cuts]
    used = sum(b - a for a, b in cuts)
    parts.append(jnp.zeros((w_in.shape[0], PROJ_W - used, w_in.shape[1]), w_in.dtype))
    return jnp.concatenate(parts, axis=1).astype(bf16)


def _mix_prompt(x3, i, p, lp, w_r):
    B, L, _ = x3.shape
    n = B * L
    x = x3.reshape(n, D_MODEL)
    proj = _inproj(x, p["norm_mix"], w_r, i)
    proj3 = proj.reshape(B, L, PROJ_W)
    y_ssd, st_ssd, cst_ssd = _ssd_prompt(proj3, lp)
    y_gdn, st_gdn, cst_gdn = _gdn_prompt(proj3, lp)
    qt, kv6_3, kv6t, gt = _prep_t(proj3, lp["q_gain"], lp["k_gain"])
    kc, vct = _cmp_prompt(kv6_3, lp)
    y_nsa = _nsa_prompt(qt, kv6_3, kv6t, kc, vct, gt, p["rel_bias"])
    x = _outproj(x, y_ssd.reshape(n, D_SSD), y_gdn.reshape(n, D_GDN), y_nsa.reshape(n, D_NSA), p["w_out"], i)
    rows = lambda off: jnp.transpose(kv6t[:, off:off + 128].reshape(B, NSA_KV, NSA_HD, L), (0, 3, 1, 2))
    keep = min(WINDOW, L)
    state = (st_ssd, cst_ssd, st_gdn, cst_gdn, rows(KV_CK), rows(KV_CV), rows(KV_SK), rows(KV_SV),
             rows(KV_WK)[:, L - keep:], rows(KV_WV)[:, L - keep:])
    return x.reshape(B, L, D_MODEL), state


def _mix_decode(x, i, p, lp, w_r, st):
    B = x.shape[0]
    proj = _inproj(x, p["norm_mix"], w_r, i)
    proj3 = proj.reshape(B, 1, PROJ_W)
    y_ssd, st_ssd, cst_ssd = _ssd_decode(proj3, st["ssd_conv"], st["ssd"], i, lp)
    y_gdn, st_gdn, cst_gdn = _gdn_decode(proj3, st["gdn_conv"], st["gdn"], i, lp)
    qn, kv6 = _prep(proj, lp["q_gain"], lp["k_gain"])
    qn3, kv6_3 = qn.reshape(B, 1, D_NSA), kv6.reshape(B, 1, 768)
    o_cmp, idx = _nsa_dec_cmp(qn3, st["page_table"], st["cmp_k"], st["cmp_v"], i, lp)
    y_nsa, win_k, win_v = _nsa_dec_sel(qn3, kv6_3, proj3, o_cmp, idx[:, :, :NSA_KV], st["page_table"],
                                       st["win_k"], st["win_v"], st["slc_k"], st["slc_v"], i, lp)
    x = _outproj(x, y_ssd.reshape(B, D_SSD), y_gdn.reshape(B, D_GDN), y_nsa.reshape(B, D_NSA), p["w_out"], i)
    rows = lambda off: kv6[:, off:off + 128].reshape(B, 1, NSA_KV, NSA_HD)
    n_buf = win_k.shape[2]
    buf = lambda w: jnp.transpose(w.reshape(B, NSA_KV, NSA_HD, n_buf), (0, 3, 1, 2))
    state = (st_ssd, cst_ssd, st_gdn, cst_gdn, rows(KV_CK), rows(KV_CV), rows(KV_SK), rows(KV_SV),
             buf(win_k), buf(win_v))
    return x, state


def kernel(x_prompt, x_sample, state_ssd, state_ssd_conv, state_gdn, state_gdn_conv, cache_cmp_k, cache_cmp_v,
           cache_slc_k, cache_slc_v, cache_win_k, cache_win_v, page_table, rel_bias, norm_ffn1, w_ffn1_gate,
           w_ffn1_up, w_ffn1_down, norm_mix, w_in, ssd_conv_w, ssd_conv_b, ssd_dt_bias, ssd_a_log, ssd_d, ssd_norm,
           gdn_conv_w, gdn_dt_bias, gdn_a_log, gdn_norm, nsa_q_norm, nsa_k_norm, nsa_cmp_pe, nsa_cmp_w1,
           nsa_cmp_w2, w_out, norm_ffn2, w_ffn2_gate, w_ffn2_up, w_ffn2_down):
    bp, lp_len, _ = x_prompt.shape
    bs = x_sample.shape[0]
    gain3 = lambda g: g.reshape(DEPTH, 1, D_MODEL)
    p = dict(rel_bias=rel_bias, norm_mix=gain3(norm_mix), w_out=w_out.astype(bf16), ssd_conv_w=ssd_conv_w,
             ssd_conv_b=ssd_conv_b,
             ssd_dt_bias=ssd_dt_bias, ssd_a_log=ssd_a_log, ssd_d=ssd_d, ssd_norm=ssd_norm, gdn_conv_w=gdn_conv_w,
             gdn_dt_bias=gdn_dt_bias, gdn_a_log=gdn_a_log, gdn_norm=gdn_norm, nsa_q_norm=nsa_q_norm,
             nsa_k_norm=nsa_k_norm, nsa_cmp_pe=nsa_cmp_pe, nsa_cmp_w1=nsa_cmp_w1, nsa_cmp_w2=nsa_cmp_w2)
    n1, n2 = gain3(norm_ffn1), gain3(norm_ffn2)
    chan_row = lambda c: jnp.transpose(c, (0, 1, 3, 4, 2)).reshape(c.shape[:2] + (NSA_KV * NSA_HD, c.shape[2]))
    st = dict(ssd=state_ssd, ssd_conv=state_ssd_conv, gdn=state_gdn, gdn_conv=state_gdn_conv,
              cmp_k=chan_row(cache_cmp_k), cmp_v=chan_row(cache_cmp_v), slc_k=chan_row(cache_slc_k),
              slc_v=chan_row(cache_slc_v), win_k=chan_row(cache_win_k), win_v=chan_row(cache_win_v),
              page_table=page_table)
    w_r = _reorder_w_in(w_in)

    hp = x_prompt.reshape(bp * lp_len, D_MODEL)
    hs = x_sample.reshape(bs, D_MODEL)
    outs_p, outs_s = [], []
    for i in range(DEPTH):
        lp = _layer_params(i, p)
        hp, hs = _ffn(hp, hs, n1, w_ffn1_gate, w_ffn1_up, w_ffn1_down, i)
        hp3, st_p = _mix_prompt(hp.reshape(bp, lp_len, D_MODEL), i, p, lp, w_r)
        hs, st_s = _mix_decode(hs, i, p, lp, w_r, st)
        hp, hs = _ffn(hp3.reshape(bp * lp_len, D_MODEL), hs, n2, w_ffn2_gate, w_ffn2_up, w_ffn2_down, i)
        outs_p.append(st_p)
        outs_s.append(st_s)
    stack = lambda outs: [jnp.stack(t) for t in zip(*outs)]
    return (hp.reshape(bp, lp_len, D_MODEL), hs.reshape(bs, 1, D_MODEL), *stack(outs_p), *stack(outs_s))
```

```python
import functools
import math

import numpy as np
import jax
import jax.numpy as jnp
from jax import lax
from jax.experimental import pallas as pl
from jax.experimental.pallas import tpu as pltpu

f32 = jnp.float32
bf16 = jnp.bfloat16
i32 = jnp.int32
HI = lax.Precision.HIGHEST

D_MODEL = 2048
DEPTH = 4
PAGE = 128
D_SSD = 1024
SSD_HD = 64
SSD_HEADS = 16
SSD_GROUPS = 2
SSD_STATE = 128
SSD_CHUNK = 128
D_GDN = 512
GDN_HD = 128
GDN_HEADS = 4
GDN_CHUNK = 64
GDN_STEP_CHUNKS = 4
D_NSA = 512
NSA_HD = 64
NSA_HEADS = 8
NSA_KV = 2
NSA_REP = 4
CMP_STRIDE = 16
CMP_LEN = 32
CMP_HID = 128
SLC_BLOCK = 64
SLC_TOPN = 16
WINDOW = 512
N_BUCKETS = 32
MAX_DISTANCE = 128
CONV_W = 4
D_FF = 5632
SSD_CONV_CH = 1536
GDN_CONV_CH = 1536
EPS = 1e-6
NEG = -1e30

XBC_OFF = 0
QKV_OFF = 1536
ZS_OFF = 3072
ZG_OFF = 4096
QN_OFF = 4608
KV_OFF = 5120
SM_OFF = 5888
PROJ_W = 6144
SM_DT, SM_A, SM_B, SM_G = 0, 16, 20, 24
KV_CK, KV_CV, KV_SK, KV_SV, KV_WK, KV_WV = 0, 128, 256, 384, 512, 640

VMEM_LIMIT = 56 * 1024 * 1024
ROW_TILE = 1024
FFN_COL_TILE = 256
PROJ_COL_TILE = 512
OUT_ROW_TILE = 2048
PREP_ROW_TILE = 512
NSA_Q_TILE = 128
TB_FAR, TB_PREV, TB_DIAG, TB_WIN0, TB_NONE, TB_KINDS = 0, 1, 2, 3, 4, 5


def _bucket_thresholds():
    exact = N_BUCKETS // 2
    d = np.arange(0, 4 * MAX_DISTANCE)
    nf = np.maximum(d, 1).astype(np.float32)
    large = exact + (np.log(nf / np.float32(exact)) / np.float32(math.log(MAX_DISTANCE / exact))
                     * np.float32(N_BUCKETS - exact)).astype(np.int32)
    bucket = np.where(d < exact, d, np.minimum(large, N_BUCKETS - 1))
    return [int(np.argmax(bucket >= b)) for b in range(N_BUCKETS)]


_THR = _bucket_thresholds()
BIAS_FAR = _THR[N_BUCKETS - 1]


def _bias_chain(d, tabcols):
    v = jnp.broadcast_to(tabcols[N_BUCKETS - 1], d.shape)
    for b in range(N_BUCKETS - 2, -1, -1):
        v = jnp.where(d < _THR[b + 1], tabcols[b], v)
    return v


def _silu(x):
    return x * jax.nn.sigmoid(x)


def _softplus(x):
    return jnp.maximum(x, 0.0) + jnp.log1p(jnp.exp(-jnp.abs(x)))


def _dot(a, b, prec=None):
    if prec is None:
        a, b = a.astype(bf16), b.astype(bf16)
    return jnp.dot(a, b, preferred_element_type=f32, precision=prec)


def _dot_nt(a, b, prec=None):
    if prec is None:
        a, b = a.astype(bf16), b.astype(bf16)
    return lax.dot_general(a, b, (((1,), (1,)), ((), ())), preferred_element_type=f32, precision=prec)


def _dot_tn(a, b, prec=None):
    if prec is None:
        a, b = a.astype(bf16), b.astype(bf16)
    return lax.dot_general(a, b, (((0,), (0,)), ((), ())), preferred_element_type=f32, precision=prec)


def _split2(a):
    hi = a.astype(bf16)
    return hi, (a - hi.astype(f32)).astype(bf16)


def _dot3(a2, b2):
    d = lambda x, y: jnp.dot(x, y, preferred_element_type=f32)
    return d(a2[0], b2[0]) + (d(a2[0], b2[1]) + d(a2[1], b2[0]))


def _split3(a):
    a1 = a.astype(bf16)
    r = a - a1.astype(f32)
    a2 = r.astype(bf16)
    return a1, a2, (r - a2.astype(f32)).astype(bf16)


def _dot_01(a, m01):
    m = m01.astype(bf16)
    return sum(jnp.dot(t, m, preferred_element_type=f32) for t in _split3(a))


def _dot_01l(m01, b):
    m = m01.astype(bf16)
    return sum(jnp.dot(m, t, preferred_element_type=f32) for t in _split3(b))


def _iota(shape, dim):
    return lax.broadcasted_iota(i32, shape, dim)


def _group_ones(n, width):
    return (_iota((n, n), 0) // width == _iota((n, n), 1) // width).astype(f32)


def _group_rms(x, width):
    ss = _dot_01(x * x, _group_ones(x.shape[1], width))
    return x * lax.rsqrt(ss * (1.0 / width) + EPS)


def _params(sem):
    return pltpu.CompilerParams(dimension_semantics=sem, vmem_limit_bytes=VMEM_LIMIT)


FFN_EXTRA = 16


def _ffn_kernel(x_ref, xs_ref, g_ref, wg_ref, wu_ref, wd_ref, o_ref, os_ref, h_ref):
    tm, ns = x_ref.shape[0], xs_ref.shape[0]

    @pl.when(pl.program_id(1) == 0)
    def _():
        rms = lambda x: x * lax.rsqrt(jnp.mean(x * x, axis=-1, keepdims=True) + EPS) * g_ref[...]
        x, xs = x_ref[...], xs_ref[...]
        h_ref[0:tm, :] = rms(x).astype(bf16)
        h_ref[tm:tm + FFN_EXTRA, :] = jnp.concatenate(
            [rms(xs), jnp.zeros((FFN_EXTRA - ns, D_MODEL), f32)], axis=0).astype(bf16)
        o_ref[...] = x
        os_ref[...] = xs

    h = h_ref[...]
    a = jnp.dot(h, wg_ref[...].astype(bf16), preferred_element_type=f32)
    u = jnp.dot(h, wu_ref[...].astype(bf16), preferred_element_type=f32)
    y = 0.5 * _dot(_silu(a) * u, wd_ref[...])
    o_ref[...] += y[0:tm]
    os_ref[...] += y[tm:tm + ns]


def _ffn(x, xs, gain, wg, wu, wd, layer):
    n, ns = x.shape[0], xs.shape[0]
    tm, tf = ROW_TILE, FFN_COL_TILE
    return pl.pallas_call(
        _ffn_kernel,
        grid=(n // tm, D_FF // tf),
        in_specs=[
            pl.BlockSpec((tm, D_MODEL), lambda i, j: (i, 0)),
            pl.BlockSpec((ns, D_MODEL), lambda i, j: (0, 0)),
            pl.BlockSpec((None, 1, D_MODEL), lambda i, j: (layer, 0, 0)),
            pl.BlockSpec((None, D_MODEL, tf), lambda i, j: (layer, 0, j)),
            pl.BlockSpec((None, D_MODEL, tf), lambda i, j: (layer, 0, j)),
            pl.BlockSpec((None, tf, D_MODEL), lambda i, j: (layer, j, 0)),
        ],
        out_specs=[pl.BlockSpec((tm, D_MODEL), lambda i, j: (i, 0)),
                   pl.BlockSpec((ns, D_MODEL), lambda i, j: (0, 0))],
        out_shape=[jax.ShapeDtypeStruct((n, D_MODEL), f32), jax.ShapeDtypeStruct((ns, D_MODEL), f32)],
        scratch_shapes=[pltpu.VMEM((tm + FFN_EXTRA, D_MODEL), bf16)],
        compiler_params=_params(("arbitrary", "arbitrary")),
        name="ffn",
    )(x, xs, gain, wg, wu, wd)


def _inproj_kernel(x_ref, g_ref, w_ref, o_ref, h_ref):
    @pl.when(pl.program_id(1) == 0)
    def _():
        x = x_ref[...]
        h = x * lax.rsqrt(jnp.mean(x * x, axis=-1, keepdims=True) + EPS) * g_ref[...]
        h_ref[...] = h.astype(bf16)

    o_ref[...] = _dot_nt(h_ref[...], w_ref[...])


def _inproj(x, gain, w_r, layer):
    n = x.shape[0]
    tm = min(n, ROW_TILE)
    tn = PROJ_COL_TILE
    return pl.pallas_call(
        _inproj_kernel,
        grid=(n // tm, PROJ_W // tn),
        in_specs=[
            pl.BlockSpec((tm, D_MODEL), lambda i, j: (i, 0)),
            pl.BlockSpec((None, 1, D_MODEL), lambda i, j: (layer, 0, 0)),
            pl.BlockSpec((None, tn, D_MODEL), lambda i, j: (layer, j, 0)),
        ],
        out_specs=pl.BlockSpec((tm, tn), lambda i, j: (i, j)),
        out_shape=jax.ShapeDtypeStruct((n, PROJ_W), f32),
        scratch_shapes=[pltpu.VMEM((tm, D_MODEL), bf16)],
        compiler_params=_params(("parallel", "arbitrary")),
        name="inproj",
    )(x, gain, w_r)


def _outproj_kernel(x_ref, ys_ref, yg_ref, yn_ref, ws_ref, wg_ref, wn_ref, o_ref):
    acc = _dot(ys_ref[...], ws_ref[...])
    acc += _dot(yg_ref[...], wg_ref[...])
    acc += _dot(yn_ref[...], wn_ref[...])
    o_ref[...] = x_ref[...] + acc


def _outproj(x, ys, yg, yn, w_out, layer):
    n = x.shape[0]
    tm = min(n, OUT_ROW_TILE)
    tn = PROJ_COL_TILE
    return pl.pallas_call(
        _outproj_kernel,
        grid=(n // tm, D_MODEL // tn),
        in_specs=[
            pl.BlockSpec((tm, tn), lambda i, j: (i, j)),
            pl.BlockSpec((tm, D_SSD), lambda i, j: (i, 0)),
            pl.BlockSpec((tm, D_GDN), lambda i, j: (i, 0)),
            pl.BlockSpec((tm, D_NSA), lambda i, j: (i, 0)),
            pl.BlockSpec((None, D_SSD, tn), lambda i, j: (layer, 0, j)),
            pl.BlockSpec((None, D_GDN, tn), lambda i, j: (layer, 2, j)),
            pl.BlockSpec((None, D_NSA, tn), lambda i, j: (layer, 3, j)),
        ],
        out_specs=pl.BlockSpec((tm, tn), lambda i, j: (i, j)),
        out_shape=jax.ShapeDtypeStruct((n, D_MODEL), f32),
        compiler_params=_params(("parallel", "arbitrary")),
        name="outproj",
    )(x, ys, yg, yn, w_out, w_out, w_out)


def _prep_kernel(q_ref, kv_ref, qg_ref, kg_ref, qn_ref, kv6_ref):
    for s in range(D_NSA // 128):
        sl = slice(s * 128, (s + 1) * 128)
        qn_ref[:, sl] = _group_rms(q_ref[:, sl], NSA_HD) * qg_ref[...]
    kv = kv_ref[...]
    kv6_ref[...] = kv[:, :6 * 128]
    kv6_ref[:, KV_SK:KV_SK + 128] = _group_rms(kv[:, KV_SK:KV_SK + 128], NSA_HD) * kg_ref[1:2, :]
    kv6_ref[:, KV_WK:KV_WK + 128] = _group_rms(kv[:, KV_WK:KV_WK + 128], NSA_HD) * kg_ref[2:3, :]


def _prep(proj, qgain, kgain):
    n = proj.shape[0]
    tm = min(n, PREP_ROW_TILE)
    return pl.pallas_call(
        _prep_kernel,
        grid=(n // tm,),
        in_specs=[
            pl.BlockSpec((tm, D_NSA), lambda i: (i, QN_OFF // D_NSA)),
            pl.BlockSpec((tm, 1024), lambda i: (i, KV_OFF // 1024)),
            pl.BlockSpec((1, 128), lambda i: (0, 0)),
            pl.BlockSpec((3, 128), lambda i: (0, 0)),
        ],
        out_specs=[pl.BlockSpec((tm, D_NSA), lambda i: (i, 0)),
                   pl.BlockSpec((tm, 768), lambda i: (i, 0))],
        out_shape=[jax.ShapeDtypeStruct((n, D_NSA), f32), jax.ShapeDtypeStruct((n, 768), f32)],
        compiler_params=_params(("parallel",)),
        name="nsa_prep",
    )(proj, proj, qgain, kgain)


def _ssd_prompt_kernel(xbc_ref, z_ref, sm_ref, cw_ref, cb_ref, dtb_ref, alog_ref, dexp_ref, nrm_ref,
                       y_ref, st_ref, cst_ref, xbuf, hst, ybuf):
    c = pl.program_id(1)
    cl = SSD_CHUNK

    @pl.when(c == 0)
    def _():
        xbuf[0:8, :] = jnp.zeros((8, SSD_CONV_CH), f32)
        hst[...] = jnp.zeros(hst.shape, f32)

    x = xbc_ref[...]
    xbuf[8:8 + cl, :] = x
    conv = cw_ref[3:4, :] * x
    for k in range(1, CONV_W):
        conv += cw_ref[3 - k:4 - k, :] * xbuf[pl.ds(8 - k, cl), :]
    xbuf[0:8, :] = x[cl - 8:cl, :]
    xc = _silu(conv + cb_ref[...])
    xs = xc[:, :D_SSD]

    dt = jnp.where(_iota((cl, 128), 1) < SSD_HEADS, _softplus(sm_ref[...] + dtb_ref[...]), 0.0)
    da = dt * (-jnp.exp(alog_ref[...]))
    row, col = _iota((cl, cl), 0), _iota((cl, cl), 1)
    causal = row >= col
    cum = _dot_01l(causal, da)
    cum_t = cum.T
    last = cum[cl - 1:cl, :]
    e_last = jnp.exp(last)
    spread = (_iota((128, D_SSD), 1) // SSD_HD == _iota((128, D_SSD), 0)).astype(f32)
    coef = _dot_01(jnp.concatenate([dt, jnp.exp(last - cum), jnp.exp(cum)], axis=0), spread)
    xdt = xs * coef[0:cl]
    xw = xdt * coef[cl:2 * cl]
    e_cum = coef[2 * cl:3 * cl]

    hpg = SSD_HEADS // SSD_GROUPS
    gw = hpg * SSD_HD
    for g in range(SSD_GROUPS):
        bm = xc[:, D_SSD + g * SSD_STATE:D_SSD + (g + 1) * SSD_STATE]
        cm = xc[:, D_SSD + (SSD_GROUPS + g) * SSD_STATE:D_SSD + (SSD_GROUPS + g + 1) * SSD_STATE]
        cb = _dot_nt(cm, bm)
        h0 = hst[g]
        y_off = _dot_nt(cm, h0) * e_cum[:, g * gw:(g + 1) * gw]
        s_new = _dot_tn(xw[:, g * gw:(g + 1) * gw], bm)
        for r in range(hpg):
            h = g * hpg + r
            seg = cum[:, h:h + 1] - cum_t[h:h + 1, :]
            decay = jnp.where(causal, jnp.exp(jnp.where(causal, seg, 0.0)), 0.0)
            rs = slice(r * SSD_HD, (r + 1) * SSD_HD)
            ybuf[:, h * SSD_HD:(h + 1) * SSD_HD] = (_dot(cb * decay, xdt[:, h * SSD_HD:(h + 1) * SSD_HD])
                                                    + y_off[:, rs])
            hst[g, rs, :] = e_last[:, h:h + 1] * h0[rs, :] + s_new[rs, :]

    y = (ybuf[...] + dexp_ref[...] * xs) * _silu(z_ref[...])
    gw = D_SSD // SSD_GROUPS
    for g in range(SSD_GROUPS):
        yg = y[:, g * gw:(g + 1) * gw]
        yg = yg * lax.rsqrt(jnp.mean(yg * yg, axis=-1, keepdims=True) + EPS)
        y_ref[:, g * gw:(g + 1) * gw] = (yg * nrm_ref[:, g * gw:(g + 1) * gw]).astype(y_ref.dtype)

    @pl.when(c == pl.num_programs(1) - 1)
    def _():
        st_ref[...] = hst[...]
        cst_ref[...] = x[cl - (CONV_W - 1):cl, :]


def _ssd_prompt(proj3, lp):
    B, L, _ = proj3.shape
    cl = SSD_CHUNK
    gh = SSD_HEADS // SSD_GROUPS * SSD_HD
    vec = lambda w: pl.BlockSpec((1, w), lambda b, c: (0, 0))
    y, st, cst = pl.pallas_call(
        _ssd_prompt_kernel,
        grid=(B, L // cl),
        in_specs=[
            pl.BlockSpec((None, cl, SSD_CONV_CH), lambda b, c: (b, c, XBC_OFF // SSD_CONV_CH)),
            pl.BlockSpec((None, cl, D_SSD), lambda b, c: (b, c, ZS_OFF // D_SSD)),
            pl.BlockSpec((None, cl, 128), lambda b, c: (b, c, SM_OFF // 128)),
            pl.BlockSpec((CONV_W, SSD_CONV_CH), lambda b, c: (0, 0)),
            vec(SSD_CONV_CH), vec(128), vec(128), vec(D_SSD), vec(D_SSD),
        ],
        out_specs=[
            pl.BlockSpec((None, cl, D_SSD), lambda b, c: (b, c, 0)),
            pl.BlockSpec((None, SSD_GROUPS, gh, SSD_STATE), lambda b, c: (b, 0, 0, 0)),
            pl.BlockSpec((None, CONV_W - 1, SSD_CONV_CH), lambda b, c: (b, 0, 0)),
        ],
        out_shape=[
            jax.ShapeDtypeStruct((B, L, D_SSD), bf16),
            jax.ShapeDtypeStruct((B, SSD_GROUPS, gh, SSD_STATE), f32),
            jax.ShapeDtypeStruct((B, CONV_W - 1, SSD_CONV_CH), f32),
        ],
        scratch_shapes=[
            pltpu.VMEM((8 + cl, SSD_CONV_CH), f32),
            pltpu.VMEM((SSD_GROUPS, gh, SSD_STATE), f32),
            pltpu.VMEM((cl, D_SSD), f32),
        ],
        compiler_params=_params(("parallel", "arbitrary")),
        name="ssd_prompt",
    )(proj3, proj3, proj3, lp["ssd_conv_w"], lp["ssd_conv_b"], lp["ssd_dtb"], lp["ssd_alog"],
      lp["ssd_dexp"], lp["ssd_norm"])
    return y, st.reshape(B, SSD_HEADS, SSD_HD, SSD_STATE), cst


def _gdn_prompt_kernel(qkv_ref, z_ref, sm_ref, cw_ref, dtb_ref, alog_ref, nrm_ref,
                       y_ref, st_ref, cst_ref, xbuf, sst):
    c = pl.program_id(1)
    cl = GDN_CHUNK
    nch = GDN_STEP_CHUNKS
    tl = nch * cl

    @pl.when(c == 0)
    def _():
        xbuf[0:8, :] = jnp.zeros((8, GDN_CONV_CH), f32)
        sst[...] = jnp.zeros(sst.shape, f32)

    x = qkv_ref[...]
    xbuf[8:8 + tl, :] = x
    conv = cw_ref[3:4, :] * x
    for k in range(1, CONV_W):
        conv += cw_ref[3 - k:4 - k, :] * xbuf[pl.ds(8 - k, tl), :]
    xbuf[0:8, :] = x[tl - 8:tl, :]
    xc = _silu(conv)

    sm = sm_ref[...]
    beta_all = jax.nn.sigmoid(sm)
    g_all = -jnp.exp(alog_ref[...]) * _softplus(sm + dtb_ref[...])
    row, col = _iota((cl, cl), 0), _iota((cl, cl), 1)
    incl = row >= col
    strict = row > col
    eye = (row == col).astype(f32)
    rt, ct = _iota((tl, tl), 0), _iota((tl, tl), 1)
    cum = _dot_01l((rt >= ct) & (rt // cl == ct // cl), g_all)
    cum_t = cum.T

    units = [(ci, h) for ci in range(nch) for h in range(GDN_HEADS)]
    q, k, v, beta, cum_c, decay = {}, {}, {}, {}, {}, {}
    for u in units:
        ci, h = u
        rows = slice(ci * cl, (ci + 1) * cl)
        qh = xc[rows, h * GDN_HD:(h + 1) * GDN_HD]
        kh = xc[rows, D_GDN + h * GDN_HD:D_GDN + (h + 1) * GDN_HD]
        v[u] = xc[rows, 2 * D_GDN + h * GDN_HD:2 * D_GDN + (h + 1) * GDN_HD]
        q[u] = qh * lax.rsqrt(jnp.sum(qh * qh, axis=-1, keepdims=True) + EPS) * (GDN_HD ** -0.5)
        k[u] = kh * lax.rsqrt(jnp.sum(kh * kh, axis=-1, keepdims=True) + EPS)
        beta[u] = beta_all[rows, SM_B + h:SM_B + h + 1]
        cum_c[u] = cum[rows, SM_A + h:SM_A + h + 1]
        seg = cum_c[u] - cum_t[SM_A + h:SM_A + h + 1, rows]
        decay[u] = jnp.where(incl, jnp.exp(jnp.where(incl, seg, 0.0)), 0.0)
    pw = {u: -jnp.where(strict, beta[u] * _dot_nt(k[u], k[u]) * decay[u], 0.0) for u in units}
    inv = {u: eye + pw[u] for u in units}
    pw2 = {u: _split2(pw[u]) for u in units}
    for _ in range(5):
        pw2 = {u: _split2(_dot3(pw2[u], pw2[u])) for u in units}
        inv = {u: inv[u] + _dot3(_split2(inv[u]), pw2[u]) for u in units}
    rhs = {u: jnp.concatenate([beta[u] * v[u], (beta[u] * jnp.exp(cum_c[u])) * k[u]], axis=1) for u in units}
    sol = {u: _dot3(_split2(inv[u]), _split2(rhs[u])) for u in units}
    uu = {u: sol[u][:, :GDN_HD] for u in units}
    ww = {u: sol[u][:, GDN_HD:] for u in units}
    qk = {u: _dot_nt(q[u], k[u]) * decay[u] for u in units}

    s = [sst[h] for h in range(GDN_HEADS)]
    for ci in range(nch):
        rows = slice(ci * cl, (ci + 1) * cl)
        hs = [(ci, h) for h in range(GDN_HEADS)]
        v_new = {u: uu[u] - _dot(ww[u], s[u[1]]) for u in hs}
        o = {u: _dot(q[u] * jnp.exp(cum_c[u]), s[u[1]]) + _dot(qk[u], v_new[u]) for u in hs}
        for u in hs:
            h = u[1]
            last = cum[ci * cl + cl - 1:ci * cl + cl, SM_A + h:SM_A + h + 1]
            s[h] = jnp.exp(last) * s[h] + _dot_tn(k[u] * jnp.exp(last - cum_c[u]), v_new[u])
        for u in hs:
            sl = slice(u[1] * GDN_HD, (u[1] + 1) * GDN_HD)
            on = o[u] * lax.rsqrt(jnp.mean(o[u] * o[u], axis=-1, keepdims=True) + EPS)
            y_ref[rows, sl] = (on * nrm_ref[...] * _silu(z_ref[rows, sl])).astype(y_ref.dtype)
    for h in range(GDN_HEADS):
        sst[h] = s[h]

    @pl.when(c == pl.num_programs(1) - 1)
    def _():
        st_ref[...] = sst[...]
        cst_ref[...] = x[tl - (CONV_W - 1):tl, :]


def _gdn_prompt(proj3, lp):
    B, L, _ = proj3.shape
    cl = GDN_CHUNK * GDN_STEP_CHUNKS
    vec = lambda w: pl.BlockSpec((1, w), lambda b, c: (0, 0))
    return pl.pallas_call(
        _gdn_prompt_kernel,
        grid=(B, L // cl),
        in_specs=[
            pl.BlockSpec((None, cl, GDN_CONV_CH), lambda b, c: (b, c, QKV_OFF // GDN_CONV_CH)),
            pl.BlockSpec((None, cl, D_GDN), lambda b, c: (b, c, ZG_OFF // D_GDN)),
            pl.BlockSpec((None, cl, 128), lambda b, c: (b, c, SM_OFF // 128)),
            pl.BlockSpec((CONV_W, GDN_CONV_CH), lambda b, c: (0, 0)),
            vec(128), vec(128), vec(GDN_HD),
        ],
        out_specs=[
            pl.BlockSpec((None, cl, D_GDN), lambda b, c: (b, c, 0)),
            pl.BlockSpec((None, GDN_HEADS, GDN_HD, GDN_HD), lambda b, c: (b, 0, 0, 0)),
            pl.BlockSpec((None, CONV_W - 1, GDN_CONV_CH), lambda b, c: (b, 0, 0)),
        ],
        out_shape=[
            jax.ShapeDtypeStruct((B, L, D_GDN), bf16),
            jax.ShapeDtypeStruct((B, GDN_HEADS, GDN_HD, GDN_HD), f32),
            jax.ShapeDtypeStruct((B, CONV_W - 1, GDN_CONV_CH), f32),
        ],
        scratch_shapes=[
            pltpu.VMEM((8 + cl, GDN_CONV_CH), f32),
            pltpu.VMEM((GDN_HEADS, GDN_HD, GDN_HD), f32),
        ],
        compiler_params=_params(("parallel", "arbitrary")),
        name="gdn_prompt",
    )(proj3, proj3, proj3, lp["gdn_conv_w"], lp["gdn_dtb"], lp["gdn_alog"], lp["gdn_norm"])


def _cmp_prompt_kernel(k_ref, v_ref, pe_ref, w1_ref, w2_ref, kg_ref, kc_ref, vct_ref):
    nb = kc_ref.shape[0]
    for t, src in enumerate((k_ref, v_ref)):
        outs = []
        p1 = [jnp.zeros((nb, CMP_HID), f32) for _ in range(NSA_KV)]
        p2 = [jnp.zeros((nb, CMP_HID), f32) for _ in range(NSA_KV)]
        for l in range(CMP_STRIDE):
            xr = src[pl.ds(l, nb, stride=CMP_STRIDE), :]
            l2 = CMP_STRIDE + l
            for g in range(NSA_KV):
                xg = xr[:, g * NSA_HD:(g + 1) * NSA_HD]
                p1[g] += _dot(xg + pe_ref[t, l:l + 1, :], w1_ref[t, l * NSA_HD:(l + 1) * NSA_HD, :])
                p2[g] += _dot(xg + pe_ref[t, l2:l2 + 1, :], w1_ref[t, l2 * NSA_HD:(l2 + 1) * NSA_HD, :])
        for g in range(NSA_KV):
            hid = p1[g] + pltpu.roll(p2[g], nb - 1, 0)
            cmp = _dot(_silu(hid), w2_ref[t])
            if t == 0:
                cmp = cmp * lax.rsqrt(jnp.mean(cmp * cmp, axis=-1, keepdims=True) + EPS) * kg_ref[0:1, 0:NSA_HD]
            outs.append(cmp)
        both = jnp.concatenate(outs, axis=1)
        if t == 0:
            kc_ref[...] = both
        else:
            vct_ref[...] = both.T


def _cmp_prompt(kv6_3, lp):
    B, L, _ = kv6_3.shape
    nb = L // CMP_STRIDE
    full = lambda s: pl.BlockSpec(s, lambda b: (0,) * len(s))
    return pl.pallas_call(
        _cmp_prompt_kernel,
        grid=(B,),
        in_specs=[
            pl.BlockSpec((None, L, 128), lambda b: (b, 0, KV_CK // 128)),
            pl.BlockSpec((None, L, 128), lambda b: (b, 0, KV_CV // 128)),
            full((2, CMP_LEN, NSA_HD)), full((2, CMP_LEN * NSA_HD, CMP_HID)), full((2, CMP_HID, NSA_HD)),
            full((3, 128)),
        ],
        out_specs=[pl.BlockSpec((None, nb, 128), lambda b: (b, 0, 0)),
                   pl.BlockSpec((None, 128, nb), lambda b: (b, 0, 0))],
        out_shape=[jax.ShapeDtypeStruct((B, nb, 128), f32), jax.ShapeDtypeStruct((B, 128, nb), f32)],
        compiler_params=_params(("parallel",)),
        name="nsa_cmp_prompt",
    )(kv6_3, kv6_3, lp["cmp_pe"], lp["cmp_w1"], lp["cmp_w2"], lp["k_gain"])


def _nsa_prompt_kernel(rb_ref, qt_ref, ks_ref, kw_ref, vst_ref, vwt_ref, kc_ref, vct_ref, gt_ref,
                       y_ref, tb_ref, tc_ref, ext_ref, selm_ref, s_ref):
    qi = pl.program_id(1)
    L = ks_ref.shape[0]
    tq = NSA_Q_TILE
    nl = NSA_REP * tq
    ncmp = kc_ref.shape[0]
    n_slc = L // SLC_BLOCK
    last = N_BUCKETS - 1

    @pl.when(qi == 0)
    def _():
        sub = _iota((128, tq), 0)
        qo = _iota((128, tq), 1)
        d_diag = qo - sub
        c_rel = jnp.where(sub < 64, sub, sub - 128)
        d_cmp = qo - CMP_STRIDE * c_rel - (CMP_LEN - 1)
        for g in range(NSA_KV):
            for r in range(NSA_REP):
                h = g * NSA_REP + r
                tab = [rb_ref[b, h] - rb_ref[last, h] for b in range(N_BUCKETS)]
                lanes = slice(r * tq, (r + 1) * tq)
                tb_ref[g, TB_FAR, :, lanes] = jnp.zeros((128, tq), f32)
                tb_ref[g, TB_PREV, :, lanes] = _bias_chain(d_diag + 128, tab)
                tb_ref[g, TB_DIAG, :, lanes] = jnp.where(d_diag >= 0, _bias_chain(d_diag, tab), NEG)
                tb_ref[g, TB_WIN0, :, lanes] = jnp.where(d_diag <= 0, 0.0, NEG)
                tb_ref[g, TB_NONE, :, lanes] = jnp.full((128, tq), NEG, f32)
                t_c = jnp.where(d_cmp < 0, 0.0, _bias_chain(d_cmp, tab))
                tc_ref[g, 0:128, lanes] = t_c
                tc_ref[g, 128:256, lanes] = t_c
        ext_ref[...] = (_iota((L, 128), 0) // SLC_BLOCK == _iota((L, 128), 1)).astype(bf16)

    q0 = qi * tq
    qpos = q0 + _iota((1, nl), 1) % tq
    qpos1 = q0 + _iota((1, tq), 1)
    sub = _iota((128, 1), 0)
    gt = gt_ref[...]

    jr = _iota((n_slc, tq), 0)
    cur = qpos1 // SLC_BLOCK
    valid = jr * SLC_BLOCK <= qpos1
    forced = (jr == 0) | (jr == cur) | (jr == cur - 1)
    j_i = _iota((n_slc, ncmp), 0)
    c_i = _iota((n_slc, ncmp), 1)
    ovl = ((c_i * CMP_STRIDE < j_i * SLC_BLOCK + SLC_BLOCK) & (c_i * CMP_STRIDE + CMP_LEN > j_i * SLC_BLOCK)
           & (c_i < ncmp - 1)).astype(f32)
    cmp_off = pl.multiple_of((128 - qi * (tq // CMP_STRIDE)) % 128, 8)

    groups = range(NSA_KV)
    gsl = [slice(g * NSA_HD, (g + 1) * NSA_HD) for g in groups]
    qt = [jnp.concatenate([qt_ref[(g * NSA_REP + r) * NSA_HD:(g * NSA_REP + r + 1) * NSA_HD, :]
                           for r in range(NSA_REP)], axis=1).astype(bf16) for g in groups]

    ok_c = (sub * CMP_STRIDE + (CMP_LEN - 1) <= qpos) & (sub < ncmp - 1)
    o_cmp = []
    for g in groups:
        st = _dot(kc_ref[:, gsl[g]], qt[g]) + tc_ref[g, pl.ds(cmp_off, 128), :]
        st = jnp.where(ok_c, st, NEG)
        e = jnp.exp(st - jnp.max(st, axis=0, keepdims=True))
        p = jnp.where(ok_c, e / jnp.sum(e, axis=0, keepdims=True), 0.0)
        o_cmp.append(_dot(vct_ref[gsl[g], :], p))
        p_sum = p[:, 0:tq] + p[:, tq:2 * tq] + p[:, 2 * tq:3 * tq] + p[:, 3 * tq:4 * tq]
        imp = _dot_01l(ovl, p_sum)
        score = jnp.where(valid, jnp.where(forced, 1e9, imp), -1e9)
        rank = jnp.zeros((n_slc, tq), f32)
        for i in range(n_slc):
            si = score[i:i + 1, :]
            rank += ((si > score) | ((si == score) & (jr > i))).astype(f32)
        sel = (rank < SLC_TOPN).astype(bf16)
        sel = jnp.concatenate([sel, jnp.zeros((128 - n_slc, tq), bf16)], axis=0)
        selx = jnp.dot(ext_ref[...], sel, preferred_element_type=f32)
        selm_ref[g] = (selx - 1.0) * -NEG

    ones_rows = jnp.ones((16, 128), f32)

    def attend(state, tiles, k_ref, vt_ref):
        s, offs = {}, []
        for ti, (kt, table) in enumerate(tiles):
            ko = pl.multiple_of(kt * 128, 128)
            offs.append(ko)
            for g in groups:
                sg = _dot(k_ref[pl.ds(ko, 128), gsl[g]], qt[g]) + tb_ref[g, table]
                s[g, ti] = sg
        out = []
        for g in groups:
            m, acc = state[g]
            m_new = m
            for ti in range(len(tiles)):
                m_new = jnp.maximum(m_new, jnp.max(s[g, ti], axis=0, keepdims=True))
            acc = jnp.exp(m - m_new) * acc
            for ti in range(len(tiles)):
                vt = jnp.concatenate([vt_ref[gsl[g], pl.ds(offs[ti], 128)], ones_rows], axis=0)
                acc += _dot(vt, jnp.exp(s[g, ti] - m_new))
            out.append((m_new, acc))
        return tuple(out)

    finish = lambda st: [acc[0:NSA_HD] / acc[NSA_HD:NSA_HD + 1] for _, acc in st]
    init = tuple((jnp.full((1, nl), NEG, f32), jnp.zeros((NSA_HD + 16, nl), f32)) for _ in groups)

    def sel_table(kt):
        return jnp.where(kt == qi, TB_DIAG, jnp.where(kt == qi - 1, TB_PREV, TB_FAR))

    def score_tiles(kts, mx):
        mx = list(mx)
        for kt in kts:
            ko = pl.multiple_of(kt * 128, 128)
            for g in groups:
                sg = (_dot(ks_ref[pl.ds(ko, 128), gsl[g]], qt[g]) + tb_ref[g, sel_table(kt)]
                      + jnp.concatenate([selm_ref[g, pl.ds(ko, 128), :]] * NSA_REP, axis=1))
                s_ref[g, kt] = sg
                parts = [sg[8 * k:8 * k + 8, :] for k in range(16)]
                while len(parts) > 1:
                    parts = [jnp.maximum(a, b) for a, b in zip(parts[0::2], parts[1::2])]
                mx[g] = jnp.maximum(mx[g], parts[0])
        return tuple(mx)

    def value_tiles(kts, accs, m):
        accs = list(accs)
        for kt in kts:
            ko = pl.multiple_of(kt * 128, 128)
            for g in groups:
                vt = jnp.concatenate([vst_ref[gsl[g], pl.ds(ko, 128)], ones_rows], axis=0)
                accs[g] = accs[g] + _dot(vt, jnp.exp(s_ref[g, kt] - m[g]))
        return tuple(accs)

    n_pair, n_odd = (qi + 1) // 2, (qi + 1) % 2
    mx = tuple(jnp.full((8, nl), NEG, f32) for _ in groups)
    mx = lax.fori_loop(0, n_pair, lambda i, c: score_tiles([2 * i, 2 * i + 1], c), mx)
    mx = lax.fori_loop(0, n_odd, lambda i, c: score_tiles([qi], c), mx)
    m_sel = [jnp.max(x, axis=0, keepdims=True) for x in mx]
    accs = tuple(jnp.zeros((NSA_HD + 16, nl), f32) for _ in groups)
    accs = lax.fori_loop(0, n_pair, lambda i, c: value_tiles([2 * i, 2 * i + 1], c, m_sel), accs)
    accs = lax.fori_loop(0, n_odd, lambda i, c: value_tiles([qi], c, m_sel), accs)
    o_slc = [acc[0:NSA_HD] / acc[NSA_HD:NSA_HD + 1] for acc in accs]

    w0 = jnp.maximum(qi - WINDOW // 128, 0)

    def win_table(kt):
        t = jnp.where(kt == qi - WINDOW // 128, TB_WIN0, TB_FAR)
        t = jnp.where(kt == qi - 1, TB_PREV, t)
        return jnp.where(kt == qi, TB_DIAG, jnp.where(kt > qi, TB_NONE, t))

    win_tiles = [(w0 + t, win_table(w0 + t)) for t in range(WINDOW // 128 + 1)]
    o_win = finish(attend(init, win_tiles, kw_ref, vwt_ref))

    y_parts = []
    for g in groups:
        for r in range(NSA_REP):
            lanes = slice(r * tq, (r + 1) * tq)
            h = g * NSA_REP + r
            gate = lambda t: gt[SM_G + t * NSA_HEADS + h:SM_G + t * NSA_HEADS + h + 1, :]
            y_parts.append(gate(0) * o_cmp[g][:, lanes] + gate(1) * o_slc[g][:, lanes]
                           + gate(2) * o_win[g][:, lanes])
    y_ref[...] = jnp.concatenate(y_parts, axis=0).T.astype(y_ref.dtype)


def _nsa_prompt(qt, kv6_3, kv6t, kc, vct, gt, rel_bias):
    B, L, _ = kv6_3.shape
    tq = NSA_Q_TILE
    nb = L // CMP_STRIDE
    return pl.pallas_call(
        _nsa_prompt_kernel,
        grid=(B, L // tq),
        in_specs=[
            pl.BlockSpec(memory_space=pltpu.SMEM),
            pl.BlockSpec((None, D_NSA, tq), lambda b, i: (b, 0, i)),
            pl.BlockSpec((None, L, 128), lambda b, i: (b, 0, KV_SK // 128)),
            pl.BlockSpec((None, L, 128), lambda b, i: (b, 0, KV_WK // 128)),
            pl.BlockSpec((None, 128, L), lambda b, i: (b, KV_SV // 128, 0)),
            pl.BlockSpec((None, 128, L), lambda b, i: (b, KV_WV // 128, 0)),
            pl.BlockSpec((None, nb, 128), lambda b, i: (b, 0, 0)),
            pl.BlockSpec((None, 128, nb), lambda b, i: (b, 0, 0)),
            pl.BlockSpec((None, 128, tq), lambda b, i: (b, 0, i)),
        ],
        out_specs=pl.BlockSpec((None, tq, D_NSA), lambda b, i: (b, i, 0)),
        out_shape=jax.ShapeDtypeStruct((B, L, D_NSA), bf16),
        scratch_shapes=[
            pltpu.VMEM((NSA_KV, TB_KINDS, 128, NSA_REP * tq), f32),
            pltpu.VMEM((NSA_KV, 256, NSA_REP * tq), f32),
            pltpu.VMEM((L, 128), bf16),
            pltpu.VMEM((NSA_KV, L, tq), f32),
            pltpu.VMEM((NSA_KV, L // 128, 128, NSA_REP * tq), f32),
        ],
        compiler_params=_params(("parallel", "arbitrary")),
        name="nsa_prompt",
    )(rel_bias, qt, kv6_3, kv6_3, kv6t, kv6t, kc, vct, gt)


def _prep_t_kernel(q_ref, kv_ref, qg_ref, kg_ref, qt_ref, kv6_ref, kv6t_ref, gt_ref):
    for s in range(D_NSA // 128):
        sl = slice(s * 128, (s + 1) * 128)
        qt_ref[sl, :] = (_group_rms(q_ref[:, sl], NSA_HD) * qg_ref[...] * (NSA_HD ** -0.5)).T
    kv = kv_ref[...]
    for j, off in enumerate((KV_CK, KV_CV, KV_SK, KV_SV, KV_WK, KV_WV)):
        x = kv[:, off:off + 128]
        if off == KV_SK:
            x = _group_rms(x, NSA_HD) * kg_ref[1:2, :]
        if off == KV_WK:
            x = _group_rms(x, NSA_HD) * kg_ref[2:3, :]
        kv6_ref[:, off:off + 128] = x
        kv6t_ref[off:off + 128, :] = x.T
    gt_ref[...] = jax.nn.sigmoid(kv[:, SM_OFF - KV_OFF:SM_OFF - KV_OFF + 128]).T


def _prep_t(proj3, qgain, kgain):
    B, L, _ = proj3.shape
    tm = min(L, PREP_ROW_TILE)
    return pl.pallas_call(
        _prep_t_kernel,
        grid=(B, L // tm),
        in_specs=[
            pl.BlockSpec((None, tm, D_NSA), lambda b, i: (b, i, QN_OFF // D_NSA)),
            pl.BlockSpec((None, tm, 1024), lambda b, i: (b, i, KV_OFF // 1024)),
            pl.BlockSpec((1, 128), lambda b, i: (0, 0)),
            pl.BlockSpec((3, 128), lambda b, i: (0, 0)),
        ],
        out_specs=[pl.BlockSpec((None, D_NSA, tm), lambda b, i: (b, 0, i)),
                   pl.BlockSpec((None, tm, 768), lambda b, i: (b, i, 0)),
                   pl.BlockSpec((None, 768, tm), lambda b, i: (b, 0, i)),
                   pl.BlockSpec((None, 128, tm), lambda b, i: (b, 0, i))],
        out_shape=[jax.ShapeDtypeStruct((B, D_NSA, L), f32), jax.ShapeDtypeStruct((B, L, 768), f32),
                   jax.ShapeDtypeStruct((B, 768, L), f32), jax.ShapeDtypeStruct((B, 128, L), f32)],
        compiler_params=_params(("parallel", "parallel")),
        name="nsa_prep_t",
    )(proj3, proj3, qgain, kgain)


def _row0(x, rows=8):
    return jnp.where(_iota((rows, x.shape[1]), 0) == 0, x, 0.0)


def _ssd_decode_kernel(xbc_ref, z_ref, sm_ref, cprev_ref, sin_ref, cw_ref, cb_ref, dtb_ref, alog_ref, dexp_ref,
                       nrm_ref, y_ref, st_ref, cst_ref):
    x = xbc_ref[...]
    conv = cw_ref[3:4, :] * x
    for j in range(CONV_W - 1):
        conv += cw_ref[j:j + 1, :] * cprev_ref[j:j + 1, :]
    cst_ref[0:2, :] = cprev_ref[1:3, :]
    cst_ref[2:3, :] = x
    xc = _silu(conv + cb_ref[...])
    xs = xc[:, :D_SSD]
    dt = _softplus(sm_ref[...] + dtb_ref[...])
    da = dt * (-jnp.exp(alog_ref[...]))
    ys = []
    for g in range(SSD_GROUPS):
        bm = xc[:, D_SSD + g * SSD_STATE:D_SSD + (g + 1) * SSD_STATE]
        cm = xc[:, D_SSD + (SSD_GROUPS + g) * SSD_STATE:D_SSD + (SSD_GROUPS + g + 1) * SSD_STATE]
        cb = jnp.sum(cm * bm, axis=-1, keepdims=True)
        bm8, cm8 = _row0(bm), _row0(cm)
        for r in range(SSD_HEADS // SSD_GROUPS):
            h = g * (SSD_HEADS // SSD_GROUPS) + r
            xdt = xs[:, h * SSD_HD:(h + 1) * SSD_HD] * dt[:, h:h + 1]
            eda = jnp.exp(da[:, h:h + 1])
            h0 = sin_ref[h]
            ys.append(cb * xdt + eda * _dot_nt(cm8, h0, HI)[0:1])
            st_ref[h] = eda * h0 + _dot_tn(_row0(xdt), bm8, HI)
    y = (jnp.concatenate(ys, axis=1) + dexp_ref[...] * xs) * _silu(z_ref[...])
    gw = D_SSD // SSD_GROUPS
    for g in range(SSD_GROUPS):
        yg = y[:, g * gw:(g + 1) * gw]
        yg = yg * lax.rsqrt(jnp.mean(yg * yg, axis=-1, keepdims=True) + EPS)
        y_ref[:, g * gw:(g + 1) * gw] = yg * nrm_ref[:, g * gw:(g + 1) * gw]


def _ssd_decode(proj3, conv_state, state, layer, lp):
    B = proj3.shape[0]
    vec = lambda w: pl.BlockSpec((1, w), lambda b: (0, 0))
    return pl.pallas_call(
        _ssd_decode_kernel,
        grid=(B,),
        in_specs=[
            pl.BlockSpec((None, 1, SSD_CONV_CH), lambda b: (b, 0, XBC_OFF // SSD_CONV_CH)),
            pl.BlockSpec((None, 1, D_SSD), lambda b: (b, 0, ZS_OFF // D_SSD)),
            pl.BlockSpec((None, 1, 128), lambda b: (b, 0, SM_OFF // 128)),
            pl.BlockSpec((None, None, CONV_W - 1, SSD_CONV_CH), lambda b: (layer, b, 0, 0)),
            pl.BlockSpec((None, None, SSD_HEADS, SSD_HD, SSD_STATE), lambda b: (layer, b, 0, 0, 0)),
            pl.BlockSpec((CONV_W, SSD_CONV_CH), lambda b: (0, 0)),
            vec(SSD_CONV_CH), vec(128), vec(128), vec(D_SSD), vec(D_SSD),
        ],
        out_specs=[
            pl.BlockSpec((None, 1, D_SSD), lambda b: (b, 0, 0)),
            pl.BlockSpec((None, SSD_HEADS, SSD_HD, SSD_STATE), lambda b: (b, 0, 0, 0)),
            pl.BlockSpec((None, CONV_W - 1, SSD_CONV_CH), lambda b: (b, 0, 0)),
        ],
        out_shape=[
            jax.ShapeDtypeStruct((B, 1, D_SSD), f32),
            jax.ShapeDtypeStruct((B, SSD_HEADS, SSD_HD, SSD_STATE), f32),
            jax.ShapeDtypeStruct((B, CONV_W - 1, SSD_CONV_CH), f32),
        ],
        compiler_params=_params(("parallel",)),
        name="ssd_decode",
    )(proj3, proj3, proj3, conv_state, state, lp["ssd_conv_w"], lp["ssd_conv_b"], lp["ssd_dtb"], lp["ssd_alog"],
      lp["ssd_dexp"], lp["ssd_norm"])


def _gdn_decode_kernel(qkv_ref, z_ref, sm_ref, cprev_ref, sin_ref, cw_ref, dtb_ref, alog_ref, nrm_ref,
                       y_ref, st_ref, cst_ref):
    x = qkv_ref[...]
    conv = cw_ref[3:4, :] * x
    for j in range(CONV_W - 1):
        conv += cw_ref[j:j + 1, :] * cprev_ref[j:j + 1, :]
    cst_ref[0:2, :] = cprev_ref[1:3, :]
    cst_ref[2:3, :] = x
    xc = _silu(conv)
    sm = sm_ref[...]
    beta_all = jax.nn.sigmoid(sm)
    g_all = -jnp.exp(alog_ref[...]) * _softplus(sm + dtb_ref[...])
    for h in range(GDN_HEADS):
        sl = slice(h * GDN_HD, (h + 1) * GDN_HD)
        qh = xc[:, sl]
        kh = xc[:, D_GDN + h * GDN_HD:D_GDN + (h + 1) * GDN_HD]
        vh = xc[:, 2 * D_GDN + h * GDN_HD:2 * D_GDN + (h + 1) * GDN_HD]
        qh = qh * lax.rsqrt(jnp.sum(qh * qh, axis=-1, keepdims=True) + EPS) * (GDN_HD ** -0.5)
        kh = kh * lax.rsqrt(jnp.sum(kh * kh, axis=-1, keepdims=True) + EPS)
        beta = beta_all[:, SM_B + h:SM_B + h + 1]
        eg = jnp.exp(g_all[:, SM_A + h:SM_A + h + 1])
        s0 = sin_ref[h]
        k8 = _row0(kh)
        v_new = beta * vh - (beta * eg) * _dot(k8, s0, HI)[0:1]
        o = eg * _dot(_row0(qh), s0, HI)[0:1] + jnp.sum(qh * kh, axis=-1, keepdims=True) * v_new
        st_ref[h] = eg * s0 + _dot_tn(k8, _row0(v_new), HI)
        o = o * lax.rsqrt(jnp.mean(o * o, axis=-1, keepdims=True) + EPS)
        y_ref[:, sl] = o * nrm_ref[...] * _silu(z_ref[:, sl])


def _gdn_decode(proj3, conv_state, state, layer, lp):
    B = proj3.shape[0]
    vec = lambda w: pl.BlockSpec((1, w), lambda b: (0, 0))
    return pl.pallas_call(
        _gdn_decode_kernel,
        grid=(B,),
        in_specs=[
            pl.BlockSpec((None, 1, GDN_CONV_CH), lambda b: (b, 0, QKV_OFF // GDN_CONV_CH)),
            pl.BlockSpec((None, 1, D_GDN), lambda b: (b, 0, ZG_OFF // D_GDN)),
            pl.BlockSpec((None, 1, 128), lambda b: (b, 0, SM_OFF // 128)),
            pl.BlockSpec((None, None, CONV_W - 1, GDN_CONV_CH), lambda b: (layer, b, 0, 0)),
            pl.BlockSpec((None, None, GDN_HEADS, GDN_HD, GDN_HD), lambda b: (layer, b, 0, 0, 0)),
            pl.BlockSpec((CONV_W, GDN_CONV_CH), lambda b: (0, 0)),
            vec(128), vec(128), vec(GDN_HD),
        ],
        out_specs=[
            pl.BlockSpec((None, 1, D_GDN), lambda b: (b, 0, 0)),
            pl.BlockSpec((None, GDN_HEADS, GDN_HD, GDN_HD), lambda b: (b, 0, 0, 0)),
            pl.BlockSpec((None, CONV_W - 1, GDN_CONV_CH), lambda b: (b, 0, 0)),
        ],
        out_shape=[
            jax.ShapeDtypeStruct((B, 1, D_GDN), f32),
            jax.ShapeDtypeStruct((B, GDN_HEADS, GDN_HD, GDN_HD), f32),
            jax.ShapeDtypeStruct((B, CONV_W - 1, GDN_CONV_CH), f32),
        ],
        compiler_params=_params(("parallel",)),
        name="gdn_decode",
    )(proj3, proj3, proj3, conv_state, state, lp["gdn_conv_w"], lp["gdn_dtb"], lp["gdn_alog"], lp["gdn_norm"])


def _query_rows(q, g):
    rows = [q[:, (g * NSA_REP + r) * NSA_HD:(g * NSA_REP + r + 1) * NSA_HD] for r in range(NSA_REP)]
    return jnp.concatenate(rows + [jnp.zeros((8 - NSA_REP, NSA_HD), f32)], axis=0) * (NSA_HD ** -0.5)


def _masked_softmax(lg, ok):
    lg = jnp.where(ok, lg, NEG)
    e = jnp.exp(lg - jnp.max(lg, axis=-1, keepdims=True))
    return jnp.where(ok, e / jnp.sum(e, axis=-1, keepdims=True), 0.0)


def _head_lanes(parts):
    return jnp.concatenate([o[r:r + 1, :] for o in parts for r in range(NSA_REP)], axis=1)


def _nsa_dec_cmp_kernel(pt_ref, q_ref, pe_ref, w1_ref, w2_ref, kg_ref, tab_ref, pk_ref, pv_ref,
                        ocmp_ref, idx_ref, stage, rows_k, rows_v, sem, *, layer, n_pages):
    b = pl.program_id(0)
    nb = pl.num_programs(0)
    n_past = n_pages * PAGE
    q_pos = n_past
    n_blk = n_past // CMP_STRIDE
    n_cmp = (n_past + 1 - CMP_LEN) // CMP_STRIDE + 1
    n_slc = -(-(n_past + 1) // SLC_BLOCK)
    nj = -(-n_slc // 128) * 128
    pools = (pk_ref, pv_ref)

    def page_copy(t, seq, p):
        return pltpu.make_async_copy(pools[t].at[layer, pt_ref[seq, p]], stage.at[t, p], sem.at[t])

    def start_all(t, seq):
        lax.fori_loop(0, n_pages, lambda p, c: (page_copy(t, seq, p).start(), c)[1], 0)

    def wait_all(t, seq):
        lax.fori_loop(0, n_pages, lambda p, c: (page_copy(t, seq, p).wait(), c)[1], 0)

    @pl.when(b == 0)
    def _():
        start_all(0, 0)
        start_all(1, 0)

    n_pair = CMP_STRIDE // 2
    share = n_pages // n_pair

    def compress(t, rows, between=None):
        p1 = jnp.zeros((n_blk, 2 * CMP_HID), f32)
        p2 = jnp.zeros((n_blk, 2 * CMP_HID), f32)
        for j in range(n_pair):
            xr = jnp.concatenate([rows[pl.ds(2 * j, n_blk, stride=CMP_STRIDE), :],
                                  rows[pl.ds(2 * j + 1, n_blk, stride=CMP_STRIDE), :]], axis=1)
            j2 = n_pair + j
            p1 += jnp.dot((xr + pe_ref[t, j:j + 1, :]).astype(bf16), w1_ref[t, j], preferred_element_type=f32)
            p2 += jnp.dot((xr + pe_ref[t, j2:j2 + 1, :]).astype(bf16), w1_ref[t, j2], preferred_element_type=f32)
            if between is not None:
                between(j)
        hid = p1 + pltpu.roll(p2, n_blk - 1, 0)
        return jnp.dot(_silu(hid).astype(bf16), w2_ref[t], preferred_element_type=f32)

    def value_pages(j):
        for p in range(j * share, (j + 1) * share):
            rows_v[p * PAGE:(p + 1) * PAGE, :] = stage[1, p].T

    wait_all(0, b)

    def key_page(p, c):
        rows_k[pl.ds(pl.multiple_of(p * PAGE, PAGE), PAGE), :] = stage[0, p].T
        return c
    lax.fori_loop(0, n_pages, key_page, 0, unroll=4)

    @pl.when(b + 1 < nb)
    def _():
        start_all(0, b + 1)

    wait_all(1, b)
    kc = _group_rms(compress(0, rows_k, value_pages), NSA_HD) * kg_ref[0:1, :]

    @pl.when(b + 1 < nb)
    def _():
        start_all(1, b + 1)

    vc = compress(1, rows_v)

    q = q_ref[...]
    lane = _iota((8, n_blk), 1)
    cend = lane * CMP_STRIDE + (CMP_LEN - 1)
    ok = (cend <= q_pos) & (lane < n_cmp)
    c_i = _iota((n_blk, nj), 0)
    j_i = _iota((n_blk, nj), 1)
    ovl = ((c_i * CMP_STRIDE < j_i * SLC_BLOCK + SLC_BLOCK) & (c_i * CMP_STRIDE + CMP_LEN > j_i * SLC_BLOCK)
           & (j_i < n_slc) & (c_i < n_cmp)).astype(f32)
    jl = _iota((1, nj), 1)
    cur = q_pos // SLC_BLOCK
    valid = jl * SLC_BLOCK <= q_pos
    forced = (jl == 0) | (jl == cur) | (jl == cur - 1)
    ii = _iota((nj, nj), 0)
    jj = _iota((nj, nj), 1)
    kk = _iota((SLC_TOPN, nj), 0)
    j16 = _iota((SLC_TOPN, nj), 1)
    o_parts, idx_cols = [], []
    for g in range(NSA_KV):
        tabcols = [tab_ref[g, :, bk:bk + 1] for bk in range(N_BUCKETS)]
        lg = _dot_nt(_query_rows(q, g), kc[:, g * NSA_HD:(g + 1) * NSA_HD]) + _bias_chain(q_pos - cend, tabcols)
        p = _masked_softmax(lg, ok)
        o_parts.append(_dot(p, vc[:, g * NSA_HD:(g + 1) * NSA_HD]))
        p_sum = p[0:1] + p[1:2] + p[2:3] + p[3:4]
        imp = _dot_01(_row0(p_sum), ovl)[0:1]
        score = jnp.where(valid, jnp.where(forced, 1e9, imp), -1e9)
        score = jnp.where(jl < n_slc, score, -3e9)
        s_c = jnp.sum(jnp.where(ii == jj, score, 0.0), axis=1, keepdims=True)
        rank_r = jnp.sum(((s_c > score) | ((s_c == score) & (ii < jj))).astype(f32), axis=0, keepdims=True)
        rank_c = jnp.sum(((score > s_c) | ((score == s_c) & (jj < ii))).astype(f32), axis=1, keepdims=True)
        sel_r = (rank_r < SLC_TOPN) & (jl < n_slc)
        sel_c = (rank_c < SLC_TOPN) & (ii[:, 0:1] < n_slc)
        pos_r = jnp.sum((sel_c & (ii < jj)).astype(f32), axis=0, keepdims=True)
        hit = (pos_r == kk.astype(f32)) & sel_r
        idx_cols.append(jnp.sum(jnp.where(hit, j16.astype(f32), 0.0), axis=1, keepdims=True))
    ocmp_ref[...] = _head_lanes(o_parts)
    l16 = _iota((SLC_TOPN, 128), 1)
    idx_ref[...] = (jnp.where(l16 == 0, idx_cols[0], 0.0) + jnp.where(l16 == 1, idx_cols[1], 0.0)).astype(i32)


def _nsa_dec_cmp(qn3, page_table, pool_k, pool_v, layer, lp):
    B = qn3.shape[0]
    n_pages = page_table.shape[1]
    full = lambda s: pl.BlockSpec(s, lambda b, pt: (0,) * len(s))
    return pl.pallas_call(
        functools.partial(_nsa_dec_cmp_kernel, layer=layer, n_pages=n_pages),
        grid_spec=pltpu.PrefetchScalarGridSpec(
            num_scalar_prefetch=1,
            grid=(B,),
            in_specs=[
                pl.BlockSpec((None, 1, D_NSA), lambda b, pt: (b, 0, 0)),
                full((2, CMP_LEN // 2, 256)), full((2, CMP_LEN // 2, 256, 2 * CMP_HID)),
                full((2, 2 * CMP_HID, 128)), full((3, 128)), full((NSA_KV, 8, N_BUCKETS)),
                pl.BlockSpec(memory_space=pl.ANY), pl.BlockSpec(memory_space=pl.ANY),
            ],
            out_specs=[pl.BlockSpec((None, 1, D_NSA), lambda b, pt: (b, 0, 0)),
                       pl.BlockSpec((None, SLC_TOPN, 128), lambda b, pt: (b, 0, 0))],
            scratch_shapes=[pltpu.VMEM((2, n_pages, NSA_KV * NSA_HD, PAGE), f32),
                            pltpu.VMEM((n_pages * PAGE, NSA_KV * NSA_HD), f32),
                            pltpu.VMEM((n_pages * PAGE, NSA_KV * NSA_HD), f32),
                            pltpu.SemaphoreType.DMA((2,))],
        ),
        out_shape=[jax.ShapeDtypeStruct((B, 1, D_NSA), f32), jax.ShapeDtypeStruct((B, SLC_TOPN, 128), i32)],
        compiler_params=_params(("arbitrary",)),
        name="nsa_dec_cmp",
    )(page_table, qn3, lp["cmp_pe2"], lp["cmp_w1bd"], lp["cmp_w2bd"], lp["k_gain"], lp["tab8"], pool_k, pool_v)


def _nsa_dec_sel_kernel(pt_ref, idx_ref, q_ref, kv_ref, sm_ref, ocmp_ref, tab_ref, wk_ref, wv_ref, pk_ref, pv_ref,
                        y_ref, wko_ref, wvo_ref, kg, vg, sem, *, layer, n_pages):
    b = pl.program_id(0)
    n_past = n_pages * PAGE
    q_pos = n_past
    n_buf = wk_ref.shape[1]
    new_blk = n_past // SLC_BLOCK
    per_page = PAGE // SLC_BLOCK

    def blk_copy(pool, buf, g, s, sm_i):
        j = jnp.minimum(idx_ref[b, s, g], new_blk - 1)
        src = pool.at[layer, pt_ref[b, j // per_page], pl.ds(g * NSA_HD, NSA_HD)]
        return pltpu.make_async_copy(src, buf.at[g, :, pl.ds(s * PAGE, PAGE)], sem.at[sm_i])

    for g in range(NSA_KV):
        for s in range(SLC_TOPN):
            blk_copy(pk_ref, kg, g, s, 0).start()
            blk_copy(pv_ref, vg, g, s, 1).start()

    kv = kv_ref[...]
    q = q_ref[...]
    gate = jax.nn.sigmoid(sm_ref[...])
    ocmp = ocmp_ref[...]

    ii, jj = _iota((128, 128), 0), _iota((128, 128), 1)
    column = lambda off: jnp.sum(jnp.where(ii == jj, kv[:, off:off + 128], 0.0), axis=1, keepdims=True)
    last_lane = _iota((128, n_buf), 1) == n_buf - 1
    wk, wv = wk_ref[...], wv_ref[...]
    wko_ref[...] = jnp.where(last_lane, column(KV_WK), pltpu.roll(wk, n_buf - 1, 1))
    wvo_ref[...] = jnp.where(last_lane, column(KV_WV), pltpu.roll(wv, n_buf - 1, 1))

    def attend(qg, kt, vt, bias, ok, k_new, v_new, bias_new, ok_new):
        lg = jnp.where(ok, _dot(qg, kt) + bias, NEG)
        lg_new = jnp.where(ok_new, jnp.sum(qg * k_new, axis=-1, keepdims=True) + bias_new, NEG)
        m = jnp.maximum(jnp.max(lg, axis=-1, keepdims=True), lg_new)
        e = jnp.where(ok, jnp.exp(lg - m), 0.0)
        e_new = jnp.where(ok_new, jnp.exp(lg_new - m), 0.0)
        den = jnp.sum(e, axis=-1, keepdims=True) + e_new
        p, p_new = e / den, e_new / den
        return _dot_nt(p, vt) + p_new * v_new

    for g in range(NSA_KV):
        for s in range(SLC_TOPN):
            blk_copy(pk_ref, kg, g, s, 0).wait()
            blk_copy(pv_ref, vg, g, s, 1).wait()

    lw = _iota((8, n_buf), 1)
    d_w = q_pos - (n_past - n_buf + lw)
    ok_w = (d_w >= 0) & (d_w <= WINDOW)
    ls = _iota((8, SLC_TOPN * PAGE), 1)
    slot = ls // PAGE
    zero = jnp.zeros((8, 1), i32)
    o_slc, o_win = [], []
    for g in range(NSA_KV):
        tabcols = [tab_ref[g, :, bk:bk + 1] for bk in range(N_BUCKETS)]
        qg = _query_rows(q, g)
        gs = slice(g * NSA_HD, (g + 1) * NSA_HD)
        b_new = _bias_chain(zero, tabcols)
        blk = jnp.zeros(ls.shape, i32)
        n_new = jnp.zeros((8, 1), i32)
        for s in range(SLC_TOPN):
            j = idx_ref[b, s, g]
            blk = jnp.where(slot == s, j, blk)
            n_new = jnp.where(j == new_blk, n_new + 1, n_new)
        k_pos = (jnp.minimum(blk, new_blk - 1) // per_page) * PAGE + ls % PAGE
        ok_s = (blk < new_blk) & (k_pos // SLC_BLOCK == blk)
        o_slc.append(attend(qg, kg[g], vg[g], _bias_chain(q_pos - k_pos, tabcols), ok_s,
                            kv[:, KV_SK + g * NSA_HD:KV_SK + (g + 1) * NSA_HD],
                            kv[:, KV_SV + g * NSA_HD:KV_SV + (g + 1) * NSA_HD], b_new, n_new > 0))
        o_win.append(attend(qg, wk[gs, :], wv[gs, :], _bias_chain(d_w, tabcols), ok_w,
                            kv[:, KV_WK + g * NSA_HD:KV_WK + (g + 1) * NSA_HD],
                            kv[:, KV_WV + g * NSA_HD:KV_WV + (g + 1) * NSA_HD], b_new, zero == 0))
    gl = lambda t: gate[:, SM_G + t * NSA_HEADS:SM_G + (t + 1) * NSA_HEADS]
    wide = lambda gt: jnp.concatenate([jnp.broadcast_to(gt[:, h:h + 1], (1, NSA_HD)) for h in range(NSA_HEADS)], axis=1)
    y_ref[...] = wide(gl(0)) * ocmp + wide(gl(1)) * _head_lanes(o_slc) + wide(gl(2)) * _head_lanes(o_win)


def _nsa_dec_sel(qn3, kv6_3, proj3, ocmp, idx, page_table, win_k, win_v, pool_k, pool_v, layer, lp):
    B = qn3.shape[0]
    n_pages = page_table.shape[1]
    n_buf = win_k.shape[3]
    full = lambda s: pl.BlockSpec(s, lambda b, pt, ix: (0,) * len(s))
    row = lambda w, j: pl.BlockSpec((None, 1, w), lambda b, pt, ix: (b, 0, j))
    win = pl.BlockSpec((None, None, 128, n_buf), lambda b, pt, ix: (layer, b, 0, 0))
    return pl.pallas_call(
        functools.partial(_nsa_dec_sel_kernel, layer=layer, n_pages=n_pages),
        grid_spec=pltpu.PrefetchScalarGridSpec(
            num_scalar_prefetch=2,
            grid=(B,),
            in_specs=[row(D_NSA, 0), row(768, 0), row(128, SM_OFF // 128), row(D_NSA, 0),
                      full((NSA_KV, 8, N_BUCKETS)), win, win,
                      pl.BlockSpec(memory_space=pl.ANY), pl.BlockSpec(memory_space=pl.ANY)],
            out_specs=[row(D_NSA, 0),
                       pl.BlockSpec((None, 128, n_buf), lambda b, pt, ix: (b, 0, 0)),
                       pl.BlockSpec((None, 128, n_buf), lambda b, pt, ix: (b, 0, 0))],
            scratch_shapes=[pltpu.VMEM((NSA_KV, NSA_HD, SLC_TOPN * PAGE), f32),
                            pltpu.VMEM((NSA_KV, NSA_HD, SLC_TOPN * PAGE), f32),
                            pltpu.SemaphoreType.DMA((2,))],
        ),
        out_shape=[jax.ShapeDtypeStruct((B, 1, D_NSA), f32),
                   jax.ShapeDtypeStruct((B, 128, n_buf), f32), jax.ShapeDtypeStruct((B, 128, n_buf), f32)],
        compiler_params=_params(("arbitrary",)),
        name="nsa_dec_sel",
    )(page_table, idx, qn3, kv6_3, proj3, ocmp, lp["tab8"], win_k, win_v, pool_k, pool_v)


def _layer_params(i, p):
    pad128 = lambda v, off: jnp.zeros((1, 128), f32).at[0, off:off + v.shape[0]].set(v)
    return dict(
        ssd_conv_w=p["ssd_conv_w"][i], ssd_conv_b=p["ssd_conv_b"][i][None],
        ssd_dtb=pad128(p["ssd_dt_bias"][i], SM_DT), ssd_alog=pad128(p["ssd_a_log"][i], SM_DT),
        ssd_dexp=jnp.repeat(p["ssd_d"][i], SSD_HD)[None], ssd_norm=p["ssd_norm"][i][None],
        gdn_conv_w=p["gdn_conv_w"][i],
        gdn_dtb=pad128(p["gdn_dt_bias"][i], SM_A), gdn_alog=pad128(p["gdn_a_log"][i], SM_A),
        gdn_norm=p["gdn_norm"][i][None],
        q_gain=jnp.tile(p["nsa_q_norm"][i], 2)[None], k_gain=jnp.tile(p["nsa_k_norm"][i], (1, 2)),
        cmp_pe=p["nsa_cmp_pe"][i], cmp_w1=p["nsa_cmp_w1"][i], cmp_w2=p["nsa_cmp_w2"][i],
        cmp_pe2=jnp.tile(p["nsa_cmp_pe"][i], (1, 1, 2)).reshape(2, CMP_LEN // 2, 256),
        cmp_w1bd=_block_diag2(p["nsa_cmp_w1"][i].reshape(2, CMP_LEN, NSA_HD, CMP_HID)).astype(bf16)
        .reshape(2, CMP_LEN // 2, 256, 2 * CMP_HID),
        cmp_w2bd=_block_diag2(p["nsa_cmp_w2"][i]).astype(bf16),
        tab8=jnp.pad(p["rel_bias"].T.reshape(NSA_KV, NSA_REP, N_BUCKETS), ((0, 0), (0, 8 - NSA_REP), (0, 0))),
    )


def _block_diag2(w):
    z = jnp.zeros_like(w)
    return jnp.concatenate([jnp.concatenate([w, z], axis=-1), jnp.concatenate([z, w], axis=-1)], axis=-2)


def _reorder_w_in(w_in):
    cuts = [(1024, 2560), (2576, 4112), (0, 1024), (4112, 4624), (4632, 5144), (5144, 5912),
            (2560, 2576), (4624, 4632), (5912, 5936)]
    wt = jnp.transpose(w_in, (0, 2, 1))
    parts = [wt[:, a:b] for a, b in cuts]
    used = sum(b - a for a, b in cuts)
    parts.append(jnp.zeros((w_in.shape[0], PROJ_W - used, w_in.shape[1]), w_in.dtype))
    return jnp.concatenate(parts, axis=1).astype(bf16)


def _mix_prompt(x3, i, p, lp, w_r):
    B, L, _ = x3.shape
    n = B * L
    x = x3.reshape(n, D_MODEL)
    proj = _inproj(x, p["norm_mix"], w_r, i)
    proj3 = proj.reshape(B, L, PROJ_W)
    y_ssd, st_ssd, cst_ssd = _ssd_prompt(proj3, lp)
    y_gdn, st_gdn, cst_gdn = _gdn_prompt(proj3, lp)
    qt, kv6_3, kv6t, gt = _prep_t(proj3, lp["q_gain"], lp["k_gain"])
    kc, vct = _cmp_prompt(kv6_3, lp)
    y_nsa = _nsa_prompt(qt, kv6_3, kv6t, kc, vct, gt, p["rel_bias"])
    x = _outproj(x, y_ssd.reshape(n, D_SSD), y_gdn.reshape(n, D_GDN), y_nsa.reshape(n, D_NSA), p["w_out"], i)
    rows = lambda off: jnp.transpose(kv6t[:, off:off + 128].reshape(B, NSA_KV, NSA_HD, L), (0, 3, 1, 2))
    keep = min(WINDOW, L)
    state = (st_ssd, cst_ssd, st_gdn, cst_gdn, rows(KV_CK), rows(KV_CV), rows(KV_SK), rows(KV_SV),
             rows(KV_WK)[:, L - keep:], rows(KV_WV)[:, L - keep:])
    return x.reshape(B, L, D_MODEL), state


def _mix_decode(x, i, p, lp, w_r, st):
    B = x.shape[0]
    proj = _inproj(x, p["norm_mix"], w_r, i)
    proj3 = proj.reshape(B, 1, PROJ_W)
    y_ssd, st_ssd, cst_ssd = _ssd_decode(proj3, st["ssd_conv"], st["ssd"], i, lp)
    y_gdn, st_gdn, cst_gdn = _gdn_decode(proj3, st["gdn_conv"], st["gdn"], i, lp)
    qn, kv6 = _prep(proj, lp["q_gain"], lp["k_gain"])
    qn3, kv6_3 = qn.reshape(B, 1, D_NSA), kv6.reshape(B, 1, 768)
    o_cmp, idx = _nsa_dec_cmp(qn3, st["page_table"], st["cmp_k"], st["cmp_v"], i, lp)
    y_nsa, win_k, win_v = _nsa_dec_sel(qn3, kv6_3, proj3, o_cmp, idx[:, :, :NSA_KV], st["page_table"],
                                       st["win_k"], st["win_v"], st["slc_k"], st["slc_v"], i, lp)
    x = _outproj(x, y_ssd.reshape(B, D_SSD), y_gdn.reshape(B, D_GDN), y_nsa.reshape(B, D_NSA), p["w_out"], i)
    rows = lambda off: kv6[:, off:off + 128].reshape(B, 1, NSA_KV, NSA_HD)
    n_buf = win_k.shape[2]
    buf = lambda w: jnp.transpose(w.reshape(B, NSA_KV, NSA_HD, n_buf), (0, 3, 1, 2))
    state = (st_ssd, cst_ssd, st_gdn, cst_gdn, rows(KV_CK), rows(KV_CV), rows(KV_SK), rows(KV_SV),
             buf(win_k), buf(win_v))
    return x, state


def kernel(x_prompt, x_sample, state_ssd, state_ssd_conv, state_gdn, state_gdn_conv, cache_cmp_k, cache_cmp_v,
           cache_slc_k, cache_slc_v, cache_win_k, cache_win_v, page_table, rel_bias, norm_ffn1, w_ffn1_gate,
           w_ffn1_up, w_ffn1_down, norm_mix, w_in, ssd_conv_w, ssd_conv_b, ssd_dt_bias, ssd_a_log, ssd_d, ssd_norm,
           gdn_conv_w, gdn_dt_bias, gdn_a_log, gdn_norm, nsa_q_norm, nsa_k_norm, nsa_cmp_pe, nsa_cmp_w1,
           nsa_cmp_w2, w_out, norm_ffn2, w_ffn2_gate, w_ffn2_up, w_ffn2_down):
    bp, lp_len, _ = x_prompt.shape
    bs = x_sample.shape[0]
    gain3 = lambda g: g.reshape(DEPTH, 1, D_MODEL)
    p = dict(rel_bias=rel_bias, norm_mix=gain3(norm_mix), w_out=w_out.astype(bf16), ssd_conv_w=ssd_conv_w,
             ssd_conv_b=ssd_conv_b,
             ssd_dt_bias=ssd_dt_bias, ssd_a_log=ssd_a_log, ssd_d=ssd_d, ssd_norm=ssd_norm, gdn_conv_w=gdn_conv_w,
             gdn_dt_bias=gdn_dt_bias, gdn_a_log=gdn_a_log, gdn_norm=gdn_norm, nsa_q_norm=nsa_q_norm,
             nsa_k_norm=nsa_k_norm, nsa_cmp_pe=nsa_cmp_pe, nsa_cmp_w1=nsa_cmp_w1, nsa_cmp_w2=nsa_cmp_w2)
    n1, n2 = gain3(norm_ffn1), gain3(norm_ffn2)
    chan_row = lambda c: jnp.transpose(c, (0, 1, 3, 4, 2)).reshape(c.shape[:2] + (NSA_KV * NSA_HD, c.shape[2]))
    st = dict(ssd=state_ssd, ssd_conv=state_ssd_conv, gdn=state_gdn, gdn_conv=state_gdn_conv,
              cmp_k=chan_row(cache_cmp_k), cmp_v=chan_row(cache_cmp_v), slc_k=chan_row(cache_slc_k),
              slc_v=chan_row(cache_slc_v), win_k=chan_row(cache_win_k), win_v=chan_row(cache_win_v),
              page_table=page_table)
    w_r = _reorder_w_in(w_in)

    hp = x_prompt.reshape(bp * lp_len, D_MODEL)
    hs = x_sample.reshape(bs, D_MODEL)
    outs_p, outs_s = [], []
    for i in range(DEPTH):
        lp = _layer_params(i, p)
        hp, hs = _ffn(hp, hs, n1, w_ffn1_gate, w_ffn1_up, w_ffn1_down, i)
        hp3, st_p = _mix_prompt(hp.reshape(bp, lp_len, D_MODEL), i, p, lp, w_r)
        hs, st_s = _mix_decode(hs, i, p, lp, w_r, st)
        hp, hs = _ffn(hp3.reshape(bp * lp_len, D_MODEL), hs, n2, w_ffn2_gate, w_ffn2_up, w_ffn2_down, i)
        outs_p.append(st_p)
        outs_s.append(st_s)
    stack = lambda outs: [jnp.stack(t) for t in zip(*outs)]
    return (hp.reshape(bp, lp_len, D_MODEL), hs.reshape(bs, 1, D_MODEL), *stack(outs_p), *stack(outs_s))
```

```python
import functools
import math

import numpy as np
import jax
import jax.numpy as jnp
from jax import lax
from jax.experimental import pallas as pl
from jax.experimental.pallas import tpu as pltpu

f32 = jnp.float32
bf16 = jnp.bfloat16
i32 = jnp.int32
HI = lax.Precision.HIGHEST

D_MODEL = 2048
DEPTH = 4
PAGE = 128
D_SSD = 1024
SSD_HD = 64
SSD_HEADS = 16
SSD_GROUPS = 2
SSD_STATE = 128
SSD_CHUNK = 128
D_GDN = 512
GDN_HD = 128
GDN_HEADS = 4
GDN_CHUNK = 64
GDN_STEP_CHUNKS = 4
D_NSA = 512
NSA_HD = 64
NSA_HEADS = 8
NSA_KV = 2
NSA_REP = 4
CMP_STRIDE = 16
CMP_LEN = 32
CMP_HID = 128
SLC_BLOCK = 64
SLC_TOPN = 16
WINDOW = 512
N_BUCKETS = 32
MAX_DISTANCE = 128
CONV_W = 4
D_FF = 5632
SSD_CONV_CH = 1536
GDN_CONV_CH = 1536
EPS = 1e-6
NEG = -1e30

XBC_OFF = 0
QKV_OFF = 1536
ZS_OFF = 3072
ZG_OFF = 4096
QN_OFF = 4608
KV_OFF = 5120
SM_OFF = 5888
PROJ_W = 6144
SM_DT, SM_A, SM_B, SM_G = 0, 16, 20, 24
KV_CK, KV_CV, KV_SK, KV_SV, KV_WK, KV_WV = 0, 128, 256, 384, 512, 640

VMEM_LIMIT = 56 * 1024 * 1024
ROW_TILE = 1024
FFN_COL_TILE = 256
PROJ_COL_TILE = 512
OUT_ROW_TILE = 2048
PREP_ROW_TILE = 512
NSA_Q_TILE = 128
TB_FAR, TB_PREV, TB_DIAG, TB_WIN0, TB_NONE, TB_KINDS = 0, 1, 2, 3, 4, 5


def _bucket_thresholds():
    exact = N_BUCKETS // 2
    d = np.arange(0, 4 * MAX_DISTANCE)
    nf = np.maximum(d, 1).astype(np.float32)
    large = exact + (np.log(nf / np.float32(exact)) / np.float32(math.log(MAX_DISTANCE / exact))
                     * np.float32(N_BUCKETS - exact)).astype(np.int32)
    bucket = np.where(d < exact, d, np.minimum(large, N_BUCKETS - 1))
    return [int(np.argmax(bucket >= b)) for b in range(N_BUCKETS)]


_THR = _bucket_thresholds()
BIAS_FAR = _THR[N_BUCKETS - 1]


def _bias_chain(d, tabcols):
    v = jnp.broadcast_to(tabcols[N_BUCKETS - 1], d.shape)
    for b in range(N_BUCKETS - 2, -1, -1):
        v = jnp.where(d < _THR[b + 1], tabcols[b], v)
    return v


def _silu(x):
    return x * jax.nn.sigmoid(x)


def _softplus(x):
    return jnp.maximum(x, 0.0) + jnp.log1p(jnp.exp(-jnp.abs(x)))


def _dot(a, b, prec=None):
    if prec is None:
        a, b = a.astype(bf16), b.astype(bf16)
    return jnp.dot(a, b, preferred_element_type=f32, precision=prec)


def _dot_nt(a, b, prec=None):
    if prec is None:
        a, b = a.astype(bf16), b.astype(bf16)
    return lax.dot_general(a, b, (((1,), (1,)), ((), ())), preferred_element_type=f32, precision=prec)


def _dot_tn(a, b, prec=None):
    if prec is None:
        a, b = a.astype(bf16), b.astype(bf16)
    return lax.dot_general(a, b, (((0,), (0,)), ((), ())), preferred_element_type=f32, precision=prec)


def _split2(a):
    hi = a.astype(bf16)
    return hi, (a - hi.astype(f32)).astype(bf16)


def _dot3(a2, b2):
    d = lambda x, y: jnp.dot(x, y, preferred_element_type=f32)
    return d(a2[0], b2[0]) + (d(a2[0], b2[1]) + d(a2[1], b2[0]))


def _split3(a):
    a1 = a.astype(bf16)
    r = a - a1.astype(f32)
    a2 = r.astype(bf16)
    return a1, a2, (r - a2.astype(f32)).astype(bf16)


def _dot_01(a, m01):
    m = m01.astype(bf16)
    return sum(jnp.dot(t, m, preferred_element_type=f32) for t in _split3(a))


def _dot_01l(m01, b):
    m = m01.astype(bf16)
    return sum(jnp.dot(m, t, preferred_element_type=f32) for t in _split3(b))


def _iota(shape, dim):
    return lax.broadcasted_iota(i32, shape, dim)


def _group_ones(n, width):
    return (_iota((n, n), 0) // width == _iota((n, n), 1) // width).astype(f32)


def _group_rms(x, width):
    ss = _dot_01(x * x, _group_ones(x.shape[1], width))
    return x * lax.rsqrt(ss * (1.0 / width) + EPS)


def _params(sem):
    return pltpu.CompilerParams(dimension_semantics=sem, vmem_limit_bytes=VMEM_LIMIT)


FFN_EXTRA = 16


def _ffn_kernel(x_ref, xs_ref, g_ref, wg_ref, wu_ref, wd_ref, o_ref, os_ref, h_ref):
    tm, ns = x_ref.shape[0], xs_ref.shape[0]

    @pl.when(pl.program_id(1) == 0)
    def _():
        rms = lambda x: x * lax.rsqrt(jnp.mean(x * x, axis=-1, keepdims=True) + EPS) * g_ref[...]
        x, xs = x_ref[...], xs_ref[...]
        h_ref[0:tm, :] = rms(x).astype(bf16)
        h_ref[tm:tm + FFN_EXTRA, :] = jnp.concatenate(
            [rms(xs), jnp.zeros((FFN_EXTRA - ns, D_MODEL), f32)], axis=0).astype(bf16)
        o_ref[...] = x
        os_ref[...] = xs

    h = h_ref[...]
    a = jnp.dot(h, wg_ref[...].astype(bf16), preferred_element_type=f32)
    u = jnp.dot(h, wu_ref[...].astype(bf16), preferred_element_type=f32)
    y = 0.5 * _dot(_silu(a) * u, wd_ref[...])
    o_ref[...] += y[0:tm]
    os_ref[...] += y[tm:tm + ns]


def _ffn(x, xs, gain, wg, wu, wd, layer):
    n, ns = x.shape[0], xs.shape[0]
    tm, tf = ROW_TILE, FFN_COL_TILE
    return pl.pallas_call(
        _ffn_kernel,
        grid=(n // tm, D_FF // tf),
        in_specs=[
            pl.BlockSpec((tm, D_MODEL), lambda i, j: (i, 0)),
            pl.BlockSpec((ns, D_MODEL), lambda i, j: (0, 0)),
            pl.BlockSpec((None, 1, D_MODEL), lambda i, j: (layer, 0, 0)),
            pl.BlockSpec((None, D_MODEL, tf), lambda i, j: (layer, 0, j)),
            pl.BlockSpec((None, D_MODEL, tf), lambda i, j: (layer, 0, j)),
            pl.BlockSpec((None, tf, D_MODEL), lambda i, j: (layer, j, 0)),
        ],
        out_specs=[pl.BlockSpec((tm, D_MODEL), lambda i, j: (i, 0)),
                   pl.BlockSpec((ns, D_MODEL), lambda i, j: (0, 0))],
        out_shape=[jax.ShapeDtypeStruct((n, D_MODEL), f32), jax.ShapeDtypeStruct((ns, D_MODEL), f32)],
        scratch_shapes=[pltpu.VMEM((tm + FFN_EXTRA, D_MODEL), bf16)],
        compiler_params=_params(("arbitrary", "arbitrary")),
        name="ffn",
    )(x, xs, gain, wg, wu, wd)


def _inproj_kernel(x_ref, g_ref, w_ref, o_ref, h_ref):
    @pl.when(pl.program_id(1) == 0)
    def _():
        x = x_ref[...]
        h = x * lax.rsqrt(jnp.mean(x * x, axis=-1, keepdims=True) + EPS) * g_ref[...]
        h_ref[...] = h.astype(bf16)

    o_ref[...] = _dot_nt(h_ref[...], w_ref[...])


def _inproj(x, gain, w_r, layer):
    n = x.shape[0]
    tm = min(n, ROW_TILE)
    tn = PROJ_COL_TILE
    return pl.pallas_call(
        _inproj_kernel,
        grid=(n // tm, PROJ_W // tn),
        in_specs=[
            pl.BlockSpec((tm, D_MODEL), lambda i, j: (i, 0)),
            pl.BlockSpec((None, 1, D_MODEL), lambda i, j: (layer, 0, 0)),
            pl.BlockSpec((None, tn, D_MODEL), lambda i, j: (layer, j, 0)),
        ],
        out_specs=pl.BlockSpec((tm, tn), lambda i, j: (i, j)),
        out_shape=jax.ShapeDtypeStruct((n, PROJ_W), f32),
        scratch_shapes=[pltpu.VMEM((tm, D_MODEL), bf16)],
        compiler_params=_params(("parallel", "arbitrary")),
        name="inproj",
    )(x, gain, w_r)


def _outproj_kernel(x_ref, ys_ref, yg_ref, yn_ref, ws_ref, wg_ref, wn_ref, o_ref):
    acc = _dot(ys_ref[...], ws_ref[...])
    acc += _dot(yg_ref[...], wg_ref[...])
    acc += _dot(yn_ref[...], wn_ref[...])
    o_ref[...] = x_ref[...] + acc


def _outproj(x, ys, yg, yn, w_out, layer):
    n = x.shape[0]
    tm = min(n, OUT_ROW_TILE)
    tn = PROJ_COL_TILE
    return pl.pallas_call(
        _outproj_kernel,
        grid=(n // tm, D_MODEL // tn),
        in_specs=[
            pl.BlockSpec((tm, tn), lambda i, j: (i, j)),
            pl.BlockSpec((tm, D_SSD), lambda i, j: (i, 0)),
            pl.BlockSpec((tm, D_GDN), lambda i, j: (i, 0)),
            pl.BlockSpec((tm, D_NSA), lambda i, j: (i, 0)),
            pl.BlockSpec((None, D_SSD, tn), lambda i, j: (layer, 0, j)),
            pl.BlockSpec((None, D_GDN, tn), lambda i, j: (layer, 2, j)),
            pl.BlockSpec((None, D_NSA, tn), lambda i, j: (layer, 3, j)),
        ],
        out_specs=pl.BlockSpec((tm, tn), lambda i, j: (i, j)),
        out_shape=jax.ShapeDtypeStruct((n, D_MODEL), f32),
        compiler_params=_params(("parallel", "arbitrary")),
        name="outproj",
    )(x, ys, yg, yn, w_out, w_out, w_out)


def _prep_kernel(q_ref, kv_ref, qg_ref, kg_ref, qn_ref, kv6_ref):
    for s in range(D_NSA // 128):
        sl = slice(s * 128, (s + 1) * 128)
        qn_ref[:, sl] = _group_rms(q_ref[:, sl], NSA_HD) * qg_ref[...]
    kv = kv_ref[...]
    kv6_ref[...] = kv[:, :6 * 128]
    kv6_ref[:, KV_SK:KV_SK + 128] = _group_rms(kv[:, KV_SK:KV_SK + 128], NSA_HD) * kg_ref[1:2, :]
    kv6_ref[:, KV_WK:KV_WK + 128] = _group_rms(kv[:, KV_WK:KV_WK + 128], NSA_HD) * kg_ref[2:3, :]


def _prep(proj, qgain, kgain):
    n = proj.shape[0]
    tm = min(n, PREP_ROW_TILE)
    return pl.pallas_call(
        _prep_kernel,
        grid=(n // tm,),
        in_specs=[
            pl.BlockSpec((tm, D_NSA), lambda i: (i, QN_OFF // D_NSA)),
            pl.BlockSpec((tm, 1024), lambda i: (i, KV_OFF // 1024)),
            pl.BlockSpec((1, 128), lambda i: (0, 0)),
            pl.BlockSpec((3, 128), lambda i: (0, 0)),
        ],
        out_specs=[pl.BlockSpec((tm, D_NSA), lambda i: (i, 0)),
                   pl.BlockSpec((tm, 768), lambda i: (i, 0))],
        out_shape=[jax.ShapeDtypeStruct((n, D_NSA), f32), jax.ShapeDtypeStruct((n, 768), f32)],
        compiler_params=_params(("parallel",)),
        name="nsa_prep",
    )(proj, proj, qgain, kgain)


def _ssd_prompt_kernel(xbc_ref, z_ref, sm_ref, cw_ref, cb_ref, dtb_ref, alog_ref, dexp_ref, nrm_ref,
                       y_ref, st_ref, cst_ref, xbuf, hst, ybuf):
    c = pl.program_id(1)
    cl = SSD_CHUNK

    @pl.when(c == 0)
    def _():
        xbuf[0:8, :] = jnp.zeros((8, SSD_CONV_CH), f32)
        hst[...] = jnp.zeros(hst.shape, f32)

    x = xbc_ref[...]
    xbuf[8:8 + cl, :] = x
    conv = cw_ref[3:4, :] * x
    for k in range(1, CONV_W):
        conv += cw_ref[3 - k:4 - k, :] * xbuf[pl.ds(8 - k, cl), :]
    xbuf[0:8, :] = x[cl - 8:cl, :]
    xc = _silu(conv + cb_ref[...])
    xs = xc[:, :D_SSD]

    dt = jnp.where(_iota((cl, 128), 1) < SSD_HEADS, _softplus(sm_ref[...] + dtb_ref[...]), 0.0)
    da = dt * (-jnp.exp(alog_ref[...]))
    row, col = _iota((cl, cl), 0), _iota((cl, cl), 1)
    causal = row >= col
    cum = _dot_01l(causal, da)
    cum_t = cum.T
    last = cum[cl - 1:cl, :]
    e_last = jnp.exp(last)
    spread = (_iota((128, D_SSD), 1) // SSD_HD == _iota((128, D_SSD), 0)).astype(f32)
    coef = _dot_01(jnp.concatenate([dt, jnp.exp(last - cum), jnp.exp(cum)], axis=0), spread)
    xdt = xs * coef[0:cl]
    xw = xdt * coef[cl:2 * cl]
    e_cum = coef[2 * cl:3 * cl]

    hpg = SSD_HEADS // SSD_GROUPS
    gw = hpg * SSD_HD
    for g in range(SSD_GROUPS):
        bm = xc[:, D_SSD + g * SSD_STATE:D_SSD + (g + 1) * SSD_STATE]
        cm = xc[:, D_SSD + (SSD_GROUPS + g) * SSD_STATE:D_SSD + (SSD_GROUPS + g + 1) * SSD_STATE]
        cb = _dot_nt(cm, bm)
        h0 = hst[g]
        y_off = _dot_nt(cm, h0) * e_cum[:, g * gw:(g + 1) * gw]
        s_new = _dot_tn(xw[:, g * gw:(g + 1) * gw], bm)
        for r in range(hpg):
            h = g * hpg + r
            seg = cum[:, h:h + 1] - cum_t[h:h + 1, :]
            decay = jnp.where(causal, jnp.exp(jnp.where(causal, seg, 0.0)), 0.0)
            rs = slice(r * SSD_HD, (r + 1) * SSD_HD)
            ybuf[:, h * SSD_HD:(h + 1) * SSD_HD] = (_dot(cb * decay, xdt[:, h * SSD_HD:(h + 1) * SSD_HD])
                                                    + y_off[:, rs])
            hst[g, rs, :] = e_last[:, h:h + 1] * h0[rs, :] + s_new[rs, :]

    y = (ybuf[...] + dexp_ref[...] * xs) * _silu(z_ref[...])
    gw = D_SSD // SSD_GROUPS
    for g in range(SSD_GROUPS):
        yg = y[:, g * gw:(g + 1) * gw]
        yg = yg * lax.rsqrt(jnp.mean(yg * yg, axis=-1, keepdims=True) + EPS)
        y_ref[:, g * gw:(g + 1) * gw] = (yg * nrm_ref[:, g * gw:(g + 1) * gw]).astype(y_ref.dtype)

    @pl.when(c == pl.num_programs(1) - 1)
    def _():
        st_ref[...] = hst[...]
        cst_ref[...] = x[cl - (CONV_W - 1):cl, :]


def _ssd_prompt(proj3, lp):
    B, L, _ = proj3.shape
    cl = SSD_CHUNK
    gh = SSD_HEADS // SSD_GROUPS * SSD_HD
    vec = lambda w: pl.BlockSpec((1, w), lambda b, c: (0, 0))
    y, st, cst = pl.pallas_call(
        _ssd_prompt_kernel,
        grid=(B, L // cl),
        in_specs=[
            pl.BlockSpec((None, cl, SSD_CONV_CH), lambda b, c: (b, c, XBC_OFF // SSD_CONV_CH)),
            pl.BlockSpec((None, cl, D_SSD), lambda b, c: (b, c, ZS_OFF // D_SSD)),
            pl.BlockSpec((None, cl, 128), lambda b, c: (b, c, SM_OFF // 128)),
            pl.BlockSpec((CONV_W, SSD_CONV_CH), lambda b, c: (0, 0)),
            vec(SSD_CONV_CH), vec(128), vec(128), vec(D_SSD), vec(D_SSD),
        ],
        out_specs=[
            pl.BlockSpec((None, cl, D_SSD), lambda b, c: (b, c, 0)),
            pl.BlockSpec((None, SSD_GROUPS, gh, SSD_STATE), lambda b, c: (b, 0, 0, 0)),
            pl.BlockSpec((None, CONV_W - 1, SSD_CONV_CH), lambda b, c: (b, 0, 0)),
        ],
        out_shape=[
            jax.ShapeDtypeStruct((B, L, D_SSD), bf16),
            jax.ShapeDtypeStruct((B, SSD_GROUPS, gh, SSD_STATE), f32),
            jax.ShapeDtypeStruct((B, CONV_W - 1, SSD_CONV_CH), f32),
        ],
        scratch_shapes=[
            pltpu.VMEM((8 + cl, SSD_CONV_CH), f32),
            pltpu.VMEM((SSD_GROUPS, gh, SSD_STATE), f32),
            pltpu.VMEM((cl, D_SSD), f32),
        ],
        compiler_params=_params(("parallel", "arbitrary")),
        name="ssd_prompt",
    )(proj3, proj3, proj3, lp["ssd_conv_w"], lp["ssd_conv_b"], lp["ssd_dtb"], lp["ssd_alog"],
      lp["ssd_dexp"], lp["ssd_norm"])
    return y, st.reshape(B, SSD_HEADS, SSD_HD, SSD_STATE), cst


def _gdn_prompt_kernel(qkv_ref, z_ref, sm_ref, cw_ref, dtb_ref, alog_ref, nrm_ref,
                       y_ref, st_ref, cst_ref, xbuf, sst):
    c = pl.program_id(1)
    cl = GDN_CHUNK
    nch = GDN_STEP_CHUNKS
    tl = nch * cl

    @pl.when(c == 0)
    def _():
        xbuf[0:8, :] = jnp.zeros((8, GDN_CONV_CH), f32)
        sst[...] = jnp.zeros(sst.shape, f32)

    x = qkv_ref[...]
    xbuf[8:8 + tl, :] = x
    conv = cw_ref[3:4, :] * x
    for k in range(1, CONV_W):
        conv += cw_ref[3 - k:4 - k, :] * xbuf[pl.ds(8 - k, tl), :]
    xbuf[0:8, :] = x[tl - 8:tl, :]
    xc = _silu(conv)

    sm = sm_ref[...]
    beta_all = jax.nn.sigmoid(sm)
    g_all = -jnp.exp(alog_ref[...]) * _softplus(sm + dtb_ref[...])
    row, col = _iota((cl, cl), 0), _iota((cl, cl), 1)
    incl = row >= col
    strict = row > col
    eye = (row == col).astype(f32)
    rt, ct = _iota((tl, tl), 0), _iota((tl, tl), 1)
    cum = _dot_01l((rt >= ct) & (rt // cl == ct // cl), g_all)
    cum_t = cum.T

    units = [(ci, h) for ci in range(nch) for h in range(GDN_HEADS)]
    q, k, v, beta, cum_c, decay = {}, {}, {}, {}, {}, {}
    for u in units:
        ci, h = u
        rows = slice(ci * cl, (ci + 1) * cl)
        qh = xc[rows, h * GDN_HD:(h + 1) * GDN_HD]
        kh = xc[rows, D_GDN + h * GDN_HD:D_GDN + (h + 1) * GDN_HD]
        v[u] = xc[rows, 2 * D_GDN + h * GDN_HD:2 * D_GDN + (h + 1) * GDN_HD]
        q[u] = qh * lax.rsqrt(jnp.sum(qh * qh, axis=-1, keepdims=True) + EPS) * (GDN_HD ** -0.5)
        k[u] = kh * lax.rsqrt(jnp.sum(kh * kh, axis=-1, keepdims=True) + EPS)
        beta[u] = beta_all[rows, SM_B + h:SM_B + h + 1]
        cum_c[u] = cum[rows, SM_A + h:SM_A + h + 1]
        seg = cum_c[u] - cum_t[SM_A + h:SM_A + h + 1, rows]
        decay[u] = jnp.where(incl, jnp.exp(jnp.where(incl, seg, 0.0)), 0.0)
    pw = {u: -jnp.where(strict, beta[u] * _dot_nt(k[u], k[u]) * decay[u], 0.0) for u in units}
    inv = {u: eye + pw[u] for u in units}
    pw2 = {u: _split2(pw[u]) for u in units}
    for _ in range(5):
        pw2 = {u: _split2(_dot3(pw2[u], pw2[u])) for u in units}
        inv = {u: inv[u] + _dot3(_split2(inv[u]), pw2[u]) for u in units}
    rhs = {u: jnp.concatenate([beta[u] * v[u], (beta[u] * jnp.exp(cum_c[u])) * k[u]], axis=1) for u in units}
    sol = {u: _dot3(_split2(inv[u]), _split2(rhs[u])) for u in units}
    uu = {u: sol[u][:, :GDN_HD] for u in units}
    ww = {u: sol[u][:, GDN_HD:] for u in units}
    qk = {u: _dot_nt(q[u], k[u]) * decay[u] for u in units}

    s = [sst[h] for h in range(GDN_HEADS)]
    for ci in range(nch):
        rows = slice(ci * cl, (ci + 1) * cl)
        hs = [(ci, h) for h in range(GDN_HEADS)]
        v_new = {u: uu[u] - _dot(ww[u], s[u[1]]) for u in hs}
        o = {u: _dot(q[u] * jnp.exp(cum_c[u]), s[u[1]]) + _dot(qk[u], v_new[u]) for u in hs}
        for u in hs:
            h = u[1]
            last = cum[ci * cl + cl - 1:ci * cl + cl, SM_A + h:SM_A + h + 1]
            s[h] = jnp.exp(last) * s[h] + _dot_tn(k[u] * jnp.exp(last - cum_c[u]), v_new[u])
        for u in hs:
            sl = slice(u[1] * GDN_HD, (u[1] + 1) * GDN_HD)
            on = o[u] * lax.rsqrt(jnp.mean(o[u] * o[u], axis=-1, keepdims=True) + EPS)
            y_ref[rows, sl] = (on * nrm_ref[...] * _silu(z_ref[rows, sl])).astype(y_ref.dtype)
    for h in range(GDN_HEADS):
        sst[h] = s[h]

    @pl.when(c == pl.num_programs(1) - 1)
    def _():
        st_ref[...] = sst[...]
        cst_ref[...] = x[tl - (CONV_W - 1):tl, :]


def _gdn_prompt(proj3, lp):
    B, L, _ = proj3.shape
    cl = GDN_CHUNK * GDN_STEP_CHUNKS
    vec = lambda w: pl.BlockSpec((1, w), lambda b, c: (0, 0))
    return pl.pallas_call(
        _gdn_prompt_kernel,
        grid=(B, L // cl),
        in_specs=[
            pl.BlockSpec((None, cl, GDN_CONV_CH), lambda b, c: (b, c, QKV_OFF // GDN_CONV_CH)),
            pl.BlockSpec((None, cl, D_GDN), lambda b, c: (b, c, ZG_OFF // D_GDN)),
            pl.BlockSpec((None, cl, 128), lambda b, c: (b, c, SM_OFF // 128)),
            pl.BlockSpec((CONV_W, GDN_CONV_CH), lambda b, c: (0, 0)),
            vec(128), vec(128), vec(GDN_HD),
        ],
        out_specs=[
            pl.BlockSpec((None, cl, D_GDN), lambda b, c: (b, c, 0)),
            pl.BlockSpec((None, GDN_HEADS, GDN_HD, GDN_HD), lambda b, c: (b, 0, 0, 0)),
            pl.BlockSpec((None, CONV_W - 1, GDN_CONV_CH), lambda b, c: (b, 0, 0)),
        ],
        out_shape=[
            jax.ShapeDtypeStruct((B, L, D_GDN), bf16),
            jax.ShapeDtypeStruct((B, GDN_HEADS, GDN_HD, GDN_HD), f32),
            jax.ShapeDtypeStruct((B, CONV_W - 1, GDN_CONV_CH), f32),
        ],
        scratch_shapes=[
            pltpu.VMEM((8 + cl, GDN_CONV_CH), f32),
            pltpu.VMEM((GDN_HEADS, GDN_HD, GDN_HD), f32),
        ],
        compiler_params=_params(("parallel", "arbitrary")),
        name="gdn_prompt",
    )(proj3, proj3, proj3, lp["gdn_conv_w"], lp["gdn_dtb"], lp["gdn_alog"], lp["gdn_norm"])


def _cmp_prompt_kernel(k_ref, v_ref, pe_ref, w1_ref, w2_ref, kg_ref, kc_ref, vct_ref):
    nb = kc_ref.shape[0]
    for t, src in enumerate((k_ref, v_ref)):
        outs = []
        p1 = [jnp.zeros((nb, CMP_HID), f32) for _ in range(NSA_KV)]
        p2 = [jnp.zeros((nb, CMP_HID), f32) for _ in range(NSA_KV)]
        for l in range(CMP_STRIDE):
            xr = src[pl.ds(l, nb, stride=CMP_STRIDE), :]
            l2 = CMP_STRIDE + l
            for g in range(NSA_KV):
                xg = xr[:, g * NSA_HD:(g + 1) * NSA_HD]
                p1[g] += _dot(xg + pe_ref[t, l:l + 1, :], w1_ref[t, l * NSA_HD:(l + 1) * NSA_HD, :])
                p2[g] += _dot(xg + pe_ref[t, l2:l2 + 1, :], w1_ref[t, l2 * NSA_HD:(l2 + 1) * NSA_HD, :])
        for g in range(NSA_KV):
            hid = p1[g] + pltpu.roll(p2[g], nb - 1, 0)
            cmp = _dot(_silu(hid), w2_ref[t])
            if t == 0:
                cmp = cmp * lax.rsqrt(jnp.mean(cmp * cmp, axis=-1, keepdims=True) + EPS) * kg_ref[0:1, 0:NSA_HD]
            outs.append(cmp)
        both = jnp.concatenate(outs, axis=1)
        if t == 0:
            kc_ref[...] = both
        else:
            vct_ref[...] = both.T


def _cmp_prompt(kv6_3, lp):
    B, L, _ = kv6_3.shape
    nb = L // CMP_STRIDE
    full = lambda s: pl.BlockSpec(s, lambda b: (0,) * len(s))
    return pl.pallas_call(
        _cmp_prompt_kernel,
        grid=(B,),
        in_specs=[
            pl.BlockSpec((None, L, 128), lambda b: (b, 0, KV_CK // 128)),
            pl.BlockSpec((None, L, 128), lambda b: (b, 0, KV_CV // 128)),
            full((2, CMP_LEN, NSA_HD)), full((2, CMP_LEN * NSA_HD, CMP_HID)), full((2, CMP_HID, NSA_HD)),
            full((3, 128)),
        ],
        out_specs=[pl.BlockSpec((None, nb, 128), lambda b: (b, 0, 0)),
                   pl.BlockSpec((None, 128, nb), lambda b: (b, 0, 0))],
        out_shape=[jax.ShapeDtypeStruct((B, nb, 128), f32), jax.ShapeDtypeStruct((B, 128, nb), f32)],
        compiler_params=_params(("parallel",)),
        name="nsa_cmp_prompt",
    )(kv6_3, kv6_3, lp["cmp_pe"], lp["cmp_w1"], lp["cmp_w2"], lp["k_gain"])


def _nsa_prompt_kernel(rb_ref, qt_ref, ks_ref, kw_ref, vst_ref, vwt_ref, kc_ref, vct_ref, gt_ref,
                       y_ref, tb_ref, tc_ref, ext_ref, selm_ref, s_ref):
    qi = pl.program_id(1)
    L = ks_ref.shape[0]
    tq = NSA_Q_TILE
    nl = NSA_REP * tq
    ncmp = kc_ref.shape[0]
    n_slc = L // SLC_BLOCK
    last = N_BUCKETS - 1

    @pl.when(qi == 0)
    def _():
        sub = _iota((128, tq), 0)
        qo = _iota((128, tq), 1)
        d_diag = qo - sub
        c_rel = jnp.where(sub < 64, sub, sub - 128)
        d_cmp = qo - CMP_STRIDE * c_rel - (CMP_LEN - 1)
        for g in range(NSA_KV):
            for r in range(NSA_REP):
                h = g * NSA_REP + r
                tab = [rb_ref[b, h] - rb_ref[last, h] for b in range(N_BUCKETS)]
                lanes = slice(r * tq, (r + 1) * tq)
                tb_ref[g, TB_FAR, :, lanes] = jnp.zeros((128, tq), f32)
                tb_ref[g, TB_PREV, :, lanes] = _bias_chain(d_diag + 128, tab)
                tb_ref[g, TB_DIAG, :, lanes] = jnp.where(d_diag >= 0, _bias_chain(d_diag, tab), NEG)
                tb_ref[g, TB_WIN0, :, lanes] = jnp.where(d_diag <= 0, 0.0, NEG)
                tb_ref[g, TB_NONE, :, lanes] = jnp.full((128, tq), NEG, f32)
                t_c = jnp.where(d_cmp < 0, 0.0, _bias_chain(d_cmp, tab))
                tc_ref[g, 0:128, lanes] = t_c
                tc_ref[g, 128:256, lanes] = t_c
        ext_ref[...] = (_iota((L, 128), 0) // SLC_BLOCK == _iota((L, 128), 1)).astype(bf16)

    q0 = qi * tq
    qpos = q0 + _iota((1, nl), 1) % tq
    qpos1 = q0 + _iota((1, tq), 1)
    sub = _iota((128, 1), 0)
    gt = gt_ref[...]

    jr = _iota((n_slc, tq), 0)
    cur = qpos1 // SLC_BLOCK
    valid = jr * SLC_BLOCK <= qpos1
    forced = (jr == 0) | (jr == cur) | (jr == cur - 1)
    j_i = _iota((n_slc, ncmp), 0)
    c_i = _iota((n_slc, ncmp), 1)
    ovl = ((c_i * CMP_STRIDE < j_i * SLC_BLOCK + SLC_BLOCK) & (c_i * CMP_STRIDE + CMP_LEN > j_i * SLC_BLOCK)
           & (c_i < ncmp - 1)).astype(f32)
    cmp_off = pl.multiple_of((128 - qi * (tq // CMP_STRIDE)) % 128, 8)

    groups = range(NSA_KV)
    gsl = [slice(g * NSA_HD, (g + 1) * NSA_HD) for g in groups]
    qt = [jnp.concatenate([qt_ref[(g * NSA_REP + r) * NSA_HD:(g * NSA_REP + r + 1) * NSA_HD, :]
                           for r in range(NSA_REP)], axis=1).astype(bf16) for g in groups]

    ok_c = (sub * CMP_STRIDE + (CMP_LEN - 1) <= qpos) & (sub < ncmp - 1)
    o_cmp = []
    for g in groups:
        st = _dot(kc_ref[:, gsl[g]], qt[g]) + tc_ref[g, pl.ds(cmp_off, 128), :]
        st = jnp.where(ok_c, st, NEG)
        e = jnp.exp(st - jnp.max(st, axis=0, keepdims=True))
        p = jnp.where(ok_c, e / jnp.sum(e, axis=0, keepdims=True), 0.0)
        o_cmp.append(_dot(vct_ref[gsl[g], :], p))
        p_sum = p[:, 0:tq] + p[:, tq:2 * tq] + p[:, 2 * tq:3 * tq] + p[:, 3 * tq:4 * tq]
        imp = _dot_01l(ovl, p_sum)
        score = jnp.where(valid, jnp.where(forced, 1e9, imp), -1e9)
        rank = jnp.zeros((n_slc, tq), f32)
        for i in range(n_slc):
            si = score[i:i + 1, :]
            rank += ((si > score) | ((si == score) & (jr > i))).astype(f32)
        sel = (rank < SLC_TOPN).astype(bf16)
        sel = jnp.concatenate([sel, jnp.zeros((128 - n_slc, tq), bf16)], axis=0)
        selx = jnp.dot(ext_ref[...], sel, preferred_element_type=f32)
        selm_ref[g] = (selx - 1.0) * -NEG

    ones_rows = jnp.ones((16, 128), f32)

    def attend(state, tiles, k_ref, vt_ref):
        s, offs = {}, []
        for ti, (kt, table) in enumerate(tiles):
            ko = pl.multiple_of(kt * 128, 128)
            offs.append(ko)
            for g in groups:
                sg = _dot(k_ref[pl.ds(ko, 128), gsl[g]], qt[g]) + tb_ref[g, table]
                s[g, ti] = sg
        out = []
        for g in groups:
            m, acc = state[g]
            m_new = m
            for ti in range(len(tiles)):
                m_new = jnp.maximum(m_new, jnp.max(s[g, ti], axis=0, keepdims=True))
            acc = jnp.exp(m - m_new) * acc
            for ti in range(len(tiles)):
                vt = jnp.concatenate([vt_ref[gsl[g], pl.ds(offs[ti], 128)], ones_rows], axis=0)
                acc += _dot(vt, jnp.exp(s[g, ti] - m_new))
            out.append((m_new, acc))
        return tuple(out)

    finish = lambda st: [acc[0:NSA_HD] / acc[NSA_HD:NSA_HD + 1] for _, acc in st]
    init = tuple((jnp.full((1, nl), NEG, f32), jnp.zeros((NSA_HD + 16, nl), f32)) for _ in groups)

    def sel_table(kt):
        return jnp.where(kt == qi, TB_DIAG, jnp.where(kt == qi - 1, TB_PREV, TB_FAR))

    def score_tiles(kts, mx):
        mx = list(mx)
        for kt in kts:
            ko = pl.multiple_of(kt * 128, 128)
            for g in groups:
                sg = (_dot(ks_ref[pl.ds(ko, 128), gsl[g]], qt[g]) + tb_ref[g, sel_table(kt)]
                      + jnp.concatenate([selm_ref[g, pl.ds(ko, 128), :]] * NSA_REP, axis=1))
                s_ref[g, kt] = sg
                parts = [sg[8 * k:8 * k + 8, :] for k in range(16)]
                while len(parts) > 1:
                    parts = [jnp.maximum(a, b) for a, b in zip(parts[0::2], parts[1::2])]
                mx[g] = jnp.maximum(mx[g], parts[0])
        return tuple(mx)

    def value_tiles(kts, accs, m):
        accs = list(accs)
        for kt in kts:
            ko = pl.multiple_of(kt * 128, 128)
            for g in groups:
                vt = jnp.concatenate([vst_ref[gsl[g], pl.ds(ko, 128)], ones_rows], axis=0)
                accs[g] = accs[g] + _dot(vt, jnp.exp(s_ref[g, kt] - m[g]))
        return tuple(accs)

    n_pair, n_odd = (qi + 1) // 2, (qi + 1) % 2
    mx = tuple(jnp.full((8, nl), NEG, f32) for _ in groups)
    mx = lax.fori_loop(0, n_pair, lambda i, c: score_tiles([2 * i, 2 * i + 1], c), mx)
    mx = lax.fori_loop(0, n_odd, lambda i, c: score_tiles([qi], c), mx)
    m_sel = [jnp.max(x, axis=0, keepdims=True) for x in mx]
    accs = tuple(jnp.zeros((NSA_HD + 16, nl), f32) for _ in groups)
    accs = lax.fori_loop(0, n_pair, lambda i, c: value_tiles([2 * i, 2 * i + 1], c, m_sel), accs)
    accs = lax.fori_loop(0, n_odd, lambda i, c: value_tiles([qi], c, m_sel), accs)
    o_slc = [acc[0:NSA_HD] / acc[NSA_HD:NSA_HD + 1] for acc in accs]

    w0 = jnp.maximum(qi - WINDOW // 128, 0)

    def win_table(kt):
        t = jnp.where(kt == qi - WINDOW // 128, TB_WIN0, TB_FAR)
        t = jnp.where(kt == qi - 1, TB_PREV, t)
        return jnp.where(kt == qi, TB_DIAG, jnp.where(kt > qi, TB_NONE, t))

    win_tiles = [(w0 + t, win_table(w0 + t)) for t in range(WINDOW // 128 + 1)]
    o_win = finish(attend(init, win_tiles, kw_ref, vwt_ref))

    y_parts = []
    for g in groups:
        for r in range(NSA_REP):
            lanes = slice(r * tq, (r + 1) * tq)
            h = g * NSA_REP + r
            gate = lambda t: gt[SM_G + t * NSA_HEADS + h:SM_G + t * NSA_HEADS + h + 1, :]
            y_parts.append(gate(0) * o_cmp[g][:, lanes] + gate(1) * o_slc[g][:, lanes]
                           + gate(2) * o_win[g][:, lanes])
    y_ref[...] = jnp.concatenate(y_parts, axis=0).T.astype(y_ref.dtype)


def _nsa_prompt(qt, kv6_3, kv6t, kc, vct, gt, rel_bias):
    B, L, _ = kv6_3.shape
    tq = NSA_Q_TILE
    nb = L // CMP_STRIDE
    return pl.pallas_call(
        _nsa_prompt_kernel,
        grid=(B, L // tq),
        in_specs=[
            pl.BlockSpec(memory_space=pltpu.SMEM),
            pl.BlockSpec((None, D_NSA, tq), lambda b, i: (b, 0, i)),
            pl.BlockSpec((None, L, 128), lambda b, i: (b, 0, KV_SK // 128)),
            pl.BlockSpec((None, L, 128), lambda b, i: (b, 0, KV_WK // 128)),
            pl.BlockSpec((None, 128, L), lambda b, i: (b, KV_SV // 128, 0)),
            pl.BlockSpec((None, 128, L), lambda b, i: (b, KV_WV // 128, 0)),
            pl.BlockSpec((None, nb, 128), lambda b, i: (b, 0, 0)),
            pl.BlockSpec((None, 128, nb), lambda b, i: (b, 0, 0)),
            pl.BlockSpec((None, 128, tq), lambda b, i: (b, 0, i)),
        ],
        out_specs=pl.BlockSpec((None, tq, D_NSA), lambda b, i: (b, i, 0)),
        out_shape=jax.ShapeDtypeStruct((B, L, D_NSA), bf16),
        scratch_shapes=[
            pltpu.VMEM((NSA_KV, TB_KINDS, 128, NSA_REP * tq), f32),
            pltpu.VMEM((NSA_KV, 256, NSA_REP * tq), f32),
            pltpu.VMEM((L, 128), bf16),
            pltpu.VMEM((NSA_KV, L, tq), f32),
            pltpu.VMEM((NSA_KV, L // 128, 128, NSA_REP * tq), f32),
        ],
        compiler_params=_params(("parallel", "arbitrary")),
        name="nsa_prompt",
    )(rel_bias, qt, kv6_3, kv6_3, kv6t, kv6t, kc, vct, gt)


def _prep_t_kernel(q_ref, kv_ref, qg_ref, kg_ref, qt_ref, kv6_ref, kv6t_ref, gt_ref):
    for s in range(D_NSA // 128):
        sl = slice(s * 128, (s + 1) * 128)
        qt_ref[sl, :] = (_group_rms(q_ref[:, sl], NSA_HD) * qg_ref[...] * (NSA_HD ** -0.5)).T
    kv = kv_ref[...]
    for j, off in enumerate((KV_CK, KV_CV, KV_SK, KV_SV, KV_WK, KV_WV)):
        x = kv[:, off:off + 128]
        if off == KV_SK:
            x = _group_rms(x, NSA_HD) * kg_ref[1:2, :]
        if off == KV_WK:
            x = _group_rms(x, NSA_HD) * kg_ref[2:3, :]
        kv6_ref[:, off:off + 128] = x
        kv6t_ref[off:off + 128, :] = x.T
    gt_ref[...] = jax.nn.sigmoid(kv[:, SM_OFF - KV_OFF:SM_OFF - KV_OFF + 128]).T


def _prep_t(proj3, qgain, kgain):
    B, L, _ = proj3.shape
    tm = min(L, PREP_ROW_TILE)
    return pl.pallas_call(
        _prep_t_kernel,
        grid=(B, L // tm),
        in_specs=[
            pl.BlockSpec((None, tm, D_NSA), lambda b, i: (b, i, QN_OFF // D_NSA)),
            pl.BlockSpec((None, tm, 1024), lambda b, i: (b, i, KV_OFF // 1024)),
            pl.BlockSpec((1, 128), lambda b, i: (0, 0)),
            pl.BlockSpec((3, 128), lambda b, i: (0, 0)),
        ],
        out_specs=[pl.BlockSpec((None, D_NSA, tm), lambda b, i: (b, 0, i)),
                   pl.BlockSpec((None, tm, 768), lambda b, i: (b, i, 0)),
                   pl.BlockSpec((None, 768, tm), lambda b, i: (b, 0, i)),
                   pl.BlockSpec((None, 128, tm), lambda b, i: (b, 0, i))],
        out_shape=[jax.ShapeDtypeStruct((B, D_NSA, L), f32), jax.ShapeDtypeStruct((B, L, 768), f32),
                   jax.ShapeDtypeStruct((B, 768, L), f32), jax.ShapeDtypeStruct((B, 128, L), f32)],
        compiler_params=_params(("parallel", "parallel")),
        name="nsa_prep_t",
    )(proj3, proj3, qgain, kgain)


def _row0(x, rows=8):
    return jnp.where(_iota((rows, x.shape[1]), 0) == 0, x, 0.0)


def _ssd_decode_kernel(xbc_ref, z_ref, sm_ref, cprev_ref, sin_ref, cw_ref, cb_ref, dtb_ref, alog_ref, dexp_ref,
                       nrm_ref, y_ref, st_ref, cst_ref):
    x = xbc_ref[...]
    conv = cw_ref[3:4, :] * x
    for j in range(CONV_W - 1):
        conv += cw_ref[j:j + 1, :] * cprev_ref[j:j + 1, :]
    cst_ref[0:2, :] = cprev_ref[1:3, :]
    cst_ref[2:3, :] = x
    xc = _silu(conv + cb_ref[...])
    xs = xc[:, :D_SSD]
    dt = _softplus(sm_ref[...] + dtb_ref[...])
    da = dt * (-jnp.exp(alog_ref[...]))
    ys = []
    for g in range(SSD_GROUPS):
        bm = xc[:, D_SSD + g * SSD_STATE:D_SSD + (g + 1) * SSD_STATE]
        cm = xc[:, D_SSD + (SSD_GROUPS + g) * SSD_STATE:D_SSD + (SSD_GROUPS + g + 1) * SSD_STATE]
        cb = jnp.sum(cm * bm, axis=-1, keepdims=True)
        bm8, cm8 = _row0(bm), _row0(cm)
        for r in range(SSD_HEADS // SSD_GROUPS):
            h = g * (SSD_HEADS // SSD_GROUPS) + r
            xdt = xs[:, h * SSD_HD:(h + 1) * SSD_HD] * dt[:, h:h + 1]
            eda = jnp.exp(da[:, h:h + 1])
            h0 = sin_ref[h]
            ys.append(cb * xdt + eda * _dot_nt(cm8, h0, HI)[0:1])
            st_ref[h] = eda * h0 + _dot_tn(_row0(xdt), bm8, HI)
    y = (jnp.concatenate(ys, axis=1) + dexp_ref[...] * xs) * _silu(z_ref[...])
    gw = D_SSD // SSD_GROUPS
    for g in range(SSD_GROUPS):
        yg = y[:, g * gw:(g + 1) * gw]
        yg = yg * lax.rsqrt(jnp.mean(yg * yg, axis=-1, keepdims=True) + EPS)
        y_ref[:, g * gw:(g + 1) * gw] = yg * nrm_ref[:, g * gw:(g + 1) * gw]


def _ssd_decode(proj3, conv_state, state, layer, lp):
    B = proj3.shape[0]
    vec = lambda w: pl.BlockSpec((1, w), lambda b: (0, 0))
    return pl.pallas_call(
        _ssd_decode_kernel,
        grid=(B,),
        in_specs=[
            pl.BlockSpec((None, 1, SSD_CONV_CH), lambda b: (b, 0, XBC_OFF // SSD_CONV_CH)),
            pl.BlockSpec((None, 1, D_SSD), lambda b: (b, 0, ZS_OFF // D_SSD)),
            pl.BlockSpec((None, 1, 128), lambda b: (b, 0, SM_OFF // 128)),
            pl.BlockSpec((None, None, CONV_W - 1, SSD_CONV_CH), lambda b: (layer, b, 0, 0)),
            pl.BlockSpec((None, None, SSD_HEADS, SSD_HD, SSD_STATE), lambda b: (layer, b, 0, 0, 0)),
            pl.BlockSpec((CONV_W, SSD_CONV_CH), lambda b: (0, 0)),
            vec(SSD_CONV_CH), vec(128), vec(128), vec(D_SSD), vec(D_SSD),
        ],
        out_specs=[
            pl.BlockSpec((None, 1, D_SSD), lambda b: (b, 0, 0)),
            pl.BlockSpec((None, SSD_HEADS, SSD_HD, SSD_STATE), lambda b: (b, 0, 0, 0)),
            pl.BlockSpec((None, CONV_W - 1, SSD_CONV_CH), lambda b: (b, 0, 0)),
        ],
        out_shape=[
            jax.ShapeDtypeStruct((B, 1, D_SSD), f32),
            jax.ShapeDtypeStruct((B, SSD_HEADS, SSD_HD, SSD_STATE), f32),
            jax.ShapeDtypeStruct((B, CONV_W - 1, SSD_CONV_CH), f32),
        ],
        compiler_params=_params(("parallel",)),
        name="ssd_decode",
    )(proj3, proj3, proj3, conv_state, state, lp["ssd_conv_w"], lp["ssd_conv_b"], lp["ssd_dtb"], lp["ssd_alog"],
      lp["ssd_dexp"], lp["ssd_norm"])


def _gdn_decode_kernel(qkv_ref, z_ref, sm_ref, cprev_ref, sin_ref, cw_ref, dtb_ref, alog_ref, nrm_ref,
                       y_ref, st_ref, cst_ref):
    x = qkv_ref[...]
    conv = cw_ref[3:4, :] * x
    for j in range(CONV_W - 1):
        conv += cw_ref[j:j + 1, :] * cprev_ref[j:j + 1, :]
    cst_ref[0:2, :] = cprev_ref[1:3, :]
    cst_ref[2:3, :] = x
    xc = _silu(conv)
    sm = sm_ref[...]
    beta_all = jax.nn.sigmoid(sm)
    g_all = -jnp.exp(alog_ref[...]) * _softplus(sm + dtb_ref[...])
    for h in range(GDN_HEADS):
        sl = slice(h * GDN_HD, (h + 1) * GDN_HD)
        qh = xc[:, sl]
        kh = xc[:, D_GDN + h * GDN_HD:D_GDN + (h + 1) * GDN_HD]
        vh = xc[:, 2 * D_GDN + h * GDN_HD:2 * D_GDN + (h + 1) * GDN_HD]
        qh = qh * lax.rsqrt(jnp.sum(qh * qh, axis=-1, keepdims=True) + EPS) * (GDN_HD ** -0.5)
        kh = kh * lax.rsqrt(jnp.sum(kh * kh, axis=-1, keepdims=True) + EPS)
        beta = beta_all[:, SM_B + h:SM_B + h + 1]
        eg = jnp.exp(g_all[:, SM_A + h:SM_A + h + 1])
        s0 = sin_ref[h]
        k8 = _row0(kh)
        v_new = beta * vh - (beta * eg) * _dot(k8, s0, HI)[0:1]
        o = eg * _dot(_row0(qh), s0, HI)[0:1] + jnp.sum(qh * kh, axis=-1, keepdims=True) * v_new
        st_ref[h] = eg * s0 + _dot_tn(k8, _row0(v_new), HI)
        o = o * lax.rsqrt(jnp.mean(o * o, axis=-1, keepdims=True) + EPS)
        y_ref[:, sl] = o * nrm_ref[...] * _silu(z_ref[:, sl])


def _gdn_decode(proj3, conv_state, state, layer, lp):
    B = proj3.shape[0]
    vec = lambda w: pl.BlockSpec((1, w), lambda b: (0, 0))
    return pl.pallas_call(
        _gdn_decode_kernel,
        grid=(B,),
        in_specs=[
            pl.BlockSpec((None, 1, GDN_CONV_CH), lambda b: (b, 0, QKV_OFF // GDN_CONV_CH)),
            pl.BlockSpec((None, 1, D_GDN), lambda b: (b, 0, ZG_OFF // D_GDN)),
            pl.BlockSpec((None, 1, 128), lambda b: (b, 0, SM_OFF // 128)),
            pl.BlockSpec((None, None, CONV_W - 1, GDN_CONV_CH), lambda b: (layer, b, 0, 0)),
            pl.BlockSpec((None, None, GDN_HEADS, GDN_HD, GDN_HD), lambda b: (layer, b, 0, 0, 0)),
            pl.BlockSpec((CONV_W, GDN_CONV_CH), lambda b: (0, 0)),
            vec(128), vec(128), vec(GDN_HD),
        ],
        out_specs=[
            pl.BlockSpec((None, 1, D_GDN), lambda b: (b, 0, 0)),
            pl.BlockSpec((None, GDN_HEADS, GDN_HD, GDN_HD), lambda b: (b, 0, 0, 0)),
            pl.BlockSpec((None, CONV_W - 1, GDN_CONV_CH), lambda b: (b, 0, 0)),
        ],
        out_shape=[
            jax.ShapeDtypeStruct((B, 1, D_GDN), f32),
            jax.ShapeDtypeStruct((B, GDN_HEADS, GDN_HD, GDN_HD), f32),
            jax.ShapeDtypeStruct((B, CONV_W - 1, GDN_CONV_CH), f32),
        ],
        compiler_params=_params(("parallel",)),
        name="gdn_decode",
    )(proj3, proj3, proj3, conv_state, state, lp["gdn_conv_w"], lp["gdn_dtb"], lp["gdn_alog"], lp["gdn_norm"])


def _query_rows(q, g):
    rows = [q[:, (g * NSA_REP + r) * NSA_HD:(g * NSA_REP + r + 1) * NSA_HD] for r in range(NSA_REP)]
    return jnp.concatenate(rows + [jnp.zeros((8 - NSA_REP, NSA_HD), f32)], axis=0) * (NSA_HD ** -0.5)


def _masked_softmax(lg, ok):
    lg = jnp.where(ok, lg, NEG)
    e = jnp.exp(lg - jnp.max(lg, axis=-1, keepdims=True))
    return jnp.where(ok, e / jnp.sum(e, axis=-1, keepdims=True), 0.0)


def _head_lanes(parts):
    return jnp.concatenate([o[r:r + 1, :] for o in parts for r in range(NSA_REP)], axis=1)


def _nsa_dec_cmp_kernel(pt_ref, q_ref, pe_ref, w1_ref, w2_ref, kg_ref, tab_ref, pk_ref, pv_ref,
                        ocmp_ref, idx_ref, stage, rows_k, rows_v, sem, *, layer, n_pages):
    b = pl.program_id(0)
    nb = pl.num_programs(0)
    n_past = n_pages * PAGE
    q_pos = n_past
    n_blk = n_past // CMP_STRIDE
    n_cmp = (n_past + 1 - CMP_LEN) // CMP_STRIDE + 1
    n_slc = -(-(n_past + 1) // SLC_BLOCK)
    nj = -(-n_slc // 128) * 128
    pools = (pk_ref, pv_ref)

    def page_copy(t, seq, p):
        return pltpu.make_async_copy(pools[t].at[layer, pt_ref[seq, p]], stage.at[t, p], sem.at[t])

    def start_all(t, seq):
        lax.fori_loop(0, n_pages, lambda p, c: (page_copy(t, seq, p).start(), c)[1], 0)

    def wait_all(t, seq):
        lax.fori_loop(0, n_pages, lambda p, c: (page_copy(t, seq, p).wait(), c)[1], 0)

    @pl.when(b == 0)
    def _():
        start_all(0, 0)
        start_all(1, 0)

    n_pair = CMP_STRIDE // 2
    share = n_pages // n_pair

    def compress(t, rows, between=None):
        p1 = jnp.zeros((n_blk, 2 * CMP_HID), f32)
        p2 = jnp.zeros((n_blk, 2 * CMP_HID), f32)
        for j in range(n_pair):
            xr = jnp.concatenate([rows[pl.ds(2 * j, n_blk, stride=CMP_STRIDE), :],
                                  rows[pl.ds(2 * j + 1, n_blk, stride=CMP_STRIDE), :]], axis=1)
            j2 = n_pair + j
            p1 += jnp.dot((xr + pe_ref[t, j:j + 1, :]).astype(bf16), w1_ref[t, j], preferred_element_type=f32)
            p2 += jnp.dot((xr + pe_ref[t, j2:j2 + 1, :]).astype(bf16), w1_ref[t, j2], preferred_element_type=f32)
            if between is not None:
                between(j)
        hid = p1 + pltpu.roll(p2, n_blk - 1, 0)
        return jnp.dot(_silu(hid).astype(bf16), w2_ref[t], preferred_element_type=f32)

    def value_pages(j):
        for p in range(j * share, (j + 1) * share):
            rows_v[p * PAGE:(p + 1) * PAGE, :] = stage[1, p].T

    def next_key_pages(j):
        for p in range(j * share, (j + 1) * share):
            rows_k[p * PAGE:(p + 1) * PAGE, :] = stage[0, p].T

    @pl.when(b == 0)
    def _():
        wait_all(0, 0)

        def key_page(p, c):
            rows_k[pl.ds(pl.multiple_of(p * PAGE, PAGE), PAGE), :] = stage[0, p].T
            return c
        lax.fori_loop(0, n_pages, key_page, 0, unroll=4)

    @pl.when(b + 1 < nb)
    def _():
        start_all(0, b + 1)

    wait_all(1, b)
    kc = _group_rms(compress(0, rows_k, value_pages), NSA_HD) * kg_ref[0:1, :]

    @pl.when(b + 1 < nb)
    def _():
        start_all(1, b + 1)
        wait_all(0, b + 1)

    vc = compress(1, rows_v, next_key_pages)

    q = q_ref[...]
    lane = _iota((8, n_blk), 1)
    cend = lane * CMP_STRIDE + (CMP_LEN - 1)
    ok = (cend <= q_pos) & (lane < n_cmp)
    c_i = _iota((n_blk, nj), 0)
    j_i = _iota((n_blk, nj), 1)
    ovl = ((c_i * CMP_STRIDE < j_i * SLC_BLOCK + SLC_BLOCK) & (c_i * CMP_STRIDE + CMP_LEN > j_i * SLC_BLOCK)
           & (j_i < n_slc) & (c_i < n_cmp)).astype(f32)
    jl = _iota((1, nj), 1)
    cur = q_pos // SLC_BLOCK
    valid = jl * SLC_BLOCK <= q_pos
    forced = (jl == 0) | (jl == cur) | (jl == cur - 1)
    ii = _iota((nj, nj), 0)
    jj = _iota((nj, nj), 1)
    kk = _iota((SLC_TOPN, nj), 0)
    j16 = _iota((SLC_TOPN, nj), 1)
    o_parts, idx_cols = [], []
    for g in range(NSA_KV):
        tabcols = [tab_ref[g, :, bk:bk + 1] for bk in range(N_BUCKETS)]
        lg = _dot_nt(_query_rows(q, g), kc[:, g * NSA_HD:(g + 1) * NSA_HD]) + _bias_chain(q_pos - cend, tabcols)
        p = _masked_softmax(lg, ok)
        o_parts.append(_dot(p, vc[:, g * NSA_HD:(g + 1) * NSA_HD]))
        p_sum = p[0:1] + p[1:2] + p[2:3] + p[3:4]
        imp = _dot_01(_row0(p_sum), ovl)[0:1]
        score = jnp.where(valid, jnp.where(forced, 1e9, imp), -1e9)
        score = jnp.where(jl < n_slc, score, -3e9)
        s_c = jnp.sum(jnp.where(ii == jj, score, 0.0), axis=1, keepdims=True)
        rank_r = jnp.sum(((s_c > score) | ((s_c == score) & (ii < jj))).astype(f32), axis=0, keepdims=True)
        rank_c = jnp.sum(((score > s_c) | ((score == s_c) & (jj < ii))).astype(f32), axis=1, keepdims=True)
        sel_r = (rank_r < SLC_TOPN) & (jl < n_slc)
        sel_c = (rank_c < SLC_TOPN) & (ii[:, 0:1] < n_slc)
        pos_r = jnp.sum((sel_c & (ii < jj)).astype(f32), axis=0, keepdims=True)
        hit = (pos_r == kk.astype(f32)) & sel_r
        idx_cols.append(jnp.sum(jnp.where(hit, j16.astype(f32), 0.0), axis=1, keepdims=True))
    ocmp_ref[...] = _head_lanes(o_parts)
    l16 = _iota((SLC_TOPN, 128), 1)
    idx_ref[...] = (jnp.where(l16 == 0, idx_cols[0], 0.0) + jnp.where(l16 == 1, idx_cols[1], 0.0)).astype(i32)


def _nsa_dec_cmp(qn3, page_table, pool_k, pool_v, layer, lp):
    B = qn3.shape[0]
    n_pages = page_table.shape[1]
    full = lambda s: pl.BlockSpec(s, lambda b, pt: (0,) * len(s))
    return pl.pallas_call(
        functools.partial(_nsa_dec_cmp_kernel, layer=layer, n_pages=n_pages),
        grid_spec=pltpu.PrefetchScalarGridSpec(
            num_scalar_prefetch=1,
            grid=(B,),
            in_specs=[
                pl.BlockSpec((None, 1, D_NSA), lambda b, pt: (b, 0, 0)),
                full((2, CMP_LEN // 2, 256)), full((2, CMP_LEN // 2, 256, 2 * CMP_HID)),
                full((2, 2 * CMP_HID, 128)), full((3, 128)), full((NSA_KV, 8, N_BUCKETS)),
                pl.BlockSpec(memory_space=pl.ANY), pl.BlockSpec(memory_space=pl.ANY),
            ],
            out_specs=[pl.BlockSpec((None, 1, D_NSA), lambda b, pt: (b, 0, 0)),
                       pl.BlockSpec((None, SLC_TOPN, 128), lambda b, pt: (b, 0, 0))],
            scratch_shapes=[pltpu.VMEM((2, n_pages, NSA_KV * NSA_HD, PAGE), f32),
                            pltpu.VMEM((n_pages * PAGE, NSA_KV * NSA_HD), f32),
                            pltpu.VMEM((n_pages * PAGE, NSA_KV * NSA_HD), f32),
                            pltpu.SemaphoreType.DMA((2,))],
        ),
        out_shape=[jax.ShapeDtypeStruct((B, 1, D_NSA), f32), jax.ShapeDtypeStruct((B, SLC_TOPN, 128), i32)],
        compiler_params=_params(("arbitrary",)),
        name="nsa_dec_cmp",
    )(page_table, qn3, lp["cmp_pe2"], lp["cmp_w1bd"], lp["cmp_w2bd"], lp["k_gain"], lp["tab8"], pool_k, pool_v)


def _nsa_dec_sel_kernel(pt_ref, idx_ref, q_ref, kv_ref, sm_ref, ocmp_ref, tab_ref, wk_ref, wv_ref, pk_ref, pv_ref,
                        y_ref, wko_ref, wvo_ref, kg, vg, sem, *, layer, n_pages):
    b = pl.program_id(0)
    n_past = n_pages * PAGE
    q_pos = n_past
    n_buf = wk_ref.shape[1]
    new_blk = n_past // SLC_BLOCK
    per_page = PAGE // SLC_BLOCK

    def blk_copy(pool, buf, g, s, sm_i):
        j = jnp.minimum(idx_ref[b, s, g], new_blk - 1)
        src = pool.at[layer, pt_ref[b, j // per_page], pl.ds(g * NSA_HD, NSA_HD)]
        return pltpu.make_async_copy(src, buf.at[g, :, pl.ds(s * PAGE, PAGE)], sem.at[sm_i])

    for g in range(NSA_KV):
        for s in range(SLC_TOPN):
            blk_copy(pk_ref, kg, g, s, 0).start()
            blk_copy(pv_ref, vg, g, s, 1).start()

    kv = kv_ref[...]
    q = q_ref[...]
    gate = jax.nn.sigmoid(sm_ref[...])
    ocmp = ocmp_ref[...]

    ii, jj = _iota((128, 128), 0), _iota((128, 128), 1)
    column = lambda off: jnp.sum(jnp.where(ii == jj, kv[:, off:off + 128], 0.0), axis=1, keepdims=True)
    last_lane = _iota((128, n_buf), 1) == n_buf - 1
    wk, wv = wk_ref[...], wv_ref[...]
    wko_ref[...] = jnp.where(last_lane, column(KV_WK), pltpu.roll(wk, n_buf - 1, 1))
    wvo_ref[...] = jnp.where(last_lane, column(KV_WV), pltpu.roll(wv, n_buf - 1, 1))

    def attend(qg, kt, vt, bias, ok, k_new, v_new, bias_new, ok_new):
        lg = jnp.where(ok, _dot(qg, kt) + bias, NEG)
        lg_new = jnp.where(ok_new, jnp.sum(qg * k_new, axis=-1, keepdims=True) + bias_new, NEG)
        m = jnp.maximum(jnp.max(lg, axis=-1, keepdims=True), lg_new)
        e = jnp.where(ok, jnp.exp(lg - m), 0.0)
        e_new = jnp.where(ok_new, jnp.exp(lg_new - m), 0.0)
        den = jnp.sum(e, axis=-1, keepdims=True) + e_new
        p, p_new = e / den, e_new / den
        return _dot_nt(p, vt) + p_new * v_new

    for g in range(NSA_KV):
        for s in range(SLC_TOPN):
            blk_copy(pk_ref, kg, g, s, 0).wait()
            blk_copy(pv_ref, vg, g, s, 1).wait()

    lw = _iota((8, n_buf), 1)
    d_w = q_pos - (n_past - n_buf + lw)
    ok_w = (d_w >= 0) & (d_w <= WINDOW)
    ls = _iota((8, SLC_TOPN * PAGE), 1)
    slot = ls // PAGE
    zero = jnp.zeros((8, 1), i32)
    o_slc, o_win = [], []
    for g in range(NSA_KV):
        tabcols = [tab_ref[g, :, bk:bk + 1] for bk in range(N_BUCKETS)]
        qg = _query_rows(q, g)
        gs = slice(g * NSA_HD, (g + 1) * NSA_HD)
        b_new = _bias_chain(zero, tabcols)
        blk = jnp.zeros(ls.shape, i32)
        n_new = jnp.zeros((8, 1), i32)
        for s in range(SLC_TOPN):
            j = idx_ref[b, s, g]
            blk = jnp.where(slot == s, j, blk)
            n_new = jnp.where(j == new_blk, n_new + 1, n_new)
        k_pos = (jnp.minimum(blk, new_blk - 1) // per_page) * PAGE + ls % PAGE
        ok_s = (blk < new_blk) & (k_pos // SLC_BLOCK == blk)
        o_slc.append(attend(qg, kg[g], vg[g], _bias_chain(q_pos - k_pos, tabcols), ok_s,
                            kv[:, KV_SK + g * NSA_HD:KV_SK + (g + 1) * NSA_HD],
                            kv[:, KV_SV + g * NSA_HD:KV_SV + (g + 1) * NSA_HD], b_new, n_new > 0))
        o_win.append(attend(qg, wk[gs, :], wv[gs, :], _bias_chain(d_w, tabcols), ok_w,
                            kv[:, KV_WK + g * NSA_HD:KV_WK + (g + 1) * NSA_HD],
                            kv[:, KV_WV + g * NSA_HD:KV_WV + (g + 1) * NSA_HD], b_new, zero == 0))
    gl = lambda t: gate[:, SM_G + t * NSA_HEADS:SM_G + (t + 1) * NSA_HEADS]
    wide = lambda gt: jnp.concatenate([jnp.broadcast_to(gt[:, h:h + 1], (1, NSA_HD)) for h in range(NSA_HEADS)], axis=1)
    y_ref[...] = wide(gl(0)) * ocmp + wide(gl(1)) * _head_lanes(o_slc) + wide(gl(2)) * _head_lanes(o_win)


def _nsa_dec_sel(qn3, kv6_3, proj3, ocmp, idx, page_table, win_k, win_v, pool_k, pool_v, layer, lp):
    B = qn3.shape[0]
    n_pages = page_table.shape[1]
    n_buf = win_k.shape[3]
    full = lambda s: pl.BlockSpec(s, lambda b, pt, ix: (0,) * len(s))
    row = lambda w, j: pl.BlockSpec((None, 1, w), lambda b, pt, ix: (b, 0, j))
    win = pl.BlockSpec((None, None, 128, n_buf), lambda b, pt, ix: (layer, b, 0, 0))
    return pl.pallas_call(
        functools.partial(_nsa_dec_sel_kernel, layer=layer, n_pages=n_pages),
        grid_spec=pltpu.PrefetchScalarGridSpec(
            num_scalar_prefetch=2,
            grid=(B,),
            in_specs=[row(D_NSA, 0), row(768, 0), row(128, SM_OFF // 128), row(D_NSA, 0),
                      full((NSA_KV, 8, N_BUCKETS)), win, win,
                      pl.BlockSpec(memory_space=pl.ANY), pl.BlockSpec(memory_space=pl.ANY)],
            out_specs=[row(D_NSA, 0),
                       pl.BlockSpec((None, 128, n_buf), lambda b, pt, ix: (b, 0, 0)),
                       pl.BlockSpec((None, 128, n_buf), lambda b, pt, ix: (b, 0, 0))],
            scratch_shapes=[pltpu.VMEM((NSA_KV, NSA_HD, SLC_TOPN * PAGE), f32),
                            pltpu.VMEM((NSA_KV, NSA_HD, SLC_TOPN * PAGE), f32),
                            pltpu.SemaphoreType.DMA((2,))],
        ),
        out_shape=[jax.ShapeDtypeStruct((B, 1, D_NSA), f32),
                   jax.ShapeDtypeStruct((B, 128, n_buf), f32), jax.ShapeDtypeStruct((B, 128, n_buf), f32)],
        compiler_params=_params(("arbitrary",)),
        name="nsa_dec_sel",
    )(page_table, idx, qn3, kv6_3, proj3, ocmp, lp["tab8"], win_k, win_v, pool_k, pool_v)


def _layer_params(i, p):
    pad128 = lambda v, off: jnp.zeros((1, 128), f32).at[0, off:off + v.shape[0]].set(v)
    return dict(
        ssd_conv_w=p["ssd_conv_w"][i], ssd_conv_b=p["ssd_conv_b"][i][None],
        ssd_dtb=pad128(p["ssd_dt_bias"][i], SM_DT), ssd_alog=pad128(p["ssd_a_log"][i], SM_DT),
        ssd_dexp=jnp.repeat(p["ssd_d"][i], SSD_HD)[None], ssd_norm=p["ssd_norm"][i][None],
        gdn_conv_w=p["gdn_conv_w"][i],
        gdn_dtb=pad128(p["gdn_dt_bias"][i], SM_A), gdn_alog=pad128(p["gdn_a_log"][i], SM_A),
        gdn_norm=p["gdn_norm"][i][None],
        q_gain=jnp.tile(p["nsa_q_norm"][i], 2)[None], k_gain=jnp.tile(p["nsa_k_norm"][i], (1, 2)),
        cmp_pe=p["nsa_cmp_pe"][i], cmp_w1=p["nsa_cmp_w1"][i], cmp_w2=p["nsa_cmp_w2"][i],
        cmp_pe2=jnp.tile(p["nsa_cmp_pe"][i], (1, 1, 2)).reshape(2, CMP_LEN // 2, 256),
        cmp_w1bd=_block_diag2(p["nsa_cmp_w1"][i].reshape(2, CMP_LEN, NSA_HD, CMP_HID)).astype(bf16)
        .reshape(2, CMP_LEN // 2, 256, 2 * CMP_HID),
        cmp_w2bd=_block_diag2(p["nsa_cmp_w2"][i]).astype(bf16),
        tab8=jnp.pad(p["rel_bias"].T.reshape(NSA_KV, NSA_REP, N_BUCKETS), ((0, 0), (0, 8 - NSA_REP), (0, 0))),
    )


def _block_diag2(w):
    z = jnp.zeros_like(w)
    return jnp.concatenate([jnp.concatenate([w, z], axis=-1), jnp.concatenate([z, w], axis=-1)], axis=-2)


def _reorder_w_in(w_in):
    cuts = [(1024, 2560), (2576, 4112), (0, 1024), (4112, 4624), (4632, 5144), (5144, 5912),
            (2560, 2576), (4624, 4632), (5912, 5936)]
    wt = jnp.transpose(w_in, (0, 2, 1))
    parts = [wt[:, a:b] for a, b in cuts]
    used = sum(b - a for a, b in cuts)
    parts.append(jnp.zeros((w_in.shape[0], PROJ_W - used, w_in.shape[1]), w_in.dtype))
    return jnp.concatenate(parts, axis=1).astype(bf16)


def _mix_prompt(x3, i, p, lp, w_r):
    B, L, _ = x3.shape
    n = B * L
    x = x3.reshape(n, D_MODEL)
    proj = _inproj(x, p["norm_mix"], w_r, i)
    proj3 = proj.reshape(B, L, PROJ_W)
    y_ssd, st_ssd, cst_ssd = _ssd_prompt(proj3, lp)
    y_gdn, st_gdn, cst_gdn = _gdn_prompt(proj3, lp)
    qt, kv6_3, kv6t, gt = _prep_t(proj3, lp["q_gain"], lp["k_gain"])
    kc, vct = _cmp_prompt(kv6_3, lp)
    y_nsa = _nsa_prompt(qt, kv6_3, kv6t, kc, vct, gt, p["rel_bias"])
    x = _outproj(x, y_ssd.reshape(n, D_SSD), y_gdn.reshape(n, D_GDN), y_nsa.reshape(n, D_NSA), p["w_out"], i)
    rows = lambda off: jnp.transpose(kv6t[:, off:off + 128].reshape(B, NSA_KV, NSA_HD, L), (0, 3, 1, 2))
    keep = min(WINDOW, L)
    state = (st_ssd, cst_ssd, st_gdn, cst_gdn, rows(KV_CK), rows(KV_CV), rows(KV_SK), rows(KV_SV),
             rows(KV_WK)[:, L - keep:], rows(KV_WV)[:, L - keep:])
    return x.reshape(B, L, D_MODEL), state


def _mix_decode(x, i, p, lp, w_r, st):
    B = x.shape[0]
    proj = _inproj(x, p["norm_mix"], w_r, i)
    proj3 = proj.reshape(B, 1, PROJ_W)
    y_ssd, st_ssd, cst_ssd = _ssd_decode(proj3, st["ssd_conv"], st["ssd"], i, lp)
    y_gdn, st_gdn, cst_gdn = _gdn_decode(proj3, st["gdn_conv"], st["gdn"], i, lp)
    qn, kv6 = _prep(proj, lp["q_gain"], lp["k_gain"])
    qn3, kv6_3 = qn.reshape(B, 1, D_NSA), kv6.reshape(B, 1, 768)
    o_cmp, idx = _nsa_dec_cmp(qn3, st["page_table"], st["cmp_k"], st["cmp_v"], i, lp)
    y_nsa, win_k, win_v = _nsa_dec_sel(qn3, kv6_3, proj3, o_cmp, idx[:, :, :NSA_KV], st["page_table"],
                                       st["win_k"], st["win_v"], st["slc_k"], st["slc_v"], i, lp)
    x = _outproj(x, y_ssd.reshape(B, D_SSD), y_gdn.reshape(B, D_GDN), y_nsa.reshape(B, D_NSA), p["w_out"], i)
    rows = lambda off: kv6[:, off:off + 128].reshape(B, 1, NSA_KV, NSA_HD)
    n_buf = win_k.shape[2]
    buf = lambda w: jnp.transpose(w.reshape(B, NSA_KV, NSA_HD, n_buf), (0, 3, 1, 2))
    state = (st_ssd, cst_ssd, st_gdn, cst_gdn, rows(KV_CK), rows(KV_CV), rows(KV_SK), rows(KV_SV),
             buf(win_k), buf(win_v))
    return x, state


def kernel(x_prompt, x_sample, state_ssd, state_ssd_conv, state_gdn, state_gdn_conv, cache_cmp_k, cache_cmp_v,
           cache_slc_k, cache_slc_v, cache_win_k, cache_win_v, page_table, rel_bias, norm_ffn1, w_ffn1_gate,
           w_ffn1_up, w_ffn1_down, norm_mix, w_in, ssd_conv_w, ssd_conv_b, ssd_dt_bias, ssd_a_log, ssd_d, ssd_norm,
           gdn_conv_w, gdn_dt_bias, gdn_a_log, gdn_norm, nsa_q_norm, nsa_k_norm, nsa_cmp_pe, nsa_cmp_w1,
           nsa_cmp_w2, w_out, norm_ffn2, w_ffn2_gate, w_ffn2_up, w_ffn2_down):
    bp, lp_len, _ = x_prompt.shape
    bs = x_sample.shape[0]
    gain3 = lambda g: g.reshape(DEPTH, 1, D_MODEL)
    p = dict(rel_bias=rel_bias, norm_mix=gain3(norm_mix), w_out=w_out.astype(bf16), ssd_conv_w=ssd_conv_w,
             ssd_conv_b=ssd_conv_b,
             ssd_dt_bias=ssd_dt_bias, ssd_a_log=ssd_a_log, ssd_d=ssd_d, ssd_norm=ssd_norm, gdn_conv_w=gdn_conv_w,
             gdn_dt_bias=gdn_dt_bias, gdn_a_log=gdn_a_log, gdn_norm=gdn_norm, nsa_q_norm=nsa_q_norm,
             nsa_k_norm=nsa_k_norm, nsa_cmp_pe=nsa_cmp_pe, nsa_cmp_w1=nsa_cmp_w1, nsa_cmp_w2=nsa_cmp_w2)
    n1, n2 = gain3(norm_ffn1), gain3(norm_ffn2)
    chan_row = lambda c: jnp.transpose(c, (0, 1, 3, 4, 2)).reshape(c.shape[:2] + (NSA_KV * NSA_HD, c.shape[2]))
    st = dict(ssd=state_ssd, ssd_conv=state_ssd_conv, gdn=state_gdn, gdn_conv=state_gdn_conv,
              cmp_k=chan_row(cache_cmp_k), cmp_v=chan_row(cache_cmp_v), slc_k=chan_row(cache_slc_k),
              slc_v=chan_row(cache_slc_v), win_k=chan_row(cache_win_k), win_v=chan_row(cache_win_v),
              page_table=page_table)
    w_r = _reorder_w_in(w_in)

    hp = x_prompt.reshape(bp * lp_len, D_MODEL)
    hs = x_sample.reshape(bs, D_MODEL)
    outs_p, outs_s = [], []
    for i in range(DEPTH):
        lp = _layer_params(i, p)
        hp, hs = _ffn(hp, hs, n1, w_ffn1_gate, w_ffn1_up, w_ffn1_down, i)
        hp3, st_p = _mix_prompt(hp.reshape(bp, lp_len, D_MODEL), i, p, lp, w_r)
        hs, st_s = _mix_decode(hs, i, p, lp, w_r, st)
        hp, hs = _ffn(hp3.reshape(bp * lp_len, D_MODEL), hs, n2, w_ffn2_gate, w_ffn2_up, w_ffn2_down, i)
        outs_p.append(st_p)
        outs_s.append(st_s)
    stack = lambda outs: [jnp.stack(t) for t in zip(*outs)]
    return (hp.reshape(bp, lp_len, D_MODEL), hs.reshape(bs, 1, D_MODEL), *stack(outs_p), *stack(outs_s))
```

```python
import functools
import math

import numpy as np
import jax
import jax.numpy as jnp
from jax import lax
from jax.experimental import pallas as pl
from jax.experimental.pallas import tpu as pltpu

f32 = jnp.float32
bf16 = jnp.bfloat16
i32 = jnp.int32
HI = lax.Precision.HIGHEST

D_MODEL = 2048
DEPTH = 4
PAGE = 128
D_SSD = 1024
SSD_HD = 64
SSD_HEADS = 16
SSD_GROUPS = 2
SSD_STATE = 128
SSD_CHUNK = 128
D_GDN = 512
GDN_HD = 128
GDN_HEADS = 4
GDN_CHUNK = 64
GDN_STEP_CHUNKS = 4
D_NSA = 512
NSA_HD = 64
NSA_HEADS = 8
NSA_KV = 2
NSA_REP = 4
CMP_STRIDE = 16
CMP_LEN = 32
CMP_HID = 128
SLC_BLOCK = 64
SLC_TOPN = 16
WINDOW = 512
N_BUCKETS = 32
MAX_DISTANCE = 128
CONV_W = 4
D_FF = 5632
SSD_CONV_CH = 1536
GDN_CONV_CH = 1536
EPS = 1e-6
NEG = -1e30

XBC_OFF = 0
QKV_OFF = 1536
ZS_OFF = 3072
ZG_OFF = 4096
QN_OFF = 4608
KV_OFF = 5120
SM_OFF = 5888
PROJ_W = 6144
SM_DT, SM_A, SM_B, SM_G = 0, 16, 20, 24
KV_CK, KV_CV, KV_SK, KV_SV, KV_WK, KV_WV = 0, 128, 256, 384, 512, 640

VMEM_LIMIT = 56 * 1024 * 1024
ROW_TILE = 1024
FFN_COL_TILE = 256
PROJ_COL_TILE = 512
OUT_ROW_TILE = 2048
PREP_ROW_TILE = 512
NSA_Q_TILE = 128
TB_FAR, TB_PREV, TB_DIAG, TB_WIN0, TB_NONE, TB_KINDS = 0, 1, 2, 3, 4, 5


def _bucket_thresholds():
    exact = N_BUCKETS // 2
    d = np.arange(0, 4 * MAX_DISTANCE)
    nf = np.maximum(d, 1).astype(np.float32)
    large = exact + (np.log(nf / np.float32(exact)) / np.float32(math.log(MAX_DISTANCE / exact))
                     * np.float32(N_BUCKETS - exact)).astype(np.int32)
    bucket = np.where(d < exact, d, np.minimum(large, N_BUCKETS - 1))
    return [int(np.argmax(bucket >= b)) for b in range(N_BUCKETS)]


_THR = _bucket_thresholds()
BIAS_FAR = _THR[N_BUCKETS - 1]


def _bias_chain(d, tabcols):
    v = jnp.broadcast_to(tabcols[N_BUCKETS - 1], d.shape)
    for b in range(N_BUCKETS - 2, -1, -1):
        v = jnp.where(d < _THR[b + 1], tabcols[b], v)
    return v


def _silu(x):
    return x * jax.nn.sigmoid(x)


def _softplus(x):
    return jnp.maximum(x, 0.0) + jnp.log1p(jnp.exp(-jnp.abs(x)))


def _dot(a, b, prec=None):
    if prec is None:
        a, b = a.astype(bf16), b.astype(bf16)
    return jnp.dot(a, b, preferred_element_type=f32, precision=prec)


def _dot_nt(a, b, prec=None):
    if prec is None:
        a, b = a.astype(bf16), b.astype(bf16)
    return lax.dot_general(a, b, (((1,), (1,)), ((), ())), preferred_element_type=f32, precision=prec)


def _dot_tn(a, b, prec=None):
    if prec is None:
        a, b = a.astype(bf16), b.astype(bf16)
    return lax.dot_general(a, b, (((0,), (0,)), ((), ())), preferred_element_type=f32, precision=prec)


def _split2(a):
    hi = a.astype(bf16)
    return hi, (a - hi.astype(f32)).astype(bf16)


def _dot3(a2, b2):
    d = lambda x, y: jnp.dot(x, y, preferred_element_type=f32)
    return d(a2[0], b2[0]) + (d(a2[0], b2[1]) + d(a2[1], b2[0]))


def _split3(a):
    a1 = a.astype(bf16)
    r = a - a1.astype(f32)
    a2 = r.astype(bf16)
    return a1, a2, (r - a2.astype(f32)).astype(bf16)


def _dot_01(a, m01):
    m = m01.astype(bf16)
    return sum(jnp.dot(t, m, preferred_element_type=f32) for t in _split3(a))


def _dot_01l(m01, b):
    m = m01.astype(bf16)
    return sum(jnp.dot(m, t, preferred_element_type=f32) for t in _split3(b))


def _iota(shape, dim):
    return lax.broadcasted_iota(i32, shape, dim)


def _group_ones(n, width):
    return (_iota((n, n), 0) // width == _iota((n, n), 1) // width).astype(f32)


def _group_rms(x, width):
    ss = _dot_01(x * x, _group_ones(x.shape[1], width))
    return x * lax.rsqrt(ss * (1.0 / width) + EPS)


def _params(sem):
    return pltpu.CompilerParams(dimension_semantics=sem, vmem_limit_bytes=VMEM_LIMIT)


FFN_EXTRA = 16


def _ffn_kernel(x_ref, xs_ref, g_ref, wg_ref, wu_ref, wd_ref, o_ref, os_ref, h_ref):
    tm, ns = x_ref.shape[0], xs_ref.shape[0]

    @pl.when(pl.program_id(1) == 0)
    def _():
        rms = lambda x: x * lax.rsqrt(jnp.mean(x * x, axis=-1, keepdims=True) + EPS) * g_ref[...]
        x, xs = x_ref[...], xs_ref[...]
        h_ref[0:tm, :] = rms(x).astype(bf16)
        h_ref[tm:tm + FFN_EXTRA, :] = jnp.concatenate(
            [rms(xs), jnp.zeros((FFN_EXTRA - ns, D_MODEL), f32)], axis=0).astype(bf16)
        o_ref[...] = x
        os_ref[...] = xs

    h = h_ref[...]
    a = jnp.dot(h, wg_ref[...].astype(bf16), preferred_element_type=f32)
    u = jnp.dot(h, wu_ref[...].astype(bf16), preferred_element_type=f32)
    y = 0.5 * _dot(_silu(a) * u, wd_ref[...])
    o_ref[...] += y[0:tm]
    os_ref[...] += y[tm:tm + ns]


def _ffn(x, xs, gain, wg, wu, wd, layer):
    n, ns = x.shape[0], xs.shape[0]
    tm, tf = ROW_TILE, FFN_COL_TILE
    return pl.pallas_call(
        _ffn_kernel,
        grid=(n // tm, D_FF // tf),
        in_specs=[
            pl.BlockSpec((tm, D_MODEL), lambda i, j: (i, 0)),
            pl.BlockSpec((ns, D_MODEL), lambda i, j: (0, 0)),
            pl.BlockSpec((None, 1, D_MODEL), lambda i, j: (layer, 0, 0)),
            pl.BlockSpec((None, D_MODEL, tf), lambda i, j: (layer, 0, j)),
            pl.BlockSpec((None, D_MODEL, tf), lambda i, j: (layer, 0, j)),
            pl.BlockSpec((None, tf, D_MODEL), lambda i, j: (layer, j, 0)),
        ],
        out_specs=[pl.BlockSpec((tm, D_MODEL), lambda i, j: (i, 0)),
                   pl.BlockSpec((ns, D_MODEL), lambda i, j: (0, 0))],
        out_shape=[jax.ShapeDtypeStruct((n, D_MODEL), f32), jax.ShapeDtypeStruct((ns, D_MODEL), f32)],
        scratch_shapes=[pltpu.VMEM((tm + FFN_EXTRA, D_MODEL), bf16)],
        compiler_params=_params(("arbitrary", "arbitrary")),
        name="ffn",
    )(x, xs, gain, wg, wu, wd)


def _inproj_kernel(x_ref, g_ref, w_ref, o_ref, h_ref):
    @pl.when(pl.program_id(1) == 0)
    def _():
        x = x_ref[...]
        h = x * lax.rsqrt(jnp.mean(x * x, axis=-1, keepdims=True) + EPS) * g_ref[...]
        h_ref[...] = h.astype(bf16)

    o_ref[...] = _dot_nt(h_ref[...], w_ref[...])


def _inproj(x, gain, w_r, layer):
    n = x.shape[0]
    tm = min(n, ROW_TILE)
    tn = 2 * PROJ_COL_TILE
    return pl.pallas_call(
        _inproj_kernel,
        grid=(n // tm, PROJ_W // tn),
        in_specs=[
            pl.BlockSpec((tm, D_MODEL), lambda i, j: (i, 0)),
            pl.BlockSpec((None, 1, D_MODEL), lambda i, j: (layer, 0, 0)),
            pl.BlockSpec((None, tn, D_MODEL), lambda i, j: (layer, j, 0)),
        ],
        out_specs=pl.BlockSpec((tm, tn), lambda i, j: (i, j)),
        out_shape=jax.ShapeDtypeStruct((n, PROJ_W), f32),
        scratch_shapes=[pltpu.VMEM((tm, D_MODEL), bf16)],
        compiler_params=_params(("parallel", "arbitrary")),
        name="inproj",
    )(x, gain, w_r)


def _outproj_kernel(x_ref, ys_ref, yg_ref, yn_ref, ws_ref, wg_ref, wn_ref, o_ref):
    acc = _dot(ys_ref[...], ws_ref[...])
    acc += _dot(yg_ref[...], wg_ref[...])
    acc += _dot(yn_ref[...], wn_ref[...])
    o_ref[...] = x_ref[...] + acc


def _outproj(x, ys, yg, yn, w_out, layer):
    n = x.shape[0]
    tm = min(n, OUT_ROW_TILE)
    tn = PROJ_COL_TILE
    return pl.pallas_call(
        _outproj_kernel,
        grid=(n // tm, D_MODEL // tn),
        in_specs=[
            pl.BlockSpec((tm, tn), lambda i, j: (i, j)),
            pl.BlockSpec((tm, D_SSD), lambda i, j: (i, 0)),
            pl.BlockSpec((tm, D_GDN), lambda i, j: (i, 0)),
            pl.BlockSpec((tm, D_NSA), lambda i, j: (i, 0)),
            pl.BlockSpec((None, D_SSD, tn), lambda i, j: (layer, 0, j)),
            pl.BlockSpec((None, D_GDN, tn), lambda i, j: (layer, 2, j)),
            pl.BlockSpec((None, D_NSA, tn), lambda i, j: (layer, 3, j)),
        ],
        out_specs=pl.BlockSpec((tm, tn), lambda i, j: (i, j)),
        out_shape=jax.ShapeDtypeStruct((n, D_MODEL), f32),
        compiler_params=_params(("parallel", "arbitrary")),
        name="outproj",
    )(x, ys, yg, yn, w_out, w_out, w_out)


def _prep_kernel(q_ref, kv_ref, qg_ref, kg_ref, qn_ref, kv6_ref):
    for s in range(D_NSA // 128):
        sl = slice(s * 128, (s + 1) * 128)
        qn_ref[:, sl] = _group_rms(q_ref[:, sl], NSA_HD) * qg_ref[...]
    kv = kv_ref[...]
    kv6_ref[...] = kv[:, :6 * 128]
    kv6_ref[:, KV_SK:KV_SK + 128] = _group_rms(kv[:, KV_SK:KV_SK + 128], NSA_HD) * kg_ref[1:2, :]
    kv6_ref[:, KV_WK:KV_WK + 128] = _group_rms(kv[:, KV_WK:KV_WK + 128], NSA_HD) * kg_ref[2:3, :]


def _prep(proj, qgain, kgain):
    n = proj.shape[0]
    tm = min(n, PREP_ROW_TILE)
    return pl.pallas_call(
        _prep_kernel,
        grid=(n // tm,),
        in_specs=[
            pl.BlockSpec((tm, D_NSA), lambda i: (i, QN_OFF // D_NSA)),
            pl.BlockSpec((tm, 1024), lambda i: (i, KV_OFF // 1024)),
            pl.BlockSpec((1, 128), lambda i: (0, 0)),
            pl.BlockSpec((3, 128), lambda i: (0, 0)),
        ],
        out_specs=[pl.BlockSpec((tm, D_NSA), lambda i: (i, 0)),
                   pl.BlockSpec((tm, 768), lambda i: (i, 0))],
        out_shape=[jax.ShapeDtypeStruct((n, D_NSA), f32), jax.ShapeDtypeStruct((n, 768), f32)],
        compiler_params=_params(("parallel",)),
        name="nsa_prep",
    )(proj, proj, qgain, kgain)


def _ssd_prompt_kernel(xbc_ref, z_ref, sm_ref, cw_ref, cb_ref, dtb_ref, alog_ref, dexp_ref, nrm_ref,
                       y_ref, st_ref, cst_ref, xbuf, hst, ybuf):
    c = pl.program_id(1)
    cl = SSD_CHUNK

    @pl.when(c == 0)
    def _():
        xbuf[0:8, :] = jnp.zeros((8, SSD_CONV_CH), f32)
        hst[...] = jnp.zeros(hst.shape, f32)

    x = xbc_ref[...]
    xbuf[8:8 + cl, :] = x
    conv = cw_ref[3:4, :] * x
    for k in range(1, CONV_W):
        conv += cw_ref[3 - k:4 - k, :] * xbuf[pl.ds(8 - k, cl), :]
    xbuf[0:8, :] = x[cl - 8:cl, :]
    xc = _silu(conv + cb_ref[...])
    xs = xc[:, :D_SSD]

    dt = jnp.where(_iota((cl, 128), 1) < SSD_HEADS, _softplus(sm_ref[...] + dtb_ref[...]), 0.0)
    da = dt * (-jnp.exp(alog_ref[...]))
    row, col = _iota((cl, cl), 0), _iota((cl, cl), 1)
    causal = row >= col
    cum = _dot_01l(causal, da)
    cum_t = cum.T
    last = cum[cl - 1:cl, :]
    e_last = jnp.exp(last)
    spread = (_iota((128, D_SSD), 1) // SSD_HD == _iota((128, D_SSD), 0)).astype(f32)
    coef = _dot_01(jnp.concatenate([dt, jnp.exp(last - cum), jnp.exp(cum)], axis=0), spread)
    xdt = xs * coef[0:cl]
    xw = xdt * coef[cl:2 * cl]
    e_cum = coef[2 * cl:3 * cl]

    hpg = SSD_HEADS // SSD_GROUPS
    gw = hpg * SSD_HD
    for g in range(SSD_GROUPS):
        bm = xc[:, D_SSD + g * SSD_STATE:D_SSD + (g + 1) * SSD_STATE]
        cm = xc[:, D_SSD + (SSD_GROUPS + g) * SSD_STATE:D_SSD + (SSD_GROUPS + g + 1) * SSD_STATE]
        cb = _dot_nt(cm, bm)
        h0 = hst[g]
        y_off = _dot_nt(cm, h0) * e_cum[:, g * gw:(g + 1) * gw]
        s_new = _dot_tn(xw[:, g * gw:(g + 1) * gw], bm)
        for r in range(hpg):
            h = g * hpg + r
            seg = cum[:, h:h + 1] - cum_t[h:h + 1, :]
            decay = jnp.where(causal, jnp.exp(jnp.where(causal, seg, 0.0)), 0.0)
            rs = slice(r * SSD_HD, (r + 1) * SSD_HD)
            ybuf[:, h * SSD_HD:(h + 1) * SSD_HD] = (_dot(cb * decay, xdt[:, h * SSD_HD:(h + 1) * SSD_HD])
                                                    + y_off[:, rs])
            hst[g, rs, :] = e_last[:, h:h + 1] * h0[rs, :] + s_new[rs, :]

    y = (ybuf[...] + dexp_ref[...] * xs) * _silu(z_ref[...])
    gw = D_SSD // SSD_GROUPS
    for g in range(SSD_GROUPS):
        yg = y[:, g * gw:(g + 1) * gw]
        yg = yg * lax.rsqrt(jnp.mean(yg * yg, axis=-1, keepdims=True) + EPS)
        y_ref[:, g * gw:(g + 1) * gw] = (yg * nrm_ref[:, g * gw:(g + 1) * gw]).astype(y_ref.dtype)

    @pl.when(c == pl.num_programs(1) - 1)
    def _():
        st_ref[...] = hst[...]
        cst_ref[...] = x[cl - (CONV_W - 1):cl, :]


def _ssd_prompt(proj3, lp):
    B, L, _ = proj3.shape
    cl = SSD_CHUNK
    gh = SSD_HEADS // SSD_GROUPS * SSD_HD
    vec = lambda w: pl.BlockSpec((1, w), lambda b, c: (0, 0))
    y, st, cst = pl.pallas_call(
        _ssd_prompt_kernel,
        grid=(B, L // cl),
        in_specs=[
            pl.BlockSpec((None, cl, SSD_CONV_CH), lambda b, c: (b, c, XBC_OFF // SSD_CONV_CH)),
            pl.BlockSpec((None, cl, D_SSD), lambda b, c: (b, c, ZS_OFF // D_SSD)),
            pl.BlockSpec((None, cl, 128), lambda b, c: (b, c, SM_OFF // 128)),
            pl.BlockSpec((CONV_W, SSD_CONV_CH), lambda b, c: (0, 0)),
            vec(SSD_CONV_CH), vec(128), vec(128), vec(D_SSD), vec(D_SSD),
        ],
        out_specs=[
            pl.BlockSpec((None, cl, D_SSD), lambda b, c: (b, c, 0)),
            pl.BlockSpec((None, SSD_GROUPS, gh, SSD_STATE), lambda b, c: (b, 0, 0, 0)),
            pl.BlockSpec((None, CONV_W - 1, SSD_CONV_CH), lambda b, c: (b, 0, 0)),
        ],
        out_shape=[
            jax.ShapeDtypeStruct((B, L, D_SSD), bf16),
            jax.ShapeDtypeStruct((B, SSD_GROUPS, gh, SSD_STATE), f32),
            jax.ShapeDtypeStruct((B, CONV_W - 1, SSD_CONV_CH), f32),
        ],
        scratch_shapes=[
            pltpu.VMEM((8 + cl, SSD_CONV_CH), f32),
            pltpu.VMEM((SSD_GROUPS, gh, SSD_STATE), f32),
            pltpu.VMEM((cl, D_SSD), f32),
        ],
        compiler_params=_params(("parallel", "arbitrary")),
        name="ssd_prompt",
    )(proj3, proj3, proj3, lp["ssd_conv_w"], lp["ssd_conv_b"], lp["ssd_dtb"], lp["ssd_alog"],
      lp["ssd_dexp"], lp["ssd_norm"])
    return y, st.reshape(B, SSD_HEADS, SSD_HD, SSD_STATE), cst


def _gdn_prompt_kernel(qkv_ref, z_ref, sm_ref, cw_ref, dtb_ref, alog_ref, nrm_ref,
                       y_ref, st_ref, cst_ref, xbuf, sst):
    c = pl.program_id(1)
    cl = GDN_CHUNK
    nch = GDN_STEP_CHUNKS
    tl = nch * cl

    @pl.when(c == 0)
    def _():
        xbuf[0:8, :] = jnp.zeros((8, GDN_CONV_CH), f32)
        sst[...] = jnp.zeros(sst.shape, f32)

    x = qkv_ref[...]
    xbuf[8:8 + tl, :] = x
    conv = cw_ref[3:4, :] * x
    for k in range(1, CONV_W):
        conv += cw_ref[3 - k:4 - k, :] * xbuf[pl.ds(8 - k, tl), :]
    xbuf[0:8, :] = x[tl - 8:tl, :]
    xc = _silu(conv)

    sm = sm_ref[...]
    beta_all = jax.nn.sigmoid(sm)
    g_all = -jnp.exp(alog_ref[...]) * _softplus(sm + dtb_ref[...])
    row, col = _iota((cl, cl), 0), _iota((cl, cl), 1)
    incl = row >= col
    strict = row > col
    eye = (row == col).astype(f32)
    rt, ct = _iota((tl, tl), 0), _iota((tl, tl), 1)
    cum = _dot_01l((rt >= ct) & (rt // cl == ct // cl), g_all)
    cum_t = cum.T

    units = [(ci, h) for ci in range(nch) for h in range(GDN_HEADS)]
    q, k, v, beta, cum_c, decay = {}, {}, {}, {}, {}, {}
    for u in units:
        ci, h = u
        rows = slice(ci * cl, (ci + 1) * cl)
        qh = xc[rows, h * GDN_HD:(h + 1) * GDN_HD]
        kh = xc[rows, D_GDN + h * GDN_HD:D_GDN + (h + 1) * GDN_HD]
        v[u] = xc[rows, 2 * D_GDN + h * GDN_HD:2 * D_GDN + (h + 1) * GDN_HD]
        q[u] = qh * lax.rsqrt(jnp.sum(qh * qh, axis=-1, keepdims=True) + EPS) * (GDN_HD ** -0.5)
        k[u] = kh * lax.rsqrt(jnp.sum(kh * kh, axis=-1, keepdims=True) + EPS)
        beta[u] = beta_all[rows, SM_B + h:SM_B + h + 1]
        cum_c[u] = cum[rows, SM_A + h:SM_A + h + 1]
        seg = cum_c[u] - cum_t[SM_A + h:SM_A + h + 1, rows]
        decay[u] = jnp.where(incl, jnp.exp(jnp.where(incl, seg, 0.0)), 0.0)
    pw = {u: -jnp.where(strict, beta[u] * _dot_nt(k[u], k[u]) * decay[u], 0.0) for u in units}
    inv = {u: eye + pw[u] for u in units}
    pw2 = {u: _split2(pw[u]) for u in units}
    for _ in range(5):
        pw2 = {u: _split2(_dot3(pw2[u], pw2[u])) for u in units}
        inv = {u: inv[u] + _dot3(_split2(inv[u]), pw2[u]) for u in units}
    rhs = {u: jnp.concatenate([beta[u] * v[u], (beta[u] * jnp.exp(cum_c[u])) * k[u]], axis=1) for u in units}
    sol = {u: _dot3(_split2(inv[u]), _split2(rhs[u])) for u in units}
    uu = {u: sol[u][:, :GDN_HD] for u in units}
    ww = {u: sol[u][:, GDN_HD:] for u in units}
    qk = {u: _dot_nt(q[u], k[u]) * decay[u] for u in units}

    s = [sst[h] for h in range(GDN_HEADS)]
    for ci in range(nch):
        rows = slice(ci * cl, (ci + 1) * cl)
        hs = [(ci, h) for h in range(GDN_HEADS)]
        v_new = {u: uu[u] - _dot(ww[u], s[u[1]]) for u in hs}
        o = {u: _dot(q[u] * jnp.exp(cum_c[u]), s[u[1]]) + _dot(qk[u], v_new[u]) for u in hs}
        for u in hs:
            h = u[1]
            last = cum[ci * cl + cl - 1:ci * cl + cl, SM_A + h:SM_A + h + 1]
            s[h] = jnp.exp(last) * s[h] + _dot_tn(k[u] * jnp.exp(last - cum_c[u]), v_new[u])
        for u in hs:
            sl = slice(u[1] * GDN_HD, (u[1] + 1) * GDN_HD)
            on = o[u] * lax.rsqrt(jnp.mean(o[u] * o[u], axis=-1, keepdims=True) + EPS)
            y_ref[rows, sl] = (on * nrm_ref[...] * _silu(z_ref[rows, sl])).astype(y_ref.dtype)
    for h in range(GDN_HEADS):
        sst[h] = s[h]

    @pl.when(c == pl.num_programs(1) - 1)
    def _():
        st_ref[...] = sst[...]
        cst_ref[...] = x[tl - (CONV_W - 1):tl, :]


def _gdn_prompt(proj3, lp):
    B, L, _ = proj3.shape
    cl = GDN_CHUNK * GDN_STEP_CHUNKS
    vec = lambda w: pl.BlockSpec((1, w), lambda b, c: (0, 0))
    return pl.pallas_call(
        _gdn_prompt_kernel,
        grid=(B, L // cl),
        in_specs=[
            pl.BlockSpec((None, cl, GDN_CONV_CH), lambda b, c: (b, c, QKV_OFF // GDN_CONV_CH)),
            pl.BlockSpec((None, cl, D_GDN), lambda b, c: (b, c, ZG_OFF // D_GDN)),
            pl.BlockSpec((None, cl, 128), lambda b, c: (b, c, SM_OFF // 128)),
            pl.BlockSpec((CONV_W, GDN_CONV_CH), lambda b, c: (0, 0)),
            vec(128), vec(128), vec(GDN_HD),
        ],
        out_specs=[
            pl.BlockSpec((None, cl, D_GDN), lambda b, c: (b, c, 0)),
            pl.BlockSpec((None, GDN_HEADS, GDN_HD, GDN_HD), lambda b, c: (b, 0, 0, 0)),
            pl.BlockSpec((None, CONV_W - 1, GDN_CONV_CH), lambda b, c: (b, 0, 0)),
        ],
        out_shape=[
            jax.ShapeDtypeStruct((B, L, D_GDN), bf16),
            jax.ShapeDtypeStruct((B, GDN_HEADS, GDN_HD, GDN_HD), f32),
            jax.ShapeDtypeStruct((B, CONV_W - 1, GDN_CONV_CH), f32),
        ],
        scratch_shapes=[
            pltpu.VMEM((8 + cl, GDN_CONV_CH), f32),
            pltpu.VMEM((GDN_HEADS, GDN_HD, GDN_HD), f32),
        ],
        compiler_params=_params(("parallel", "arbitrary")),
        name="gdn_prompt",
    )(proj3, proj3, proj3, lp["gdn_conv_w"], lp["gdn_dtb"], lp["gdn_alog"], lp["gdn_norm"])


def _cmp_prompt_kernel(k_ref, v_ref, pe_ref, w1_ref, w2_ref, kg_ref, kc_ref, vct_ref):
    nb = kc_ref.shape[0]
    for t, src in enumerate((k_ref, v_ref)):
        outs = []
        p1 = [jnp.zeros((nb, CMP_HID), f32) for _ in range(NSA_KV)]
        p2 = [jnp.zeros((nb, CMP_HID), f32) for _ in range(NSA_KV)]
        for l in range(CMP_STRIDE):
            xr = src[pl.ds(l, nb, stride=CMP_STRIDE), :]
            l2 = CMP_STRIDE + l
            for g in range(NSA_KV):
                xg = xr[:, g * NSA_HD:(g + 1) * NSA_HD]
                p1[g] += _dot(xg + pe_ref[t, l:l + 1, :], w1_ref[t, l * NSA_HD:(l + 1) * NSA_HD, :])
                p2[g] += _dot(xg + pe_ref[t, l2:l2 + 1, :], w1_ref[t, l2 * NSA_HD:(l2 + 1) * NSA_HD, :])
        for g in range(NSA_KV):
            hid = p1[g] + pltpu.roll(p2[g], nb - 1, 0)
            cmp = _dot(_silu(hid), w2_ref[t])
            if t == 0:
                cmp = cmp * lax.rsqrt(jnp.mean(cmp * cmp, axis=-1, keepdims=True) + EPS) * kg_ref[0:1, 0:NSA_HD]
            outs.append(cmp)
        both = jnp.concatenate(outs, axis=1)
        if t == 0:
            kc_ref[...] = both
        else:
            vct_ref[...] = both.T


def _cmp_prompt(kv6_3, lp):
    B, L, _ = kv6_3.shape
    nb = L // CMP_STRIDE
    full = lambda s: pl.BlockSpec(s, lambda b: (0,) * len(s))
    return pl.pallas_call(
        _cmp_prompt_kernel,
        grid=(B,),
        in_specs=[
            pl.BlockSpec((None, L, 128), lambda b: (b, 0, KV_CK // 128)),
            pl.BlockSpec((None, L, 128), lambda b: (b, 0, KV_CV // 128)),
            full((2, CMP_LEN, NSA_HD)), full((2, CMP_LEN * NSA_HD, CMP_HID)), full((2, CMP_HID, NSA_HD)),
            full((3, 128)),
        ],
        out_specs=[pl.BlockSpec((None, nb, 128), lambda b: (b, 0, 0)),
                   pl.BlockSpec((None, 128, nb), lambda b: (b, 0, 0))],
        out_shape=[jax.ShapeDtypeStruct((B, nb, 128), f32), jax.ShapeDtypeStruct((B, 128, nb), f32)],
        compiler_params=_params(("parallel",)),
        name="nsa_cmp_prompt",
    )(kv6_3, kv6_3, lp["cmp_pe"], lp["cmp_w1"], lp["cmp_w2"], lp["k_gain"])


def _nsa_prompt_kernel(rb_ref, qt_ref, ks_ref, kw_ref, vst_ref, vwt_ref, kc_ref, vct_ref, gt_ref,
                       y_ref, tb_ref, tc_ref, ext_ref, selm_ref, s_ref):
    qi = pl.program_id(1)
    L = ks_ref.shape[0]
    tq = NSA_Q_TILE
    nl = NSA_REP * tq
    ncmp = kc_ref.shape[0]
    n_slc = L // SLC_BLOCK
    last = N_BUCKETS - 1

    @pl.when(qi == 0)
    def _():
        sub = _iota((128, tq), 0)
        qo = _iota((128, tq), 1)
        d_diag = qo - sub
        c_rel = jnp.where(sub < 64, sub, sub - 128)
        d_cmp = qo - CMP_STRIDE * c_rel - (CMP_LEN - 1)
        for g in range(NSA_KV):
            for r in range(NSA_REP):
                h = g * NSA_REP + r
                tab = [rb_ref[b, h] - rb_ref[last, h] for b in range(N_BUCKETS)]
                lanes = slice(r * tq, (r + 1) * tq)
                tb_ref[g, TB_FAR, :, lanes] = jnp.zeros((128, tq), f32)
                tb_ref[g, TB_PREV, :, lanes] = _bias_chain(d_diag + 128, tab)
                tb_ref[g, TB_DIAG, :, lanes] = jnp.where(d_diag >= 0, _bias_chain(d_diag, tab), NEG)
                tb_ref[g, TB_WIN0, :, lanes] = jnp.where(d_diag <= 0, 0.0, NEG)
                tb_ref[g, TB_NONE, :, lanes] = jnp.full((128, tq), NEG, f32)
                t_c = jnp.where(d_cmp < 0, 0.0, _bias_chain(d_cmp, tab))
                tc_ref[g, 0:128, lanes] = t_c
                tc_ref[g, 128:256, lanes] = t_c
        ext_ref[...] = (_iota((L, 128), 0) // SLC_BLOCK == _iota((L, 128), 1)).astype(bf16)

    q0 = qi * tq
    qpos = q0 + _iota((1, nl), 1) % tq
    qpos1 = q0 + _iota((1, tq), 1)
    sub = _iota((128, 1), 0)
    gt = gt_ref[...]

    jr = _iota((n_slc, tq), 0)
    cur = qpos1 // SLC_BLOCK
    valid = jr * SLC_BLOCK <= qpos1
    forced = (jr == 0) | (jr == cur) | (jr == cur - 1)
    j_i = _iota((n_slc, ncmp), 0)
    c_i = _iota((n_slc, ncmp), 1)
    ovl = ((c_i * CMP_STRIDE < j_i * SLC_BLOCK + SLC_BLOCK) & (c_i * CMP_STRIDE + CMP_LEN > j_i * SLC_BLOCK)
           & (c_i < ncmp - 1)).astype(f32)
    cmp_off = pl.multiple_of((128 - qi * (tq // CMP_STRIDE)) % 128, 8)

    groups = range(NSA_KV)
    gsl = [slice(g * NSA_HD, (g + 1) * NSA_HD) for g in groups]
    qt = [jnp.concatenate([qt_ref[(g * NSA_REP + r) * NSA_HD:(g * NSA_REP + r + 1) * NSA_HD, :]
                           for r in range(NSA_REP)], axis=1).astype(bf16) for g in groups]

    ok_c = (sub * CMP_STRIDE + (CMP_LEN - 1) <= qpos) & (sub < ncmp - 1)
    o_cmp = []
    for g in groups:
        st = _dot(kc_ref[:, gsl[g]], qt[g]) + tc_ref[g, pl.ds(cmp_off, 128), :]
        st = jnp.where(ok_c, st, NEG)
        e = jnp.exp(st - jnp.max(st, axis=0, keepdims=True))
        p = jnp.where(ok_c, e / jnp.sum(e, axis=0, keepdims=True), 0.0)
        o_cmp.append(_dot(vct_ref[gsl[g], :], p))
        p_sum = p[:, 0:tq] + p[:, tq:2 * tq] + p[:, 2 * tq:3 * tq] + p[:, 3 * tq:4 * tq]
        imp = _dot_01l(ovl, p_sum)
        score = jnp.where(valid, jnp.where(forced, 1e9, imp), -1e9)
        rank = jnp.zeros((n_slc, tq), f32)
        for i in range(n_slc):
            si = score[i:i + 1, :]
            rank += ((si > score) | ((si == score) & (jr > i))).astype(f32)
        sel = (rank < SLC_TOPN).astype(bf16)
        sel = jnp.concatenate([sel, jnp.zeros((128 - n_slc, tq), bf16)], axis=0)
        selx = jnp.dot(ext_ref[...], sel, preferred_element_type=f32)
        selm_ref[g] = (selx - 1.0) * -NEG

    ones_rows = jnp.ones((16, 128), f32)

    def attend(state, tiles, k_ref, vt_ref):
        s, offs = {}, []
        for ti, (kt, table) in enumerate(tiles):
            ko = pl.multiple_of(kt * 128, 128)
            offs.append(ko)
            for g in groups:
                sg = _dot(k_ref[pl.ds(ko, 128), gsl[g]], qt[g]) + tb_ref[g, table]
                s[g, ti] = sg
        out = []
        for g in groups:
            m, acc = state[g]
            m_new = m
            for ti in range(len(tiles)):
                m_new = jnp.maximum(m_new, jnp.max(s[g, ti], axis=0, keepdims=True))
            acc = jnp.exp(m - m_new) * acc
            for ti in range(len(tiles)):
                vt = jnp.concatenate([vt_ref[gsl[g], pl.ds(offs[ti], 128)], ones_rows], axis=0)
                acc += _dot(vt, jnp.exp(s[g, ti] - m_new))
            out.append((m_new, acc))
        return tuple(out)

    finish = lambda st: [acc[0:NSA_HD] / acc[NSA_HD:NSA_HD + 1] for _, acc in st]
    init = tuple((jnp.full((1, nl), NEG, f32), jnp.zeros((NSA_HD + 16, nl), f32)) for _ in groups)

    def sel_table(kt):
        return jnp.where(kt == qi, TB_DIAG, jnp.where(kt == qi - 1, TB_PREV, TB_FAR))

    def score_tiles(kts, mx):
        mx = list(mx)
        for kt in kts:
            ko = pl.multiple_of(kt * 128, 128)
            for g in groups:
                sg = (_dot(ks_ref[pl.ds(ko, 128), gsl[g]], qt[g]) + tb_ref[g, sel_table(kt)]
                      + jnp.concatenate([selm_ref[g, pl.ds(ko, 128), :]] * NSA_REP, axis=1))
                s_ref[g, kt] = sg
                parts = [sg[8 * k:8 * k + 8, :] for k in range(16)]
                while len(parts) > 1:
                    parts = [jnp.maximum(a, b) for a, b in zip(parts[0::2], parts[1::2])]
                mx[g] = jnp.maximum(mx[g], parts[0])
        return tuple(mx)

    def value_tiles(kts, accs, m):
        accs = list(accs)
        for kt in kts:
            ko = pl.multiple_of(kt * 128, 128)
            for g in groups:
                vt = jnp.concatenate([vst_ref[gsl[g], pl.ds(ko, 128)], ones_rows], axis=0)
                accs[g] = accs[g] + _dot(vt, jnp.exp(s_ref[g, kt] - m[g]))
        return tuple(accs)

    n_pair, n_odd = (qi + 1) // 2, (qi + 1) % 2
    mx = tuple(jnp.full((8, nl), NEG, f32) for _ in groups)
    mx = lax.fori_loop(0, n_pair, lambda i, c: score_tiles([2 * i, 2 * i + 1], c), mx)
    mx = lax.fori_loop(0, n_odd, lambda i, c: score_tiles([qi], c), mx)
    m_sel = [jnp.max(x, axis=0, keepdims=True) for x in mx]
    accs = tuple(jnp.zeros((NSA_HD + 16, nl), f32) for _ in groups)
    accs = lax.fori_loop(0, n_pair, lambda i, c: value_tiles([2 * i, 2 * i + 1], c, m_sel), accs)
    accs = lax.fori_loop(0, n_odd, lambda i, c: value_tiles([qi], c, m_sel), accs)
    o_slc = [acc[0:NSA_HD] / acc[NSA_HD:NSA_HD + 1] for acc in accs]

    w0 = jnp.maximum(qi - WINDOW // 128, 0)

    def win_table(kt):
        t = jnp.where(kt == qi - WINDOW // 128, TB_WIN0, TB_FAR)
        t = jnp.where(kt == qi - 1, TB_PREV, t)
        return jnp.where(kt == qi, TB_DIAG, jnp.where(kt > qi, TB_NONE, t))

    win_tiles = [(w0 + t, win_table(w0 + t)) for t in range(WINDOW // 128 + 1)]
    o_win = finish(attend(init, win_tiles, kw_ref, vwt_ref))

    y_parts = []
    for g in groups:
        for r in range(NSA_REP):
            lanes = slice(r * tq, (r + 1) * tq)
            h = g * NSA_REP + r
            gate = lambda t: gt[SM_G + t * NSA_HEADS + h:SM_G + t * NSA_HEADS + h + 1, :]
            y_parts.append(gate(0) * o_cmp[g][:, lanes] + gate(1) * o_slc[g][:, lanes]
                           + gate(2) * o_win[g][:, lanes])
    y_ref[...] = jnp.concatenate(y_parts, axis=0).T.astype(y_ref.dtype)


def _nsa_prompt(qt, kv6_3, kv6t, kc, vct, gt, rel_bias):
    B, L, _ = kv6_3.shape
    tq = NSA_Q_TILE
    nb = L // CMP_STRIDE
    return pl.pallas_call(
        _nsa_prompt_kernel,
        grid=(B, L // tq),
        in_specs=[
            pl.BlockSpec(memory_space=pltpu.SMEM),
            pl.BlockSpec((None, D_NSA, tq), lambda b, i: (b, 0, i)),
            pl.BlockSpec((None, L, 128), lambda b, i: (b, 0, KV_SK // 128)),
            pl.BlockSpec((None, L, 128), lambda b, i: (b, 0, KV_WK // 128)),
            pl.BlockSpec((None, 128, L), lambda b, i: (b, KV_SV // 128, 0)),
            pl.BlockSpec((None, 128, L), lambda b, i: (b, KV_WV // 128, 0)),
            pl.BlockSpec((None, nb, 128), lambda b, i: (b, 0, 0)),
            pl.BlockSpec((None, 128, nb), lambda b, i: (b, 0, 0)),
            pl.BlockSpec((None, 128, tq), lambda b, i: (b, 0, i)),
        ],
        out_specs=pl.BlockSpec((None, tq, D_NSA), lambda b, i: (b, i, 0)),
        out_shape=jax.ShapeDtypeStruct((B, L, D_NSA), bf16),
        scratch_shapes=[
            pltpu.VMEM((NSA_KV, TB_KINDS, 128, NSA_REP * tq), f32),
            pltpu.VMEM((NSA_KV, 256, NSA_REP * tq), f32),
            pltpu.VMEM((L, 128), bf16),
            pltpu.VMEM((NSA_KV, L, tq), f32),
            pltpu.VMEM((NSA_KV, L // 128, 128, NSA_REP * tq), f32),
        ],
        compiler_params=_params(("parallel", "arbitrary")),
        name="nsa_prompt",
    )(rel_bias, qt, kv6_3, kv6_3, kv6t, kv6t, kc, vct, gt)


def _prep_t_kernel(q_ref, kv_ref, qg_ref, kg_ref, qt_ref, kv6_ref, kv6t_ref, gt_ref):
    for s in range(D_NSA // 128):
        sl = slice(s * 128, (s + 1) * 128)
        qt_ref[sl, :] = (_group_rms(q_ref[:, sl], NSA_HD) * qg_ref[...] * (NSA_HD ** -0.5)).T
    kv = kv_ref[...]
    for j, off in enumerate((KV_CK, KV_CV, KV_SK, KV_SV, KV_WK, KV_WV)):
        x = kv[:, off:off + 128]
        if off == KV_SK:
            x = _group_rms(x, NSA_HD) * kg_ref[1:2, :]
        if off == KV_WK:
            x = _group_rms(x, NSA_HD) * kg_ref[2:3, :]
        kv6_ref[:, off:off + 128] = x
        kv6t_ref[off:off + 128, :] = x.T
    gt_ref[...] = jax.nn.sigmoid(kv[:, SM_OFF - KV_OFF:SM_OFF - KV_OFF + 128]).T


def _prep_t(proj3, qgain, kgain):
    B, L, _ = proj3.shape
    tm = min(L, PREP_ROW_TILE)
    return pl.pallas_call(
        _prep_t_kernel,
        grid=(B, L // tm),
        in_specs=[
            pl.BlockSpec((None, tm, D_NSA), lambda b, i: (b, i, QN_OFF // D_NSA)),
            pl.BlockSpec((None, tm, 1024), lambda b, i: (b, i, KV_OFF // 1024)),
            pl.BlockSpec((1, 128), lambda b, i: (0, 0)),
            pl.BlockSpec((3, 128), lambda b, i: (0, 0)),
        ],
        out_specs=[pl.BlockSpec((None, D_NSA, tm), lambda b, i: (b, 0, i)),
                   pl.BlockSpec((None, tm, 768), lambda b, i: (b, i, 0)),
                   pl.BlockSpec((None, 768, tm), lambda b, i: (b, 0, i)),
                   pl.BlockSpec((None, 128, tm), lambda b, i: (b, 0, i))],
        out_shape=[jax.ShapeDtypeStruct((B, D_NSA, L), f32), jax.ShapeDtypeStruct((B, L, 768), f32),
                   jax.ShapeDtypeStruct((B, 768, L), f32), jax.ShapeDtypeStruct((B, 128, L), f32)],
        compiler_params=_params(("parallel", "parallel")),
        name="nsa_prep_t",
    )(proj3, proj3, qgain, kgain)


def _row0(x, rows=8):
    return jnp.where(_iota((rows, x.shape[1]), 0) == 0, x, 0.0)


def _ssd_decode_kernel(xbc_ref, z_ref, sm_ref, cprev_ref, sin_ref, cw_ref, cb_ref, dtb_ref, alog_ref, dexp_ref,
                       nrm_ref, y_ref, st_ref, cst_ref):
    x = xbc_ref[...]
    conv = cw_ref[3:4, :] * x
    for j in range(CONV_W - 1):
        conv += cw_ref[j:j + 1, :] * cprev_ref[j:j + 1, :]
    cst_ref[0:2, :] = cprev_ref[1:3, :]
    cst_ref[2:3, :] = x
    xc = _silu(conv + cb_ref[...])
    xs = xc[:, :D_SSD]
    dt = _softplus(sm_ref[...] + dtb_ref[...])
    da = dt * (-jnp.exp(alog_ref[...]))
    ys = []
    for g in range(SSD_GROUPS):
        bm = xc[:, D_SSD + g * SSD_STATE:D_SSD + (g + 1) * SSD_STATE]
        cm = xc[:, D_SSD + (SSD_GROUPS + g) * SSD_STATE:D_SSD + (SSD_GROUPS + g + 1) * SSD_STATE]
        cb = jnp.sum(cm * bm, axis=-1, keepdims=True)
        bm8, cm8 = _row0(bm), _row0(cm)
        for r in range(SSD_HEADS // SSD_GROUPS):
            h = g * (SSD_HEADS // SSD_GROUPS) + r
            xdt = xs[:, h * SSD_HD:(h + 1) * SSD_HD] * dt[:, h:h + 1]
            eda = jnp.exp(da[:, h:h + 1])
            h0 = sin_ref[h]
            ys.append(cb * xdt + eda * _dot_nt(cm8, h0, HI)[0:1])
            st_ref[h] = eda * h0 + _dot_tn(_row0(xdt), bm8, HI)
    y = (jnp.concatenate(ys, axis=1) + dexp_ref[...] * xs) * _silu(z_ref[...])
    gw = D_SSD // SSD_GROUPS
    for g in range(SSD_GROUPS):
        yg = y[:, g * gw:(g + 1) * gw]
        yg = yg * lax.rsqrt(jnp.mean(yg * yg, axis=-1, keepdims=True) + EPS)
        y_ref[:, g * gw:(g + 1) * gw] = yg * nrm_ref[:, g * gw:(g + 1) * gw]


def _ssd_decode(proj3, conv_state, state, layer, lp):
    B = proj3.shape[0]
    vec = lambda w: pl.BlockSpec((1, w), lambda b: (0, 0))
    return pl.pallas_call(
        _ssd_decode_kernel,
        grid=(B,),
        in_specs=[
            pl.BlockSpec((None, 1, SSD_CONV_CH), lambda b: (b, 0, XBC_OFF // SSD_CONV_CH)),
            pl.BlockSpec((None, 1, D_SSD), lambda b: (b, 0, ZS_OFF // D_SSD)),
            pl.BlockSpec((None, 1, 128), lambda b: (b, 0, SM_OFF // 128)),
            pl.BlockSpec((None, None, CONV_W - 1, SSD_CONV_CH), lambda b: (layer, b, 0, 0)),
            pl.BlockSpec((None, None, SSD_HEADS, SSD_HD, SSD_STATE), lambda b: (layer, b, 0, 0, 0)),
            pl.BlockSpec((CONV_W, SSD_CONV_CH), lambda b: (0, 0)),
            vec(SSD_CONV_CH), vec(128), vec(128), vec(D_SSD), vec(D_SSD),
        ],
        out_specs=[
            pl.BlockSpec((None, 1, D_SSD), lambda b: (b, 0, 0)),
            pl.BlockSpec((None, SSD_HEADS, SSD_HD, SSD_STATE), lambda b: (b, 0, 0, 0)),
            pl.BlockSpec((None, CONV_W - 1, SSD_CONV_CH), lambda b: (b, 0, 0)),
        ],
        out_shape=[
            jax.ShapeDtypeStruct((B, 1, D_SSD), f32),
            jax.ShapeDtypeStruct((B, SSD_HEADS, SSD_HD, SSD_STATE), f32),
            jax.ShapeDtypeStruct((B, CONV_W - 1, SSD_CONV_CH), f32),
        ],
        compiler_params=_params(("parallel",)),
        name="ssd_decode",
    )(proj3, proj3, proj3, conv_state, state, lp["ssd_conv_w"], lp["ssd_conv_b"], lp["ssd_dtb"], lp["ssd_alog"],
      lp["ssd_dexp"], lp["ssd_norm"])


def _gdn_decode_kernel(qkv_ref, z_ref, sm_ref, cprev_ref, sin_ref, cw_ref, dtb_ref, alog_ref, nrm_ref,
                       y_ref, st_ref, cst_ref):
    x = qkv_ref[...]
    conv = cw_ref[3:4, :] * x
    for j in range(CONV_W - 1):
        conv += cw_ref[j:j + 1, :] * cprev_ref[j:j + 1, :]
    cst_ref[0:2, :] = cprev_ref[1:3, :]
    cst_ref[2:3, :] = x
    xc = _silu(conv)
    sm = sm_ref[...]
    beta_all = jax.nn.sigmoid(sm)
    g_all = -jnp.exp(alog_ref[...]) * _softplus(sm + dtb_ref[...])
    for h in range(GDN_HEADS):
        sl = slice(h * GDN_HD, (h + 1) * GDN_HD)
        qh = xc[:, sl]
        kh = xc[:, D_GDN + h * GDN_HD:D_GDN + (h + 1) * GDN_HD]
        vh = xc[:, 2 * D_GDN + h * GDN_HD:2 * D_GDN + (h + 1) * GDN_HD]
        qh = qh * lax.rsqrt(jnp.sum(qh * qh, axis=-1, keepdims=True) + EPS) * (GDN_HD ** -0.5)
        kh = kh * lax.rsqrt(jnp.sum(kh * kh, axis=-1, keepdims=True) + EPS)
        beta = beta_all[:, SM_B + h:SM_B + h + 1]
        eg = jnp.exp(g_all[:, SM_A + h:SM_A + h + 1])
        s0 = sin_ref[h]
        k8 = _row0(kh)
        v_new = beta * vh - (beta * eg) * _dot(k8, s0, HI)[0:1]
        o = eg * _dot(_row0(qh), s0, HI)[0:1] + jnp.sum(qh * kh, axis=-1, keepdims=True) * v_new
        st_ref[h] = eg * s0 + _dot_tn(k8, _row0(v_new), HI)
        o = o * lax.rsqrt(jnp.mean(o * o, axis=-1, keepdims=True) + EPS)
        y_ref[:, sl] = o * nrm_ref[...] * _silu(z_ref[:, sl])


def _gdn_decode(proj3, conv_state, state, layer, lp):
    B = proj3.shape[0]
    vec = lambda w: pl.BlockSpec((1, w), lambda b: (0, 0))
    return pl.pallas_call(
        _gdn_decode_kernel,
        grid=(B,),
        in_specs=[
            pl.BlockSpec((None, 1, GDN_CONV_CH), lambda b: (b, 0, QKV_OFF // GDN_CONV_CH)),
            pl.BlockSpec((None, 1, D_GDN), lambda b: (b, 0, ZG_OFF // D_GDN)),
            pl.BlockSpec((None, 1, 128), lambda b: (b, 0, SM_OFF // 128)),
            pl.BlockSpec((None, None, CONV_W - 1, GDN_CONV_CH), lambda b: (layer, b, 0, 0)),
            pl.BlockSpec((None, None, GDN_HEADS, GDN_HD, GDN_HD), lambda b: (layer, b, 0, 0, 0)),
            pl.BlockSpec((CONV_W, GDN_CONV_CH), lambda b: (0, 0)),
            vec(128), vec(128), vec(GDN_HD),
        ],
        out_specs=[
            pl.BlockSpec((None, 1, D_GDN), lambda b: (b, 0, 0)),
            pl.BlockSpec((None, GDN_HEADS, GDN_HD, GDN_HD), lambda b: (b, 0, 0, 0)),
            pl.BlockSpec((None, CONV_W - 1, GDN_CONV_CH), lambda b: (b, 0, 0)),
        ],
        out_shape=[
            jax.ShapeDtypeStruct((B, 1, D_GDN), f32),
            jax.ShapeDtypeStruct((B, GDN_HEADS, GDN_HD, GDN_HD), f32),
            jax.ShapeDtypeStruct((B, CONV_W - 1, GDN_CONV_CH), f32),
        ],
        compiler_params=_params(("parallel",)),
        name="gdn_decode",
    )(proj3, proj3, proj3, conv_state, state, lp["gdn_conv_w"], lp["gdn_dtb"], lp["gdn_alog"], lp["gdn_norm"])


def _query_rows(q, g):
    rows = [q[:, (g * NSA_REP + r) * NSA_HD:(g * NSA_REP + r + 1) * NSA_HD] for r in range(NSA_REP)]
    return jnp.concatenate(rows + [jnp.zeros((8 - NSA_REP, NSA_HD), f32)], axis=0) * (NSA_HD ** -0.5)


def _masked_softmax(lg, ok):
    lg = jnp.where(ok, lg, NEG)
    e = jnp.exp(lg - jnp.max(lg, axis=-1, keepdims=True))
    return jnp.where(ok, e / jnp.sum(e, axis=-1, keepdims=True), 0.0)


def _head_lanes(parts):
    return jnp.concatenate([o[r:r + 1, :] for o in parts for r in range(NSA_REP)], axis=1)


def _nsa_dec_cmp_kernel(pt_ref, q_ref, pe_ref, w1_ref, w2_ref, kg_ref, tab_ref, pk_ref, pv_ref,
                        ocmp_ref, idx_ref, stage, rows_k, rows_v, sem, *, layer, n_pages):
    b = pl.program_id(0)
    nb = pl.num_programs(0)
    n_past = n_pages * PAGE
    q_pos = n_past
    n_blk = n_past // CMP_STRIDE
    n_cmp = (n_past + 1 - CMP_LEN) // CMP_STRIDE + 1
    n_slc = -(-(n_past + 1) // SLC_BLOCK)
    nj = -(-n_slc // 128) * 128
    pools = (pk_ref, pv_ref)

    def page_copy(t, seq, p):
        return pltpu.make_async_copy(pools[t].at[layer, pt_ref[seq, p]], stage.at[t, p], sem.at[t])

    def start_all(t, seq):
        lax.fori_loop(0, n_pages, lambda p, c: (page_copy(t, seq, p).start(), c)[1], 0)

    def wait_all(t, seq):
        lax.fori_loop(0, n_pages, lambda p, c: (page_copy(t, seq, p).wait(), c)[1], 0)

    @pl.when(b == 0)
    def _():
        start_all(0, 0)
        start_all(1, 0)

    n_pair = CMP_STRIDE // 2
    share = n_pages // n_pair

    def compress(t, rows, between=None):
        p1 = jnp.zeros((n_blk, 2 * CMP_HID), f32)
        p2 = jnp.zeros((n_blk, 2 * CMP_HID), f32)
        for j in range(n_pair):
            xr = jnp.concatenate([rows[pl.ds(2 * j, n_blk, stride=CMP_STRIDE), :],
                                  rows[pl.ds(2 * j + 1, n_blk, stride=CMP_STRIDE), :]], axis=1)
            j2 = n_pair + j
            p1 += jnp.dot((xr + pe_ref[t, j:j + 1, :]).astype(bf16), w1_ref[t, j], preferred_element_type=f32)
            p2 += jnp.dot((xr + pe_ref[t, j2:j2 + 1, :]).astype(bf16), w1_ref[t, j2], preferred_element_type=f32)
            if between is not None:
                between(j)
        hid = p1 + pltpu.roll(p2, n_blk - 1, 0)
        return jnp.dot(_silu(hid).astype(bf16), w2_ref[t], preferred_element_type=f32)

    def value_pages(j):
        for p in range(j * share, (j + 1) * share):
            rows_v[p * PAGE:(p + 1) * PAGE, :] = stage[1, p].T

    def next_key_pages(j):
        for p in range(j * share, (j + 1) * share):
            rows_k[p * PAGE:(p + 1) * PAGE, :] = stage[0, p].T

    @pl.when(b == 0)
    def _():
        wait_all(0, 0)

        def key_page(p, c):
            rows_k[pl.ds(pl.multiple_of(p * PAGE, PAGE), PAGE), :] = stage[0, p].T
            return c
        lax.fori_loop(0, n_pages, key_page, 0, unroll=4)

    @pl.when(b + 1 < nb)
    def _():
        start_all(0, b + 1)

    wait_all(1, b)
    kc = _group_rms(compress(0, rows_k, value_pages), NSA_HD) * kg_ref[0:1, :]

    @pl.when(b + 1 < nb)
    def _():
        start_all(1, b + 1)
        wait_all(0, b + 1)

    vc = compress(1, rows_v, next_key_pages)

    q = q_ref[...]
    lane = _iota((8, n_blk), 1)
    cend = lane * CMP_STRIDE + (CMP_LEN - 1)
    ok = (cend <= q_pos) & (lane < n_cmp)
    c_i = _iota((n_blk, nj), 0)
    j_i = _iota((n_blk, nj), 1)
    ovl = ((c_i * CMP_STRIDE < j_i * SLC_BLOCK + SLC_BLOCK) & (c_i * CMP_STRIDE + CMP_LEN > j_i * SLC_BLOCK)
           & (j_i < n_slc) & (c_i < n_cmp)).astype(f32)
    jl = _iota((1, nj), 1)
    cur = q_pos // SLC_BLOCK
    valid = jl * SLC_BLOCK <= q_pos
    forced = (jl == 0) | (jl == cur) | (jl == cur - 1)
    ii = _iota((nj, nj), 0)
    jj = _iota((nj, nj), 1)
    kk = _iota((SLC_TOPN, nj), 0)
    j16 = _iota((SLC_TOPN, nj), 1)
    o_parts, idx_cols = [], []
    for g in range(NSA_KV):
        tabcols = [tab_ref[g, :, bk:bk + 1] for bk in range(N_BUCKETS)]
        lg = _dot_nt(_query_rows(q, g), kc[:, g * NSA_HD:(g + 1) * NSA_HD]) + _bias_chain(q_pos - cend, tabcols)
        p = _masked_softmax(lg, ok)
        o_parts.append(_dot(p, vc[:, g * NSA_HD:(g + 1) * NSA_HD]))
        p_sum = p[0:1] + p[1:2] + p[2:3] + p[3:4]
        imp = _dot_01(_row0(p_sum), ovl)[0:1]
        score = jnp.where(valid, jnp.where(forced, 1e9, imp), -1e9)
        score = jnp.where(jl < n_slc, score, -3e9)
        s_c = jnp.sum(jnp.where(ii == jj, score, 0.0), axis=1, keepdims=True)
        rank_r = jnp.sum(((s_c > score) | ((s_c == score) & (ii < jj))).astype(f32), axis=0, keepdims=True)
        rank_c = jnp.sum(((score > s_c) | ((score == s_c) & (jj < ii))).astype(f32), axis=1, keepdims=True)
        sel_r = (rank_r < SLC_TOPN) & (jl < n_slc)
        sel_c = (rank_c < SLC_TOPN) & (ii[:, 0:1] < n_slc)
        pos_r = jnp.sum((sel_c & (ii < jj)).astype(f32), axis=0, keepdims=True)
        hit = (pos_r == kk.astype(f32)) & sel_r
        idx_cols.append(jnp.sum(jnp.where(hit, j16.astype(f32), 0.0), axis=1, keepdims=True))
    ocmp_ref[...] = _head_lanes(o_parts)
    l16 = _iota((SLC_TOPN, 128), 1)
    idx_ref[...] = (jnp.where(l16 == 0, idx_cols[0], 0.0) + jnp.where(l16 == 1, idx_cols[1], 0.0)).astype(i32)


def _nsa_dec_cmp(qn3, page_table, pool_k, pool_v, layer, lp):
    B = qn3.shape[0]
    n_pages = page_table.shape[1]
    full = lambda s: pl.BlockSpec(s, lambda b, pt: (0,) * len(s))
    return pl.pallas_call(
        functools.partial(_nsa_dec_cmp_kernel, layer=layer, n_pages=n_pages),
        grid_spec=pltpu.PrefetchScalarGridSpec(
            num_scalar_prefetch=1,
            grid=(B,),
            in_specs=[
                pl.BlockSpec((None, 1, D_NSA), lambda b, pt: (b, 0, 0)),
                full((2, CMP_LEN // 2, 256)), full((2, CMP_LEN // 2, 256, 2 * CMP_HID)),
                full((2, 2 * CMP_HID, 128)), full((3, 128)), full((NSA_KV, 8, N_BUCKETS)),
                pl.BlockSpec(memory_space=pl.ANY), pl.BlockSpec(memory_space=pl.ANY),
            ],
            out_specs=[pl.BlockSpec((None, 1, D_NSA), lambda b, pt: (b, 0, 0)),
                       pl.BlockSpec((None, SLC_TOPN, 128), lambda b, pt: (b, 0, 0))],
            scratch_shapes=[pltpu.VMEM((2, n_pages, NSA_KV * NSA_HD, PAGE), f32),
                            pltpu.VMEM((n_pages * PAGE, NSA_KV * NSA_HD), f32),
                            pltpu.VMEM((n_pages * PAGE, NSA_KV * NSA_HD), f32),
                            pltpu.SemaphoreType.DMA((2,))],
        ),
        out_shape=[jax.ShapeDtypeStruct((B, 1, D_NSA), f32), jax.ShapeDtypeStruct((B, SLC_TOPN, 128), i32)],
        compiler_params=_params(("arbitrary",)),
        name="nsa_dec_cmp",
    )(page_table, qn3, lp["cmp_pe2"], lp["cmp_w1bd"], lp["cmp_w2bd"], lp["k_gain"], lp["tab8"], pool_k, pool_v)


def _nsa_dec_sel_kernel(pt_ref, idx_ref, q_ref, kv_ref, sm_ref, ocmp_ref, tab_ref, wk_ref, wv_ref, pk_ref, pv_ref,
                        y_ref, wko_ref, wvo_ref, kg, vg, sem, *, layer, n_pages):
    b = pl.program_id(0)
    n_past = n_pages * PAGE
    q_pos = n_past
    n_buf = wk_ref.shape[1]
    new_blk = n_past // SLC_BLOCK
    per_page = PAGE // SLC_BLOCK

    def blk_copy(pool, buf, g, s, sm_i):
        j = jnp.minimum(idx_ref[b, s, g], new_blk - 1)
        src = pool.at[layer, pt_ref[b, j // per_page], pl.ds(g * NSA_HD, NSA_HD)]
        return pltpu.make_async_copy(src, buf.at[g, :, pl.ds(s * PAGE, PAGE)], sem.at[sm_i])

    for g in range(NSA_KV):
        for s in range(SLC_TOPN):
            blk_copy(pk_ref, kg, g, s, 0).start()
            blk_copy(pv_ref, vg, g, s, 1).start()

    kv = kv_ref[...]
    q = q_ref[...]
    gate = jax.nn.sigmoid(sm_ref[...])
    ocmp = ocmp_ref[...]

    ii, jj = _iota((128, 128), 0), _iota((128, 128), 1)
    column = lambda off: jnp.sum(jnp.where(ii == jj, kv[:, off:off + 128], 0.0), axis=1, keepdims=True)
    last_lane = _iota((128, n_buf), 1) == n_buf - 1
    wk, wv = wk_ref[...], wv_ref[...]
    wko_ref[...] = jnp.where(last_lane, column(KV_WK), pltpu.roll(wk, n_buf - 1, 1))
    wvo_ref[...] = jnp.where(last_lane, column(KV_WV), pltpu.roll(wv, n_buf - 1, 1))

    def attend(qg, kt, vt, bias, ok, k_new, v_new, bias_new, ok_new):
        lg = jnp.where(ok, _dot(qg, kt) + bias, NEG)
        lg_new = jnp.where(ok_new, jnp.sum(qg * k_new, axis=-1, keepdims=True) + bias_new, NEG)
        m = jnp.maximum(jnp.max(lg, axis=-1, keepdims=True), lg_new)
        e = jnp.where(ok, jnp.exp(lg - m), 0.0)
        e_new = jnp.where(ok_new, jnp.exp(lg_new - m), 0.0)
        den = jnp.sum(e, axis=-1, keepdims=True) + e_new
        p, p_new = e / den, e_new / den
        return _dot_nt(p, vt) + p_new * v_new

    for g in range(NSA_KV):
        for s in range(SLC_TOPN):
            blk_copy(pk_ref, kg, g, s, 0).wait()
            blk_copy(pv_ref, vg, g, s, 1).wait()

    lw = _iota((8, n_buf), 1)
    d_w = q_pos - (n_past - n_buf + lw)
    ok_w = (d_w >= 0) & (d_w <= WINDOW)
    ls = _iota((8, SLC_TOPN * PAGE), 1)
    slot = ls // PAGE
    zero = jnp.zeros((8, 1), i32)
    o_slc, o_win = [], []
    for g in range(NSA_KV):
        tabcols = [tab_ref[g, :, bk:bk + 1] for bk in range(N_BUCKETS)]
        qg = _query_rows(q, g)
        gs = slice(g * NSA_HD, (g + 1) * NSA_HD)
        b_new = _bias_chain(zero, tabcols)
        blk = jnp.zeros(ls.shape, i32)
        n_new = jnp.zeros((8, 1), i32)
        for s in range(SLC_TOPN):
            j = idx_ref[b, s, g]
            blk = jnp.where(slot == s, j, blk)
            n_new = jnp.where(j == new_blk, n_new + 1, n_new)
        k_pos = (jnp.minimum(blk, new_blk - 1) // per_page) * PAGE + ls % PAGE
        ok_s = (blk < new_blk) & (k_pos // SLC_BLOCK == blk)
        o_slc.append(attend(qg, kg[g], vg[g], _bias_chain(q_pos - k_pos, tabcols), ok_s,
                            kv[:, KV_SK + g * NSA_HD:KV_SK + (g + 1) * NSA_HD],
                            kv[:, KV_SV + g * NSA_HD:KV_SV + (g + 1) * NSA_HD], b_new, n_new > 0))
        o_win.append(attend(qg, wk[gs, :], wv[gs, :], _bias_chain(d_w, tabcols), ok_w,
                            kv[:, KV_WK + g * NSA_HD:KV_WK + (g + 1) * NSA_HD],
                            kv[:, KV_WV + g * NSA_HD:KV_WV + (g + 1) * NSA_HD], b_new, zero == 0))
    gl = lambda t: gate[:, SM_G + t * NSA_HEADS:SM_G + (t + 1) * NSA_HEADS]
    wide = lambda gt: jnp.concatenate([jnp.broadcast_to(gt[:, h:h + 1], (1, NSA_HD)) for h in range(NSA_HEADS)], axis=1)
    y_ref[...] = wide(gl(0)) * ocmp + wide(gl(1)) * _head_lanes(o_slc) + wide(gl(2)) * _head_lanes(o_win)


def _nsa_dec_sel(qn3, kv6_3, proj3, ocmp, idx, page_table, win_k, win_v, pool_k, pool_v, layer, lp):
    B = qn3.shape[0]
    n_pages = page_table.shape[1]
    n_buf = win_k.shape[3]
    full = lambda s: pl.BlockSpec(s, lambda b, pt, ix: (0,) * len(s))
    row = lambda w, j: pl.BlockSpec((None, 1, w), lambda b, pt, ix: (b, 0, j))
    win = pl.BlockSpec((None, None, 128, n_buf), lambda b, pt, ix: (layer, b, 0, 0))
    return pl.pallas_call(
        functools.partial(_nsa_dec_sel_kernel, layer=layer, n_pages=n_pages),
        grid_spec=pltpu.PrefetchScalarGridSpec(
            num_scalar_prefetch=2,
            grid=(B,),
            in_specs=[row(D_NSA, 0), row(768, 0), row(128, SM_OFF // 128), row(D_NSA, 0),
                      full((NSA_KV, 8, N_BUCKETS)), win, win,
                      pl.BlockSpec(memory_space=pl.ANY), pl.BlockSpec(memory_space=pl.ANY)],
            out_specs=[row(D_NSA, 0),
                       pl.BlockSpec((None, 128, n_buf), lambda b, pt, ix: (b, 0, 0)),
                       pl.BlockSpec((None, 128, n_buf), lambda b, pt, ix: (b, 0, 0))],
            scratch_shapes=[pltpu.VMEM((NSA_KV, NSA_HD, SLC_TOPN * PAGE), f32),
                            pltpu.VMEM((NSA_KV, NSA_HD, SLC_TOPN * PAGE), f32),
                            pltpu.SemaphoreType.DMA((2,))],
        ),
        out_shape=[jax.ShapeDtypeStruct((B, 1, D_NSA), f32),
                   jax.ShapeDtypeStruct((B, 128, n_buf), f32), jax.ShapeDtypeStruct((B, 128, n_buf), f32)],
        compiler_params=_params(("arbitrary",)),
        name="nsa_dec_sel",
    )(page_table, idx, qn3, kv6_3, proj3, ocmp, lp["tab8"], win_k, win_v, pool_k, pool_v)


def _layer_params(i, p):
    pad128 = lambda v, off: jnp.zeros((1, 128), f32).at[0, off:off + v.shape[0]].set(v)
    return dict(
        ssd_conv_w=p["ssd_conv_w"][i], ssd_conv_b=p["ssd_conv_b"][i][None],
        ssd_dtb=pad128(p["ssd_dt_bias"][i], SM_DT), ssd_alog=pad128(p["ssd_a_log"][i], SM_DT),
        ssd_dexp=jnp.repeat(p["ssd_d"][i], SSD_HD)[None], ssd_norm=p["ssd_norm"][i][None],
        gdn_conv_w=p["gdn_conv_w"][i],
        gdn_dtb=pad128(p["gdn_dt_bias"][i], SM_A), gdn_alog=pad128(p["gdn_a_log"][i], SM_A),
        gdn_norm=p["gdn_norm"][i][None],
        q_gain=jnp.tile(p["nsa_q_norm"][i], 2)[None], k_gain=jnp.tile(p["nsa_k_norm"][i], (1, 2)),
        cmp_pe=p["nsa_cmp_pe"][i], cmp_w1=p["nsa_cmp_w1"][i], cmp_w2=p["nsa_cmp_w2"][i],
        cmp_pe2=jnp.tile(p["nsa_cmp_pe"][i], (1, 1, 2)).reshape(2, CMP_LEN // 2, 256),
        cmp_w1bd=_block_diag2(p["nsa_cmp_w1"][i].reshape(2, CMP_LEN, NSA_HD, CMP_HID)).astype(bf16)
        .reshape(2, CMP_LEN // 2, 256, 2 * CMP_HID),
        cmp_w2bd=_block_diag2(p["nsa_cmp_w2"][i]).astype(bf16),
        tab8=jnp.pad(p["rel_bias"].T.reshape(NSA_KV, NSA_REP, N_BUCKETS), ((0, 0), (0, 8 - NSA_REP), (0, 0))),
    )


def _block_diag2(w):
    z = jnp.zeros_like(w)
    return jnp.concatenate([jnp.concatenate([w, z], axis=-1), jnp.concatenate([z, w], axis=-1)], axis=-2)


def _reorder_w_in(w_in):
    cuts = [(1024, 2560), (2576, 4112), (0, 1024), (4112, 4624), (4632, 5144), (5144, 5912),
            (2560, 2576), (4624, 4632), (5912, 5936)]
    wt = jnp.transpose(w_in, (0, 2, 1))
    parts = [wt[:, a:b] for a, b in cuts]
    used = sum(b - a for a, b in cuts)
    parts.append(jnp.zeros((w_in.shape[0], PROJ_W - used, w_in.shape[1]), w_in.dtype))
    return jnp.concatenate(parts, axis=1).astype(bf16)


def _mix_prompt(x3, i, p, lp, w_r):
    B, L, _ = x3.shape
    n = B * L
    x = x3.reshape(n, D_MODEL)
    proj = _inproj(x, p["norm_mix"], w_r, i)
    proj3 = proj.reshape(B, L, PROJ_W)
    y_ssd, st_ssd, cst_ssd = _ssd_prompt(proj3, lp)
    y_gdn, st_gdn, cst_gdn = _gdn_prompt(proj3, lp)
    qt, kv6_3, kv6t, gt = _prep_t(proj3, lp["q_gain"], lp["k_gain"])
    kc, vct = _cmp_prompt(kv6_3, lp)
    y_nsa = _nsa_prompt(qt, kv6_3, kv6t, kc, vct, gt, p["rel_bias"])
    x = _outproj(x, y_ssd.reshape(n, D_SSD), y_gdn.reshape(n, D_GDN), y_nsa.reshape(n, D_NSA), p["w_out"], i)
    rows = lambda off: jnp.transpose(kv6t[:, off:off + 128].reshape(B, NSA_KV, NSA_HD, L), (0, 3, 1, 2))
    keep = min(WINDOW, L)
    state = (st_ssd, cst_ssd, st_gdn, cst_gdn, rows(KV_CK), rows(KV_CV), rows(KV_SK), rows(KV_SV),
             rows(KV_WK)[:, L - keep:], rows(KV_WV)[:, L - keep:])
    return x.reshape(B, L, D_MODEL), state


def _mix_decode(x, i, p, lp, w_r, st):
    B = x.shape[0]
    proj = _inproj(x, p["norm_mix"], w_r, i)
    proj3 = proj.reshape(B, 1, PROJ_W)
    y_ssd, st_ssd, cst_ssd = _ssd_decode(proj3, st["ssd_conv"], st["ssd"], i, lp)
    y_gdn, st_gdn, cst_gdn = _gdn_decode(proj3, st["gdn_conv"], st["gdn"], i, lp)
    qn, kv6 = _prep(proj, lp["q_gain"], lp["k_gain"])
    qn3, kv6_3 = qn.reshape(B, 1, D_NSA), kv6.reshape(B, 1, 768)
    o_cmp, idx = _nsa_dec_cmp(qn3, st["page_table"], st["cmp_k"], st["cmp_v"], i, lp)
    y_nsa, win_k, win_v = _nsa_dec_sel(qn3, kv6_3, proj3, o_cmp, idx[:, :, :NSA_KV], st["page_table"],
                                       st["win_k"], st["win_v"], st["slc_k"], st["slc_v"], i, lp)
    x = _outproj(x, y_ssd.reshape(B, D_SSD), y_gdn.reshape(B, D_GDN), y_nsa.reshape(B, D_NSA), p["w_out"], i)
    rows = lambda off: kv6[:, off:off + 128].reshape(B, 1, NSA_KV, NSA_HD)
    n_buf = win_k.shape[2]
    buf = lambda w: jnp.transpose(w.reshape(B, NSA_KV, NSA_HD, n_buf), (0, 3, 1, 2))
    state = (st_ssd, cst_ssd, st_gdn, cst_gdn, rows(KV_CK), rows(KV_CV), rows(KV_SK), rows(KV_SV),
             buf(win_k), buf(win_v))
    return x, state


def kernel(x_prompt, x_sample, state_ssd, state_ssd_conv, state_gdn, state_gdn_conv, cache_cmp_k, cache_cmp_v,
           cache_slc_k, cache_slc_v, cache_win_k, cache_win_v, page_table, rel_bias, norm_ffn1, w_ffn1_gate,
           w_ffn1_up, w_ffn1_down, norm_mix, w_in, ssd_conv_w, ssd_conv_b, ssd_dt_bias, ssd_a_log, ssd_d, ssd_norm,
           gdn_conv_w, gdn_dt_bias, gdn_a_log, gdn_norm, nsa_q_norm, nsa_k_norm, nsa_cmp_pe, nsa_cmp_w1,
           nsa_cmp_w2, w_out, norm_ffn2, w_ffn2_gate, w_ffn2_up, w_ffn2_down):
    bp, lp_len, _ = x_prompt.shape
    bs = x_sample.shape[0]
    gain3 = lambda g: g.reshape(DEPTH, 1, D_MODEL)
    p = dict(rel_bias=rel_bias, norm_mix=gain3(norm_mix), w_out=w_out.astype(bf16), ssd_conv_w=ssd_conv_w,
             ssd_conv_b=ssd_conv_b,
             ssd_dt_bias=ssd_dt_bias, ssd_a_log=ssd_a_log, ssd_d=ssd_d, ssd_norm=ssd_norm, gdn_conv_w=gdn_conv_w,
             gdn_dt_bias=gdn_dt_bias, gdn_a_log=gdn_a_log, gdn_norm=gdn_norm, nsa_q_norm=nsa_q_norm,
             nsa_k_norm=nsa_k_norm, nsa_cmp_pe=nsa_cmp_pe, nsa_cmp_w1=nsa_cmp_w1, nsa_cmp_w2=nsa_cmp_w2)
    n1, n2 = gain3(norm_ffn1), gain3(norm_ffn2)
    chan_row = lambda c: jnp.transpose(c, (0, 1, 3, 4, 2)).reshape(c.shape[:2] + (NSA_KV * NSA_HD, c.shape[2]))
    st = dict(ssd=state_ssd, ssd_conv=state_ssd_conv, gdn=state_gdn, gdn_conv=state_gdn_conv,
              cmp_k=chan_row(cache_cmp_k), cmp_v=chan_row(cache_cmp_v), slc_k=chan_row(cache_slc_k),
              slc_v=chan_row(cache_slc_v), win_k=chan_row(cache_win_k), win_v=chan_row(cache_win_v),
              page_table=page_table)
    w_r = _reorder_w_in(w_in)

    hp = x_prompt.reshape(bp * lp_len, D_MODEL)
    hs = x_sample.reshape(bs, D_MODEL)
    outs_p, outs_s = [], []
    for i in range(DEPTH):
        lp = _layer_params(i, p)
        hp, hs = _ffn(hp, hs, n1, w_ffn1_gate, w_ffn1_up, w_ffn1_down, i)
        hp3, st_p = _mix_prompt(hp.reshape(bp, lp_len, D_MODEL), i, p, lp, w_r)
        hs, st_s = _mix_decode(hs, i, p, lp, w_r, st)
        hp, hs = _ffn(hp3.reshape(bp * lp_len, D_MODEL), hs, n2, w_ffn2_gate, w_ffn2_up, w_ffn2_down, i)
        outs_p.append(st_p)
        outs_s.append(st_s)
    stack = lambda outs: [jnp.stack(t) for t in zip(*outs)]
    return (hp.reshape(bp, lp_len, D_MODEL), hs.reshape(bs, 1, D_MODEL), *stack(outs_p), *stack(outs_s))
```

```python
import functools
import math

import numpy as np
import jax
import jax.numpy as jnp
from jax import lax
from jax.experimental import pallas as pl
from jax.experimental.pallas import tpu as pltpu

f32 = jnp.float32
bf16 = jnp.bfloat16
i32 = jnp.int32
HI = lax.Precision.HIGHEST

D_MODEL = 2048
DEPTH = 4
PAGE = 128
D_SSD = 1024
SSD_HD = 64
SSD_HEADS = 16
SSD_GROUPS = 2
SSD_STATE = 128
SSD_CHUNK = 128
D_GDN = 512
GDN_HD = 128
GDN_HEADS = 4
GDN_CHUNK = 64
GDN_STEP_CHUNKS = 4
D_NSA = 512
NSA_HD = 64
NSA_HEADS = 8
NSA_KV = 2
NSA_REP = 4
CMP_STRIDE = 16
CMP_LEN = 32
CMP_HID = 128
SLC_BLOCK = 64
SLC_TOPN = 16
WINDOW = 512
N_BUCKETS = 32
MAX_DISTANCE = 128
CONV_W = 4
D_FF = 5632
SSD_CONV_CH = 1536
GDN_CONV_CH = 1536
EPS = 1e-6
NEG = -1e30

XBC_OFF = 0
QKV_OFF = 1536
ZS_OFF = 3072
ZG_OFF = 4096
QN_OFF = 4608
KV_OFF = 5120
SM_OFF = 5888
PROJ_W = 6144
SM_DT, SM_A, SM_B, SM_G = 0, 16, 20, 24
KV_CK, KV_CV, KV_SK, KV_SV, KV_WK, KV_WV = 0, 128, 256, 384, 512, 640

VMEM_LIMIT = 56 * 1024 * 1024
ROW_TILE = 1024
FFN_COL_TILE = 256
PROJ_COL_TILE = 512
OUT_ROW_TILE = 2048
PREP_ROW_TILE = 512
NSA_Q_TILE = 128
TB_FAR, TB_PREV, TB_DIAG, TB_WIN0, TB_NONE, TB_KINDS = 0, 1, 2, 3, 4, 5


def _bucket_thresholds():
    exact = N_BUCKETS // 2
    d = np.arange(0, 4 * MAX_DISTANCE)
    nf = np.maximum(d, 1).astype(np.float32)
    large = exact + (np.log(nf / np.float32(exact)) / np.float32(math.log(MAX_DISTANCE / exact))
                     * np.float32(N_BUCKETS - exact)).astype(np.int32)
    bucket = np.where(d < exact, d, np.minimum(large, N_BUCKETS - 1))
    return [int(np.argmax(bucket >= b)) for b in range(N_BUCKETS)]


_THR = _bucket_thresholds()
BIAS_FAR = _THR[N_BUCKETS - 1]


def _bias_chain(d, tabcols):
    v = jnp.broadcast_to(tabcols[N_BUCKETS - 1], d.shape)
    for b in range(N_BUCKETS - 2, -1, -1):
        v = jnp.where(d < _THR[b + 1], tabcols[b], v)
    return v


def _silu(x):
    return x * jax.nn.sigmoid(x)


def _softplus(x):
    return jnp.maximum(x, 0.0) + jnp.log1p(jnp.exp(-jnp.abs(x)))


def _dot(a, b, prec=None):
    if prec is None:
        a, b = a.astype(bf16), b.astype(bf16)
    return jnp.dot(a, b, preferred_element_type=f32, precision=prec)


def _dot_nt(a, b, prec=None):
    if prec is None:
        a, b = a.astype(bf16), b.astype(bf16)
    return lax.dot_general(a, b, (((1,), (1,)), ((), ())), preferred_element_type=f32, precision=prec)


def _dot_tn(a, b, prec=None):
    if prec is None:
        a, b = a.astype(bf16), b.astype(bf16)
    return lax.dot_general(a, b, (((0,), (0,)), ((), ())), preferred_element_type=f32, precision=prec)


def _split2(a):
    hi = a.astype(bf16)
    return hi, (a - hi.astype(f32)).astype(bf16)


def _dot3(a2, b2):
    d = lambda x, y: jnp.dot(x, y, preferred_element_type=f32)
    return d(a2[0], b2[0]) + (d(a2[0], b2[1]) + d(a2[1], b2[0]))


def _split3(a):
    a1 = a.astype(bf16)
    r = a - a1.astype(f32)
    a2 = r.astype(bf16)
    return a1, a2, (r - a2.astype(f32)).astype(bf16)


def _dot_01(a, m01):
    m = m01.astype(bf16)
    return sum(jnp.dot(t, m, preferred_element_type=f32) for t in _split3(a))


def _dot_01l(m01, b):
    m = m01.astype(bf16)
    return sum(jnp.dot(m, t, preferred_element_type=f32) for t in _split3(b))


def _iota(shape, dim):
    return lax.broadcasted_iota(i32, shape, dim)


def _group_ones(n, width):
    return (_iota((n, n), 0) // width == _iota((n, n), 1) // width).astype(f32)


def _group_rms(x, width):
    ss = _dot_01(x * x, _group_ones(x.shape[1], width))
    return x * lax.rsqrt(ss * (1.0 / width) + EPS)


def _params(sem):
    return pltpu.CompilerParams(dimension_semantics=sem, vmem_limit_bytes=VMEM_LIMIT)


FFN_EXTRA = 16


def _ffn_kernel(x_ref, xs_ref, g_ref, wg_ref, wu_ref, wd_ref, o_ref, os_ref, h_ref):
    tm, ns = x_ref.shape[0], xs_ref.shape[0]

    @pl.when(pl.program_id(1) == 0)
    def _():
        rms = lambda x: x * lax.rsqrt(jnp.mean(x * x, axis=-1, keepdims=True) + EPS) * g_ref[...]
        x, xs = x_ref[...], xs_ref[...]
        h_ref[0:tm, :] = rms(x).astype(bf16)
        h_ref[tm:tm + FFN_EXTRA, :] = jnp.concatenate(
            [rms(xs), jnp.zeros((FFN_EXTRA - ns, D_MODEL), f32)], axis=0).astype(bf16)
        o_ref[...] = x
        os_ref[...] = xs

    h = h_ref[...]
    a = jnp.dot(h, wg_ref[...].astype(bf16), preferred_element_type=f32)
    u = jnp.dot(h, wu_ref[...].astype(bf16), preferred_element_type=f32)
    y = 0.5 * _dot(_silu(a) * u, wd_ref[...])
    o_ref[...] += y[0:tm]
    os_ref[...] += y[tm:tm + ns]


def _ffn(x, xs, gain, wg, wu, wd, layer):
    n, ns = x.shape[0], xs.shape[0]
    tm, tf = ROW_TILE, FFN_COL_TILE
    return pl.pallas_call(
        _ffn_kernel,
        grid=(n // tm, D_FF // tf),
        in_specs=[
            pl.BlockSpec((tm, D_MODEL), lambda i, j: (i, 0)),
            pl.BlockSpec((ns, D_MODEL), lambda i, j: (0, 0)),
            pl.BlockSpec((None, 1, D_MODEL), lambda i, j: (layer, 0, 0)),
            pl.BlockSpec((None, D_MODEL, tf), lambda i, j: (layer, 0, j)),
            pl.BlockSpec((None, D_MODEL, tf), lambda i, j: (layer, 0, j)),
            pl.BlockSpec((None, tf, D_MODEL), lambda i, j: (layer, j, 0)),
        ],
        out_specs=[pl.BlockSpec((tm, D_MODEL), lambda i, j: (i, 0)),
                   pl.BlockSpec((ns, D_MODEL), lambda i, j: (0, 0))],
        out_shape=[jax.ShapeDtypeStruct((n, D_MODEL), f32), jax.ShapeDtypeStruct((ns, D_MODEL), f32)],
        scratch_shapes=[pltpu.VMEM((tm + FFN_EXTRA, D_MODEL), bf16)],
        compiler_params=_params(("arbitrary", "arbitrary")),
        name="ffn",
    )(x, xs, gain, wg, wu, wd)


def _inproj_kernel(x_ref, g_ref, w_ref, o_ref, h_ref):
    @pl.when(pl.program_id(1) == 0)
    def _():
        x = x_ref[...]
        h = x * lax.rsqrt(jnp.mean(x * x, axis=-1, keepdims=True) + EPS) * g_ref[...]
        h_ref[...] = h.astype(bf16)

    o_ref[...] = _dot_nt(h_ref[...], w_ref[...])


def _inproj(x, gain, w_r, layer):
    n = x.shape[0]
    tm = min(n, ROW_TILE)
    tn = 2 * PROJ_COL_TILE
    return pl.pallas_call(
        _inproj_kernel,
        grid=(n // tm, PROJ_W // tn),
        in_specs=[
            pl.BlockSpec((tm, D_MODEL), lambda i, j: (i, 0)),
            pl.BlockSpec((None, 1, D_MODEL), lambda i, j: (layer, 0, 0)),
            pl.BlockSpec((None, tn, D_MODEL), lambda i, j: (layer, j, 0)),
        ],
        out_specs=pl.BlockSpec((tm, tn), lambda i, j: (i, j)),
        out_shape=jax.ShapeDtypeStruct((n, PROJ_W), f32),
        scratch_shapes=[pltpu.VMEM((tm, D_MODEL), bf16)],
        compiler_params=_params(("parallel", "arbitrary")),
        name="inproj",
    )(x, gain, w_r)


def _outproj_kernel(x_ref, ys_ref, yg_ref, yn_ref, ws_ref, wg_ref, wn_ref, o_ref):
    acc = _dot(ys_ref[...], ws_ref[...])
    acc += _dot(yg_ref[...], wg_ref[...])
    acc += _dot(yn_ref[...], wn_ref[...])
    o_ref[...] = x_ref[...] + acc


def _outproj(x, ys, yg, yn, w_out, layer):
    n = x.shape[0]
    tm = min(n, OUT_ROW_TILE)
    tn = PROJ_COL_TILE
    return pl.pallas_call(
        _outproj_kernel,
        grid=(n // tm, D_MODEL // tn),
        in_specs=[
            pl.BlockSpec((tm, tn), lambda i, j: (i, j)),
            pl.BlockSpec((tm, D_SSD), lambda i, j: (i, 0)),
            pl.BlockSpec((tm, D_GDN), lambda i, j: (i, 0)),
            pl.BlockSpec((tm, D_NSA), lambda i, j: (i, 0)),
            pl.BlockSpec((None, D_SSD, tn), lambda i, j: (layer, 0, j)),
            pl.BlockSpec((None, D_GDN, tn), lambda i, j: (layer, 2, j)),
            pl.BlockSpec((None, D_NSA, tn), lambda i, j: (layer, 3, j)),
        ],
        out_specs=pl.BlockSpec((tm, tn), lambda i, j: (i, j)),
        out_shape=jax.ShapeDtypeStruct((n, D_MODEL), f32),
        compiler_params=_params(("parallel", "arbitrary")),
        name="outproj",
    )(x, ys, yg, yn, w_out, w_out, w_out)


def _prep_kernel(q_ref, kv_ref, qg_ref, kg_ref, qn_ref, kv6_ref):
    for s in range(D_NSA // 128):
        sl = slice(s * 128, (s + 1) * 128)
        qn_ref[:, sl] = _group_rms(q_ref[:, sl], NSA_HD) * qg_ref[...]
    kv = kv_ref[...]
    kv6_ref[...] = kv[:, :6 * 128]
    kv6_ref[:, KV_SK:KV_SK + 128] = _group_rms(kv[:, KV_SK:KV_SK + 128], NSA_HD) * kg_ref[1:2, :]
    kv6_ref[:, KV_WK:KV_WK + 128] = _group_rms(kv[:, KV_WK:KV_WK + 128], NSA_HD) * kg_ref[2:3, :]


def _prep(proj, qgain, kgain):
    n = proj.shape[0]
    tm = min(n, PREP_ROW_TILE)
    return pl.pallas_call(
        _prep_kernel,
        grid=(n // tm,),
        in_specs=[
            pl.BlockSpec((tm, D_NSA), lambda i: (i, QN_OFF // D_NSA)),
            pl.BlockSpec((tm, 1024), lambda i: (i, KV_OFF // 1024)),
            pl.BlockSpec((1, 128), lambda i: (0, 0)),
            pl.BlockSpec((3, 128), lambda i: (0, 0)),
        ],
        out_specs=[pl.BlockSpec((tm, D_NSA), lambda i: (i, 0)),
                   pl.BlockSpec((tm, 768), lambda i: (i, 0))],
        out_shape=[jax.ShapeDtypeStruct((n, D_NSA), f32), jax.ShapeDtypeStruct((n, 768), f32)],
        compiler_params=_params(("parallel",)),
        name="nsa_prep",
    )(proj, proj, qgain, kgain)


def _ssd_prompt_kernel(xbc_ref, z_ref, sm_ref, cw_ref, cb_ref, dtb_ref, alog_ref, dexp_ref, nrm_ref,
                       y_ref, st_ref, cst_ref, xbuf, hst, ybuf):
    c = pl.program_id(1)
    cl = SSD_CHUNK

    @pl.when(c == 0)
    def _():
        xbuf[0:8, :] = jnp.zeros((8, SSD_CONV_CH), f32)
        hst[...] = jnp.zeros(hst.shape, f32)

    x = xbc_ref[...]
    xbuf[8:8 + cl, :] = x
    conv = cw_ref[3:4, :] * x
    for k in range(1, CONV_W):
        conv += cw_ref[3 - k:4 - k, :] * xbuf[pl.ds(8 - k, cl), :]
    xbuf[0:8, :] = x[cl - 8:cl, :]
    xc = _silu(conv + cb_ref[...])
    xs = xc[:, :D_SSD]

    dt = jnp.where(_iota((cl, 128), 1) < SSD_HEADS, _softplus(sm_ref[...] + dtb_ref[...]), 0.0)
    da = dt * (-jnp.exp(alog_ref[...]))
    row, col = _iota((cl, cl), 0), _iota((cl, cl), 1)
    causal = row >= col
    cum = _dot_01l(causal, da)
    cum_t = cum.T
    last = cum[cl - 1:cl, :]
    e_last = jnp.exp(last)
    spread = (_iota((128, D_SSD), 1) // SSD_HD == _iota((128, D_SSD), 0)).astype(f32)
    coef = _dot_01(jnp.concatenate([dt, jnp.exp(last - cum), jnp.exp(cum)], axis=0), spread)
    xdt = xs * coef[0:cl]
    xw = xdt * coef[cl:2 * cl]
    e_cum = coef[2 * cl:3 * cl]

    hpg = SSD_HEADS // SSD_GROUPS
    gw = hpg * SSD_HD
    for g in range(SSD_GROUPS):
        bm = xc[:, D_SSD + g * SSD_STATE:D_SSD + (g + 1) * SSD_STATE]
        cm = xc[:, D_SSD + (SSD_GROUPS + g) * SSD_STATE:D_SSD + (SSD_GROUPS + g + 1) * SSD_STATE]
        cb = _dot_nt(cm, bm)
        h0 = hst[g]
        y_off = _dot_nt(cm, h0) * e_cum[:, g * gw:(g + 1) * gw]
        s_new = _dot_tn(xw[:, g * gw:(g + 1) * gw], bm)
        for r in range(hpg):
            h = g * hpg + r
            seg = cum[:, h:h + 1] - cum_t[h:h + 1, :]
            decay = jnp.where(causal, jnp.exp(jnp.where(causal, seg, 0.0)), 0.0)
            rs = slice(r * SSD_HD, (r + 1) * SSD_HD)
            ybuf[:, h * SSD_HD:(h + 1) * SSD_HD] = (_dot(cb * decay, xdt[:, h * SSD_HD:(h + 1) * SSD_HD])
                                                    + y_off[:, rs])
            hst[g, rs, :] = e_last[:, h:h + 1] * h0[rs, :] + s_new[rs, :]

    y = (ybuf[...] + dexp_ref[...] * xs) * _silu(z_ref[...])
    gw = D_SSD // SSD_GROUPS
    for g in range(SSD_GROUPS):
        yg = y[:, g * gw:(g + 1) * gw]
        yg = yg * lax.rsqrt(jnp.mean(yg * yg, axis=-1, keepdims=True) + EPS)
        y_ref[:, g * gw:(g + 1) * gw] = (yg * nrm_ref[:, g * gw:(g + 1) * gw]).astype(y_ref.dtype)

    @pl.when(c == pl.num_programs(1) - 1)
    def _():
        st_ref[...] = hst[...]
        cst_ref[...] = x[cl - (CONV_W - 1):cl, :]


def _ssd_prompt(proj3, lp):
    B, L, _ = proj3.shape
    cl = SSD_CHUNK
    gh = SSD_HEADS // SSD_GROUPS * SSD_HD
    vec = lambda w: pl.BlockSpec((1, w), lambda b, c: (0, 0))
    y, st, cst = pl.pallas_call(
        _ssd_prompt_kernel,
        grid=(B, L // cl),
        in_specs=[
            pl.BlockSpec((None, cl, SSD_CONV_CH), lambda b, c: (b, c, XBC_OFF // SSD_CONV_CH)),
            pl.BlockSpec((None, cl, D_SSD), lambda b, c: (b, c, ZS_OFF // D_SSD)),
            pl.BlockSpec((None, cl, 128), lambda b, c: (b, c, SM_OFF // 128)),
            pl.BlockSpec((CONV_W, SSD_CONV_CH), lambda b, c: (0, 0)),
            vec(SSD_CONV_CH), vec(128), vec(128), vec(D_SSD), vec(D_SSD),
        ],
        out_specs=[
            pl.BlockSpec((None, cl, D_SSD), lambda b, c: (b, c, 0)),
            pl.BlockSpec((None, SSD_GROUPS, gh, SSD_STATE), lambda b, c: (b, 0, 0, 0)),
            pl.BlockSpec((None, CONV_W - 1, SSD_CONV_CH), lambda b, c: (b, 0, 0)),
        ],
        out_shape=[
            jax.ShapeDtypeStruct((B, L, D_SSD), bf16),
            jax.ShapeDtypeStruct((B, SSD_GROUPS, gh, SSD_STATE), f32),
            jax.ShapeDtypeStruct((B, CONV_W - 1, SSD_CONV_CH), f32),
        ],
        scratch_shapes=[
            pltpu.VMEM((8 + cl, SSD_CONV_CH), f32),
            pltpu.VMEM((SSD_GROUPS, gh, SSD_STATE), f32),
            pltpu.VMEM((cl, D_SSD), f32),
        ],
        compiler_params=_params(("parallel", "arbitrary")),
        name="ssd_prompt",
    )(proj3, proj3, proj3, lp["ssd_conv_w"], lp["ssd_conv_b"], lp["ssd_dtb"], lp["ssd_alog"],
      lp["ssd_dexp"], lp["ssd_norm"])
    return y, st.reshape(B, SSD_HEADS, SSD_HD, SSD_STATE), cst


def _gdn_prompt_kernel(qkv_ref, z_ref, sm_ref, cw_ref, dtb_ref, alog_ref, nrm_ref,
                       y_ref, st_ref, cst_ref, xbuf, sst):
    c = pl.program_id(1)
    cl = GDN_CHUNK
    nch = GDN_STEP_CHUNKS
    tl = nch * cl

    @pl.when(c == 0)
    def _():
        xbuf[0:8, :] = jnp.zeros((8, GDN_CONV_CH), f32)
        sst[...] = jnp.zeros(sst.shape, f32)

    x = qkv_ref[...]
    xbuf[8:8 + tl, :] = x
    conv = cw_ref[3:4, :] * x
    for k in range(1, CONV_W):
        conv += cw_ref[3 - k:4 - k, :] * xbuf[pl.ds(8 - k, tl), :]
    xbuf[0:8, :] = x[tl - 8:tl, :]
    xc = _silu(conv)

    sm = sm_ref[...]
    beta_all = jax.nn.sigmoid(sm)
    g_all = -jnp.exp(alog_ref[...]) * _softplus(sm + dtb_ref[...])
    row, col = _iota((cl, cl), 0), _iota((cl, cl), 1)
    incl = row >= col
    strict = row > col
    eye = (row == col).astype(f32)
    rt, ct = _iota((tl, tl), 0), _iota((tl, tl), 1)
    cum = _dot_01l((rt >= ct) & (rt // cl == ct // cl), g_all)
    cum_t = cum.T

    units = [(ci, h) for ci in range(nch) for h in range(GDN_HEADS)]
    q, k, v, beta, cum_c, decay = {}, {}, {}, {}, {}, {}
    for u in units:
        ci, h = u
        rows = slice(ci * cl, (ci + 1) * cl)
        qh = xc[rows, h * GDN_HD:(h + 1) * GDN_HD]
        kh = xc[rows, D_GDN + h * GDN_HD:D_GDN + (h + 1) * GDN_HD]
        v[u] = xc[rows, 2 * D_GDN + h * GDN_HD:2 * D_GDN + (h + 1) * GDN_HD]
        q[u] = qh * lax.rsqrt(jnp.sum(qh * qh, axis=-1, keepdims=True) + EPS) * (GDN_HD ** -0.5)
        k[u] = kh * lax.rsqrt(jnp.sum(kh * kh, axis=-1, keepdims=True) + EPS)
        beta[u] = beta_all[rows, SM_B + h:SM_B + h + 1]
        cum_c[u] = cum[rows, SM_A + h:SM_A + h + 1]
        seg = cum_c[u] - cum_t[SM_A + h:SM_A + h + 1, rows]
        decay[u] = jnp.where(incl, jnp.exp(jnp.where(incl, seg, 0.0)), 0.0)
    pw = {u: -jnp.where(strict, beta[u] * _dot_nt(k[u], k[u]) * decay[u], 0.0) for u in units}
    inv = {u: eye + pw[u] for u in units}
    pw2 = {u: _split2(pw[u]) for u in units}
    for _ in range(5):
        pw2 = {u: _split2(_dot3(pw2[u], pw2[u])) for u in units}
        inv = {u: inv[u] + _dot3(_split2(inv[u]), pw2[u]) for u in units}
    rhs = {u: jnp.concatenate([beta[u] * v[u], (beta[u] * jnp.exp(cum_c[u])) * k[u]], axis=1) for u in units}
    sol = {u: _dot3(_split2(inv[u]), _split2(rhs[u])) for u in units}
    uu = {u: sol[u][:, :GDN_HD] for u in units}
    ww = {u: sol[u][:, GDN_HD:] for u in units}
    qk = {u: _dot_nt(q[u], k[u]) * decay[u] for u in units}

    s = [sst[h] for h in range(GDN_HEADS)]
    for ci in range(nch):
        rows = slice(ci * cl, (ci + 1) * cl)
        hs = [(ci, h) for h in range(GDN_HEADS)]
        v_new = {u: uu[u] - _dot(ww[u], s[u[1]]) for u in hs}
        o = {u: _dot(q[u] * jnp.exp(cum_c[u]), s[u[1]]) + _dot(qk[u], v_new[u]) for u in hs}
        for u in hs:
            h = u[1]
            last = cum[ci * cl + cl - 1:ci * cl + cl, SM_A + h:SM_A + h + 1]
            s[h] = jnp.exp(last) * s[h] + _dot_tn(k[u] * jnp.exp(last - cum_c[u]), v_new[u])
        for u in hs:
            sl = slice(u[1] * GDN_HD, (u[1] + 1) * GDN_HD)
            on = o[u] * lax.rsqrt(jnp.mean(o[u] * o[u], axis=-1, keepdims=True) + EPS)
            y_ref[rows, sl] = (on * nrm_ref[...] * _silu(z_ref[rows, sl])).astype(y_ref.dtype)
    for h in range(GDN_HEADS):
        sst[h] = s[h]

    @pl.when(c == pl.num_programs(1) - 1)
    def _():
        st_ref[...] = sst[...]
        cst_ref[...] = x[tl - (CONV_W - 1):tl, :]


def _gdn_prompt(proj3, lp):
    B, L, _ = proj3.shape
    cl = GDN_CHUNK * GDN_STEP_CHUNKS
    vec = lambda w: pl.BlockSpec((1, w), lambda b, c: (0, 0))
    return pl.pallas_call(
        _gdn_prompt_kernel,
        grid=(B, L // cl),
        in_specs=[
            pl.BlockSpec((None, cl, GDN_CONV_CH), lambda b, c: (b, c, QKV_OFF // GDN_CONV_CH)),
            pl.BlockSpec((None, cl, D_GDN), lambda b, c: (b, c, ZG_OFF // D_GDN)),
            pl.BlockSpec((None, cl, 128), lambda b, c: (b, c, SM_OFF // 128)),
            pl.BlockSpec((CONV_W, GDN_CONV_CH), lambda b, c: (0, 0)),
            vec(128), vec(128), vec(GDN_HD),
        ],
        out_specs=[
            pl.BlockSpec((None, cl, D_GDN), lambda b, c: (b, c, 0)),
            pl.BlockSpec((None, GDN_HEADS, GDN_HD, GDN_HD), lambda b, c: (b, 0, 0, 0)),
            pl.BlockSpec((None, CONV_W - 1, GDN_CONV_CH), lambda b, c: (b, 0, 0)),
        ],
        out_shape=[
            jax.ShapeDtypeStruct((B, L, D_GDN), bf16),
            jax.ShapeDtypeStruct((B, GDN_HEADS, GDN_HD, GDN_HD), f32),
            jax.ShapeDtypeStruct((B, CONV_W - 1, GDN_CONV_CH), f32),
        ],
        scratch_shapes=[
            pltpu.VMEM((8 + cl, GDN_CONV_CH), f32),
            pltpu.VMEM((GDN_HEADS, GDN_HD, GDN_HD), f32),
        ],
        compiler_params=_params(("parallel", "arbitrary")),
        name="gdn_prompt",
    )(proj3, proj3, proj3, lp["gdn_conv_w"], lp["gdn_dtb"], lp["gdn_alog"], lp["gdn_norm"])


def _cmp_prompt_kernel(k_ref, v_ref, pe_ref, w1_ref, w2_ref, kg_ref, kc_ref, vct_ref):
    nb = kc_ref.shape[0]
    for t, src in enumerate((k_ref, v_ref)):
        outs = []
        p1 = [jnp.zeros((nb, CMP_HID), f32) for _ in range(NSA_KV)]
        p2 = [jnp.zeros((nb, CMP_HID), f32) for _ in range(NSA_KV)]
        for l in range(CMP_STRIDE):
            xr = src[pl.ds(l, nb, stride=CMP_STRIDE), :]
            l2 = CMP_STRIDE + l
            for g in range(NSA_KV):
                xg = xr[:, g * NSA_HD:(g + 1) * NSA_HD]
                p1[g] += _dot(xg + pe_ref[t, l:l + 1, :], w1_ref[t, l * NSA_HD:(l + 1) * NSA_HD, :])
                p2[g] += _dot(xg + pe_ref[t, l2:l2 + 1, :], w1_ref[t, l2 * NSA_HD:(l2 + 1) * NSA_HD, :])
        for g in range(NSA_KV):
            hid = p1[g] + pltpu.roll(p2[g], nb - 1, 0)
            cmp = _dot(_silu(hid), w2_ref[t])
            if t == 0:
                cmp = cmp * lax.rsqrt(jnp.mean(cmp * cmp, axis=-1, keepdims=True) + EPS) * kg_ref[0:1, 0:NSA_HD]
            outs.append(cmp)
        both = jnp.concatenate(outs, axis=1)
        if t == 0:
            kc_ref[...] = both
        else:
            vct_ref[...] = both.T


def _cmp_prompt(kv6_3, lp):
    B, L, _ = kv6_3.shape
    nb = L // CMP_STRIDE
    full = lambda s: pl.BlockSpec(s, lambda b: (0,) * len(s))
    return pl.pallas_call(
        _cmp_prompt_kernel,
        grid=(B,),
        in_specs=[
            pl.BlockSpec((None, L, 128), lambda b: (b, 0, KV_CK // 128)),
            pl.BlockSpec((None, L, 128), lambda b: (b, 0, KV_CV // 128)),
            full((2, CMP_LEN, NSA_HD)), full((2, CMP_LEN * NSA_HD, CMP_HID)), full((2, CMP_HID, NSA_HD)),
            full((3, 128)),
        ],
        out_specs=[pl.BlockSpec((None, nb, 128), lambda b: (b, 0, 0)),
                   pl.BlockSpec((None, 128, nb), lambda b: (b, 0, 0))],
        out_shape=[jax.ShapeDtypeStruct((B, nb, 128), f32), jax.ShapeDtypeStruct((B, 128, nb), f32)],
        compiler_params=_params(("parallel",)),
        name="nsa_cmp_prompt",
    )(kv6_3, kv6_3, lp["cmp_pe"], lp["cmp_w1"], lp["cmp_w2"], lp["k_gain"])


def _nsa_prompt_kernel(rb_ref, qt_ref, ks_ref, kw_ref, vst_ref, vwt_ref, kc_ref, vct_ref, gt_ref,
                       y_ref, tb_ref, tc_ref, ext_ref, selm_ref, s_ref):
    qi = pl.program_id(1)
    L = ks_ref.shape[0]
    tq = NSA_Q_TILE
    nl = NSA_REP * tq
    ncmp = kc_ref.shape[0]
    n_slc = L // SLC_BLOCK
    last = N_BUCKETS - 1

    @pl.when(qi == 0)
    def _():
        sub = _iota((128, tq), 0)
        qo = _iota((128, tq), 1)
        d_diag = qo - sub
        c_rel = jnp.where(sub < 64, sub, sub - 128)
        d_cmp = qo - CMP_STRIDE * c_rel - (CMP_LEN - 1)
        for g in range(NSA_KV):
            for r in range(NSA_REP):
                h = g * NSA_REP + r
                tab = [rb_ref[b, h] - rb_ref[last, h] for b in range(N_BUCKETS)]
                lanes = slice(r * tq, (r + 1) * tq)
                tb_ref[g, TB_FAR, :, lanes] = jnp.zeros((128, tq), f32)
                tb_ref[g, TB_PREV, :, lanes] = _bias_chain(d_diag + 128, tab)
                tb_ref[g, TB_DIAG, :, lanes] = jnp.where(d_diag >= 0, _bias_chain(d_diag, tab), NEG)
                tb_ref[g, TB_WIN0, :, lanes] = jnp.where(d_diag <= 0, 0.0, NEG)
                tb_ref[g, TB_NONE, :, lanes] = jnp.full((128, tq), NEG, f32)
                t_c = jnp.where(d_cmp < 0, 0.0, _bias_chain(d_cmp, tab))
                tc_ref[g, 0:128, lanes] = t_c
                tc_ref[g, 128:256, lanes] = t_c
        ext_ref[...] = (_iota((L, 128), 0) // SLC_BLOCK == _iota((L, 128), 1)).astype(bf16)

    q0 = qi * tq
    qpos = q0 + _iota((1, nl), 1) % tq
    qpos1 = q0 + _iota((1, tq), 1)
    sub = _iota((128, 1), 0)
    gt = gt_ref[...]

    jr = _iota((n_slc, tq), 0)
    cur = qpos1 // SLC_BLOCK
    valid = jr * SLC_BLOCK <= qpos1
    forced = (jr == 0) | (jr == cur) | (jr == cur - 1)
    j_i = _iota((n_slc, ncmp), 0)
    c_i = _iota((n_slc, ncmp), 1)
    ovl = ((c_i * CMP_STRIDE < j_i * SLC_BLOCK + SLC_BLOCK) & (c_i * CMP_STRIDE + CMP_LEN > j_i * SLC_BLOCK)
           & (c_i < ncmp - 1)).astype(f32)
    cmp_off = pl.multiple_of((128 - qi * (tq // CMP_STRIDE)) % 128, 8)

    groups = range(NSA_KV)
    gsl = [slice(g * NSA_HD, (g + 1) * NSA_HD) for g in groups]
    qt = [jnp.concatenate([qt_ref[(g * NSA_REP + r) * NSA_HD:(g * NSA_REP + r + 1) * NSA_HD, :]
                           for r in range(NSA_REP)], axis=1).astype(bf16) for g in groups]

    ok_c = (sub * CMP_STRIDE + (CMP_LEN - 1) <= qpos) & (sub < ncmp - 1)
    o_cmp = []
    for g in groups:
        st = _dot(kc_ref[:, gsl[g]], qt[g]) + tc_ref[g, pl.ds(cmp_off, 128), :]
        st = jnp.where(ok_c, st, NEG)
        e = jnp.exp(st - jnp.max(st, axis=0, keepdims=True))
        p = jnp.where(ok_c, e / jnp.sum(e, axis=0, keepdims=True), 0.0)
        o_cmp.append(_dot(vct_ref[gsl[g], :], p))
        p_sum = p[:, 0:tq] + p[:, tq:2 * tq] + p[:, 2 * tq:3 * tq] + p[:, 3 * tq:4 * tq]
        imp = _dot_01l(ovl, p_sum)
        score = jnp.where(valid, jnp.where(forced, 1e9, imp), -1e9)
        rank = jnp.zeros((n_slc, tq), f32)
        for i in range(n_slc):
            si = score[i:i + 1, :]
            rank += ((si > score) | ((si == score) & (jr > i))).astype(f32)
        sel = (rank < SLC_TOPN).astype(bf16)
        sel = jnp.concatenate([sel, jnp.zeros((128 - n_slc, tq), bf16)], axis=0)
        selx = jnp.dot(ext_ref[...], sel, preferred_element_type=f32)
        selm_ref[g] = (selx - 1.0) * -NEG

    ones_rows = jnp.ones((16, 128), f32)

    def attend(state, tiles, k_ref, vt_ref):
        s, offs = {}, []
        for ti, (kt, table) in enumerate(tiles):
            ko = pl.multiple_of(kt * 128, 128)
            offs.append(ko)
            for g in groups:
                sg = _dot(k_ref[pl.ds(ko, 128), gsl[g]], qt[g]) + tb_ref[g, table]
                s[g, ti] = sg
        out = []
        for g in groups:
            m, acc = state[g]
            m_new = m
            for ti in range(len(tiles)):
                m_new = jnp.maximum(m_new, jnp.max(s[g, ti], axis=0, keepdims=True))
            acc = jnp.exp(m - m_new) * acc
            for ti in range(len(tiles)):
                vt = jnp.concatenate([vt_ref[gsl[g], pl.ds(offs[ti], 128)], ones_rows], axis=0)
                acc += _dot(vt, jnp.exp(s[g, ti] - m_new))
            out.append((m_new, acc))
        return tuple(out)

    finish = lambda st: [acc[0:NSA_HD] / acc[NSA_HD:NSA_HD + 1] for _, acc in st]
    init = tuple((jnp.full((1, nl), NEG, f32), jnp.zeros((NSA_HD + 16, nl), f32)) for _ in groups)

    def sel_table(kt):
        return jnp.where(kt == qi, TB_DIAG, jnp.where(kt == qi - 1, TB_PREV, TB_FAR))

    def score_tiles(kts, mx):
        mx = list(mx)
        for kt in kts:
            ko = pl.multiple_of(kt * 128, 128)
            for g in groups:
                sg = (_dot(ks_ref[pl.ds(ko, 128), gsl[g]], qt[g]) + tb_ref[g, sel_table(kt)]
                      + jnp.concatenate([selm_ref[g, pl.ds(ko, 128), :]] * NSA_REP, axis=1))
                s_ref[g, kt] = sg
                parts = [sg[8 * k:8 * k + 8, :] for k in range(16)]
                while len(parts) > 1:
                    parts = [jnp.maximum(a, b) for a, b in zip(parts[0::2], parts[1::2])]
                mx[g] = jnp.maximum(mx[g], parts[0])
        return tuple(mx)

    def value_tiles(kts, accs, m):
        accs = list(accs)
        for kt in kts:
            ko = pl.multiple_of(kt * 128, 128)
            for g in groups:
                vt = jnp.concatenate([vst_ref[gsl[g], pl.ds(ko, 128)], ones_rows], axis=0)
                accs[g] = accs[g] + _dot(vt, jnp.exp(s_ref[g, kt] - m[g]))
        return tuple(accs)

    n_pair, n_odd = (qi + 1) // 2, (qi + 1) % 2
    mx = tuple(jnp.full((8, nl), NEG, f32) for _ in groups)
    mx = lax.fori_loop(0, n_pair, lambda i, c: score_tiles([2 * i, 2 * i + 1], c), mx)
    mx = lax.fori_loop(0, n_odd, lambda i, c: score_tiles([qi], c), mx)
    m_sel = [jnp.max(x, axis=0, keepdims=True) for x in mx]
    accs = tuple(jnp.zeros((NSA_HD + 16, nl), f32) for _ in groups)
    accs = lax.fori_loop(0, n_pair, lambda i, c: value_tiles([2 * i, 2 * i + 1], c, m_sel), accs)
    accs = lax.fori_loop(0, n_odd, lambda i, c: value_tiles([qi], c, m_sel), accs)
    o_slc = [acc[0:NSA_HD] / acc[NSA_HD:NSA_HD + 1] for acc in accs]

    w0 = jnp.maximum(qi - WINDOW // 128, 0)

    def win_table(kt):
        t = jnp.where(kt == qi - WINDOW // 128, TB_WIN0, TB_FAR)
        t = jnp.where(kt == qi - 1, TB_PREV, t)
        return jnp.where(kt == qi, TB_DIAG, jnp.where(kt > qi, TB_NONE, t))

    win_tiles = [(w0 + t, win_table(w0 + t)) for t in range(WINDOW // 128 + 1)]
    o_win = finish(attend(init, win_tiles, kw_ref, vwt_ref))

    y_parts = []
    for g in groups:
        for r in range(NSA_REP):
            lanes = slice(r * tq, (r + 1) * tq)
            h = g * NSA_REP + r
            gate = lambda t: gt[SM_G + t * NSA_HEADS + h:SM_G + t * NSA_HEADS + h + 1, :]
            y_parts.append(gate(0) * o_cmp[g][:, lanes] + gate(1) * o_slc[g][:, lanes]
                           + gate(2) * o_win[g][:, lanes])
    y_ref[...] = jnp.concatenate(y_parts, axis=0).T.astype(y_ref.dtype)


def _nsa_prompt(qt, kv6_3, kv6t, kc, vct, gt, rel_bias):
    B, L, _ = kv6_3.shape
    tq = NSA_Q_TILE
    nb = L // CMP_STRIDE
    return pl.pallas_call(
        _nsa_prompt_kernel,
        grid=(B, L // tq),
        in_specs=[
            pl.BlockSpec(memory_space=pltpu.SMEM),
            pl.BlockSpec((None, D_NSA, tq), lambda b, i: (b, 0, i)),
            pl.BlockSpec((None, L, 128), lambda b, i: (b, 0, KV_SK // 128)),
            pl.BlockSpec((None, L, 128), lambda b, i: (b, 0, KV_WK // 128)),
            pl.BlockSpec((None, 128, L), lambda b, i: (b, KV_SV // 128, 0)),
            pl.BlockSpec((None, 128, L), lambda b, i: (b, KV_WV // 128, 0)),
            pl.BlockSpec((None, nb, 128), lambda b, i: (b, 0, 0)),
            pl.BlockSpec((None, 128, nb), lambda b, i: (b, 0, 0)),
            pl.BlockSpec((None, 128, tq), lambda b, i: (b, 0, i)),
        ],
        out_specs=pl.BlockSpec((None, tq, D_NSA), lambda b, i: (b, i, 0)),
        out_shape=jax.ShapeDtypeStruct((B, L, D_NSA), bf16),
        scratch_shapes=[
            pltpu.VMEM((NSA_KV, TB_KINDS, 128, NSA_REP * tq), f32),
            pltpu.VMEM((NSA_KV, 256, NSA_REP * tq), f32),
            pltpu.VMEM((L, 128), bf16),
            pltpu.VMEM((NSA_KV, L, tq), f32),
            pltpu.VMEM((NSA_KV, L // 128, 128, NSA_REP * tq), f32),
        ],
        compiler_params=_params(("parallel", "arbitrary")),
        name="nsa_prompt",
    )(rel_bias, qt, kv6_3, kv6_3, kv6t, kv6t, kc, vct, gt)


def _prep_t_kernel(q_ref, kv_ref, qg_ref, kg_ref, qt_ref, kv6_ref, kv6t_ref, gt_ref):
    for s in range(D_NSA // 128):
        sl = slice(s * 128, (s + 1) * 128)
        qt_ref[sl, :] = (_group_rms(q_ref[:, sl], NSA_HD) * qg_ref[...] * (NSA_HD ** -0.5)).T
    kv = kv_ref[...]
    for j, off in enumerate((KV_CK, KV_CV, KV_SK, KV_SV, KV_WK, KV_WV)):
        x = kv[:, off:off + 128]
        if off == KV_SK:
            x = _group_rms(x, NSA_HD) * kg_ref[1:2, :]
        if off == KV_WK:
            x = _group_rms(x, NSA_HD) * kg_ref[2:3, :]
        kv6_ref[:, off:off + 128] = x
        kv6t_ref[off:off + 128, :] = x.T
    gt_ref[...] = jax.nn.sigmoid(kv[:, SM_OFF - KV_OFF:SM_OFF - KV_OFF + 128]).T


def _prep_t(proj3, qgain, kgain):
    B, L, _ = proj3.shape
    tm = min(L, PREP_ROW_TILE)
    return pl.pallas_call(
        _prep_t_kernel,
        grid=(B, L // tm),
        in_specs=[
            pl.BlockSpec((None, tm, D_NSA), lambda b, i: (b, i, QN_OFF // D_NSA)),
            pl.BlockSpec((None, tm, 1024), lambda b, i: (b, i, KV_OFF // 1024)),
            pl.BlockSpec((1, 128), lambda b, i: (0, 0)),
            pl.BlockSpec((3, 128), lambda b, i: (0, 0)),
        ],
        out_specs=[pl.BlockSpec((None, D_NSA, tm), lambda b, i: (b, 0, i)),
                   pl.BlockSpec((None, tm, 768), lambda b, i: (b, i, 0)),
                   pl.BlockSpec((None, 768, tm), lambda b, i: (b, 0, i)),
                   pl.BlockSpec((None, 128, tm), lambda b, i: (b, 0, i))],
        out_shape=[jax.ShapeDtypeStruct((B, D_NSA, L), f32), jax.ShapeDtypeStruct((B, L, 768), f32),
                   jax.ShapeDtypeStruct((B, 768, L), f32), jax.ShapeDtypeStruct((B, 128, L), f32)],
        compiler_params=_params(("parallel", "parallel")),
        name="nsa_prep_t",
    )(proj3, proj3, qgain, kgain)


def _row0(x, rows=8):
    return jnp.where(_iota((rows, x.shape[1]), 0) == 0, x, 0.0)


def _ssd_decode_kernel(xbc_ref, z_ref, sm_ref, cprev_ref, sin_ref, cw_ref, cb_ref, dtb_ref, alog_ref, dexp_ref,
                       nrm_ref, y_ref, st_ref, cst_ref):
    x = xbc_ref[...]
    conv = cw_ref[3:4, :] * x
    for j in range(CONV_W - 1):
        conv += cw_ref[j:j + 1, :] * cprev_ref[j:j + 1, :]
    cst_ref[0:2, :] = cprev_ref[1:3, :]
    cst_ref[2:3, :] = x
    xc = _silu(conv + cb_ref[...])
    xs = xc[:, :D_SSD]
    dt = _softplus(sm_ref[...] + dtb_ref[...])
    da = dt * (-jnp.exp(alog_ref[...]))
    ys = []
    for g in range(SSD_GROUPS):
        bm = xc[:, D_SSD + g * SSD_STATE:D_SSD + (g + 1) * SSD_STATE]
        cm = xc[:, D_SSD + (SSD_GROUPS + g) * SSD_STATE:D_SSD + (SSD_GROUPS + g + 1) * SSD_STATE]
        cb = jnp.sum(cm * bm, axis=-1, keepdims=True)
        bm8, cm8 = _row0(bm), _row0(cm)
        for r in range(SSD_HEADS // SSD_GROUPS):
            h = g * (SSD_HEADS // SSD_GROUPS) + r
            xdt = xs[:, h * SSD_HD:(h + 1) * SSD_HD] * dt[:, h:h + 1]
            eda = jnp.exp(da[:, h:h + 1])
            h0 = sin_ref[h]
            ys.append(cb * xdt + eda * _dot_nt(cm8, h0, HI)[0:1])
            st_ref[h] = eda * h0 + _dot_tn(_row0(xdt), bm8, HI)
    y = (jnp.concatenate(ys, axis=1) + dexp_ref[...] * xs) * _silu(z_ref[...])
    gw = D_SSD // SSD_GROUPS
    for g in range(SSD_GROUPS):
        yg = y[:, g * gw:(g + 1) * gw]
        yg = yg * lax.rsqrt(jnp.mean(yg * yg, axis=-1, keepdims=True) + EPS)
        y_ref[:, g * gw:(g + 1) * gw] = yg * nrm_ref[:, g * gw:(g + 1) * gw]


def _ssd_decode(proj3, conv_state, state, layer, lp):
    B = proj3.shape[0]
    vec = lambda w: pl.BlockSpec((1, w), lambda b: (0, 0))
    return pl.pallas_call(
        _ssd_decode_kernel,
        grid=(B,),
        in_specs=[
            pl.BlockSpec((None, 1, SSD_CONV_CH), lambda b: (b, 0, XBC_OFF // SSD_CONV_CH)),
            pl.BlockSpec((None, 1, D_SSD), lambda b: (b, 0, ZS_OFF // D_SSD)),
            pl.BlockSpec((None, 1, 128), lambda b: (b, 0, SM_OFF // 128)),
            pl.BlockSpec((None, None, CONV_W - 1, SSD_CONV_CH), lambda b: (layer, b, 0, 0)),
            pl.BlockSpec((None, None, SSD_HEADS, SSD_HD, SSD_STATE), lambda b: (layer, b, 0, 0, 0)),
            pl.BlockSpec((CONV_W, SSD_CONV_CH), lambda b: (0, 0)),
            vec(SSD_CONV_CH), vec(128), vec(128), vec(D_SSD), vec(D_SSD),
        ],
        out_specs=[
            pl.BlockSpec((None, 1, D_SSD), lambda b: (b, 0, 0)),
            pl.BlockSpec((None, SSD_HEADS, SSD_HD, SSD_STATE), lambda b: (b, 0, 0, 0)),
            pl.BlockSpec((None, CONV_W - 1, SSD_CONV_CH), lambda b: (b, 0, 0)),
        ],
        out_shape=[
            jax.ShapeDtypeStruct((B, 1, D_SSD), f32),
            jax.ShapeDtypeStruct((B, SSD_HEADS, SSD_HD, SSD_STATE), f32),
            jax.ShapeDtypeStruct((B, CONV_W - 1, SSD_CONV_CH), f32),
        ],
        compiler_params=_params(("parallel",)),
        name="ssd_decode",
    )(proj3, proj3, proj3, conv_state, state, lp["ssd_conv_w"], lp["ssd_conv_b"], lp["ssd_dtb"], lp["ssd_alog"],
      lp["ssd_dexp"], lp["ssd_norm"])


def _gdn_decode_kernel(qkv_ref, z_ref, sm_ref, cprev_ref, sin_ref, cw_ref, dtb_ref, alog_ref, nrm_ref,
                       y_ref, st_ref, cst_ref):
    x = qkv_ref[...]
    conv = cw_ref[3:4, :] * x
    for j in range(CONV_W - 1):
        conv += cw_ref[j:j + 1, :] * cprev_ref[j:j + 1, :]
    cst_ref[0:2, :] = cprev_ref[1:3, :]
    cst_ref[2:3, :] = x
    xc = _silu(conv)
    sm = sm_ref[...]
    beta_all = jax.nn.sigmoid(sm)
    g_all = -jnp.exp(alog_ref[...]) * _softplus(sm + dtb_ref[...])
    for h in range(GDN_HEADS):
        sl = slice(h * GDN_HD, (h + 1) * GDN_HD)
        qh = xc[:, sl]
        kh = xc[:, D_GDN + h * GDN_HD:D_GDN + (h + 1) * GDN_HD]
        vh = xc[:, 2 * D_GDN + h * GDN_HD:2 * D_GDN + (h + 1) * GDN_HD]
        qh = qh * lax.rsqrt(jnp.sum(qh * qh, axis=-1, keepdims=True) + EPS) * (GDN_HD ** -0.5)
        kh = kh * lax.rsqrt(jnp.sum(kh * kh, axis=-1, keepdims=True) + EPS)
        beta = beta_all[:, SM_B + h:SM_B + h + 1]
        eg = jnp.exp(g_all[:, SM_A + h:SM_A + h + 1])
        s0 = sin_ref[h]
        k8 = _row0(kh)
        v_new = beta * vh - (beta * eg) * _dot(k8, s0, HI)[0:1]
        o = eg * _dot(_row0(qh), s0, HI)[0:1] + jnp.sum(qh * kh, axis=-1, keepdims=True) * v_new
        st_ref[h] = eg * s0 + _dot_tn(k8, _row0(v_new), HI)
        o = o * lax.rsqrt(jnp.mean(o * o, axis=-1, keepdims=True) + EPS)
        y_ref[:, sl] = o * nrm_ref[...] * _silu(z_ref[:, sl])


def _gdn_decode(proj3, conv_state, state, layer, lp):
    B = proj3.shape[0]
    vec = lambda w: pl.BlockSpec((1, w), lambda b: (0, 0))
    return pl.pallas_call(
        _gdn_decode_kernel,
        grid=(B,),
        in_specs=[
            pl.BlockSpec((None, 1, GDN_CONV_CH), lambda b: (b, 0, QKV_OFF // GDN_CONV_CH)),
            pl.BlockSpec((None, 1, D_GDN), lambda b: (b, 0, ZG_OFF // D_GDN)),
            pl.BlockSpec((None, 1, 128), lambda b: (b, 0, SM_OFF // 128)),
            pl.BlockSpec((None, None, CONV_W - 1, GDN_CONV_CH), lambda b: (layer, b, 0, 0)),
            pl.BlockSpec((None, None, GDN_HEADS, GDN_HD, GDN_HD), lambda b: (layer, b, 0, 0, 0)),
            pl.BlockSpec((CONV_W, GDN_CONV_CH), lambda b: (0, 0)),
            vec(128), vec(128), vec(GDN_HD),
        ],
        out_specs=[
            pl.BlockSpec((None, 1, D_GDN), lambda b: (b, 0, 0)),
            pl.BlockSpec((None, GDN_HEADS, GDN_HD, GDN_HD), lambda b: (b, 0, 0, 0)),
            pl.BlockSpec((None, CONV_W - 1, GDN_CONV_CH), lambda b: (b, 0, 0)),
        ],
        out_shape=[
            jax.ShapeDtypeStruct((B, 1, D_GDN), f32),
            jax.ShapeDtypeStruct((B, GDN_HEADS, GDN_HD, GDN_HD), f32),
            jax.ShapeDtypeStruct((B, CONV_W - 1, GDN_CONV_CH), f32),
        ],
        compiler_params=_params(("parallel",)),
        name="gdn_decode",
    )(proj3, proj3, proj3, conv_state, state, lp["gdn_conv_w"], lp["gdn_dtb"], lp["gdn_alog"], lp["gdn_norm"])


def _query_rows(q, g):
    rows = [q[:, (g * NSA_REP + r) * NSA_HD:(g * NSA_REP + r + 1) * NSA_HD] for r in range(NSA_REP)]
    return jnp.concatenate(rows + [jnp.zeros((8 - NSA_REP, NSA_HD), f32)], axis=0) * (NSA_HD ** -0.5)


def _masked_softmax(lg, ok):
    lg = jnp.where(ok, lg, NEG)
    e = jnp.exp(lg - jnp.max(lg, axis=-1, keepdims=True))
    return jnp.where(ok, e / jnp.sum(e, axis=-1, keepdims=True), 0.0)


def _head_lanes(parts):
    return jnp.concatenate([o[r:r + 1, :] for o in parts for r in range(NSA_REP)], axis=1)


def _nsa_dec_cmp_kernel(pt_ref, q_ref, pe_ref, w1_ref, w2_ref, kg_ref, tab_ref, pk_ref, pv_ref,
                        ocmp_ref, idx_ref, stage, rows_k, rows_v, sem, *, layer, n_pages):
    b = pl.program_id(0)
    nb = pl.num_programs(0)
    n_past = n_pages * PAGE
    q_pos = n_past
    n_blk = n_past // CMP_STRIDE
    n_cmp = (n_past + 1 - CMP_LEN) // CMP_STRIDE + 1
    n_slc = -(-(n_past + 1) // SLC_BLOCK)
    nj = -(-n_slc // 128) * 128
    pools = (pk_ref, pv_ref)

    def page_copy(t, seq, p):
        return pltpu.make_async_copy(pools[t].at[layer, pt_ref[seq, p]], stage.at[t, p], sem.at[t])

    def start_all(t, seq):
        lax.fori_loop(0, n_pages, lambda p, c: (page_copy(t, seq, p).start(), c)[1], 0)

    def wait_all(t, seq):
        lax.fori_loop(0, n_pages, lambda p, c: (page_copy(t, seq, p).wait(), c)[1], 0)

    @pl.when(b == 0)
    def _():
        start_all(0, 0)
        start_all(1, 0)

    n_pair = CMP_STRIDE // 2
    share = n_pages // n_pair

    def compress(t, rows, between=None):
        p1 = jnp.zeros((n_blk, 2 * CMP_HID), f32)
        p2 = jnp.zeros((n_blk, 2 * CMP_HID), f32)
        for j in range(n_pair):
            xr = jnp.concatenate([rows[pl.ds(2 * j, n_blk, stride=CMP_STRIDE), :],
                                  rows[pl.ds(2 * j + 1, n_blk, stride=CMP_STRIDE), :]], axis=1)
            j2 = n_pair + j
            p1 += jnp.dot((xr + pe_ref[t, j:j + 1, :]).astype(bf16), w1_ref[t, j], preferred_element_type=f32)
            p2 += jnp.dot((xr + pe_ref[t, j2:j2 + 1, :]).astype(bf16), w1_ref[t, j2], preferred_element_type=f32)
            if between is not None:
                between(j)
        hid = p1 + pltpu.roll(p2, n_blk - 1, 0)
        return jnp.dot(_silu(hid).astype(bf16), w2_ref[t], preferred_element_type=f32)

    def value_pages(j):
        for p in range(j * share, (j + 1) * share):
            rows_v[p * PAGE:(p + 1) * PAGE, :] = stage[1, p].T

    def next_key_pages(j):
        for p in range(j * share, (j + 1) * share):
            rows_k[p * PAGE:(p + 1) * PAGE, :] = stage[0, p].T

    @pl.when(b == 0)
    def _():
        wait_all(0, 0)

        def key_page(p, c):
            rows_k[pl.ds(pl.multiple_of(p * PAGE, PAGE), PAGE), :] = stage[0, p].T
            return c
        lax.fori_loop(0, n_pages, key_page, 0, unroll=4)

    @pl.when(b + 1 < nb)
    def _():
        start_all(0, b + 1)

    wait_all(1, b)
    kc = _group_rms(compress(0, rows_k, value_pages), NSA_HD) * kg_ref[0:1, :]

    @pl.when(b + 1 < nb)
    def _():
        start_all(1, b + 1)
        wait_all(0, b + 1)

    vc = compress(1, rows_v, next_key_pages)

    q = q_ref[...]
    lane = _iota((8, n_blk), 1)
    cend = lane * CMP_STRIDE + (CMP_LEN - 1)
    ok = (cend <= q_pos) & (lane < n_cmp)
    c_i = _iota((n_blk, nj), 0)
    j_i = _iota((n_blk, nj), 1)
    ovl = ((c_i * CMP_STRIDE < j_i * SLC_BLOCK + SLC_BLOCK) & (c_i * CMP_STRIDE + CMP_LEN > j_i * SLC_BLOCK)
           & (j_i < n_slc) & (c_i < n_cmp)).astype(f32)
    jl = _iota((1, nj), 1)
    cur = q_pos // SLC_BLOCK
    valid = jl * SLC_BLOCK <= q_pos
    forced = (jl == 0) | (jl == cur) | (jl == cur - 1)
    ii = _iota((nj, nj), 0)
    jj = _iota((nj, nj), 1)
    kk = _iota((SLC_TOPN, nj), 0)
    j16 = _iota((SLC_TOPN, nj), 1)
    o_parts, idx_cols = [], []
    for g in range(NSA_KV):
        tabcols = [tab_ref[g, :, bk:bk + 1] for bk in range(N_BUCKETS)]
        lg = _dot_nt(_query_rows(q, g), kc[:, g * NSA_HD:(g + 1) * NSA_HD]) + _bias_chain(q_pos - cend, tabcols)
        p = _masked_softmax(lg, ok)
        o_parts.append(_dot(p, vc[:, g * NSA_HD:(g + 1) * NSA_HD]))
        p_sum = p[0:1] + p[1:2] + p[2:3] + p[3:4]
        imp = _dot_01(_row0(p_sum), ovl)[0:1]
        score = jnp.where(valid, jnp.where(forced, 1e9, imp), -1e9)
        score = jnp.where(jl < n_slc, score, -3e9)
        s_c = jnp.sum(jnp.where(ii == jj, score, 0.0), axis=1, keepdims=True)
        rank_r = jnp.sum(((s_c > score) | ((s_c == score) & (ii < jj))).astype(f32), axis=0, keepdims=True)
        rank_c = jnp.sum(((score > s_c) | ((score == s_c) & (jj < ii))).astype(f32), axis=1, keepdims=True)
        sel_r = (rank_r < SLC_TOPN) & (jl < n_slc)
        sel_c = (rank_c < SLC_TOPN) & (ii[:, 0:1] < n_slc)
        pos_r = jnp.sum((sel_c & (ii < jj)).astype(f32), axis=0, keepdims=True)
        hit = (pos_r == kk.astype(f32)) & sel_r
        idx_cols.append(jnp.sum(jnp.where(hit, j16.astype(f32), 0.0), axis=1, keepdims=True))
    ocmp_ref[...] = _head_lanes(o_parts)
    l16 = _iota((SLC_TOPN, 128), 1)
    idx_ref[...] = (jnp.where(l16 == 0, idx_cols[0], 0.0) + jnp.where(l16 == 1, idx_cols[1], 0.0)).astype(i32)


def _nsa_dec_cmp(qn3, page_table, pool_k, pool_v, layer, lp):
    B = qn3.shape[0]
    n_pages = page_table.shape[1]
    full = lambda s: pl.BlockSpec(s, lambda b, pt: (0,) * len(s))
    return pl.pallas_call(
        functools.partial(_nsa_dec_cmp_kernel, layer=layer, n_pages=n_pages),
        grid_spec=pltpu.PrefetchScalarGridSpec(
            num_scalar_prefetch=1,
            grid=(B,),
            in_specs=[
                pl.BlockSpec((None, 1, D_NSA), lambda b, pt: (b, 0, 0)),
                full((2, CMP_LEN // 2, 256)), full((2, CMP_LEN // 2, 256, 2 * CMP_HID)),
                full((2, 2 * CMP_HID, 128)), full((3, 128)), full((NSA_KV, 8, N_BUCKETS)),
                pl.BlockSpec(memory_space=pl.ANY), pl.BlockSpec(memory_space=pl.ANY),
            ],
            out_specs=[pl.BlockSpec((None, 1, D_NSA), lambda b, pt: (b, 0, 0)),
                       pl.BlockSpec((None, SLC_TOPN, 128), lambda b, pt: (b, 0, 0))],
            scratch_shapes=[pltpu.VMEM((2, n_pages, NSA_KV * NSA_HD, PAGE), f32),
                            pltpu.VMEM((n_pages * PAGE, NSA_KV * NSA_HD), f32),
                            pltpu.VMEM((n_pages * PAGE, NSA_KV * NSA_HD), f32),
                            pltpu.SemaphoreType.DMA((2,))],
        ),
        out_shape=[jax.ShapeDtypeStruct((B, 1, D_NSA), f32), jax.ShapeDtypeStruct((B, SLC_TOPN, 128), i32)],
        compiler_params=_params(("arbitrary",)),
        name="nsa_dec_cmp",
    )(page_table, qn3, lp["cmp_pe2"], lp["cmp_w1bd"], lp["cmp_w2bd"], lp["k_gain"], lp["tab8"], pool_k, pool_v)


def _nsa_dec_sel_kernel(pt_ref, idx_ref, q_ref, kv_ref, sm_ref, ocmp_ref, tab_ref, wk_ref, wv_ref, pk_ref, pv_ref,
                        y_ref, wko_ref, wvo_ref, kg, vg, sem, *, layer, n_pages):
    b = pl.program_id(0)
    n_past = n_pages * PAGE
    q_pos = n_past
    n_buf = wk_ref.shape[1]
    new_blk = n_past // SLC_BLOCK
    per_page = PAGE // SLC_BLOCK

    def blk_copy(pool, buf, g, s, sm_i):
        j = jnp.minimum(idx_ref[b, s, g], new_blk - 1)
        src = pool.at[layer, pt_ref[b, j // per_page], pl.ds(g * NSA_HD, NSA_HD)]
        return pltpu.make_async_copy(src, buf.at[g, :, pl.ds(s * PAGE, PAGE)], sem.at[sm_i])

    for g in range(NSA_KV):
        for s in range(SLC_TOPN):
            blk_copy(pk_ref, kg, g, s, 0).start(priority=s % 2)
            blk_copy(pv_ref, vg, g, s, 1).start(priority=(s + 1) % 2)

    kv = kv_ref[...]
    q = q_ref[...]
    gate = jax.nn.sigmoid(sm_ref[...])
    ocmp = ocmp_ref[...]

    ii, jj = _iota((128, 128), 0), _iota((128, 128), 1)
    column = lambda off: jnp.sum(jnp.where(ii == jj, kv[:, off:off + 128], 0.0), axis=1, keepdims=True)
    last_lane = _iota((128, n_buf), 1) == n_buf - 1
    wk, wv = wk_ref[...], wv_ref[...]
    wko_ref[...] = jnp.where(last_lane, column(KV_WK), pltpu.roll(wk, n_buf - 1, 1))
    wvo_ref[...] = jnp.where(last_lane, column(KV_WV), pltpu.roll(wv, n_buf - 1, 1))

    def attend(qg, kt, vt, bias, ok, k_new, v_new, bias_new, ok_new):
        lg = jnp.where(ok, _dot(qg, kt) + bias, NEG)
        lg_new = jnp.where(ok_new, jnp.sum(qg * k_new, axis=-1, keepdims=True) + bias_new, NEG)
        m = jnp.maximum(jnp.max(lg, axis=-1, keepdims=True), lg_new)
        e = jnp.where(ok, jnp.exp(lg - m), 0.0)
        e_new = jnp.where(ok_new, jnp.exp(lg_new - m), 0.0)
        den = jnp.sum(e, axis=-1, keepdims=True) + e_new
        p, p_new = e / den, e_new / den
        return _dot_nt(p, vt) + p_new * v_new

    for g in range(NSA_KV):
        for s in range(SLC_TOPN):
            blk_copy(pk_ref, kg, g, s, 0).wait()
            blk_copy(pv_ref, vg, g, s, 1).wait()

    lw = _iota((8, n_buf), 1)
    d_w = q_pos - (n_past - n_buf + lw)
    ok_w = (d_w >= 0) & (d_w <= WINDOW)
    ls = _iota((8, SLC_TOPN * PAGE), 1)
    slot = ls // PAGE
    zero = jnp.zeros((8, 1), i32)
    o_slc, o_win = [], []
    for g in range(NSA_KV):
        tabcols = [tab_ref[g, :, bk:bk + 1] for bk in range(N_BUCKETS)]
        qg = _query_rows(q, g)
        gs = slice(g * NSA_HD, (g + 1) * NSA_HD)
        b_new = _bias_chain(zero, tabcols)
        blk = jnp.zeros(ls.shape, i32)
        n_new = jnp.zeros((8, 1), i32)
        for s in range(SLC_TOPN):
            j = idx_ref[b, s, g]
            blk = jnp.where(slot == s, j, blk)
            n_new = jnp.where(j == new_blk, n_new + 1, n_new)
        k_pos = (jnp.minimum(blk, new_blk - 1) // per_page) * PAGE + ls % PAGE
        ok_s = (blk < new_blk) & (k_pos // SLC_BLOCK == blk)
        o_slc.append(attend(qg, kg[g], vg[g], _bias_chain(q_pos - k_pos, tabcols), ok_s,
                            kv[:, KV_SK + g * NSA_HD:KV_SK + (g + 1) * NSA_HD],
                            kv[:, KV_SV + g * NSA_HD:KV_SV + (g + 1) * NSA_HD], b_new, n_new > 0))
        o_win.append(attend(qg, wk[gs, :], wv[gs, :], _bias_chain(d_w, tabcols), ok_w,
                            kv[:, KV_WK + g * NSA_HD:KV_WK + (g + 1) * NSA_HD],
                            kv[:, KV_WV + g * NSA_HD:KV_WV + (g + 1) * NSA_HD], b_new, zero == 0))
    gl = lambda t: gate[:, SM_G + t * NSA_HEADS:SM_G + (t + 1) * NSA_HEADS]
    wide = lambda gt: jnp.concatenate([jnp.broadcast_to(gt[:, h:h + 1], (1, NSA_HD)) for h in range(NSA_HEADS)], axis=1)
    y_ref[...] = wide(gl(0)) * ocmp + wide(gl(1)) * _head_lanes(o_slc) + wide(gl(2)) * _head_lanes(o_win)


def _nsa_dec_sel(qn3, kv6_3, proj3, ocmp, idx, page_table, win_k, win_v, pool_k, pool_v, layer, lp):
    B = qn3.shape[0]
    n_pages = page_table.shape[1]
    n_buf = win_k.shape[3]
    full = lambda s: pl.BlockSpec(s, lambda b, pt, ix: (0,) * len(s))
    row = lambda w, j: pl.BlockSpec((None, 1, w), lambda b, pt, ix: (b, 0, j))
    win = pl.BlockSpec((None, None, 128, n_buf), lambda b, pt, ix: (layer, b, 0, 0))
    return pl.pallas_call(
        functools.partial(_nsa_dec_sel_kernel, layer=layer, n_pages=n_pages),
        grid_spec=pltpu.PrefetchScalarGridSpec(
            num_scalar_prefetch=2,
            grid=(B,),
            in_specs=[row(D_NSA, 0), row(768, 0), row(128, SM_OFF // 128), row(D_NSA, 0),
                      full((NSA_KV, 8, N_BUCKETS)), win, win,
                      pl.BlockSpec(memory_space=pl.ANY), pl.BlockSpec(memory_space=pl.ANY)],
            out_specs=[row(D_NSA, 0),
                       pl.BlockSpec((None, 128, n_buf), lambda b, pt, ix: (b, 0, 0)),
                       pl.BlockSpec((None, 128, n_buf), lambda b, pt, ix: (b, 0, 0))],
            scratch_shapes=[pltpu.VMEM((NSA_KV, NSA_HD, SLC_TOPN * PAGE), f32),
                            pltpu.VMEM((NSA_KV, NSA_HD, SLC_TOPN * PAGE), f32),
                            pltpu.SemaphoreType.DMA((2,))],
        ),
        out_shape=[jax.ShapeDtypeStruct((B, 1, D_NSA), f32),
                   jax.ShapeDtypeStruct((B, 128, n_buf), f32), jax.ShapeDtypeStruct((B, 128, n_buf), f32)],
        compiler_params=_params(("arbitrary",)),
        name="nsa_dec_sel",
    )(page_table, idx, qn3, kv6_3, proj3, ocmp, lp["tab8"], win_k, win_v, pool_k, pool_v)


def _layer_params(i, p):
    pad128 = lambda v, off: jnp.zeros((1, 128), f32).at[0, off:off + v.shape[0]].set(v)
    return dict(
        ssd_conv_w=p["ssd_conv_w"][i], ssd_conv_b=p["ssd_conv_b"][i][None],
        ssd_dtb=pad128(p["ssd_dt_bias"][i], SM_DT), ssd_alog=pad128(p["ssd_a_log"][i], SM_DT),
        ssd_dexp=jnp.repeat(p["ssd_d"][i], SSD_HD)[None], ssd_norm=p["ssd_norm"][i][None],
        gdn_conv_w=p["gdn_conv_w"][i],
        gdn_dtb=pad128(p["gdn_dt_bias"][i], SM_A), gdn_alog=pad128(p["gdn_a_log"][i], SM_A),
        gdn_norm=p["gdn_norm"][i][None],
        q_gain=jnp.tile(p["nsa_q_norm"][i], 2)[None], k_gain=jnp.tile(p["nsa_k_norm"][i], (1, 2)),
        cmp_pe=p["nsa_cmp_pe"][i], cmp_w1=p["nsa_cmp_w1"][i], cmp_w2=p["nsa_cmp_w2"][i],
        cmp_pe2=jnp.tile(p["nsa_cmp_pe"][i], (1, 1, 2)).reshape(2, CMP_LEN // 2, 256),
        cmp_w1bd=_block_diag2(p["nsa_cmp_w1"][i].reshape(2, CMP_LEN, NSA_HD, CMP_HID)).astype(bf16)
        .reshape(2, CMP_LEN // 2, 256, 2 * CMP_HID),
        cmp_w2bd=_block_diag2(p["nsa_cmp_w2"][i]).astype(bf16),
        tab8=jnp.pad(p["rel_bias"].T.reshape(NSA_KV, NSA_REP, N_BUCKETS), ((0, 0), (0, 8 - NSA_REP), (0, 0))),
    )


def _block_diag2(w):
    z = jnp.zeros_like(w)
    return jnp.concatenate([jnp.concatenate([w, z], axis=-1), jnp.concatenate([z, w], axis=-1)], axis=-2)


def _reorder_w_in(w_in):
    cuts = [(1024, 2560), (2576, 4112), (0, 1024), (4112, 4624), (4632, 5144), (5144, 5912),
            (2560, 2576), (4624, 4632), (5912, 5936)]
    wt = jnp.transpose(w_in, (0, 2, 1))
    parts = [wt[:, a:b] for a, b in cuts]
    used = sum(b - a for a, b in cuts)
    parts.append(jnp.zeros((w_in.shape[0], PROJ_W - used, w_in.shape[1]), w_in.dtype))
    return jnp.concatenate(parts, axis=1).astype(bf16)


def _mix_prompt(x3, i, p, lp, w_r):
    B, L, _ = x3.shape
    n = B * L
    x = x3.reshape(n, D_MODEL)
    proj = _inproj(x, p["norm_mix"], w_r, i)
    proj3 = proj.reshape(B, L, PROJ_W)
    y_ssd, st_ssd, cst_ssd = _ssd_prompt(proj3, lp)
    y_gdn, st_gdn, cst_gdn = _gdn_prompt(proj3, lp)
    qt, kv6_3, kv6t, gt = _prep_t(proj3, lp["q_gain"], lp["k_gain"])
    kc, vct = _cmp_prompt(kv6_3, lp)
    y_nsa = _nsa_prompt(qt, kv6_3, kv6t, kc, vct, gt, p["rel_bias"])
    x = _outproj(x, y_ssd.reshape(n, D_SSD), y_gdn.reshape(n, D_GDN), y_nsa.reshape(n, D_NSA), p["w_out"], i)
    rows = lambda off: jnp.transpose(kv6t[:, off:off + 128].reshape(B, NSA_KV, NSA_HD, L), (0, 3, 1, 2))
    keep = min(WINDOW, L)
    state = (st_ssd, cst_ssd, st_gdn, cst_gdn, rows(KV_CK), rows(KV_CV), rows(KV_SK), rows(KV_SV),
             rows(KV_WK)[:, L - keep:], rows(KV_WV)[:, L - keep:])
    return x.reshape(B, L, D_MODEL), state


def _mix_decode(x, i, p, lp, w_r, st):
    B = x.shape[0]
    proj = _inproj(x, p["norm_mix"], w_r, i)
    proj3 = proj.reshape(B, 1, PROJ_W)
    y_ssd, st_ssd, cst_ssd = _ssd_decode(proj3, st["ssd_conv"], st["ssd"], i, lp)
    y_gdn, st_gdn, cst_gdn = _gdn_decode(proj3, st["gdn_conv"], st["gdn"], i, lp)
    qn, kv6 = _prep(proj, lp["q_gain"], lp["k_gain"])
    qn3, kv6_3 = qn.reshape(B, 1, D_NSA), kv6.reshape(B, 1, 768)
    o_cmp, idx = _nsa_dec_cmp(qn3, st["page_table"], st["cmp_k"], st["cmp_v"], i, lp)
    y_nsa, win_k, win_v = _nsa_dec_sel(qn3, kv6_3, proj3, o_cmp, idx[:, :, :NSA_KV], st["page_table"],
                                       st["win_k"], st["win_v"], st["slc_k"], st["slc_v"], i, lp)
    x = _outproj(x, y_ssd.reshape(B, D_SSD), y_gdn.reshape(B, D_GDN), y_nsa.reshape(B, D_NSA), p["w_out"], i)
    rows = lambda off: kv6[:, off:off + 128].reshape(B, 1, NSA_KV, NSA_HD)
    n_buf = win_k.shape[2]
    buf = lambda w: jnp.transpose(w.reshape(B, NSA_KV, NSA_HD, n_buf), (0, 3, 1, 2))
    state = (st_ssd, cst_ssd, st_gdn, cst_gdn, rows(KV_CK), rows(KV_CV), rows(KV_SK), rows(KV_SV),
             buf(win_k), buf(win_v))
    return x, state


def kernel(x_prompt, x_sample, state_ssd, state_ssd_conv, state_gdn, state_gdn_conv, cache_cmp_k, cache_cmp_v,
           cache_slc_k, cache_slc_v, cache_win_k, cache_win_v, page_table, rel_bias, norm_ffn1, w_ffn1_gate,
           w_ffn1_up, w_ffn1_down, norm_mix, w_in, ssd_conv_w, ssd_conv_b, ssd_dt_bias, ssd_a_log, ssd_d, ssd_norm,
           gdn_conv_w, gdn_dt_bias, gdn_a_log, gdn_norm, nsa_q_norm, nsa_k_norm, nsa_cmp_pe, nsa_cmp_w1,
           nsa_cmp_w2, w_out, norm_ffn2, w_ffn2_gate, w_ffn2_up, w_ffn2_down):
    bp, lp_len, _ = x_prompt.shape
    bs = x_sample.shape[0]
    gain3 = lambda g: g.reshape(DEPTH, 1, D_MODEL)
    p = dict(rel_bias=rel_bias, norm_mix=gain3(norm_mix), w_out=w_out.astype(bf16), ssd_conv_w=ssd_conv_w,
             ssd_conv_b=ssd_conv_b,
             ssd_dt_bias=ssd_dt_bias, ssd_a_log=ssd_a_log, ssd_d=ssd_d, ssd_norm=ssd_norm, gdn_conv_w=gdn_conv_w,
             gdn_dt_bias=gdn_dt_bias, gdn_a_log=gdn_a_log, gdn_norm=gdn_norm, nsa_q_norm=nsa_q_norm,
             nsa_k_norm=nsa_k_norm, nsa_cmp_pe=nsa_cmp_pe, nsa_cmp_w1=nsa_cmp_w1, nsa_cmp_w2=nsa_cmp_w2)
    n1, n2 = gain3(norm_ffn1), gain3(norm_ffn2)
    chan_row = lambda c: jnp.transpose(c, (0, 1, 3, 4, 2)).reshape(c.shape[:2] + (NSA_KV * NSA_HD, c.shape[2]))
    st = dict(ssd=state_ssd, ssd_conv=state_ssd_conv, gdn=state_gdn, gdn_conv=state_gdn_conv,
              cmp_k=chan_row(cache_cmp_k), cmp_v=chan_row(cache_cmp_v), slc_k=chan_row(cache_slc_k),
              slc_v=chan_row(cache_slc_v), win_k=chan_row(cache_win_k), win_v=chan_row(cache_win_v),
              page_table=page_table)
    w_r = _reorder_w_in(w_in)

    hp = x_prompt.reshape(bp * lp_len, D_MODEL)
    hs = x_sample.reshape(bs, D_MODEL)
    outs_p, outs_s = [], []
    for i in range(DEPTH):
        lp = _layer_params(i, p)
        hp, hs = _ffn(hp, hs, n1, w_ffn1_gate, w_ffn1_up, w_ffn1_down, i)
        hp3, st_p = _mix_prompt(hp.reshape(bp, lp_len, D_MODEL), i, p, lp, w_r)
        hs, st_s = _mix_decode(hs, i, p, lp, w_r, st)
        hp, hs = _ffn(hp3.reshape(bp * lp_len, D_MODEL), hs, n2, w_ffn2_gate, w_ffn2_up, w_ffn2_down, i)
        outs_p.append(st_p)
        outs_s.append(st_s)
    stack = lambda outs: [jnp.stack(t) for t in zip(*outs)]
    return (hp.reshape(bp, lp_len, D_MODEL), hs.reshape(bs, 1, D_MODEL), *stack(outs_p), *stack(outs_s))
```
